```python
import jax, jax.numpy as jnp
from jax import lax
import numpy as np

D_MODEL = 1024
BATCH = 16
SEQ = 256
DEPTH = 2
DEC_BATCH = 4
DEC_SEQ = 1024
PAST_LEN = 256

GRID_W = 64
GLA_HEADS = 4
GLA_DK = 64
GLA_DV = 128
GLA_KEY = GLA_HEADS * GLA_DK
GLA_VAL = GLA_HEADS * GLA_DV
GLA_GATE_RANK = 16
GLA_GATE_NORM = 16.0
GLA_CHUNK = 64
CONV_CH = 256
CONV_WIDTH = 31
NA_HEADS = 4
NA_HD = 64
NA_DIM = NA_HEADS * NA_HD
NA_WIN_ROWS = 8
NA_WIN_COLS = 16
NA_QBLOCK = 16
NA_KBLOCK = NA_QBLOCK + NA_WIN_COLS
MIX_WIDTH = GLA_VAL + CONV_CH + NA_DIM
IN_SPLITS = (GLA_KEY, GLA_KEY, GLA_VAL, GLA_VAL, 2 * GLA_GATE_RANK, 2 * CONV_CH, NA_DIM, NA_DIM, NA_DIM)
IN_WIDTH = 2 * GLA_KEY + 2 * GLA_VAL + 2 * GLA_GATE_RANK + 2 * CONV_CH + 3 * NA_DIM
N_EXPERTS = 64
TOP_K = 8
N_GROUPS = 8
TOPK_GROUPS = 4
EXPERT_DIM = 256
SHARED_DIM = 256
ROUTED_SCALE = 2.5
MOE_BLOCK = 128
EPS = 1e-6

kernel_name = 'hybrid_gla_conv_natten_moe_diffusion_step'


def rms_norm(x, g):
    xf = x.astype(jnp.float32)
    y = xf * lax.rsqrt(jnp.mean(xf * xf, axis=-1, keepdims=True) + EPS)
    return (y * g.astype(jnp.float32)).astype(x.dtype)


def layer_norm(x, g, b):
    xf = x.astype(jnp.float32)
    xc = xf - jnp.mean(xf, axis=-1, keepdims=True)
    y = xc * lax.rsqrt(jnp.mean(xc * xc, axis=-1, keepdims=True) + EPS)
    return (y * g.astype(jnp.float32) + b.astype(jnp.float32)).astype(x.dtype)


def to_heads(t, n_heads):
    b, n, w = t.shape
    return t.reshape(b, n, n_heads, w // n_heads).transpose(0, 2, 1, 3)


def from_heads(t):
    b, h, n, d = t.shape
    return t.transpose(0, 2, 1, 3).reshape(b, n, h * d)


def ada_mod(cvec, w_ada, b_ada):
    m = jax.nn.silu(cvec) @ w_ada + b_ada
    return jnp.split(m[..., None, :], 6, axis=-1)


def gla_chunked(q, k, v, log_a, s0):
    b, h, n, dk = q.shape
    dv = v.shape[-1]
    nc = n // GLA_CHUNK
    q = q.reshape(b, h, nc, GLA_CHUNK, dk)
    k = k.reshape(b, h, nc, GLA_CHUNK, dk)
    v = v.reshape(b, h, nc, GLA_CHUNK, dv)
    bc = jnp.cumsum(log_a.reshape(b, h, nc, GLA_CHUNK, dk), axis=3)
    causal = np.tril(np.ones((GLA_CHUNK, GLA_CHUNK), dtype=bool))[:, :, None]
    diff = bc[:, :, :, :, None, :] - bc[:, :, :, None, :, :]
    decay = jnp.exp(jnp.where(causal, diff, -jnp.inf))
    attn = jnp.sum(q[:, :, :, :, None, :] * k[:, :, :, None, :, :] * decay, axis=-1)
    o_intra = jnp.einsum('bhnij,bhnjv->bhniv', attn, v)
    b_last = bc[:, :, :, -1:, :]
    u = jnp.einsum('bhncd,bhncv->bhndv', k * jnp.exp(b_last - bc), v)
    g = jnp.exp(b_last[:, :, :, 0, :])

    def step(s, inp):
        g_n, u_n = inp
        return g_n[..., None] * s + u_n, s

    s_fin, s_prev = lax.scan(step, s0, (jnp.moveaxis(g, 2, 0), jnp.moveaxis(u, 2, 0)))
    s_prev = jnp.moveaxis(s_prev, 0, 2)
    o_inter = jnp.einsum('bhncd,bhndv->bhncv', q * jnp.exp(bc), s_prev)
    return (o_intra + o_inter).reshape(b, h, n, dv), s_fin


def gla_mixer(gq, gk, gv, gg, glr, w_gk, b_gk, out_norm, s0):
    f32 = jnp.float32
    b, n, _ = gq.shape
    q = to_heads(gq, GLA_HEADS).astype(f32) * GLA_DK ** -0.5
    k = to_heads(gk, GLA_HEADS).astype(f32)
    v = to_heads(gv, GLA_HEADS).astype(f32)
    lr = glr.reshape(b, n, 2, GLA_GATE_RANK).astype(f32)
    z = jnp.einsum('bnsr,srk->bnsk', lr, w_gk.astype(f32)) + b_gk.astype(f32)
    log_a = jax.nn.log_sigmoid(z) / GLA_GATE_NORM
    la_f = to_heads(log_a[:, :, 0], GLA_HEADS)
    la_b = to_heads(log_a[:, :, 1], GLA_HEADS)
    s0 = s0.astype(f32)
    o_f, s_f = gla_chunked(q, k, v, la_f, s0[:, 0])
    flip = lambda t: jnp.flip(t, axis=2)
    o_b, s_b = gla_chunked(flip(q), flip(k), flip(v), flip(la_b), s0[:, 1])
    o = rms_norm(o_f + flip(o_b), out_norm)
    out = from_heads(o) * jax.nn.silu(gg.astype(f32))
    return out.astype(gq.dtype), jnp.stack([s_f, s_b], axis=1).astype(gq.dtype)


def conv_module(a, dw_w, dw_b, ln_g, ln_b, w_pw):
    h = a[..., :CONV_CH] * jax.nn.sigmoid(a[..., CONV_CH:])
    pad = CONV_WIDTH // 2
    h = lax.conv_general_dilated(h, dw_w[:, None, :], window_strides=(1,), padding=[(pad, pad)],
                                 dimension_numbers=('NWC', 'WIO', 'NWC'), feature_group_count=CONV_CH) + dw_b
    h = jax.nn.silu(layer_norm(h, ln_g, ln_b))
    return h @ w_pw


def attend_dense(q, k, v):
    s = jnp.einsum('bhnd,bhld->bhnl', q, k).astype(jnp.float32) * NA_HD ** -0.5
    p = jax.nn.softmax(s, axis=-1).astype(v.dtype)
    return jnp.einsum('bhnl,bhld->bhnd', p, v)


def neighborhood_attention(q, k, v, k_ctx, v_ctx, rpb):
    b, nh, n, d = q.shape
    rows = n // GRID_W
    kr = min(NA_WIN_ROWS, rows)
    nb = GRID_W // NA_QBLOCK
    kw = kr * NA_KBLOCK
    r = np.arange(rows)
    key_rows = np.clip(r - kr // 2, 0, rows - kr)[:, None] + np.arange(kr)[None, :]
    blk = np.arange(nb)
    key_cols = np.clip(blk * NA_QBLOCK - NA_WIN_COLS // 2, 0, GRID_W - NA_KBLOCK)[:, None] + np.arange(NA_KBLOCK)[None, :]
    kidx = (key_rows[:, None, :, None] * GRID_W + key_cols[None, :, None, :]).reshape(rows, nb, kw)
    kg = k[:, :, kidx]
    vg = v[:, :, kidx]
    q_cols = blk[:, None] * NA_QBLOCK + np.arange(NA_QBLOCK)[None, :]
    win_start = np.clip(q_cols - NA_WIN_COLS // 2, 0, GRID_W - NA_WIN_COLS)
    kc = key_cols[:, None, :]
    in_win = (kc >= win_start[:, :, None]) & (kc < win_start[:, :, None] + NA_WIN_COLS)
    mask = np.broadcast_to(in_win[:, :, None, :], (nb, NA_QBLOCK, kr, NA_KBLOCK)).reshape(nb, NA_QBLOCK, kw)
    row_off = key_rows - r[:, None] + NA_WIN_ROWS - 1
    col_off = np.clip(kc - q_cols[:, :, None] + NA_WIN_COLS - 1, 0, 2 * NA_WIN_COLS - 2)
    bias = rpb[:, row_off[:, :, None, None, None], col_off[None, None, :, :, :]]
    bias = bias.transpose(0, 1, 3, 4, 2, 5).reshape(nh, rows, nb, NA_QBLOCK, kw).astype(jnp.float32)
    scale = NA_HD ** -0.5
    qb = q.reshape(b, nh, rows, nb, NA_QBLOCK, d)
    s_loc = jnp.einsum('bhrnqd,bhrnkd->bhrnqk', qb, kg).astype(jnp.float32) * scale + bias[None]
    s_loc = jnp.where(mask, s_loc, -jnp.inf)
    s_ctx = jnp.einsum('bhrnqd,bhld->bhrnql', qb, k_ctx).astype(jnp.float32) * scale
    p = jax.nn.softmax(jnp.concatenate([s_loc, s_ctx], axis=-1), axis=-1).astype(v.dtype)
    o = (jnp.einsum('bhrnqk,bhrnkd->bhrnqd', p[..., :kw], vg)
         + jnp.einsum('bhrnql,bhld->bhrnqd', p[..., kw:], v_ctx))
    return o.reshape(b, nh, n, d)


def moe(h, router_w, router_bias, w_gate, w_up, w_down, sh_gate, sh_up, sh_down):
    b, n, d = h.shape
    per_group = N_EXPERTS // N_GROUPS

    def block(t):
        scores = jax.nn.sigmoid((t @ router_w).astype(jnp.float32))
        biased = scores + router_bias.astype(jnp.float32)
        grp_score = jnp.sum(lax.top_k(biased.reshape(-1, N_GROUPS, per_group), 2)[0], axis=-1)
        top_grp = lax.top_k(grp_score, TOPK_GROUPS)[1]
        grp_keep = jnp.sum(jax.nn.one_hot(top_grp, N_GROUPS, dtype=jnp.float32), axis=1) > 0
        masked = jnp.where(jnp.repeat(grp_keep, per_group, axis=-1), biased, -jnp.inf)
        top_e = lax.top_k(masked, TOP_K)[1]
        sel = jnp.take_along_axis(scores, top_e, axis=-1)
        wts = sel / jnp.sum(sel, axis=-1, keepdims=True) * ROUTED_SCALE
        gate = jnp.einsum('tk,tke->te', wts, jax.nn.one_hot(top_e, N_EXPERTS, dtype=jnp.float32)).astype(t.dtype)
        act = jax.nn.silu(jnp.einsum('td,edf->tef', t, w_gate)) * jnp.einsum('td,edf->tef', t, w_up)
        routed = jnp.einsum('tef,efd->td', act * gate[:, :, None], w_down)
        shared = (jax.nn.silu(t @ sh_gate) * (t @ sh_up)) @ sh_down
        return routed + shared

    y = lax.map(block, h.reshape(-1, MOE_BLOCK, d))
    return y.reshape(b, n, d)


def trunk_layer(x, cvec, lw, gla_s0, k_ctx, v_ctx):
    sh1, sc1, g1, sh2, sc2, g2 = ada_mod(cvec, lw['w_ada'], lw['b_ada'])
    h = rms_norm(x, lw['norm_mix']) * (1 + sc1) + sh1
    split_points = [int(s) for s in np.cumsum(IN_SPLITS)[:-1]]
    gq, gk, gv, gg, glr, conv_in, nq, nk, nv = jnp.split(h @ lw['w_in'], split_points, axis=-1)
    gla_o, gla_state = gla_mixer(gq, gk, gv, gg, glr, lw['gla_w_gate'], lw['gla_b_gate'], lw['gla_out_norm'], gla_s0)
    conv_o = conv_module(conv_in, lw['conv_dw'], lw['conv_dw_b'], lw['conv_ln_g'], lw['conv_ln_b'], lw['conv_pw'])
    q = rms_norm(to_heads(nq, NA_HEADS), lw['na_q_norm'])
    k = rms_norm(to_heads(nk, NA_HEADS), lw['na_k_norm'])
    v = to_heads(nv, NA_HEADS)
    if k_ctx is None:
        na = attend_dense(q, k, v)
    else:
        na = neighborhood_attention(q, k, v, k_ctx, v_ctx, lw['na_rpb'])
    mix = jnp.concatenate([gla_o, conv_o, from_heads(na)], axis=-1) @ lw['w_out']
    x = x + g1 * mix
    h = rms_norm(x, lw['norm_ffn']) * (1 + sc2) + sh2
    x = x + g2 * moe(h, lw['router_w'], lw['router_bias'], lw['exp_w_gate'], lw['exp_w_up'], lw['exp_w_down'],
                     lw['sh_w_gate'], lw['sh_w_up'], lw['sh_w_down'])
    return x, gla_state, k, v


def setup_inputs(seed: int = 0) -> dict:
    key = jax.random.key(seed)
    keys = iter(jax.random.split(key, 40))

    def rnd(shape, scale):
        return jax.random.normal(next(keys), shape, jnp.float32) * scale

    L = DEPTH
    return {
        'x_prompt': rnd((BATCH, SEQ, D_MODEL), 1.0),
        'x_sample': rnd((DEC_BATCH, DEC_SEQ, D_MODEL), 1.0),
        'state_gla': rnd((DEC_BATCH, DEPTH, 2, GLA_HEADS, GLA_DK, GLA_DV), 0.3),
        'cache_na_k': rnd((DEC_BATCH, DEPTH, NA_HEADS, PAST_LEN, NA_HD), 1.0),
        'cache_na_v': rnd((DEC_BATCH, DEPTH, NA_HEADS, PAST_LEN, NA_HD), 1.0),
        'c': rnd((DEC_BATCH, D_MODEL), 1.0),
        'c_ctx': rnd((D_MODEL,), 1.0),
        'w_ada': rnd((L, D_MODEL, 6 * D_MODEL), 0.5 * D_MODEL ** -0.5),
        'b_ada': rnd((L, 6 * D_MODEL), 0.01),
        'norm_mix': 1.0 + rnd((L, D_MODEL), 0.01),
        'norm_ffn': 1.0 + rnd((L, D_MODEL), 0.01),
        'w_in': rnd((L, D_MODEL, IN_WIDTH), D_MODEL ** -0.5),
        'gla_w_gate': rnd((L, 2, GLA_GATE_RANK, GLA_KEY), GLA_GATE_RANK ** -0.5),
        'gla_b_gate': rnd((L, 2, GLA_KEY), 0.1),
        'gla_out_norm': 1.0 + rnd((L, GLA_DV), 0.01),
        'conv_dw': rnd((L, CONV_WIDTH, CONV_CH), CONV_WIDTH ** -0.5),
        'conv_dw_b': rnd((L, CONV_CH), 0.01),
        'conv_ln_g': 1.0 + rnd((L, CONV_CH), 0.01),
        'conv_ln_b': rnd((L, CONV_CH), 0.01),
        'conv_pw': rnd((L, CONV_CH, CONV_CH), CONV_CH ** -0.5),
        'na_q_norm': 1.0 + rnd((L, NA_HD), 0.01),
        'na_k_norm': 1.0 + rnd((L, NA_HD), 0.01),
        'na_rpb': rnd((L, NA_HEADS, 2 * NA_WIN_ROWS - 1, 2 * NA_WIN_COLS - 1), 0.1),
        'w_out': rnd((L, MIX_WIDTH, D_MODEL), MIX_WIDTH ** -0.5),
        'router_w': rnd((L, D_MODEL, N_EXPERTS), D_MODEL ** -0.5),
        'router_bias': rnd((L, N_EXPERTS), 0.01),
        'exp_w_gate': rnd((L, N_EXPERTS, D_MODEL, EXPERT_DIM), D_MODEL ** -0.5),
        'exp_w_up': rnd((L, N_EXPERTS, D_MODEL, EXPERT_DIM), D_MODEL ** -0.5),
        'exp_w_down': rnd((L, N_EXPERTS, EXPERT_DIM, D_MODEL), EXPERT_DIM ** -0.5),
        'sh_w_gate': rnd((L, D_MODEL, SHARED_DIM), D_MODEL ** -0.5),
        'sh_w_up': rnd((L, D_MODEL, SHARED_DIM), D_MODEL ** -0.5),
        'sh_w_down': rnd((L, SHARED_DIM, D_MODEL), SHARED_DIM ** -0.5),
    }


def reference(x_prompt, x_sample, state_gla, cache_na_k, cache_na_v, c, c_ctx, w_ada, b_ada, norm_mix, norm_ffn,
              w_in, gla_w_gate, gla_b_gate, gla_out_norm, conv_dw, conv_dw_b, conv_ln_g, conv_ln_b, conv_pw,
              na_q_norm, na_k_norm, na_rpb, w_out, router_w, router_bias, exp_w_gate, exp_w_up, exp_w_down,
              sh_w_gate, sh_w_up, sh_w_down):
    y_prompt = x_prompt
    y_sample = x_sample
    gla_states, ctx_keys, ctx_vals = [], [], []
    zero_state = jnp.zeros((x_prompt.shape[0], 2, GLA_HEADS, GLA_DK, GLA_DV), x_prompt.dtype)
    for l in range(DEPTH):
        lw = dict(w_ada=w_ada[l], b_ada=b_ada[l], norm_mix=norm_mix[l], norm_ffn=norm_ffn[l], w_in=w_in[l],
                  gla_w_gate=gla_w_gate[l], gla_b_gate=gla_b_gate[l], gla_out_norm=gla_out_norm[l],
                  conv_dw=conv_dw[l], conv_dw_b=conv_dw_b[l], conv_ln_g=conv_ln_g[l], conv_ln_b=conv_ln_b[l],
                  conv_pw=conv_pw[l], na_q_norm=na_q_norm[l], na_k_norm=na_k_norm[l], na_rpb=na_rpb[l],
                  w_out=w_out[l], router_w=router_w[l], router_bias=router_bias[l], exp_w_gate=exp_w_gate[l],
                  exp_w_up=exp_w_up[l], exp_w_down=exp_w_down[l], sh_w_gate=sh_w_gate[l], sh_w_up=sh_w_up[l],
                  sh_w_down=sh_w_down[l])
        y_prompt, s_l, k_l, v_l = trunk_layer(y_prompt, c_ctx, lw, zero_state, None, None)
        gla_states.append(s_l)
        ctx_keys.append(k_l)
        ctx_vals.append(v_l)
        y_sample, _, _, _ = trunk_layer(y_sample, c, lw, state_gla[:, l], cache_na_k[:, l], cache_na_v[:, l])
    new_state_gla = jnp.stack(gla_states, axis=1)
    new_cache_na_k = jnp.stack(ctx_keys, axis=1)
    new_cache_na_v = jnp.stack(ctx_vals, axis=1)
    return (y_prompt, y_sample, new_state_gla, new_cache_na_k, new_cache_na_v)
```

```python
import functools

import numpy as np
import jax
import jax.numpy as jnp
from jax import lax
from jax.experimental import pallas as pl
from jax.experimental.pallas import tpu as pltpu

D_MODEL = 1024
BATCH = 16
SEQ = 256
DEPTH = 2
DEC_BATCH = 4
DEC_SEQ = 1024
PAST_LEN = 256
GRID_W = 64
GLA_HEADS = 4
GLA_DK = 64
GLA_DV = 128
GLA_KEY = GLA_HEADS * GLA_DK
GLA_VAL = GLA_HEADS * GLA_DV
GLA_GATE_RANK = 16
GLA_GATE_NORM = 16.0
GLA_CHUNK = 64
CONV_CH = 256
CONV_WIDTH = 31
NA_HEADS = 4
NA_HD = 64
NA_DIM = NA_HEADS * NA_HD
NA_WIN_ROWS = 8
NA_WIN_COLS = 16
N_EXPERTS = 64
TOP_K = 8
N_GROUPS = 8
TOPK_GROUPS = 4
EXPERT_DIM = 256
SHARED_DIM = 256
ROUTED_SCALE = 2.5
EPS = 1e-6

N_CTX_TOK = BATCH * SEQ
N_LAT_TOK = DEC_BATCH * DEC_SEQ
N_TOK = N_CTX_TOK + N_LAT_TOK
ROW_TILE = 256
N_CTX_TILES = N_CTX_TOK // ROW_TILE
TILES_PER_LAT_SEQ = DEC_SEQ // ROW_TILE
MOD_ROWS = 8
GLA_IN_W = 1664
GLA_LR_OFF = 2 * GLA_KEY + 2 * GLA_VAL
VMEM_LIMIT = 56 * 1024 * 1024

_BF = jnp.bfloat16
_F32 = jnp.float32


def _bf(x):
    return x.astype(_BF)


def _dot(a, b):
    return jnp.dot(a, b, preferred_element_type=_F32)


def _dot_nt(a, b):
    return lax.dot_general(a, b, (((1,), (1,)), ((), ())), preferred_element_type=_F32)


def _dot_tn(a, b):
    return lax.dot_general(a, b, (((0,), (0,)), ((), ())), preferred_element_type=_F32)


def _split2(x):
    hi = _bf(x)
    lo = _bf(x - hi.astype(_F32))
    return hi, lo


def _split3(x):
    hi = _bf(x)
    r = x - hi.astype(_F32)
    mid = _bf(r)
    lo = _bf(r - mid.astype(_F32))
    return hi, mid, lo


def _dot3(a, b, dot=_dot):
    a_hi, a_lo = _split2(a)
    b_hi, b_lo = _split2(b)
    return (dot(a_lo, b_hi) + dot(a_hi, b_lo)) + dot(a_hi, b_hi)


def _sigmoid(x):
    return 1.0 / (1.0 + jnp.exp(-x))


def _silu(x):
    return x * _sigmoid(x)


def _rms(x, w):
    return x * lax.rsqrt(jnp.mean(x * x, axis=-1, keepdims=True) + EPS) * w


def _params(sem):
    return pltpu.CompilerParams(dimension_semantics=sem, vmem_limit_bytes=VMEM_LIMIT)


def _mod_row(i):
    return jnp.where(i < N_CTX_TILES, 0, 1 + (i - N_CTX_TILES) // TILES_PER_LAT_SEQ)


def _ada_kernel(cv_ref, w_ref, b_ref, o_ref):
    o_ref[0] = _dot3(_silu(cv_ref[...]), w_ref[0]) + b_ref[0]


def _ada_mod(cvec, w_ada, b_ada):
    tn = 1024
    n_out = 6 * D_MODEL
    return pl.pallas_call(
        _ada_kernel,
        out_shape=jax.ShapeDtypeStruct((DEPTH, MOD_ROWS, n_out), _F32),
        grid=(DEPTH, n_out // tn),
        in_specs=[
            pl.BlockSpec((MOD_ROWS, D_MODEL), lambda l, j: (0, 0)),
            pl.BlockSpec((1, D_MODEL, tn), lambda l, j: (l, 0, j)),
            pl.BlockSpec((1, 1, tn), lambda l, j: (l, 0, j)),
        ],
        out_specs=pl.BlockSpec((1, MOD_ROWS, tn), lambda l, j: (l, 0, j)),
        compiler_params=_params(("arbitrary", "arbitrary")),
        name="ada_mod",
    )(cvec, w_ada, b_ada.reshape(DEPTH, 1, n_out))


def _inproj_kernel(x_ref, nw_ref, mod_ref, wg_ref, wc_ref, wa_ref, g_ref, c_ref, a_ref):
    row = _mod_row(pl.program_id(0))
    sh = mod_ref[pl.ds(row, 1), 0:D_MODEL]
    sc = mod_ref[pl.ds(row, 1), D_MODEL:2 * D_MODEL]
    h = _bf(_rms(x_ref[...], nw_ref[...]) * (1.0 + sc) + sh)
    g_ref[...] = _dot(h, wg_ref[...])
    c_ref[...] = _dot(h, wc_ref[...])
    a_ref[...] = _dot(h, wa_ref[...])


def _in_proj(x, norm_w, mod, w_gla, w_conv, w_na):
    full = lambda i: (0, 0)
    tile = lambda i: (i, 0)
    return pl.pallas_call(
        _inproj_kernel,
        out_shape=(
            jax.ShapeDtypeStruct((N_TOK, GLA_IN_W), _F32),
            jax.ShapeDtypeStruct((N_TOK, 2 * CONV_CH), _F32),
            jax.ShapeDtypeStruct((N_TOK, 3 * NA_DIM), _F32),
        ),
        grid=(N_TOK // ROW_TILE,),
        in_specs=[
            pl.BlockSpec((ROW_TILE, D_MODEL), tile),
            pl.BlockSpec((1, D_MODEL), full),
            pl.BlockSpec((MOD_ROWS, 6 * D_MODEL), full),
            pl.BlockSpec((D_MODEL, GLA_IN_W), full),
            pl.BlockSpec((D_MODEL, 2 * CONV_CH), full),
            pl.BlockSpec((D_MODEL, 3 * NA_DIM), full),
        ],
        out_specs=(
            pl.BlockSpec((ROW_TILE, GLA_IN_W), tile),
            pl.BlockSpec((ROW_TILE, 2 * CONV_CH), tile),
            pl.BlockSpec((ROW_TILE, 3 * NA_DIM), tile),
        ),
        compiler_params=_params(("arbitrary",)),
        name="in_proj",
    )(x, norm_w, mod, w_gla, w_conv, w_na)


_GLA_LEVELS = (32, 16, 8, 4, 2, 1)
_N_EXP_BLOCKS = 2 + len(_GLA_LEVELS)
_N_MASKS = len(_GLA_LEVELS) + 1


def _gla_constants():
    cs = GLA_CHUNK
    r = np.arange(cs)
    i = r[:, None]
    c = r[None, :]
    w = np.zeros((2, _N_EXP_BLOCKS, cs, cs), np.float32)
    m = np.zeros((2, _N_MASKS, cs, cs), np.float32)
    w[0, 0] = c <= i
    w[0, 1] = c > i
    w[1, 0] = c >= i
    w[1, 1] = c < i
    for lv, half in enumerate(_GLA_LEVELS):
        mid = (r // (2 * half)) * (2 * half) + half
        mi = mid[:, None]
        second = (r >= mid)[:, None]
        same = (r[:, None] // (2 * half)) == (r[None, :] // (2 * half))
        w[0, 2 + lv] = np.where(second, (c >= mi) & (c <= i), (c > i) & (c <= mi - 1))
        w[1, 2 + lv] = np.where(second, (c >= mi) & (c <= i - 1), (c >= i) & (c <= mi - 1))
        m[0, lv] = same & (i >= mi) & (c < mi)
        m[1, lv] = same & (i < mi) & (c >= mi)
    m[:, _N_MASKS - 1] = np.eye(cs)
    w_all = w.reshape(2, _N_EXP_BLOCKS * cs, cs)
    lmask = np.tile(m, (1, 1, 1, GLA_HEADS))
    kr = np.arange(GLA_HEADS * cs)
    bdk = (kr[:, None] // cs) == (np.arange(GLA_KEY)[None, :] // GLA_DK)
    bdv = (kr[:, None] // cs) == (np.arange(GLA_VAL)[None, :] // GLA_DV)
    bdst = (np.arange(GLA_VAL)[:, None] // GLA_DV) == (np.arange(GLA_KEY)[None, :] // GLA_DK)
    return (jnp.asarray(w_all, _BF), jnp.asarray(lmask, _F32), jnp.asarray(bdk, _F32),
            jnp.asarray(bdv, _F32), jnp.asarray(bdst, _F32))


def _gla_kernel(g_ref, wz_ref, bz_ref, wall_ref, lmask_ref, bdk_ref, bdv_ref, bdst_ref, onorm_ref, st0_ref,
                o_ref, stfin_ref, of_scr, st_scr, *, n):
    cs = GLA_CHUNK
    nc = n // cs

    def chunk(c, d):
        rows = pl.ds(pl.multiple_of(c * cs, cs), cs)
        q = g_ref[rows, 0:GLA_KEY] * (GLA_DK ** -0.5)
        k = g_ref[rows, GLA_KEY:2 * GLA_KEY]
        v = g_ref[rows, 2 * GLA_KEY:2 * GLA_KEY + GLA_VAL]
        lr = g_ref[rows, GLA_LR_OFF:GLA_IN_W]
        z = _dot3(lr, wz_ref[:, d * GLA_KEY:(d + 1) * GLA_KEY]) + bz_ref[:, d * GLA_KEY:(d + 1) * GLA_KEY]
        la = (jnp.minimum(z, 0.0) - jnp.log1p(jnp.exp(-jnp.abs(z)))) * (1.0 / GLA_GATE_NORM)
        la_hi, la_mid, la_lo = _split3(la)
        w = wall_ref[d]
        f = jnp.exp((_dot(w, la_lo) + _dot(w, la_mid)) + _dot(w, la_hi))
        st = st_scr[...]
        o = _dot_nt(_bf(q * f[0:cs]), _bf(st))
        p = jnp.zeros((cs, GLA_HEADS * cs), _F32)
        for lv in range(_N_MASKS):
            if lv < len(_GLA_LEVELS):
                fl = f[(2 + lv) * cs:(3 + lv) * cs]
                ql, kl = q * fl, k * fl
            else:
                ql, kl = q, k
            k_bd = _bf(jnp.concatenate([kl] * GLA_HEADS, axis=0) * bdk_ref[...])
            p = p + lmask_ref[d, lv] * _dot_nt(_bf(ql), k_bd)
        v_bd = _bf(jnp.concatenate([v] * GLA_HEADS, axis=0) * bdv_ref[...])
        o = o + _dot(_bf(p), v_bd)
        decay = f[cs - 1:cs] if d == 0 else f[0:1]
        st_scr[...] = st * decay + _dot_tn(_bf(v), _bf(k * f[cs:2 * cs])) * bdst_ref[...]
        return rows, o

    st_scr[...] = st0_ref[0, 0]

    def fwd(c, carry):
        rows, o = chunk(c, 0)
        of_scr[rows, :] = o
        return carry

    lax.fori_loop(0, nc, fwd, 0)
    stfin_ref[0, 0] = st_scr[...]
    st_scr[...] = st0_ref[0, 1]

    def bwd(t, carry):
        rows, o = chunk(nc - 1 - t, 1)
        o = o + of_scr[rows, :]
        for h in range(GLA_HEADS):
            cols = slice(h * GLA_DV, (h + 1) * GLA_DV)
            gate = g_ref[rows, 2 * GLA_KEY + GLA_VAL + h * GLA_DV:2 * GLA_KEY + GLA_VAL + (h + 1) * GLA_DV]
            o_ref[rows, cols] = _rms(o[:, cols], onorm_ref[...]) * _silu(gate)
        return carry

    lax.fori_loop(0, nc, bwd, 0)
    stfin_ref[0, 1] = st_scr[...]


def _gla(g_all, wz, bz, consts, onorm, st0, *, n, n_seq, first_block):
    w_all, lmask, bdk, bdv, bdst = consts
    full2 = lambda i: (0, 0)
    return pl.pallas_call(
        functools.partial(_gla_kernel, n=n),
        out_shape=(
            jax.ShapeDtypeStruct((n_seq * n, GLA_VAL), _F32),
            jax.ShapeDtypeStruct((n_seq, 2, GLA_VAL, GLA_KEY), _F32),
        ),
        grid=(n_seq,),
        in_specs=[
            pl.BlockSpec((n, GLA_IN_W), lambda i: (first_block + i, 0)),
            pl.BlockSpec(wz.shape, full2),
            pl.BlockSpec(bz.shape, full2),
            pl.BlockSpec(w_all.shape, lambda i: (0, 0, 0)),
            pl.BlockSpec(lmask.shape, lambda i: (0, 0, 0, 0)),
            pl.BlockSpec(bdk.shape, full2),
            pl.BlockSpec(bdv.shape, full2),
            pl.BlockSpec(bdst.shape, full2),
            pl.BlockSpec((1, GLA_DV), full2),
            pl.BlockSpec((1, 2, GLA_VAL, GLA_KEY), lambda i: (i, 0, 0, 0)),
        ],
        out_specs=(
            pl.BlockSpec((n, GLA_VAL), lambda i: (i, 0)),
            pl.BlockSpec((1, 2, GLA_VAL, GLA_KEY), lambda i: (i, 0, 0, 0)),
        ),
        scratch_shapes=[pltpu.VMEM((n, GLA_VAL), _F32), pltpu.VMEM((GLA_VAL, GLA_KEY), _F32)],
        compiler_params=_params(("arbitrary",)),
        name=f"gla_{n}",
    )(g_all, wz, bz, w_all, lmask, bdk, bdv, bdst, onorm, st0)


def _state_to_blockdiag(s):
    b = s.shape[0]
    st = jnp.swapaxes(s, -1, -2)
    eye = jnp.eye(GLA_HEADS, dtype=s.dtype)
    out = st[:, :, :, :, None, :] * eye[None, None, :, None, :, None]
    return out.reshape(b, 2, GLA_VAL, GLA_KEY)


def _blockdiag_to_state(st):
    b = st.shape[0]
    s6 = st.reshape(b, 2, GLA_HEADS, GLA_DV, GLA_HEADS, GLA_DK)
    diag = jnp.stack([s6[:, :, h, :, h, :] for h in range(GLA_HEADS)], axis=2)
    return jnp.swapaxes(diag, -1, -2)


_CONV_PAD = 16
_CONV_ROWS = 128


def _conv_kernel(c_ref, dw_ref, dwb_ref, lng_ref, lnb_ref, pw_ref, o_ref, pad_scr, *, n):
    zeros = jnp.zeros((_CONV_PAD, CONV_CH), _F32)
    pad_scr[0:_CONV_PAD, :] = zeros
    pad_scr[_CONV_PAD + n:2 * _CONV_PAD + n, :] = zeros
    for r0 in range(0, n, _CONV_ROWS):
        a = c_ref[r0:r0 + _CONV_ROWS, :]
        pad_scr[_CONV_PAD + r0:_CONV_PAD + r0 + _CONV_ROWS, :] = a[:, :CONV_CH] * _sigmoid(a[:, CONV_CH:])
    half = CONV_WIDTH // 2
    for r0 in range(0, n, _CONV_ROWS):
        acc = jnp.zeros((_CONV_ROWS, CONV_CH), _F32)
        for w in range(CONV_WIDTH):
            s = _CONV_PAD + r0 + w - half
            acc = acc + pad_scr[s:s + _CONV_ROWS, :] * dw_ref[w:w + 1, :]
        acc = acc + dwb_ref[...]
        xc = acc - jnp.mean(acc, axis=-1, keepdims=True)
        y = xc * lax.rsqrt(jnp.mean(xc * xc, axis=-1, keepdims=True) + EPS) * lng_ref[...] + lnb_ref[...]
        o_ref[r0:r0 + _CONV_ROWS, :] = _dot(_bf(_silu(y)), pw_ref[...])


def _conv(c_all, dw, dwb, lng, lnb, pw, *, n, n_seq, first_block):
    full2 = lambda i: (0, 0)
    return pl.pallas_call(
        functools.partial(_conv_kernel, n=n),
        out_shape=jax.ShapeDtypeStruct((n_seq * n, CONV_CH), _F32),
        grid=(n_seq,),
        in_specs=[
            pl.BlockSpec((n, 2 * CONV_CH), lambda i: (first_block + i, 0)),
            pl.BlockSpec((CONV_WIDTH, CONV_CH), full2),
            pl.BlockSpec((1, CONV_CH), full2),
            pl.BlockSpec((1, CONV_CH), full2),
            pl.BlockSpec((1, CONV_CH), full2),
            pl.BlockSpec((CONV_CH, CONV_CH), full2),
        ],
        out_specs=pl.BlockSpec((n, CONV_CH), lambda i: (i, 0)),
        scratch_shapes=[pltpu.VMEM((n + 2 * _CONV_PAD, CONV_CH), _F32)],
        compiler_params=_params(("arbitrary",)),
        name=f"conv_{n}",
    )(c_all, dw, dwb, lng, lnb, pw)


def _softmax_rows(s):
    e = jnp.exp(s - jnp.max(s, axis=-1, keepdims=True))
    return e / jnp.sum(e, axis=-1, keepdims=True)


def _na_ctx_kernel(a_ref, qn_ref, kn_ref, o_ref, kc_ref, vc_ref):
    for h in range(NA_HEADS):
        cols = slice(h * NA_HD, (h + 1) * NA_HD)
        q = _rms(a_ref[:, h * NA_HD:(h + 1) * NA_HD], qn_ref[...])
        k = _rms(a_ref[:, NA_DIM + h * NA_HD:NA_DIM + (h + 1) * NA_HD], kn_ref[...])
        v = a_ref[:, 2 * NA_DIM + h * NA_HD:2 * NA_DIM + (h + 1) * NA_HD]
        kc_ref[0, h] = k
        vc_ref[0, h] = v
        p = _softmax_rows(_dot_nt(_bf(q), _bf(k)) * (NA_HD ** -0.5))
        o_ref[:, cols] = _dot(_bf(p), _bf(v))


def _na_ctx(a_all, qn, kn):
    full2 = lambda i: (0, 0)
    cache = jax.ShapeDtypeStruct((BATCH, NA_HEADS, SEQ, NA_HD), _F32)
    cache_spec = pl.BlockSpec((1, NA_HEADS, SEQ, NA_HD), lambda i: (i, 0, 0, 0))
    return pl.pallas_call(
        _na_ctx_kernel,
        out_shape=(jax.ShapeDtypeStruct((N_CTX_TOK, NA_DIM), _F32), cache, cache),
        grid=(BATCH,),
        in_specs=[
            pl.BlockSpec((SEQ, 3 * NA_DIM), lambda i: (i, 0)),
            pl.BlockSpec((1, NA_HD), full2),
            pl.BlockSpec((1, NA_HD), full2),
        ],
        out_specs=(pl.BlockSpec((SEQ, NA_DIM), lambda i: (i, 0)), cache_spec, cache_spec),
        compiler_params=_params(("arbitrary",)),
        name="na_ctx",
    )(a_all, qn, kn)


_NA_ROWS = DEC_SEQ // GRID_W
_NA_KEYS = NA_WIN_ROWS * GRID_W
_NA_VARIANTS = NA_WIN_ROWS


def _na_bias_table(rpb):
    t = np.arange(_NA_VARIANTS)
    kr = np.arange(NA_WIN_ROWS)
    qc = np.arange(GRID_W)
    kc = np.arange(GRID_W)
    row_off = kr[None, :] - t[:, None] + NA_WIN_ROWS - 1
    col_off = np.clip(kc[None, :] - qc[:, None] + NA_WIN_COLS - 1, 0, 2 * NA_WIN_COLS - 2)
    win_start = np.clip(qc - NA_WIN_COLS // 2, 0, GRID_W - NA_WIN_COLS)
    in_win = (kc[None, :] >= win_start[:, None]) & (kc[None, :] < win_start[:, None] + NA_WIN_COLS)
    bias = rpb[:, row_off[:, :, None, None], col_off[None, None, :, :]]
    bias = jnp.where(in_win[None, None, None], bias.astype(_F32), -jnp.inf)
    return bias.transpose(0, 1, 3, 2, 4).reshape(NA_HEADS, _NA_VARIANTS, GRID_W, _NA_KEYS)


def _na_lat_kernel(a_ref, kctx_ref, vctx_ref, bias_ref, qn_ref, kn_ref, o_ref, q_scr, k_scr, v_scr):
    scale = NA_HD ** -0.5
    for h in range(NA_HEADS):
        q_scr[...] = _bf(_rms(a_ref[:, h * NA_HD:(h + 1) * NA_HD], qn_ref[...]))
        k_scr[...] = _bf(_rms(a_ref[:, NA_DIM + h * NA_HD:NA_DIM + (h + 1) * NA_HD], kn_ref[...]))
        v_scr[...] = _bf(a_ref[:, 2 * NA_DIM + h * NA_HD:2 * NA_DIM + (h + 1) * NA_HD])
        k_ctx = _bf(kctx_ref[0, h])
        v_ctx = _bf(vctx_ref[0, h])

        def row(r, carry):
            start = jnp.clip(r - NA_WIN_ROWS // 2, 0, _NA_ROWS - NA_WIN_ROWS)
            qrows = pl.ds(pl.multiple_of(r * GRID_W, GRID_W), GRID_W)
            krows = pl.ds(pl.multiple_of(start * GRID_W, GRID_W), _NA_KEYS)
            q = q_scr[qrows, :]
            s_loc = _dot_nt(q, k_scr[krows, :]) * scale + bias_ref[h, r - start]
            s_ctx = _dot_nt(q, k_ctx) * scale
            m = jnp.maximum(jnp.max(s_loc, axis=-1, keepdims=True), jnp.max(s_ctx, axis=-1, keepdims=True))
            e_loc = jnp.exp(s_loc - m)
            e_ctx = jnp.exp(s_ctx - m)
            den = jnp.sum(e_loc, axis=-1, keepdims=True) + jnp.sum(e_ctx, axis=-1, keepdims=True)
            o = _dot(_bf(e_loc / den), v_scr[krows, :]) + _dot(_bf(e_ctx / den), v_ctx)
            o_ref[qrows, h * NA_HD:(h + 1) * NA_HD] = o
            return carry

        lax.fori_loop(0, _NA_ROWS, row, 0)


def _na_lat(a_all, k_ctx, v_ctx, bias, qn, kn):
    full2 = lambda i: (0, 0)
    ctx_spec = pl.BlockSpec((1, NA_HEADS, PAST_LEN, NA_HD), lambda i: (i, 0, 0, 0))
    return pl.pallas_call(
        _na_lat_kernel,
        out_shape=jax.ShapeDtypeStruct((N_LAT_TOK, NA_DIM), _F32),
        grid=(DEC_BATCH,),
        in_specs=[
            pl.BlockSpec((DEC_SEQ, 3 * NA_DIM), lambda i: (N_CTX_TOK // DEC_SEQ + i, 0)),
            ctx_spec,
            ctx_spec,
            pl.BlockSpec(bias.shape, lambda i: (0, 0, 0, 0)),
            pl.BlockSpec((1, NA_HD), full2),
            pl.BlockSpec((1, NA_HD), full2),
        ],
        out_specs=pl.BlockSpec((DEC_SEQ, NA_DIM), lambda i: (i, 0)),
        scratch_shapes=[pltpu.VMEM((DEC_SEQ, NA_HD), _BF)] * 3,
        compiler_params=_params(("arbitrary",)),
        name="na_lat",
    )(a_all, k_ctx, v_ctx, bias, qn, kn)


def _outproj_kernel(gc_ref, gl_ref, cc_ref, cl_ref, nc_ref, nl_ref, x_ref, mod_ref, nw_ref, wo_ref, rwt_ref,
                    x1_ref, h2_ref, lg_ref):
    i = pl.program_id(0)
    is_ctx = i < N_CTX_TILES
    row = _mod_row(i)
    gla = jnp.where(is_ctx, gc_ref[...], gl_ref[...])
    conv = jnp.where(is_ctx, cc_ref[...], cl_ref[...])
    na = jnp.where(is_ctx, nc_ref[...], nl_ref[...])
    mix = (_dot(_bf(gla), wo_ref[0:GLA_VAL, :])
           + _dot(_bf(conv), wo_ref[GLA_VAL:GLA_VAL + CONV_CH, :])
           + _dot(_bf(na), wo_ref[GLA_VAL + CONV_CH:, :]))
    g1 = mod_ref[pl.ds(row, 1), 2 * D_MODEL:3 * D_MODEL]
    sh2 = mod_ref[pl.ds(row, 1), 3 * D_MODEL:4 * D_MODEL]
    sc2 = mod_ref[pl.ds(row, 1), 4 * D_MODEL:5 * D_MODEL]
    x1 = x_ref[...] + g1 * mix
    h2 = _rms(x1, nw_ref[...]) * (1.0 + sc2) + sh2
    x1_ref[...] = x1
    h2_ref[...] = _bf(h2)
    lg_ref[...] = _dot3(rwt_ref[...], h2, dot=_dot_nt)


def _out_proj(gla_c, gla_l, conv_c, conv_l, na_c, na_l, x, mod, norm_w, w_out, router_wt):
    full2 = lambda i: (0, 0)
    tile = lambda i: (i, 0)
    last_ctx = N_CTX_TILES - 1
    ctx_tile = lambda i: (jnp.minimum(i, last_ctx), 0)
    lat_tile = lambda i: (jnp.maximum(i - N_CTX_TILES, 0), 0)
    return pl.pallas_call(
        _outproj_kernel,
        out_shape=(
            jax.ShapeDtypeStruct((N_TOK, D_MODEL), _F32),
            jax.ShapeDtypeStruct((N_TOK, D_MODEL), _BF),
            jax.ShapeDtypeStruct((N_EXPERTS, N_TOK), _F32),
        ),
        grid=(N_TOK // ROW_TILE,),
        in_specs=[
            pl.BlockSpec((ROW_TILE, GLA_VAL), ctx_tile),
            pl.BlockSpec((ROW_TILE, GLA_VAL), lat_tile),
            pl.BlockSpec((ROW_TILE, CONV_CH), ctx_tile),
            pl.BlockSpec((ROW_TILE, CONV_CH), lat_tile),
            pl.BlockSpec((ROW_TILE, NA_DIM), ctx_tile),
            pl.BlockSpec((ROW_TILE, NA_DIM), lat_tile),
            pl.BlockSpec((ROW_TILE, D_MODEL), tile),
            pl.BlockSpec((MOD_ROWS, 6 * D_MODEL), full2),
            pl.BlockSpec((1, D_MODEL), full2),
            pl.BlockSpec((D_MODEL, D_MODEL), full2),
            pl.BlockSpec((N_EXPERTS, D_MODEL), full2),
        ],
        out_specs=(
            pl.BlockSpec((ROW_TILE, D_MODEL), tile),
            pl.BlockSpec((ROW_TILE, D_MODEL), tile),
            pl.BlockSpec((N_EXPERTS, ROW_TILE), lambda i: (0, i)),
        ),
        compiler_params=_params(("arbitrary",)),
        name="out_proj",
    )(gla_c, gla_l, conv_c, conv_l, na_c, na_l, x, mod, norm_w, w_out, router_wt)


_PER_GROUP = N_EXPERTS // N_GROUPS
_ROUTE_TILE = 1024


def _first_max(x, idx, axes, sentinel):
    m = x
    for ax in axes:
        m = jnp.max(m, axis=ax, keepdims=True)
    first = jnp.where(x == m, idx, sentinel)
    for ax in axes:
        first = jnp.min(first, axis=ax, keepdims=True)
    return m, first


def _router_kernel(lg_ref, rb_ref, gate_ref):
    t = lg_ref.shape[1]
    shape3 = (N_GROUPS, _PER_GROUP, t)
    scores = _sigmoid(lg_ref[...])
    biased = (scores + rb_ref[...]).reshape(shape3)
    scores = scores.reshape(shape3)
    neg = -jnp.inf
    in_grp = lax.broadcasted_iota(jnp.int32, shape3, 1)
    grp = lax.broadcasted_iota(jnp.int32, (N_GROUPS, 1, t), 0)
    expert = lax.broadcasted_iota(jnp.int32, shape3, 0) * _PER_GROUP + in_grp
    m1, i1 = _first_max(biased, in_grp, (1,), _PER_GROUP)
    m2 = jnp.max(jnp.where(in_grp == i1, neg, biased), axis=1, keepdims=True)
    gscore = m1 + m2
    keep = jnp.zeros((N_GROUPS, 1, t), _F32)
    for _ in range(TOPK_GROUPS):
        _, gi = _first_max(gscore, grp, (0,), N_GROUPS)
        hit = grp == gi
        keep = jnp.where(hit, 1.0, keep)
        gscore = jnp.where(hit, neg, gscore)
    masked = jnp.where(keep > 0.0, biased, neg)
    picked = jnp.zeros(shape3, _F32)
    for _ in range(TOP_K):
        _, ei = _first_max(masked, expert, (1, 0), N_EXPERTS)
        hit = expert == ei
        picked = jnp.where(hit, scores, picked)
        masked = jnp.where(hit, neg, masked)
    den = jnp.sum(jnp.sum(picked, axis=1, keepdims=True), axis=0, keepdims=True)
    gate_ref[...] = (picked / den * ROUTED_SCALE).reshape(N_EXPERTS, t)


def _router(logits_t, router_bias):
    return pl.pallas_call(
        _router_kernel,
        out_shape=jax.ShapeDtypeStruct((N_EXPERTS, N_TOK), _F32),
        grid=(N_TOK // _ROUTE_TILE,),
        in_specs=[
            pl.BlockSpec((N_EXPERTS, _ROUTE_TILE), lambda i: (0, i)),
            pl.BlockSpec((N_EXPERTS, 1), lambda i: (0, 0)),
        ],
        out_specs=pl.BlockSpec((N_EXPERTS, _ROUTE_TILE), lambda i: (0, i)),
        compiler_params=_params(("arbitrary",)),
        name="router",
    )(logits_t, router_bias)


_MOE_TILE = 1024


def _moe_kernel(h_ref, gate_ref, x1_ref, mod_ref, wg_ref, wu_ref, wd_ref, sg_ref, su_ref, sd_ref, o_ref, acc_scr):
    t = pl.program_id(0)
    e = pl.program_id(1)

    @pl.when(e == 0)
    def _():
        acc_scr[...] = jnp.zeros_like(acc_scr)

    h = h_ref[...]
    act = _silu(_dot(h, _bf(wg_ref[0]))) * _dot(h, _bf(wu_ref[0]))
    lane = lax.broadcasted_iota(jnp.int32, gate_ref.shape, 1)
    gate = jnp.sum(jnp.where(lane == e, gate_ref[...], 0.0), axis=-1, keepdims=True)
    acc_scr[...] += _dot(_bf(act * gate), _bf(wd_ref[0]))

    @pl.when(e == N_EXPERTS - 1)
    def _():
        shared = _dot(_bf(_silu(_dot(h, _bf(sg_ref[...]))) * _dot(h, _bf(su_ref[...]))), _bf(sd_ref[...]))
        row = jnp.where(t < N_CTX_TOK // _MOE_TILE, 0, 1 + (t - N_CTX_TOK // _MOE_TILE) // (DEC_SEQ // _MOE_TILE))
        g2 = mod_ref[pl.ds(row, 1), 5 * D_MODEL:6 * D_MODEL]
        o_ref[...] = x1_ref[...] + g2 * (acc_scr[...] + shared)


def _moe(h2, gates, x1, mod, wg, wu, wd, sg, su, sd):
    tile = lambda t, e: (t, 0)
    full2 = lambda t, e: (0, 0)
    expert = lambda t, e: (e, 0, 0)
    return pl.pallas_call(
        _moe_kernel,
        out_shape=jax.ShapeDtypeStruct((N_TOK, D_MODEL), _F32),
        grid=(N_TOK // _MOE_TILE, N_EXPERTS),
        in_specs=[
            pl.BlockSpec((_MOE_TILE, D_MODEL), tile),
            pl.BlockSpec((_MOE_TILE, N_EXPERTS), tile),
            pl.BlockSpec((_MOE_TILE, D_MODEL), tile),
            pl.BlockSpec((MOD_ROWS, 6 * D_MODEL), full2),
            pl.BlockSpec((1, D_MODEL, EXPERT_DIM), expert),
            pl.BlockSpec((1, D_MODEL, EXPERT_DIM), expert),
            pl.BlockSpec((1, EXPERT_DIM, D_MODEL), expert),
            pl.BlockSpec((D_MODEL, SHARED_DIM), full2),
            pl.BlockSpec((D_MODEL, SHARED_DIM), full2),
            pl.BlockSpec((SHARED_DIM, D_MODEL), full2),
        ],
        out_specs=pl.BlockSpec((_MOE_TILE, D_MODEL), tile),
        scratch_shapes=[pltpu.VMEM((_MOE_TILE, D_MODEL), _F32)],
        compiler_params=_params(("arbitrary", "arbitrary")),
        name="moe",
    )(h2, gates, x1, mod, wg, wu, wd, sg, su, sd)


def kernel(x_prompt, x_sample, state_gla, cache_na_k, cache_na_v, c, c_ctx, w_ada, b_ada, norm_mix, norm_ffn, w_in, gla_w_gate, gla_b_gate, gla_out_norm, conv_dw, conv_dw_b, conv_ln_g, conv_ln_b, conv_pw, na_q_norm, na_k_norm, na_rpb, w_out, router_w, router_bias, exp_w_gate, exp_w_up, exp_w_down, sh_w_gate, sh_w_up, sh_w_down):
    x = jnp.concatenate([x_prompt.reshape(N_CTX_TOK, D_MODEL), x_sample.reshape(N_LAT_TOK, D_MODEL)], axis=0)
    cvec = jnp.concatenate([c_ctx[None], c, jnp.zeros((MOD_ROWS - 1 - DEC_BATCH, D_MODEL), _F32)], axis=0)
    mod_all = _ada_mod(cvec, w_ada, b_ada)
    gla_consts = _gla_constants()
    zero_state = jnp.zeros((BATCH, 2, GLA_VAL, GLA_KEY), _F32)
    lat_first = N_CTX_TOK // DEC_SEQ

    states, keys, vals = [], [], []
    for l in range(DEPTH):
        mod = mod_all[l]
        wi = w_in[l]
        w_gla = _bf(jnp.pad(wi[:, :GLA_LR_OFF + 2 * GLA_GATE_RANK], ((0, 0), (0, GLA_IN_W - GLA_LR_OFF - 2 * GLA_GATE_RANK))))
        conv_off = GLA_LR_OFF + 2 * GLA_GATE_RANK
        w_conv = _bf(wi[:, conv_off:conv_off + 2 * CONV_CH])
        w_na = _bf(wi[:, conv_off + 2 * CONV_CH:])
        g_all, c_all, a_all = _in_proj(x, norm_mix[l][None], mod, w_gla, w_conv, w_na)

        wz = jnp.zeros((GLA_IN_W - GLA_LR_OFF, 2 * GLA_KEY), _F32)
        wz = wz.at[:GLA_GATE_RANK, :GLA_KEY].set(gla_w_gate[l, 0])
        wz = wz.at[GLA_GATE_RANK:2 * GLA_GATE_RANK, GLA_KEY:].set(gla_w_gate[l, 1])
        bz = gla_b_gate[l].reshape(1, 2 * GLA_KEY)
        onorm = gla_out_norm[l][None]
        gla_c, st_c = _gla(g_all, wz, bz, gla_consts, onorm, zero_state, n=SEQ, n_seq=BATCH, first_block=0)
        gla_l, _ = _gla(g_all, wz, bz, gla_consts, onorm, _state_to_blockdiag(state_gla[:, l]),
                        n=DEC_SEQ, n_seq=DEC_BATCH, first_block=lat_first)

        conv_args = (conv_dw[l], conv_dw_b[l][None], conv_ln_g[l][None], conv_ln_b[l][None], _bf(conv_pw[l]))
        conv_c = _conv(c_all, *conv_args, n=SEQ, n_seq=BATCH, first_block=0)
        conv_l = _conv(c_all, *conv_args, n=DEC_SEQ, n_seq=DEC_BATCH, first_block=lat_first)

        qn, kn = na_q_norm[l][None], na_k_norm[l][None]
        na_c, k_l, v_l = _na_ctx(a_all, qn, kn)
        na_l = _na_lat(a_all, cache_na_k[:, l], cache_na_v[:, l], _na_bias_table(na_rpb[l]), qn, kn)

        x1, h2, logits_t = _out_proj(gla_c, gla_l, conv_c, conv_l, na_c, na_l, x, mod, norm_ffn[l][None],
                                     _bf(w_out[l]), router_w[l].T)
        gates = _router(logits_t, router_bias[l][:, None]).T
        x = _moe(h2, gates, x1, mod, exp_w_gate[l], exp_w_up[l], exp_w_down[l],
                 sh_w_gate[l], sh_w_up[l], sh_w_down[l])

        states.append(_blockdiag_to_state(st_c))
        keys.append(k_l)
        vals.append(v_l)

    y_prompt = x[:N_CTX_TOK].reshape(BATCH, SEQ, D_MODEL)
    y_sample = x[N_CTX_TOK:].reshape(DEC_BATCH, DEC_SEQ, D_MODEL)
    return (y_prompt, y_sample, jnp.stack(states, axis=1), jnp.stack(keys, axis=1), jnp.stack(vals, axis=1))
```

```python
import functools

import numpy as np
import jax
import jax.numpy as jnp
from jax import lax
from jax.experimental import pallas as pl
from jax.experimental.pallas import tpu as pltpu

D_MODEL = 1024
BATCH = 16
SEQ = 256
DEPTH = 2
DEC_BATCH = 4
DEC_SEQ = 1024
PAST_LEN = 256
GRID_W = 64
GLA_HEADS = 4
GLA_DK = 64
GLA_DV = 128
GLA_KEY = GLA_HEADS * GLA_DK
GLA_VAL = GLA_HEADS * GLA_DV
GLA_GATE_RANK = 16
GLA_GATE_NORM = 16.0
GLA_CHUNK = 64
CONV_CH = 256
CONV_WIDTH = 31
NA_HEADS = 4
NA_HD = 64
NA_DIM = NA_HEADS * NA_HD
NA_WIN_ROWS = 8
NA_WIN_COLS = 16
N_EXPERTS = 64
TOP_K = 8
N_GROUPS = 8
TOPK_GROUPS = 4
EXPERT_DIM = 256
SHARED_DIM = 256
ROUTED_SCALE = 2.5
EPS = 1e-6

N_CTX_TOK = BATCH * SEQ
N_LAT_TOK = DEC_BATCH * DEC_SEQ
N_TOK = N_CTX_TOK + N_LAT_TOK
ROW_TILE = 256
N_CTX_TILES = N_CTX_TOK // ROW_TILE
TILES_PER_LAT_SEQ = DEC_SEQ // ROW_TILE
MOD_ROWS = 8
GLA_IN_W = 1664
GLA_LR_OFF = 2 * GLA_KEY + 2 * GLA_VAL
VMEM_LIMIT = 56 * 1024 * 1024

_BF = jnp.bfloat16
_F32 = jnp.float32


def _bf(x):
    return x.astype(_BF)


def _dot(a, b):
    return jnp.dot(a, b, preferred_element_type=_F32)


def _dot_nt(a, b):
    return lax.dot_general(a, b, (((1,), (1,)), ((), ())), preferred_element_type=_F32)


def _dot_tn(a, b):
    return lax.dot_general(a, b, (((0,), (0,)), ((), ())), preferred_element_type=_F32)


def _split2(x):
    hi = _bf(x)
    lo = _bf(x - hi.astype(_F32))
    return hi, lo


def _split3(x):
    hi = _bf(x)
    r = x - hi.astype(_F32)
    mid = _bf(r)
    lo = _bf(r - mid.astype(_F32))
    return hi, mid, lo


def _dot3(a, b, dot=_dot):
    a_hi, a_lo = _split2(a)
    b_hi, b_lo = _split2(b)
    return (dot(a_lo, b_hi) + dot(a_hi, b_lo)) + dot(a_hi, b_hi)


def _sigmoid(x):
    return 1.0 / (1.0 + jnp.exp(-x))


def _silu(x):
    return x * _sigmoid(x)


def _rms(x, w):
    return x * lax.rsqrt(jnp.mean(x * x, axis=-1, keepdims=True) + EPS) * w


def _params(sem):
    return pltpu.CompilerParams(dimension_semantics=sem, vmem_limit_bytes=VMEM_LIMIT)


def _mod_row(i):
    return jnp.where(i < N_CTX_TILES, 0, 1 + (i - N_CTX_TILES) // TILES_PER_LAT_SEQ)


def _ada_kernel(cv_ref, w_ref, b_ref, o_ref):
    o_ref[0] = _dot3(_silu(cv_ref[...]), w_ref[0]) + b_ref[0]


def _ada_mod(cvec, w_ada, b_ada):
    tn = 1024
    n_out = 6 * D_MODEL
    return pl.pallas_call(
        _ada_kernel,
        out_shape=jax.ShapeDtypeStruct((DEPTH, MOD_ROWS, n_out), _F32),
        grid=(DEPTH, n_out // tn),
        in_specs=[
            pl.BlockSpec((MOD_ROWS, D_MODEL), lambda l, j: (0, 0)),
            pl.BlockSpec((1, D_MODEL, tn), lambda l, j: (l, 0, j)),
            pl.BlockSpec((1, 1, tn), lambda l, j: (l, 0, j)),
        ],
        out_specs=pl.BlockSpec((1, MOD_ROWS, tn), lambda l, j: (l, 0, j)),
        compiler_params=_params(("arbitrary", "arbitrary")),
        name="ada_mod",
    )(cvec, w_ada, b_ada.reshape(DEPTH, 1, n_out))


def _inproj_kernel(x_ref, nw_ref, mod_ref, wg_ref, wc_ref, wa_ref, g_ref, c_ref, a_ref):
    row = _mod_row(pl.program_id(0))
    sh = mod_ref[pl.ds(row, 1), 0:D_MODEL]
    sc = mod_ref[pl.ds(row, 1), D_MODEL:2 * D_MODEL]
    h = _bf(_rms(x_ref[...], nw_ref[...]) * (1.0 + sc) + sh)
    g_ref[...] = _dot(h, wg_ref[...])
    c_ref[...] = _dot(h, wc_ref[...])
    a_ref[...] = _dot(h, wa_ref[...])


def _in_proj(x, norm_w, mod, w_gla, w_conv, w_na):
    full = lambda i: (0, 0)
    tile = lambda i: (i, 0)
    return pl.pallas_call(
        _inproj_kernel,
        out_shape=(
            jax.ShapeDtypeStruct((N_TOK, GLA_IN_W), _F32),
            jax.ShapeDtypeStruct((N_TOK, 2 * CONV_CH), _F32),
            jax.ShapeDtypeStruct((N_TOK, 3 * NA_DIM), _F32),
        ),
        grid=(N_TOK // ROW_TILE,),
        in_specs=[
            pl.BlockSpec((ROW_TILE, D_MODEL), tile),
            pl.BlockSpec((1, D_MODEL), full),
            pl.BlockSpec((MOD_ROWS, 6 * D_MODEL), full),
            pl.BlockSpec((D_MODEL, GLA_IN_W), full),
            pl.BlockSpec((D_MODEL, 2 * CONV_CH), full),
            pl.BlockSpec((D_MODEL, 3 * NA_DIM), full),
        ],
        out_specs=(
            pl.BlockSpec((ROW_TILE, GLA_IN_W), tile),
            pl.BlockSpec((ROW_TILE, 2 * CONV_CH), tile),
            pl.BlockSpec((ROW_TILE, 3 * NA_DIM), tile),
        ),
        compiler_params=_params(("arbitrary",)),
        name="in_proj",
    )(x, norm_w, mod, w_gla, w_conv, w_na)


_GLA_LEVELS = (32, 16, 8, 4, 2, 1)
_N_EXP_BLOCKS = 2 + len(_GLA_LEVELS)
_N_MASKS = len(_GLA_LEVELS) + 1


def _gla_constants():
    cs = GLA_CHUNK
    r = np.arange(cs)
    i = r[:, None]
    c = r[None, :]
    w = np.zeros((2, _N_EXP_BLOCKS, cs, cs), np.float32)
    m = np.zeros((2, _N_MASKS, cs, cs), np.float32)
    w[0, 0] = c <= i
    w[0, 1] = c > i
    w[1, 0] = c >= i
    w[1, 1] = c < i
    for lv, half in enumerate(_GLA_LEVELS):
        mid = (r // (2 * half)) * (2 * half) + half
        mi = mid[:, None]
        second = (r >= mid)[:, None]
        same = (r[:, None] // (2 * half)) == (r[None, :] // (2 * half))
        w[0, 2 + lv] = np.where(second, (c >= mi) & (c <= i), (c > i) & (c <= mi - 1))
        w[1, 2 + lv] = np.where(second, (c >= mi) & (c <= i - 1), (c >= i) & (c <= mi - 1))
        m[0, lv] = same & (i >= mi) & (c < mi)
        m[1, lv] = same & (i < mi) & (c >= mi)
    m[:, _N_MASKS - 1] = np.eye(cs)
    w_all = w.reshape(2, _N_EXP_BLOCKS * cs, cs)
    lmask = np.tile(m, (1, 1, 1, GLA_HEADS))
    kr = np.arange(GLA_HEADS * cs)
    bdk = (kr[:, None] // cs) == (np.arange(GLA_KEY)[None, :] // GLA_DK)
    bdv = (kr[:, None] // cs) == (np.arange(GLA_VAL)[None, :] // GLA_DV)
    bdst = (np.arange(GLA_VAL)[:, None] // GLA_DV) == (np.arange(GLA_KEY)[None, :] // GLA_DK)
    return (jnp.asarray(w_all, _BF), jnp.asarray(lmask, _F32), jnp.asarray(bdk, _F32),
            jnp.asarray(bdv, _F32), jnp.asarray(bdst, _F32))


def _gla_kernel(g_ref, wz_ref, bz_ref, wall_ref, lmask_ref, bdk_ref, bdv_ref, bdst_ref, onorm_ref, st0_ref,
                o_ref, stfin_ref, of_scr, st_scr, *, n):
    cs = GLA_CHUNK
    nc = n // cs

    def chunk(c, d):
        rows = pl.ds(pl.multiple_of(c * cs, cs), cs)
        q = g_ref[rows, 0:GLA_KEY] * (GLA_DK ** -0.5)
        k = g_ref[rows, GLA_KEY:2 * GLA_KEY]
        v = g_ref[rows, 2 * GLA_KEY:2 * GLA_KEY + GLA_VAL]
        lr = g_ref[rows, GLA_LR_OFF:GLA_IN_W]
        z = _dot3(lr, wz_ref[:, d * GLA_KEY:(d + 1) * GLA_KEY]) + bz_ref[:, d * GLA_KEY:(d + 1) * GLA_KEY]
        la = (jnp.minimum(z, 0.0) - jnp.log1p(jnp.exp(-jnp.abs(z)))) * (1.0 / GLA_GATE_NORM)
        la_hi, la_mid, la_lo = _split3(la)
        w = wall_ref[d]
        f = jnp.exp((_dot(w, la_lo) + _dot(w, la_mid)) + _dot(w, la_hi))
        st = st_scr[...]
        o = _dot_nt(_bf(q * f[0:cs]), _bf(st))
        p = jnp.zeros((cs, GLA_HEADS * cs), _F32)
        for lv in range(_N_MASKS):
            if lv < len(_GLA_LEVELS):
                fl = f[(2 + lv) * cs:(3 + lv) * cs]
                ql, kl = q * fl, k * fl
            else:
                ql, kl = q, k
            k_bd = _bf(jnp.concatenate([kl] * GLA_HEADS, axis=0) * bdk_ref[...])
            p = p + lmask_ref[d, lv] * _dot_nt(_bf(ql), k_bd)
        v_bd = _bf(jnp.concatenate([v] * GLA_HEADS, axis=0) * bdv_ref[...])
        o = o + _dot(_bf(p), v_bd)
        decay = f[cs - 1:cs] if d == 0 else f[0:1]
        st_scr[...] = st * decay + _dot_tn(_bf(v), _bf(k * f[cs:2 * cs])) * bdst_ref[...]
        return rows, o

    st_scr[...] = st0_ref[0, 0]

    def fwd(c, carry):
        rows, o = chunk(c, 0)
        of_scr[rows, :] = o
        return carry

    lax.fori_loop(0, nc, fwd, 0)
    stfin_ref[0, 0] = st_scr[...]
    st_scr[...] = st0_ref[0, 1]

    def bwd(t, carry):
        rows, o = chunk(nc - 1 - t, 1)
        o = o + of_scr[rows, :]
        for h in range(GLA_HEADS):
            cols = slice(h * GLA_DV, (h + 1) * GLA_DV)
            gate = g_ref[rows, 2 * GLA_KEY + GLA_VAL + h * GLA_DV:2 * GLA_KEY + GLA_VAL + (h + 1) * GLA_DV]
            o_ref[rows, cols] = _rms(o[:, cols], onorm_ref[...]) * _silu(gate)
        return carry

    lax.fori_loop(0, nc, bwd, 0)
    stfin_ref[0, 1] = st_scr[...]


def _gla(g_all, wz, bz, consts, onorm, st0, *, n, n_seq, first_block):
    w_all, lmask, bdk, bdv, bdst = consts
    full2 = lambda i: (0, 0)
    return pl.pallas_call(
        functools.partial(_gla_kernel, n=n),
        out_shape=(
            jax.ShapeDtypeStruct((n_seq * n, GLA_VAL), _F32),
            jax.ShapeDtypeStruct((n_seq, 2, GLA_VAL, GLA_KEY), _F32),
        ),
        grid=(n_seq,),
        in_specs=[
            pl.BlockSpec((n, GLA_IN_W), lambda i: (first_block + i, 0)),
            pl.BlockSpec(wz.shape, full2),
            pl.BlockSpec(bz.shape, full2),
            pl.BlockSpec(w_all.shape, lambda i: (0, 0, 0)),
            pl.BlockSpec(lmask.shape, lambda i: (0, 0, 0, 0)),
            pl.BlockSpec(bdk.shape, full2),
            pl.BlockSpec(bdv.shape, full2),
            pl.BlockSpec(bdst.shape, full2),
            pl.BlockSpec((1, GLA_DV), full2),
            pl.BlockSpec((1, 2, GLA_VAL, GLA_KEY), lambda i: (i, 0, 0, 0)),
        ],
        out_specs=(
            pl.BlockSpec((n, GLA_VAL), lambda i: (i, 0)),
            pl.BlockSpec((1, 2, GLA_VAL, GLA_KEY), lambda i: (i, 0, 0, 0)),
        ),
        scratch_shapes=[pltpu.VMEM((n, GLA_VAL), _F32), pltpu.VMEM((GLA_VAL, GLA_KEY), _F32)],
        compiler_params=_params(("arbitrary",)),
        name=f"gla_{n}",
    )(g_all, wz, bz, w_all, lmask, bdk, bdv, bdst, onorm, st0)


def _state_to_blockdiag(s):
    b = s.shape[0]
    st = jnp.swapaxes(s, -1, -2)
    eye = jnp.eye(GLA_HEADS, dtype=s.dtype)
    out = st[:, :, :, :, None, :] * eye[None, None, :, None, :, None]
    return out.reshape(b, 2, GLA_VAL, GLA_KEY)


def _blockdiag_to_state(st):
    b = st.shape[0]
    s6 = st.reshape(b, 2, GLA_HEADS, GLA_DV, GLA_HEADS, GLA_DK)
    diag = jnp.stack([s6[:, :, h, :, h, :] for h in range(GLA_HEADS)], axis=2)
    return jnp.swapaxes(diag, -1, -2)


_CONV_PAD = 16
_CONV_ROWS = 128


def _conv_kernel(c_ref, dw_ref, dwb_ref, lng_ref, lnb_ref, pw_ref, o_ref, pad_scr, *, n):
    zeros = jnp.zeros((_CONV_PAD, CONV_CH), _F32)
    pad_scr[0:_CONV_PAD, :] = zeros
    pad_scr[_CONV_PAD + n:2 * _CONV_PAD + n, :] = zeros
    for r0 in range(0, n, _CONV_ROWS):
        a = c_ref[r0:r0 + _CONV_ROWS, :]
        pad_scr[_CONV_PAD + r0:_CONV_PAD + r0 + _CONV_ROWS, :] = a[:, :CONV_CH] * _sigmoid(a[:, CONV_CH:])
    half = CONV_WIDTH // 2
    for r0 in range(0, n, _CONV_ROWS):
        acc = jnp.zeros((_CONV_ROWS, CONV_CH), _F32)
        for w in range(CONV_WIDTH):
            s = _CONV_PAD + r0 + w - half
            acc = acc + pad_scr[s:s + _CONV_ROWS, :] * dw_ref[w:w + 1, :]
        acc = acc + dwb_ref[...]
        xc = acc - jnp.mean(acc, axis=-1, keepdims=True)
        y = xc * lax.rsqrt(jnp.mean(xc * xc, axis=-1, keepdims=True) + EPS) * lng_ref[...] + lnb_ref[...]
        o_ref[r0:r0 + _CONV_ROWS, :] = _dot(_bf(_silu(y)), pw_ref[...])


def _conv(c_all, dw, dwb, lng, lnb, pw, *, n, n_seq, first_block):
    full2 = lambda i: (0, 0)
    return pl.pallas_call(
        functools.partial(_conv_kernel, n=n),
        out_shape=jax.ShapeDtypeStruct((n_seq * n, CONV_CH), _F32),
        grid=(n_seq,),
        in_specs=[
            pl.BlockSpec((n, 2 * CONV_CH), lambda i: (first_block + i, 0)),
            pl.BlockSpec((CONV_WIDTH, CONV_CH), full2),
            pl.BlockSpec((1, CONV_CH), full2),
            pl.BlockSpec((1, CONV_CH), full2),
            pl.BlockSpec((1, CONV_CH), full2),
            pl.BlockSpec((CONV_CH, CONV_CH), full2),
        ],
        out_specs=pl.BlockSpec((n, CONV_CH), lambda i: (i, 0)),
        scratch_shapes=[pltpu.VMEM((n + 2 * _CONV_PAD, CONV_CH), _F32)],
        compiler_params=_params(("arbitrary",)),
        name=f"conv_{n}",
    )(c_all, dw, dwb, lng, lnb, pw)


def _softmax_rows(s):
    e = jnp.exp(s - jnp.max(s, axis=-1, keepdims=True))
    return e / jnp.sum(e, axis=-1, keepdims=True)


def _na_ctx_kernel(a_ref, qn_ref, kn_ref, o_ref, kc_ref, vc_ref):
    for h in range(NA_HEADS):
        cols = slice(h * NA_HD, (h + 1) * NA_HD)
        q = _rms(a_ref[:, h * NA_HD:(h + 1) * NA_HD], qn_ref[...])
        k = _rms(a_ref[:, NA_DIM + h * NA_HD:NA_DIM + (h + 1) * NA_HD], kn_ref[...])
        v = a_ref[:, 2 * NA_DIM + h * NA_HD:2 * NA_DIM + (h + 1) * NA_HD]
        kc_ref[0, h] = k
        vc_ref[0, h] = v
        p = _softmax_rows(_dot_nt(_bf(q), _bf(k)) * (NA_HD ** -0.5))
        o_ref[:, cols] = _dot(_bf(p), _bf(v))


def _na_ctx(a_all, qn, kn):
    full2 = lambda i: (0, 0)
    cache = jax.ShapeDtypeStruct((BATCH, NA_HEADS, SEQ, NA_HD), _F32)
    cache_spec = pl.BlockSpec((1, NA_HEADS, SEQ, NA_HD), lambda i: (i, 0, 0, 0))
    return pl.pallas_call(
        _na_ctx_kernel,
        out_shape=(jax.ShapeDtypeStruct((N_CTX_TOK, NA_DIM), _F32), cache, cache),
        grid=(BATCH,),
        in_specs=[
            pl.BlockSpec((SEQ, 3 * NA_DIM), lambda i: (i, 0)),
            pl.BlockSpec((1, NA_HD), full2),
            pl.BlockSpec((1, NA_HD), full2),
        ],
        out_specs=(pl.BlockSpec((SEQ, NA_DIM), lambda i: (i, 0)), cache_spec, cache_spec),
        compiler_params=_params(("arbitrary",)),
        name="na_ctx",
    )(a_all, qn, kn)


_NA_ROWS = DEC_SEQ // GRID_W
_NA_KEYS = NA_WIN_ROWS * GRID_W
_NA_VARIANTS = NA_WIN_ROWS


_RPB_ROWS = 2 * NA_WIN_ROWS - 1
_RPB_COLS = 2 * NA_WIN_COLS - 1


def _na_bias_constants():
    qc = np.arange(GRID_W)[:, None]
    kc = np.arange(GRID_W)[None, :]
    shift = np.stack([(kc - qc + NA_WIN_COLS - 1) == co for co in range(_RPB_COLS)]).astype(np.float32)
    win_start = np.clip(qc - NA_WIN_COLS // 2, 0, GRID_W - NA_WIN_COLS)
    in_win = ((kc >= win_start) & (kc < win_start + NA_WIN_COLS)).astype(np.float32)
    return jnp.asarray(shift), jnp.asarray(in_win)


def _na_bias_kernel(rpb_ref, shift_ref, win_ref, o_ref):
    h = pl.program_id(0)
    in_win = win_ref[...] > 0.0
    tiles = []
    for ro in range(_RPB_ROWS):
        base = (h * _RPB_ROWS + ro) * _RPB_COLS
        acc = rpb_ref[base] * shift_ref[0]
        for co in range(1, _RPB_COLS):
            acc = acc + rpb_ref[base + co] * shift_ref[co]
        tiles.append(jnp.where(in_win, acc, -jnp.inf))
    for t in range(_NA_VARIANTS):
        o_ref[0, t] = jnp.concatenate([tiles[kr - t + NA_WIN_ROWS - 1] for kr in range(NA_WIN_ROWS)], axis=1)


def _na_bias_table(rpb):
    shift, in_win = _na_bias_constants()
    return pl.pallas_call(
        _na_bias_kernel,
        out_shape=jax.ShapeDtypeStruct((NA_HEADS, _NA_VARIANTS, GRID_W, _NA_KEYS), _F32),
        grid=(NA_HEADS,),
        in_specs=[
            pl.BlockSpec(memory_space=pltpu.SMEM),
            pl.BlockSpec(shift.shape, lambda h: (0, 0, 0)),
            pl.BlockSpec(in_win.shape, lambda h: (0, 0)),
        ],
        out_specs=pl.BlockSpec((1, _NA_VARIANTS, GRID_W, _NA_KEYS), lambda h: (h, 0, 0, 0)),
        compiler_params=_params(("arbitrary",)),
        name="na_bias",
    )(rpb.reshape(-1), shift, in_win)


def _na_lat_kernel(a_ref, kctx_ref, vctx_ref, bias_ref, qn_ref, kn_ref, o_ref, q_scr, k_scr, v_scr):
    scale = NA_HD ** -0.5
    for h in range(NA_HEADS):
        q_scr[...] = _bf(_rms(a_ref[:, h * NA_HD:(h + 1) * NA_HD], qn_ref[...]))
        k_scr[...] = _bf(_rms(a_ref[:, NA_DIM + h * NA_HD:NA_DIM + (h + 1) * NA_HD], kn_ref[...]))
        v_scr[...] = _bf(a_ref[:, 2 * NA_DIM + h * NA_HD:2 * NA_DIM + (h + 1) * NA_HD])
        k_ctx = _bf(kctx_ref[0, h])
        v_ctx = _bf(vctx_ref[0, h])

        def row(r, carry):
            start = jnp.clip(r - NA_WIN_ROWS // 2, 0, _NA_ROWS - NA_WIN_ROWS)
            qrows = pl.ds(pl.multiple_of(r * GRID_W, GRID_W), GRID_W)
            krows = pl.ds(pl.multiple_of(start * GRID_W, GRID_W), _NA_KEYS)
            q = q_scr[qrows, :]
            s_loc = _dot_nt(q, k_scr[krows, :]) * scale + bias_ref[h, r - start]
            s_ctx = _dot_nt(q, k_ctx) * scale
            m = jnp.maximum(jnp.max(s_loc, axis=-1, keepdims=True), jnp.max(s_ctx, axis=-1, keepdims=True))
            e_loc = jnp.exp(s_loc - m)
            e_ctx = jnp.exp(s_ctx - m)
            den = jnp.sum(e_loc, axis=-1, keepdims=True) + jnp.sum(e_ctx, axis=-1, keepdims=True)
            o = _dot(_bf(e_loc / den), v_scr[krows, :]) + _dot(_bf(e_ctx / den), v_ctx)
            o_ref[qrows, h * NA_HD:(h + 1) * NA_HD] = o
            return carry

        lax.fori_loop(0, _NA_ROWS, row, 0)


def _na_lat(a_all, k_ctx, v_ctx, bias, qn, kn):
    full2 = lambda i: (0, 0)
    ctx_spec = pl.BlockSpec((1, NA_HEADS, PAST_LEN, NA_HD), lambda i: (i, 0, 0, 0))
    return pl.pallas_call(
        _na_lat_kernel,
        out_shape=jax.ShapeDtypeStruct((N_LAT_TOK, NA_DIM), _F32),
        grid=(DEC_BATCH,),
        in_specs=[
            pl.BlockSpec((DEC_SEQ, 3 * NA_DIM), lambda i: (N_CTX_TOK // DEC_SEQ + i, 0)),
            ctx_spec,
            ctx_spec,
            pl.BlockSpec(bias.shape, lambda i: (0, 0, 0, 0)),
            pl.BlockSpec((1, NA_HD), full2),
            pl.BlockSpec((1, NA_HD), full2),
        ],
        out_specs=pl.BlockSpec((DEC_SEQ, NA_DIM), lambda i: (i, 0)),
        scratch_shapes=[pltpu.VMEM((DEC_SEQ, NA_HD), _BF)] * 3,
        compiler_params=_params(("arbitrary",)),
        name="na_lat",
    )(a_all, k_ctx, v_ctx, bias, qn, kn)


def _outproj_kernel(gc_ref, gl_ref, cc_ref, cl_ref, nc_ref, nl_ref, x_ref, mod_ref, nw_ref, wo_ref, rwt_ref,
                    x1_ref, h2_ref, lg_ref):
    i = pl.program_id(0)
    is_ctx = i < N_CTX_TILES
    row = _mod_row(i)
    gla = jnp.where(is_ctx, gc_ref[...], gl_ref[...])
    conv = jnp.where(is_ctx, cc_ref[...], cl_ref[...])
    na = jnp.where(is_ctx, nc_ref[...], nl_ref[...])
    mix = (_dot(_bf(gla), wo_ref[0:GLA_VAL, :])
           + _dot(_bf(conv), wo_ref[GLA_VAL:GLA_VAL + CONV_CH, :])
           + _dot(_bf(na), wo_ref[GLA_VAL + CONV_CH:, :]))
    g1 = mod_ref[pl.ds(row, 1), 2 * D_MODEL:3 * D_MODEL]
    sh2 = mod_ref[pl.ds(row, 1), 3 * D_MODEL:4 * D_MODEL]
    sc2 = mod_ref[pl.ds(row, 1), 4 * D_MODEL:5 * D_MODEL]
    x1 = x_ref[...] + g1 * mix
    h2 = _rms(x1, nw_ref[...]) * (1.0 + sc2) + sh2
    x1_ref[...] = x1
    h2_ref[...] = _bf(h2)
    lg_ref[...] = _dot3(rwt_ref[...], h2, dot=_dot_nt)


def _out_proj(gla_c, gla_l, conv_c, conv_l, na_c, na_l, x, mod, norm_w, w_out, router_wt):
    full2 = lambda i: (0, 0)
    tile = lambda i: (i, 0)
    last_ctx = N_CTX_TILES - 1
    ctx_tile = lambda i: (jnp.minimum(i, last_ctx), 0)
    lat_tile = lambda i: (jnp.maximum(i - N_CTX_TILES, 0), 0)
    return pl.pallas_call(
        _outproj_kernel,
        out_shape=(
            jax.ShapeDtypeStruct((N_TOK, D_MODEL), _F32),
            jax.ShapeDtypeStruct((N_TOK, D_MODEL), _BF),
            jax.ShapeDtypeStruct((N_EXPERTS, N_TOK), _F32),
        ),
        grid=(N_TOK // ROW_TILE,),
        in_specs=[
            pl.BlockSpec((ROW_TILE, GLA_VAL), ctx_tile),
            pl.BlockSpec((ROW_TILE, GLA_VAL), lat_tile),
            pl.BlockSpec((ROW_TILE, CONV_CH), ctx_tile),
            pl.BlockSpec((ROW_TILE, CONV_CH), lat_tile),
            pl.BlockSpec((ROW_TILE, NA_DIM), ctx_tile),
            pl.BlockSpec((ROW_TILE, NA_DIM), lat_tile),
            pl.BlockSpec((ROW_TILE, D_MODEL), tile),
            pl.BlockSpec((MOD_ROWS, 6 * D_MODEL), full2),
            pl.BlockSpec((1, D_MODEL), full2),
            pl.BlockSpec((D_MODEL, D_MODEL), full2),
            pl.BlockSpec((N_EXPERTS, D_MODEL), full2),
        ],
        out_specs=(
            pl.BlockSpec((ROW_TILE, D_MODEL), tile),
            pl.BlockSpec((ROW_TILE, D_MODEL), tile),
            pl.BlockSpec((N_EXPERTS, ROW_TILE), lambda i: (0, i)),
        ),
        compiler_params=_params(("arbitrary",)),
        name="out_proj",
    )(gla_c, gla_l, conv_c, conv_l, na_c, na_l, x, mod, norm_w, w_out, router_wt)


_PER_GROUP = N_EXPERTS // N_GROUPS
_ROUTE_TILE = 1024


def _first_max(x, idx, axes, sentinel):
    m = x
    for ax in axes:
        m = jnp.max(m, axis=ax, keepdims=True)
    first = jnp.where(x == m, idx, sentinel)
    for ax in axes:
        first = jnp.min(first, axis=ax, keepdims=True)
    return m, first


def _router_kernel(lg_ref, rb_ref, gate_ref):
    t = lg_ref.shape[1]
    shape3 = (N_GROUPS, _PER_GROUP, t)
    scores = _sigmoid(lg_ref[...])
    biased = (scores + rb_ref[...]).reshape(shape3)
    scores = scores.reshape(shape3)
    neg = -jnp.inf
    in_grp = lax.broadcasted_iota(jnp.int32, shape3, 1)
    grp = lax.broadcasted_iota(jnp.int32, (N_GROUPS, 1, t), 0)
    expert = lax.broadcasted_iota(jnp.int32, shape3, 0) * _PER_GROUP + in_grp
    m1, i1 = _first_max(biased, in_grp, (1,), _PER_GROUP)
    m2 = jnp.max(jnp.where(in_grp == i1, neg, biased), axis=1, keepdims=True)
    gscore = m1 + m2
    keep = jnp.zeros((N_GROUPS, 1, t), _F32)
    for _ in range(TOPK_GROUPS):
        _, gi = _first_max(gscore, grp, (0,), N_GROUPS)
        hit = grp == gi
        keep = jnp.where(hit, 1.0, keep)
        gscore = jnp.where(hit, neg, gscore)
    masked = jnp.where(keep > 0.0, biased, neg)
    picked = jnp.zeros(shape3, _F32)
    for _ in range(TOP_K):
        _, ei = _first_max(masked, expert, (1, 0), N_EXPERTS)
        hit = expert == ei
        picked = jnp.where(hit, scores, picked)
        masked = jnp.where(hit, neg, masked)
    den = jnp.sum(jnp.sum(picked, axis=1, keepdims=True), axis=0, keepdims=True)
    gate_ref[...] = (picked / den * ROUTED_SCALE).reshape(N_EXPERTS, t)


def _router(logits_t, router_bias):
    return pl.pallas_call(
        _router_kernel,
        out_shape=jax.ShapeDtypeStruct((N_EXPERTS, N_TOK), _F32),
        grid=(N_TOK // _ROUTE_TILE,),
        in_specs=[
            pl.BlockSpec((N_EXPERTS, _ROUTE_TILE), lambda i: (0, i)),
            pl.BlockSpec((N_EXPERTS, 1), lambda i: (0, 0)),
        ],
        out_specs=pl.BlockSpec((N_EXPERTS, _ROUTE_TILE), lambda i: (0, i)),
        compiler_params=_params(("arbitrary",)),
        name="router",
    )(logits_t, router_bias)


_MOE_TILE = 1024


def _moe_kernel(h_ref, gate_ref, x1_ref, mod_ref, wg_ref, wu_ref, wd_ref, sg_ref, su_ref, sd_ref, o_ref, acc_scr):
    t = pl.program_id(0)
    e = pl.program_id(1)

    @pl.when(e == 0)
    def _():
        acc_scr[...] = jnp.zeros_like(acc_scr)

    h = h_ref[...]
    act = _silu(_dot(h, _bf(wg_ref[0, 0]))) * _dot(h, _bf(wu_ref[0, 0]))
    lane = lax.broadcasted_iota(jnp.int32, gate_ref.shape, 1)
    gate = jnp.sum(jnp.where(lane == e, gate_ref[...], 0.0), axis=-1, keepdims=True)
    acc_scr[...] += _dot(_bf(act * gate), _bf(wd_ref[0, 0]))

    @pl.when(e == N_EXPERTS - 1)
    def _():
        shared = _dot(_bf(_silu(_dot(h, _bf(sg_ref[0]))) * _dot(h, _bf(su_ref[0]))), _bf(sd_ref[0]))
        row = jnp.where(t < N_CTX_TOK // _MOE_TILE, 0, 1 + (t - N_CTX_TOK // _MOE_TILE) // (DEC_SEQ // _MOE_TILE))
        g2 = mod_ref[pl.ds(row, 1), 5 * D_MODEL:6 * D_MODEL]
        o_ref[...] = x1_ref[...] + g2 * (acc_scr[...] + shared)


def _moe(layer, h2, gates, x1, mod, wg, wu, wd, sg, su, sd):
    tile = lambda t, e: (t, 0)
    full2 = lambda t, e: (0, 0)
    expert = lambda t, e: (layer, e, 0, 0)
    shared = lambda t, e: (layer, 0, 0)
    return pl.pallas_call(
        _moe_kernel,
        out_shape=jax.ShapeDtypeStruct((N_TOK, D_MODEL), _F32),
        grid=(N_TOK // _MOE_TILE, N_EXPERTS),
        in_specs=[
            pl.BlockSpec((_MOE_TILE, D_MODEL), tile),
            pl.BlockSpec((_MOE_TILE, N_EXPERTS), tile),
            pl.BlockSpec((_MOE_TILE, D_MODEL), tile),
            pl.BlockSpec((MOD_ROWS, 6 * D_MODEL), full2),
            pl.BlockSpec((1, 1, D_MODEL, EXPERT_DIM), expert),
            pl.BlockSpec((1, 1, D_MODEL, EXPERT_DIM), expert),
            pl.BlockSpec((1, 1, EXPERT_DIM, D_MODEL), expert),
            pl.BlockSpec((1, D_MODEL, SHARED_DIM), shared),
            pl.BlockSpec((1, D_MODEL, SHARED_DIM), shared),
            pl.BlockSpec((1, SHARED_DIM, D_MODEL), shared),
        ],
        out_specs=pl.BlockSpec((_MOE_TILE, D_MODEL), tile),
        scratch_shapes=[pltpu.VMEM((_MOE_TILE, D_MODEL), _F32)],
        compiler_params=_params(("arbitrary", "arbitrary")),
        name="moe",
    )(h2, gates, x1, mod, wg, wu, wd, sg, su, sd)


def kernel(x_prompt, x_sample, state_gla, cache_na_k, cache_na_v, c, c_ctx, w_ada, b_ada, norm_mix, norm_ffn, w_in, gla_w_gate, gla_b_gate, gla_out_norm, conv_dw, conv_dw_b, conv_ln_g, conv_ln_b, conv_pw, na_q_norm, na_k_norm, na_rpb, w_out, router_w, router_bias, exp_w_gate, exp_w_up, exp_w_down, sh_w_gate, sh_w_up, sh_w_down):
    x = jnp.concatenate([x_prompt.reshape(N_CTX_TOK, D_MODEL), x_sample.reshape(N_LAT_TOK, D_MODEL)], axis=0)
    cvec = jnp.concatenate([c_ctx[None], c, jnp.zeros((MOD_ROWS - 1 - DEC_BATCH, D_MODEL), _F32)], axis=0)
    mod_all = _ada_mod(cvec, w_ada, b_ada)
    gla_consts = _gla_constants()
    zero_state = jnp.zeros((BATCH, 2, GLA_VAL, GLA_KEY), _F32)
    lat_first = N_CTX_TOK // DEC_SEQ

    states, keys, vals = [], [], []
    for l in range(DEPTH):
        mod = mod_all[l]
        wi = w_in[l]
        w_gla = _bf(jnp.pad(wi[:, :GLA_LR_OFF + 2 * GLA_GATE_RANK], ((0, 0), (0, GLA_IN_W - GLA_LR_OFF - 2 * GLA_GATE_RANK))))
        conv_off = GLA_LR_OFF + 2 * GLA_GATE_RANK
        w_conv = _bf(wi[:, conv_off:conv_off + 2 * CONV_CH])
        w_na = _bf(wi[:, conv_off + 2 * CONV_CH:])
        g_all, c_all, a_all = _in_proj(x, norm_mix[l][None], mod, w_gla, w_conv, w_na)

        wz = jnp.zeros((GLA_IN_W - GLA_LR_OFF, 2 * GLA_KEY), _F32)
        wz = wz.at[:GLA_GATE_RANK, :GLA_KEY].set(gla_w_gate[l, 0])
        wz = wz.at[GLA_GATE_RANK:2 * GLA_GATE_RANK, GLA_KEY:].set(gla_w_gate[l, 1])
        bz = gla_b_gate[l].reshape(1, 2 * GLA_KEY)
        onorm = gla_out_norm[l][None]
        gla_c, st_c = _gla(g_all, wz, bz, gla_consts, onorm, zero_state, n=SEQ, n_seq=BATCH, first_block=0)
        gla_l, _ = _gla(g_all, wz, bz, gla_consts, onorm, _state_to_blockdiag(state_gla[:, l]),
                        n=DEC_SEQ, n_seq=DEC_BATCH, first_block=lat_first)

        conv_args = (conv_dw[l], conv_dw_b[l][None], conv_ln_g[l][None], conv_ln_b[l][None], _bf(conv_pw[l]))
        conv_c = _conv(c_all, *conv_args, n=SEQ, n_seq=BATCH, first_block=0)
        conv_l = _conv(c_all, *conv_args, n=DEC_SEQ, n_seq=DEC_BATCH, first_block=lat_first)

        qn, kn = na_q_norm[l][None], na_k_norm[l][None]
        na_c, k_l, v_l = _na_ctx(a_all, qn, kn)
        na_l = _na_lat(a_all, cache_na_k[:, l], cache_na_v[:, l], _na_bias_table(na_rpb[l]), qn, kn)

        x1, h2, logits_t = _out_proj(gla_c, gla_l, conv_c, conv_l, na_c, na_l, x, mod, norm_ffn[l][None],
                                     _bf(w_out[l]), router_w[l].T)
        gates = _router(logits_t, router_bias[l][:, None]).T
        x = _moe(l, h2, gates, x1, mod, exp_w_gate, exp_w_up, exp_w_down, sh_w_gate, sh_w_up, sh_w_down)

        states.append(_blockdiag_to_state(st_c))
        keys.append(k_l)
        vals.append(v_l)

    y_prompt = x[:N_CTX_TOK].reshape(BATCH, SEQ, D_MODEL)
    y_sample = x[N_CTX_TOK:].reshape(DEC_BATCH, DEC_SEQ, D_MODEL)
    return (y_prompt, y_sample, jnp.stack(states, axis=1), jnp.stack(keys, axis=1), jnp.stack(vals, axis=1))
```

```python
import functools

import numpy as np
import jax
import jax.numpy as jnp
from jax import lax
from jax.experimental import pallas as pl
from jax.experimental.pallas import tpu as pltpu

D_MODEL = 1024
BATCH = 16
SEQ = 256
DEPTH = 2
DEC_BATCH = 4
DEC_SEQ = 1024
PAST_LEN = 256
GRID_W = 64
GLA_HEADS = 4
GLA_DK = 64
GLA_DV = 128
GLA_KEY = GLA_HEADS * GLA_DK
GLA_VAL = GLA_HEADS * GLA_DV
GLA_GATE_RANK = 16
GLA_GATE_NORM = 16.0
GLA_CHUNK = 64
CONV_CH = 256
CONV_WIDTH = 31
NA_HEADS = 4
NA_HD = 64
NA_DIM = NA_HEADS * NA_HD
NA_WIN_ROWS = 8
NA_WIN_COLS = 16
N_EXPERTS = 64
TOP_K = 8
N_GROUPS = 8
TOPK_GROUPS = 4
EXPERT_DIM = 256
SHARED_DIM = 256
ROUTED_SCALE = 2.5
EPS = 1e-6

N_CTX_TOK = BATCH * SEQ
N_LAT_TOK = DEC_BATCH * DEC_SEQ
N_TOK = N_CTX_TOK + N_LAT_TOK
ROW_TILE = 256
N_CTX_TILES = N_CTX_TOK // ROW_TILE
TILES_PER_LAT_SEQ = DEC_SEQ // ROW_TILE
MOD_ROWS = 8
GLA_IN_W = 1664
GLA_LR_OFF = 2 * GLA_KEY + 2 * GLA_VAL
VMEM_LIMIT = 56 * 1024 * 1024

_BF = jnp.bfloat16
_F32 = jnp.float32


def _bf(x):
    return x.astype(_BF)


def _dot(a, b):
    return jnp.dot(a, b, preferred_element_type=_F32)


def _dot_nt(a, b):
    return lax.dot_general(a, b, (((1,), (1,)), ((), ())), preferred_element_type=_F32)


def _dot_tn(a, b):
    return lax.dot_general(a, b, (((0,), (0,)), ((), ())), preferred_element_type=_F32)


def _split2(x):
    hi = _bf(x)
    lo = _bf(x - hi.astype(_F32))
    return hi, lo


def _split3(x):
    hi = _bf(x)
    r = x - hi.astype(_F32)
    mid = _bf(r)
    lo = _bf(r - mid.astype(_F32))
    return hi, mid, lo


def _dot3(a, b, dot=_dot):
    a_hi, a_lo = _split2(a)
    b_hi, b_lo = _split2(b)
    return (dot(a_lo, b_hi) + dot(a_hi, b_lo)) + dot(a_hi, b_hi)


def _sigmoid(x):
    return 1.0 / (1.0 + jnp.exp(-x))


def _silu(x):
    return x * _sigmoid(x)


def _rms(x, w):
    return x * lax.rsqrt(jnp.mean(x * x, axis=-1, keepdims=True) + EPS) * w


def _params(sem):
    return pltpu.CompilerParams(dimension_semantics=sem, vmem_limit_bytes=VMEM_LIMIT)


def _mod_row(i):
    return jnp.where(i < N_CTX_TILES, 0, 1 + (i - N_CTX_TILES) // TILES_PER_LAT_SEQ)


def _ada_kernel(cv_ref, w_ref, b_ref, o_ref):
    o_ref[0] = _dot3(_silu(cv_ref[...]), w_ref[0]) + b_ref[0]


def _ada_mod(cvec, w_ada, b_ada):
    tn = 1024
    n_out = 6 * D_MODEL
    return pl.pallas_call(
        _ada_kernel,
        out_shape=jax.ShapeDtypeStruct((DEPTH, MOD_ROWS, n_out), _F32),
        grid=(DEPTH, n_out // tn),
        in_specs=[
            pl.BlockSpec((MOD_ROWS, D_MODEL), lambda l, j: (0, 0)),
            pl.BlockSpec((1, D_MODEL, tn), lambda l, j: (l, 0, j)),
            pl.BlockSpec((1, 1, tn), lambda l, j: (l, 0, j)),
        ],
        out_specs=pl.BlockSpec((1, MOD_ROWS, tn), lambda l, j: (l, 0, j)),
        compiler_params=_params(("arbitrary", "arbitrary")),
        name="ada_mod",
    )(cvec, w_ada, b_ada.reshape(DEPTH, 1, n_out))


def _inproj_kernel(xc_ref, xl_ref, nw_ref, mod_ref, wg_ref, wc_ref, wa_ref, g_ref, c_ref, a_ref):
    i = pl.program_id(0)
    row = _mod_row(i)
    sh = mod_ref[pl.ds(row, 1), 0:D_MODEL]
    sc = mod_ref[pl.ds(row, 1), D_MODEL:2 * D_MODEL]
    x = jnp.where(i < N_CTX_TILES, xc_ref[...], xl_ref[...])
    h = _bf(_rms(x, nw_ref[...]) * (1.0 + sc) + sh)
    g_ref[...] = _dot(h, wg_ref[...])
    c_ref[...] = _dot(h, wc_ref[...])
    a_ref[...] = _dot(h, wa_ref[...])


def _stream_tiles(lat_first_tile):
    ctx_tile = lambda i: (jnp.minimum(i, N_CTX_TILES - 1), 0)
    lat_tile = lambda i: (jnp.maximum(i - N_CTX_TILES, 0) + lat_first_tile, 0)
    return ctx_tile, lat_tile


def _in_proj(x_ctx, x_lat, lat_first_tile, norm_w, mod, w_gla, w_conv, w_na):
    full = lambda i: (0, 0)
    tile = lambda i: (i, 0)
    ctx_tile, lat_tile = _stream_tiles(lat_first_tile)
    return pl.pallas_call(
        _inproj_kernel,
        out_shape=(
            jax.ShapeDtypeStruct((N_TOK, GLA_IN_W), _F32),
            jax.ShapeDtypeStruct((N_TOK, 2 * CONV_CH), _F32),
            jax.ShapeDtypeStruct((N_TOK, 3 * NA_DIM), _F32),
        ),
        grid=(N_TOK // ROW_TILE,),
        in_specs=[
            pl.BlockSpec((ROW_TILE, D_MODEL), ctx_tile),
            pl.BlockSpec((ROW_TILE, D_MODEL), lat_tile),
            pl.BlockSpec((1, D_MODEL), full),
            pl.BlockSpec((MOD_ROWS, 6 * D_MODEL), full),
            pl.BlockSpec((D_MODEL, GLA_IN_W), full),
            pl.BlockSpec((D_MODEL, 2 * CONV_CH), full),
            pl.BlockSpec((D_MODEL, 3 * NA_DIM), full),
        ],
        out_specs=(
            pl.BlockSpec((ROW_TILE, GLA_IN_W), tile),
            pl.BlockSpec((ROW_TILE, 2 * CONV_CH), tile),
            pl.BlockSpec((ROW_TILE, 3 * NA_DIM), tile),
        ),
        compiler_params=_params(("arbitrary",)),
        name="in_proj",
    )(x_ctx, x_lat, norm_w, mod, w_gla, w_conv, w_na)


_GLA_LEVELS = (32, 16, 8, 4, 2, 1)
_N_EXP_BLOCKS = 2 + len(_GLA_LEVELS)
_N_MASKS = len(_GLA_LEVELS) + 1


def _gla_constants():
    cs = GLA_CHUNK
    r = np.arange(cs)
    i = r[:, None]
    c = r[None, :]
    w = np.zeros((2, _N_EXP_BLOCKS, cs, cs), np.float32)
    m = np.zeros((2, _N_MASKS, cs, cs), np.float32)
    w[0, 0] = c <= i
    w[0, 1] = c > i
    w[1, 0] = c >= i
    w[1, 1] = c < i
    for lv, half in enumerate(_GLA_LEVELS):
        mid = (r // (2 * half)) * (2 * half) + half
        mi = mid[:, None]
        second = (r >= mid)[:, None]
        same = (r[:, None] // (2 * half)) == (r[None, :] // (2 * half))
        w[0, 2 + lv] = np.where(second, (c >= mi) & (c <= i), (c > i) & (c <= mi - 1))
        w[1, 2 + lv] = np.where(second, (c >= mi) & (c <= i - 1), (c >= i) & (c <= mi - 1))
        m[0, lv] = same & (i >= mi) & (c < mi)
        m[1, lv] = same & (i < mi) & (c >= mi)
    m[:, _N_MASKS - 1] = np.eye(cs)
    w_all = w.reshape(2, _N_EXP_BLOCKS * cs, cs)
    lmask = np.tile(m, (1, 1, 1, GLA_HEADS))
    kr = np.arange(GLA_HEADS * cs)
    bdk = (kr[:, None] // cs) == (np.arange(GLA_KEY)[None, :] // GLA_DK)
    bdv = (kr[:, None] // cs) == (np.arange(GLA_VAL)[None, :] // GLA_DV)
    bdst = (np.arange(GLA_VAL)[:, None] // GLA_DV) == (np.arange(GLA_KEY)[None, :] // GLA_DK)
    return (jnp.asarray(w_all, _BF), jnp.asarray(lmask, _F32), jnp.asarray(bdk, _F32),
            jnp.asarray(bdv, _F32), jnp.asarray(bdst, _F32))


def _gla_kernel(g_ref, wz_ref, bz_ref, wall_ref, lmask_ref, bdk_ref, bdv_ref, bdst_ref, onorm_ref, st0_ref,
                o_ref, stfin_ref, la_scr, o_scr, st_scr, *, n):
    cs = GLA_CHUNK
    nc = n // cs
    gate_rows = 128

    def log_decays(t, carry):
        rows = pl.ds(pl.multiple_of(t * gate_rows, gate_rows), gate_rows)
        z = _dot3(g_ref[rows, GLA_LR_OFF:GLA_IN_W], wz_ref[...]) + bz_ref[...]
        la_scr[rows, :] = (jnp.minimum(z, 0.0) - jnp.log1p(jnp.exp(-jnp.abs(z)))) * (1.0 / GLA_GATE_NORM)
        return carry

    lax.fori_loop(0, n // gate_rows, log_decays, 0)

    def chunk(c, d):
        rows = pl.ds(pl.multiple_of(c * cs, cs), cs)
        q = g_ref[rows, 0:GLA_KEY] * (GLA_DK ** -0.5)
        k = g_ref[rows, GLA_KEY:2 * GLA_KEY]
        v = g_ref[rows, 2 * GLA_KEY:2 * GLA_KEY + GLA_VAL]
        la_hi, la_mid, la_lo = _split3(la_scr[rows, d * GLA_KEY:(d + 1) * GLA_KEY])
        w = wall_ref[d]
        f = jnp.exp((_dot(w, la_lo) + _dot(w, la_mid)) + _dot(w, la_hi))
        st = st_scr[d]
        o = _dot_nt(_bf(q * f[0:cs]), _bf(st))
        p = jnp.zeros((cs, GLA_HEADS * cs), _F32)
        for lv in range(_N_MASKS):
            if lv < len(_GLA_LEVELS):
                fl = f[(2 + lv) * cs:(3 + lv) * cs]
                ql, kl = q * fl, k * fl
            else:
                ql, kl = q, k
            k_bd = _bf(jnp.concatenate([kl] * GLA_HEADS, axis=0) * bdk_ref[...])
            p = p + lmask_ref[d, lv] * _dot_nt(_bf(ql), k_bd)
        v_bd = _bf(jnp.concatenate([v] * GLA_HEADS, axis=0) * bdv_ref[...])
        o_scr[d, rows, :] = o + _dot(_bf(p), v_bd)
        decay = f[cs - 1:cs] if d == 0 else f[0:1]
        st_scr[d] = st * decay + _dot_tn(_bf(v), _bf(k * f[cs:2 * cs])) * bdst_ref[...]

    st_scr[...] = st0_ref[0]

    def scan(i, carry):
        chunk(i, 0)
        chunk(nc - 1 - i, 1)
        return carry

    lax.fori_loop(0, nc, scan, 0)
    stfin_ref[0] = st_scr[...]

    def finish(c, carry):
        rows = pl.ds(pl.multiple_of(c * cs, cs), cs)
        o = o_scr[1, rows, :] + o_scr[0, rows, :]
        for h in range(GLA_HEADS):
            cols = slice(h * GLA_DV, (h + 1) * GLA_DV)
            gate = g_ref[rows, 2 * GLA_KEY + GLA_VAL + h * GLA_DV:2 * GLA_KEY + GLA_VAL + (h + 1) * GLA_DV]
            o_ref[rows, cols] = _rms(o[:, cols], onorm_ref[...]) * _silu(gate)
        return carry

    lax.fori_loop(0, nc, finish, 0)


def _gla(g_all, wz, bz, consts, onorm, st0, *, n, n_seq, first_block):
    w_all, lmask, bdk, bdv, bdst = consts
    full2 = lambda i: (0, 0)
    return pl.pallas_call(
        functools.partial(_gla_kernel, n=n),
        out_shape=(
            jax.ShapeDtypeStruct((n_seq * n, GLA_VAL), _F32),
            jax.ShapeDtypeStruct((n_seq, 2, GLA_VAL, GLA_KEY), _F32),
        ),
        grid=(n_seq,),
        in_specs=[
            pl.BlockSpec((n, GLA_IN_W), lambda i: (first_block + i, 0)),
            pl.BlockSpec(wz.shape, full2),
            pl.BlockSpec(bz.shape, full2),
            pl.BlockSpec(w_all.shape, lambda i: (0, 0, 0)),
            pl.BlockSpec(lmask.shape, lambda i: (0, 0, 0, 0)),
            pl.BlockSpec(bdk.shape, full2),
            pl.BlockSpec(bdv.shape, full2),
            pl.BlockSpec(bdst.shape, full2),
            pl.BlockSpec((1, GLA_DV), full2),
            pl.BlockSpec((1, 2, GLA_VAL, GLA_KEY), lambda i: (i, 0, 0, 0)),
        ],
        out_specs=(
            pl.BlockSpec((n, GLA_VAL), lambda i: (i, 0)),
            pl.BlockSpec((1, 2, GLA_VAL, GLA_KEY), lambda i: (i, 0, 0, 0)),
        ),
        scratch_shapes=[pltpu.VMEM((n, 2 * GLA_KEY), _F32), pltpu.VMEM((2, n, GLA_VAL), _F32),
                        pltpu.VMEM((2, GLA_VAL, GLA_KEY), _F32)],
        compiler_params=_params(("arbitrary",)),
        name=f"gla_{n}",
    )(g_all, wz, bz, w_all, lmask, bdk, bdv, bdst, onorm, st0)


def _state_to_blockdiag(s):
    b = s.shape[0]
    st = jnp.swapaxes(s, -1, -2)
    eye = jnp.eye(GLA_HEADS, dtype=s.dtype)
    out = st[:, :, :, :, None, :] * eye[None, None, :, None, :, None]
    return out.reshape(b, 2, GLA_VAL, GLA_KEY)


def _blockdiag_to_state(st):
    b = st.shape[0]
    s6 = st.reshape(b, 2, GLA_HEADS, GLA_DV, GLA_HEADS, GLA_DK)
    diag = jnp.stack([s6[:, :, h, :, h, :] for h in range(GLA_HEADS)], axis=2)
    return jnp.swapaxes(diag, -1, -2)


_CONV_PAD = 16
_CONV_ROWS = 128


def _conv_kernel(c_ref, dw_ref, dwb_ref, lng_ref, lnb_ref, pw_ref, o_ref, pad_scr, *, n):
    zeros = jnp.zeros((_CONV_PAD, CONV_CH), _F32)
    pad_scr[0:_CONV_PAD, :] = zeros
    pad_scr[_CONV_PAD + n:2 * _CONV_PAD + n, :] = zeros
    for r0 in range(0, n, _CONV_ROWS):
        a = c_ref[r0:r0 + _CONV_ROWS, :]
        pad_scr[_CONV_PAD + r0:_CONV_PAD + r0 + _CONV_ROWS, :] = a[:, :CONV_CH] * _sigmoid(a[:, CONV_CH:])
    half = CONV_WIDTH // 2
    for r0 in range(0, n, _CONV_ROWS):
        acc = jnp.zeros((_CONV_ROWS, CONV_CH), _F32)
        for w in range(CONV_WIDTH):
            s = _CONV_PAD + r0 + w - half
            acc = acc + pad_scr[s:s + _CONV_ROWS, :] * dw_ref[w:w + 1, :]
        acc = acc + dwb_ref[...]
        xc = acc - jnp.mean(acc, axis=-1, keepdims=True)
        y = xc * lax.rsqrt(jnp.mean(xc * xc, axis=-1, keepdims=True) + EPS) * lng_ref[...] + lnb_ref[...]
        o_ref[r0:r0 + _CONV_ROWS, :] = _dot(_bf(_silu(y)), pw_ref[...])


def _conv(c_all, dw, dwb, lng, lnb, pw, *, n, n_seq, first_block):
    full2 = lambda i: (0, 0)
    return pl.pallas_call(
        functools.partial(_conv_kernel, n=n),
        out_shape=jax.ShapeDtypeStruct((n_seq * n, CONV_CH), _F32),
        grid=(n_seq,),
        in_specs=[
            pl.BlockSpec((n, 2 * CONV_CH), lambda i: (first_block + i, 0)),
            pl.BlockSpec((CONV_WIDTH, CONV_CH), full2),
            pl.BlockSpec((1, CONV_CH), full2),
            pl.BlockSpec((1, CONV_CH), full2),
            pl.BlockSpec((1, CONV_CH), full2),
            pl.BlockSpec((CONV_CH, CONV_CH), full2),
        ],
        out_specs=pl.BlockSpec((n, CONV_CH), lambda i: (i, 0)),
        scratch_shapes=[pltpu.VMEM((n + 2 * _CONV_PAD, CONV_CH), _F32)],
        compiler_params=_params(("arbitrary",)),
        name=f"conv_{n}",
    )(c_all, dw, dwb, lng, lnb, pw)


def _softmax_rows(s):
    e = jnp.exp(s - jnp.max(s, axis=-1, keepdims=True))
    return e / jnp.sum(e, axis=-1, keepdims=True)


def _na_ctx_kernel(a_ref, qn_ref, kn_ref, o_ref, kc_ref, vc_ref):
    for h in range(NA_HEADS):
        cols = slice(h * NA_HD, (h + 1) * NA_HD)
        q = _rms(a_ref[:, h * NA_HD:(h + 1) * NA_HD], qn_ref[...])
        k = _rms(a_ref[:, NA_DIM + h * NA_HD:NA_DIM + (h + 1) * NA_HD], kn_ref[...])
        v = a_ref[:, 2 * NA_DIM + h * NA_HD:2 * NA_DIM + (h + 1) * NA_HD]
        kc_ref[0, h] = k
        vc_ref[0, h] = v
        p = _softmax_rows(_dot_nt(_bf(q), _bf(k)) * (NA_HD ** -0.5))
        o_ref[:, cols] = _dot(_bf(p), _bf(v))


def _na_ctx(a_all, qn, kn):
    full2 = lambda i: (0, 0)
    cache = jax.ShapeDtypeStruct((BATCH, NA_HEADS, SEQ, NA_HD), _F32)
    cache_spec = pl.BlockSpec((1, NA_HEADS, SEQ, NA_HD), lambda i: (i, 0, 0, 0))
    return pl.pallas_call(
        _na_ctx_kernel,
        out_shape=(jax.ShapeDtypeStruct((N_CTX_TOK, NA_DIM), _F32), cache, cache),
        grid=(BATCH,),
        in_specs=[
            pl.BlockSpec((SEQ, 3 * NA_DIM), lambda i: (i, 0)),
            pl.BlockSpec((1, NA_HD), full2),
            pl.BlockSpec((1, NA_HD), full2),
        ],
        out_specs=(pl.BlockSpec((SEQ, NA_DIM), lambda i: (i, 0)), cache_spec, cache_spec),
        compiler_params=_params(("arbitrary",)),
        name="na_ctx",
    )(a_all, qn, kn)


_NA_ROWS = DEC_SEQ // GRID_W
_NA_KEYS = NA_WIN_ROWS * GRID_W
_NA_VARIANTS = NA_WIN_ROWS
_NA_ROW_UNROLL = 4


_RPB_ROWS = 2 * NA_WIN_ROWS - 1
_RPB_COLS = 2 * NA_WIN_COLS - 1


def _na_bias_constants():
    qc = np.arange(GRID_W)[:, None]
    kc = np.arange(GRID_W)[None, :]
    shift = np.stack([(kc - qc + NA_WIN_COLS - 1) == co for co in range(_RPB_COLS)]).astype(np.float32)
    win_start = np.clip(qc - NA_WIN_COLS // 2, 0, GRID_W - NA_WIN_COLS)
    in_win = ((kc >= win_start) & (kc < win_start + NA_WIN_COLS)).astype(np.float32)
    return jnp.asarray(shift), jnp.asarray(in_win)


def _na_bias_kernel(rpb_ref, shift_ref, win_ref, o_ref):
    h = pl.program_id(0)
    in_win = win_ref[...] > 0.0
    tiles = []
    for ro in range(_RPB_ROWS):
        base = (h * _RPB_ROWS + ro) * _RPB_COLS
        acc = rpb_ref[base] * shift_ref[0]
        for co in range(1, _RPB_COLS):
            acc = acc + rpb_ref[base + co] * shift_ref[co]
        tiles.append(jnp.where(in_win, acc, -jnp.inf))
    for t in range(_NA_VARIANTS):
        o_ref[0, t] = jnp.concatenate([tiles[kr - t + NA_WIN_ROWS - 1] for kr in range(NA_WIN_ROWS)], axis=1)


def _na_bias_table(rpb):
    shift, in_win = _na_bias_constants()
    return pl.pallas_call(
        _na_bias_kernel,
        out_shape=jax.ShapeDtypeStruct((NA_HEADS, _NA_VARIANTS, GRID_W, _NA_KEYS), _F32),
        grid=(NA_HEADS,),
        in_specs=[
            pl.BlockSpec(memory_space=pltpu.SMEM),
            pl.BlockSpec(shift.shape, lambda h: (0, 0, 0)),
            pl.BlockSpec(in_win.shape, lambda h: (0, 0)),
        ],
        out_specs=pl.BlockSpec((1, _NA_VARIANTS, GRID_W, _NA_KEYS), lambda h: (h, 0, 0, 0)),
        compiler_params=_params(("arbitrary",)),
        name="na_bias",
    )(rpb.reshape(-1), shift, in_win)


def _na_lat_kernel(a_ref, kctx_ref, vctx_ref, bias_ref, qn_ref, kn_ref, o_ref, q_scr, k_scr, v_scr):
    scale = NA_HD ** -0.5
    for h in range(NA_HEADS):
        q_scr[...] = _bf(_rms(a_ref[:, h * NA_HD:(h + 1) * NA_HD], qn_ref[...]))
        k_scr[...] = _bf(_rms(a_ref[:, NA_DIM + h * NA_HD:NA_DIM + (h + 1) * NA_HD], kn_ref[...]))
        v_scr[...] = _bf(a_ref[:, 2 * NA_DIM + h * NA_HD:2 * NA_DIM + (h + 1) * NA_HD])
        k_ctx = _bf(kctx_ref[0, h])
        v_ctx = _bf(vctx_ref[0, h])

        def row(r):
            start = jnp.clip(r - NA_WIN_ROWS // 2, 0, _NA_ROWS - NA_WIN_ROWS)
            qrows = pl.ds(pl.multiple_of(r * GRID_W, GRID_W), GRID_W)
            krows = pl.ds(pl.multiple_of(start * GRID_W, GRID_W), _NA_KEYS)
            q = q_scr[qrows, :]
            s_loc = _dot_nt(q, k_scr[krows, :]) * scale + bias_ref[h, r - start]
            s_ctx = _dot_nt(q, k_ctx) * scale
            m = jnp.maximum(jnp.max(s_loc, axis=-1, keepdims=True), jnp.max(s_ctx, axis=-1, keepdims=True))
            e_loc = jnp.exp(s_loc - m)
            e_ctx = jnp.exp(s_ctx - m)
            inv = 1.0 / (jnp.sum(e_loc, axis=-1, keepdims=True) + jnp.sum(e_ctx, axis=-1, keepdims=True))
            o = _dot(_bf(e_loc * inv), v_scr[krows, :]) + _dot(_bf(e_ctx * inv), v_ctx)
            o_ref[qrows, h * NA_HD:(h + 1) * NA_HD] = o

        def row_group(g, carry):
            for dr in range(_NA_ROW_UNROLL):
                row(g * _NA_ROW_UNROLL + dr)
            return carry

        lax.fori_loop(0, _NA_ROWS // _NA_ROW_UNROLL, row_group, 0)


def _na_lat(a_all, k_ctx, v_ctx, bias, qn, kn):
    full2 = lambda i: (0, 0)
    ctx_spec = pl.BlockSpec((1, NA_HEADS, PAST_LEN, NA_HD), lambda i: (i, 0, 0, 0))
    return pl.pallas_call(
        _na_lat_kernel,
        out_shape=jax.ShapeDtypeStruct((N_LAT_TOK, NA_DIM), _F32),
        grid=(DEC_BATCH,),
        in_specs=[
            pl.BlockSpec((DEC_SEQ, 3 * NA_DIM), lambda i: (N_CTX_TOK // DEC_SEQ + i, 0)),
            ctx_spec,
            ctx_spec,
            pl.BlockSpec(bias.shape, lambda i: (0, 0, 0, 0)),
            pl.BlockSpec((1, NA_HD), full2),
            pl.BlockSpec((1, NA_HD), full2),
        ],
        out_specs=pl.BlockSpec((DEC_SEQ, NA_DIM), lambda i: (i, 0)),
        scratch_shapes=[pltpu.VMEM((DEC_SEQ, NA_HD), _BF)] * 3,
        compiler_params=_params(("arbitrary",)),
        name="na_lat",
    )(a_all, k_ctx, v_ctx, bias, qn, kn)


def _outproj_kernel(gc_ref, gl_ref, cc_ref, cl_ref, nc_ref, nl_ref, xc_ref, xl_ref, mod_ref, nw_ref, wo_ref,
                    rwt_ref, x1_ref, h2_ref, lg_ref):
    i = pl.program_id(0)
    is_ctx = i < N_CTX_TILES
    row = _mod_row(i)
    gla = jnp.where(is_ctx, gc_ref[...], gl_ref[...])
    conv = jnp.where(is_ctx, cc_ref[...], cl_ref[...])
    na = jnp.where(is_ctx, nc_ref[...], nl_ref[...])
    mix = (_dot(_bf(gla), wo_ref[0:GLA_VAL, :])
           + _dot(_bf(conv), wo_ref[GLA_VAL:GLA_VAL + CONV_CH, :])
           + _dot(_bf(na), wo_ref[GLA_VAL + CONV_CH:, :]))
    g1 = mod_ref[pl.ds(row, 1), 2 * D_MODEL:3 * D_MODEL]
    sh2 = mod_ref[pl.ds(row, 1), 3 * D_MODEL:4 * D_MODEL]
    sc2 = mod_ref[pl.ds(row, 1), 4 * D_MODEL:5 * D_MODEL]
    x1 = jnp.where(is_ctx, xc_ref[...], xl_ref[...]) + g1 * mix
    h2 = _rms(x1, nw_ref[...]) * (1.0 + sc2) + sh2
    x1_ref[...] = x1
    h2_ref[...] = h2
    lg_ref[...] = _dot3(rwt_ref[...], h2, dot=_dot_nt)


def _out_proj(gla_c, gla_l, conv_c, conv_l, na_c, na_l, x_ctx, x_lat, lat_first_tile, mod, norm_w, w_out,
              router_wt):
    full2 = lambda i: (0, 0)
    tile = lambda i: (i, 0)
    ctx_tile, lat_tile = _stream_tiles(0)
    _, x_lat_tile = _stream_tiles(lat_first_tile)
    return pl.pallas_call(
        _outproj_kernel,
        out_shape=(
            jax.ShapeDtypeStruct((N_TOK, D_MODEL), _F32),
            jax.ShapeDtypeStruct((N_TOK, D_MODEL), _F32),
            jax.ShapeDtypeStruct((N_EXPERTS, N_TOK), _F32),
        ),
        grid=(N_TOK // ROW_TILE,),
        in_specs=[
            pl.BlockSpec((ROW_TILE, GLA_VAL), ctx_tile),
            pl.BlockSpec((ROW_TILE, GLA_VAL), lat_tile),
            pl.BlockSpec((ROW_TILE, CONV_CH), ctx_tile),
            pl.BlockSpec((ROW_TILE, CONV_CH), lat_tile),
            pl.BlockSpec((ROW_TILE, NA_DIM), ctx_tile),
            pl.BlockSpec((ROW_TILE, NA_DIM), lat_tile),
            pl.BlockSpec((ROW_TILE, D_MODEL), ctx_tile),
            pl.BlockSpec((ROW_TILE, D_MODEL), x_lat_tile),
            pl.BlockSpec((MOD_ROWS, 6 * D_MODEL), full2),
            pl.BlockSpec((1, D_MODEL), full2),
            pl.BlockSpec((D_MODEL, D_MODEL), full2),
            pl.BlockSpec((N_EXPERTS, D_MODEL), full2),
        ],
        out_specs=(
            pl.BlockSpec((ROW_TILE, D_MODEL), tile),
            pl.BlockSpec((ROW_TILE, D_MODEL), tile),
            pl.BlockSpec((N_EXPERTS, ROW_TILE), lambda i: (0, i)),
        ),
        compiler_params=_params(("arbitrary",)),
        name="out_proj",
    )(gla_c, gla_l, conv_c, conv_l, na_c, na_l, x_ctx, x_lat, mod, norm_w, w_out, router_wt)


_PER_GROUP = N_EXPERTS // N_GROUPS
_ROUTE_TILE = 1024


def _first_max(x, idx, axes, sentinel):
    m = x
    for ax in axes:
        m = jnp.max(m, axis=ax, keepdims=True)
    first = jnp.where(x == m, idx, sentinel)
    for ax in axes:
        first = jnp.min(first, axis=ax, keepdims=True)
    return m, first


def _router_kernel(lg_ref, rb_ref, eid_ref, wts_ref):
    t = lg_ref.shape[1]
    shape3 = (N_GROUPS, _PER_GROUP, t)
    scores = _sigmoid(lg_ref[...])
    biased = (scores + rb_ref[...]).reshape(shape3)
    scores = scores.reshape(shape3)
    neg = -jnp.inf
    in_grp = lax.broadcasted_iota(jnp.int32, shape3, 1)
    grp = lax.broadcasted_iota(jnp.int32, (N_GROUPS, 1, t), 0)
    expert = lax.broadcasted_iota(jnp.int32, shape3, 0) * _PER_GROUP + in_grp
    m1, i1 = _first_max(biased, in_grp, (1,), _PER_GROUP)
    m2 = jnp.max(jnp.where(in_grp == i1, neg, biased), axis=1, keepdims=True)
    gscore = m1 + m2
    keep = jnp.zeros((N_GROUPS, 1, t), _F32)
    for _ in range(TOPK_GROUPS):
        _, gi = _first_max(gscore, grp, (0,), N_GROUPS)
        hit = grp == gi
        keep = jnp.where(hit, 1.0, keep)
        gscore = jnp.where(hit, neg, gscore)
    masked = jnp.where(keep > 0.0, biased, neg)
    ids, picked = [], []
    for _ in range(TOP_K):
        _, ei = _first_max(masked, expert, (1, 0), N_EXPERTS)
        hit = expert == ei
        sc = jnp.sum(jnp.sum(jnp.where(hit, scores, 0.0), axis=1, keepdims=True), axis=0, keepdims=True)
        ids.append(ei.reshape(1, t))
        picked.append(sc.reshape(1, t))
        masked = jnp.where(hit, neg, masked)
    picked = jnp.concatenate(picked, axis=0)
    den = jnp.sum(picked, axis=0, keepdims=True)
    eid_ref[...] = jnp.concatenate(ids, axis=0)
    wts_ref[...] = picked / den * ROUTED_SCALE


def _router(logits_t, router_bias):
    tile = lambda i: (0, i)
    return pl.pallas_call(
        _router_kernel,
        out_shape=(jax.ShapeDtypeStruct((TOP_K, N_TOK), jnp.int32), jax.ShapeDtypeStruct((TOP_K, N_TOK), _F32)),
        grid=(N_TOK // _ROUTE_TILE,),
        in_specs=[
            pl.BlockSpec((N_EXPERTS, _ROUTE_TILE), tile),
            pl.BlockSpec((N_EXPERTS, 1), lambda i: (0, 0)),
        ],
        out_specs=(pl.BlockSpec((TOP_K, _ROUTE_TILE), tile), pl.BlockSpec((TOP_K, _ROUTE_TILE), tile)),
        compiler_params=_params(("arbitrary",)),
        name="router",
    )(logits_t, router_bias)


_HALF_TOK = N_CTX_TOK
_N_HALVES = N_TOK // _HALF_TOK
_HALF_ASSIGN = _HALF_TOK * TOP_K
_GB = 256
_GB_MAX = _HALF_ASSIGN // _GB + N_EXPERTS
_LANE = 128
_ROW_CHUNKS = D_MODEL // _LANE
_GS = _GB + 8
_RMW_BATCH = 8
_TOK_BITS = 12
assert _HALF_TOK == 1 << _TOK_BITS


def _moe_plan(eid, wts):
    tok = jnp.arange(N_TOK, dtype=jnp.int32)
    key = (((tok >> _TOK_BITS) * N_EXPERTS)[None, :] + eid) * _HALF_TOK + (tok & (_HALF_TOK - 1))[None, :]
    key_s, w_s = lax.sort((key.reshape(-1), wts.reshape(-1)), num_keys=1)
    n_grp = _N_HALVES * N_EXPERTS
    bounds = jnp.arange(n_grp + 1, dtype=jnp.int32) * _HALF_TOK
    grp_start = jnp.searchsorted(key_s, bounds, side="left").astype(jnp.int32)
    count = (grp_start[1:] - grp_start[:-1]).reshape(_N_HALVES, N_EXPERTS)
    nblk = (count + _GB - 1) // _GB
    blk_end = jnp.cumsum(nblk, axis=1)
    blk_start = blk_end - nblk
    b = jnp.arange(_GB_MAX, dtype=jnp.int32)
    grp = jnp.sum((b[None, :, None] >= blk_end[:, None, :]).astype(jnp.int32), axis=-1)
    n_used = blk_end[:, -1]
    last = jnp.take_along_axis(grp, jnp.maximum(n_used - 1, 0)[:, None], axis=1)
    grp = jnp.where(b[None, :] < n_used[:, None], grp, last)
    within = b[None, :] - jnp.take_along_axis(blk_start, grp, axis=1)
    start = jnp.take_along_axis(grp_start[:-1].reshape(_N_HALVES, N_EXPERTS), grp, axis=1) + within * _GB
    length = jnp.clip(jnp.take_along_axis(count, grp, axis=1) - within * _GB, 0, _GB)
    pad = jnp.zeros((_GB,), jnp.int32)
    return (jnp.concatenate([key_s, pad]), jnp.concatenate([w_s, pad.astype(_F32)]),
            grp.astype(jnp.int32), start.astype(jnp.int32), length.astype(jnp.int32), n_used.astype(jnp.int32))


def _gmm_kernel(be_ref, bstart_ref, blen_ref, nused_ref, key_ref, gate_ref,
                h_ref, wg_ref, wu_ref, wd_ref, acc_ref, xt_scr, yt_scr):
    b = pl.program_id(0)

    @pl.when(b == 0)
    def _():
        acc_ref[...] = jnp.zeros_like(acc_ref)
        xt_scr[...] = jnp.zeros_like(xt_scr)

    @pl.when(b < nused_ref[0])
    def _():
        start = bstart_ref[b]
        length = blen_ref[b]
        for mi in range(_GB):
            tok = key_ref[start + mi] & (_HALF_TOK - 1)
            src = pl.ds(pl.multiple_of(tok * _ROW_CHUNKS, _ROW_CHUNKS), _ROW_CHUNKS)
            xt_scr[pl.ds(mi, _ROW_CHUNKS, stride=_GS), :] = h_ref[src, :]
        x = _bf(jnp.concatenate([xt_scr[j * _GS:j * _GS + _GB, :] for j in range(_ROW_CHUNKS)], axis=-1))
        act = _silu(_dot(x, _bf(wg_ref[0, 0]))) * _dot(x, _bf(wu_ref[0, 0]))
        y = _dot(_bf(act), _bf(wd_ref[0, 0]))
        for j in range(_ROW_CHUNKS):
            yt_scr[j * _GS:j * _GS + _GB, :] = y[:, j * _LANE:(j + 1) * _LANE]
        for m0 in range(0, _GB, _RMW_BATCH):
            pending = []
            for mi in range(m0, m0 + _RMW_BATCH):
                valid = mi < length
                tok = jnp.where(valid, key_ref[start + mi] & (_HALF_TOK - 1), _HALF_TOK)
                gate = jnp.where(valid, gate_ref[start + mi], 0.0)
                rows = pl.ds(pl.multiple_of(tok * _ROW_CHUNKS, _ROW_CHUNKS), _ROW_CHUNKS)
                pending.append((rows, acc_ref[rows, :] + gate * yt_scr[pl.ds(mi, _ROW_CHUNKS, stride=_GS), :]))
            for rows, val in pending:
                acc_ref[rows, :] = val


def _gmm(layer, half, plan, h_rows, wg, wu, wd):
    key_s, w_s, grp, start, length, n_used = plan
    expert = lambda b, be, *_: (layer, be[b], 0, 0)
    acc_rows = (_HALF_TOK + 1) * _ROW_CHUNKS
    grid_spec = pltpu.PrefetchScalarGridSpec(
        num_scalar_prefetch=6,
        grid=(_GB_MAX,),
        in_specs=[
            pl.BlockSpec((_HALF_TOK * _ROW_CHUNKS, _LANE), lambda b, *_: (half, 0), pipeline_mode=pl.Buffered(1)),
            pl.BlockSpec((1, 1, D_MODEL, EXPERT_DIM), expert),
            pl.BlockSpec((1, 1, D_MODEL, EXPERT_DIM), expert),
            pl.BlockSpec((1, 1, EXPERT_DIM, D_MODEL), expert),
        ],
        out_specs=pl.BlockSpec((acc_rows, _LANE), lambda b, *_: (0, 0), pipeline_mode=pl.Buffered(1)),
        scratch_shapes=[pltpu.VMEM((_ROW_CHUNKS * _GS, _LANE), _F32)] * 2,
    )
    acc = pl.pallas_call(
        _gmm_kernel,
        out_shape=jax.ShapeDtypeStruct((acc_rows, _LANE), _F32),
        grid_spec=grid_spec,
        compiler_params=_params(("arbitrary",)),
        name="moe_experts",
    )(grp[half], start[half], length[half], n_used[half:half + 1], key_s, w_s, h_rows, wg, wu, wd)
    return acc.reshape(_HALF_TOK + 1, D_MODEL)


_FIN_TILE = 512


def _moe_finish_kernel(h_ref, rc_ref, rl_ref, x1_ref, mod_ref, sg_ref, su_ref, sd_ref, o_ref):
    t = pl.program_id(0)
    tiles_per_half = _HALF_TOK // _FIN_TILE
    is_ctx = t < tiles_per_half
    row = jnp.where(is_ctx, 0, 1 + (t - tiles_per_half) // (DEC_SEQ // _FIN_TILE))
    h = _bf(h_ref[...])
    shared = _dot(_bf(_silu(_dot(h, _bf(sg_ref[0]))) * _dot(h, _bf(su_ref[0]))), _bf(sd_ref[0]))
    routed = jnp.where(is_ctx, rc_ref[...], rl_ref[...])
    g2 = mod_ref[pl.ds(row, 1), 5 * D_MODEL:6 * D_MODEL]
    o_ref[...] = x1_ref[...] + g2 * (routed + shared)


def _moe_finish(layer, h2, routed_c, routed_l, x1, mod, sg, su, sd):
    tile = lambda t: (t, 0)
    tiles_per_half = _HALF_TOK // _FIN_TILE
    ctx_tile = lambda t: (jnp.minimum(t, tiles_per_half - 1), 0)
    lat_tile = lambda t: (jnp.maximum(t - tiles_per_half, 0), 0)
    shared = lambda t: (layer, 0, 0)
    return pl.pallas_call(
        _moe_finish_kernel,
        out_shape=jax.ShapeDtypeStruct((N_TOK, D_MODEL), _F32),
        grid=(N_TOK // _FIN_TILE,),
        in_specs=[
            pl.BlockSpec((_FIN_TILE, D_MODEL), tile),
            pl.BlockSpec((_FIN_TILE, D_MODEL), ctx_tile),
            pl.BlockSpec((_FIN_TILE, D_MODEL), lat_tile),
            pl.BlockSpec((_FIN_TILE, D_MODEL), tile),
            pl.BlockSpec((MOD_ROWS, 6 * D_MODEL), lambda t: (0, 0)),
            pl.BlockSpec((1, D_MODEL, SHARED_DIM), shared),
            pl.BlockSpec((1, D_MODEL, SHARED_DIM), shared),
            pl.BlockSpec((1, SHARED_DIM, D_MODEL), shared),
        ],
        out_specs=pl.BlockSpec((_FIN_TILE, D_MODEL), tile),
        compiler_params=_params(("arbitrary",)),
        name="moe_finish",
    )(h2, routed_c, routed_l, x1, mod, sg, su, sd)


def kernel(x_prompt, x_sample, state_gla, cache_na_k, cache_na_v, c, c_ctx, w_ada, b_ada, norm_mix, norm_ffn, w_in, gla_w_gate, gla_b_gate, gla_out_norm, conv_dw, conv_dw_b, conv_ln_g, conv_ln_b, conv_pw, na_q_norm, na_k_norm, na_rpb, w_out, router_w, router_bias, exp_w_gate, exp_w_up, exp_w_down, sh_w_gate, sh_w_up, sh_w_down):
    x_ctx, x_lat, x_lat_tile = x_prompt.reshape(N_CTX_TOK, D_MODEL), x_sample.reshape(N_LAT_TOK, D_MODEL), 0
    cvec = jnp.concatenate([c_ctx[None], c, jnp.zeros((MOD_ROWS - 1 - DEC_BATCH, D_MODEL), _F32)], axis=0)
    mod_all = _ada_mod(cvec, w_ada, b_ada)
    gla_consts = _gla_constants()
    zero_state = jnp.zeros((BATCH, 2, GLA_VAL, GLA_KEY), _F32)
    lat_first = N_CTX_TOK // DEC_SEQ

    states, keys, vals = [], [], []
    for l in range(DEPTH):
        mod = mod_all[l]
        wi = w_in[l]
        w_gla = _bf(jnp.pad(wi[:, :GLA_LR_OFF + 2 * GLA_GATE_RANK], ((0, 0), (0, GLA_IN_W - GLA_LR_OFF - 2 * GLA_GATE_RANK))))
        conv_off = GLA_LR_OFF + 2 * GLA_GATE_RANK
        w_conv = _bf(wi[:, conv_off:conv_off + 2 * CONV_CH])
        w_na = _bf(wi[:, conv_off + 2 * CONV_CH:])
        g_all, c_all, a_all = _in_proj(x_ctx, x_lat, x_lat_tile, norm_mix[l][None], mod, w_gla, w_conv, w_na)

        wz = jnp.zeros((GLA_IN_W - GLA_LR_OFF, 2 * GLA_KEY), _F32)
        wz = wz.at[:GLA_GATE_RANK, :GLA_KEY].set(gla_w_gate[l, 0])
        wz = wz.at[GLA_GATE_RANK:2 * GLA_GATE_RANK, GLA_KEY:].set(gla_w_gate[l, 1])
        bz = gla_b_gate[l].reshape(1, 2 * GLA_KEY)
        onorm = gla_out_norm[l][None]
        gla_c, st_c = _gla(g_all, wz, bz, gla_consts, onorm, zero_state, n=SEQ, n_seq=BATCH, first_block=0)
        gla_l, _ = _gla(g_all, wz, bz, gla_consts, onorm, _state_to_blockdiag(state_gla[:, l]),
                        n=DEC_SEQ, n_seq=DEC_BATCH, first_block=lat_first)

        conv_args = (conv_dw[l], conv_dw_b[l][None], conv_ln_g[l][None], conv_ln_b[l][None], _bf(conv_pw[l]))
        conv_c = _conv(c_all, *conv_args, n=SEQ, n_seq=BATCH, first_block=0)
        conv_l = _conv(c_all, *conv_args, n=DEC_SEQ, n_seq=DEC_BATCH, first_block=lat_first)

        qn, kn = na_q_norm[l][None], na_k_norm[l][None]
        na_c, k_l, v_l = _na_ctx(a_all, qn, kn)
        na_l = _na_lat(a_all, cache_na_k[:, l], cache_na_v[:, l], _na_bias_table(na_rpb[l]), qn, kn)

        x1, h2, logits_t = _out_proj(gla_c, gla_l, conv_c, conv_l, na_c, na_l, x_ctx, x_lat, x_lat_tile, mod,
                                     norm_ffn[l][None], _bf(w_out[l]), router_w[l].T)
        plan = _moe_plan(*_router(logits_t, router_bias[l][:, None]))
        h_rows = h2.reshape(N_TOK * _ROW_CHUNKS, _LANE)
        routed = [_gmm(l, half, plan, h_rows, exp_w_gate, exp_w_up, exp_w_down) for half in range(_N_HALVES)]
        x = _moe_finish(l, h2, routed[0], routed[1], x1, mod, sh_w_gate, sh_w_up, sh_w_down)
        x_ctx, x_lat, x_lat_tile = x, x, N_CTX_TILES

        states.append(_blockdiag_to_state(st_c))
        keys.append(k_l)
        vals.append(v_l)

    y_prompt = x[:N_CTX_TOK].reshape(BATCH, SEQ, D_MODEL)
    y_sample = x[N_CTX_TOK:].reshape(DEC_BATCH, DEC_SEQ, D_MODEL)
    return (y_prompt, y_sample, jnp.stack(states, axis=1), jnp.stack(keys, axis=1), jnp.stack(vals, axis=1))
```

```python
import functools

import numpy as np
import jax
import jax.numpy as jnp
from jax import lax
from jax.experimental import pallas as pl
from jax.experimental.pallas import tpu as pltpu

D_MODEL = 1024
BATCH = 16
SEQ = 256
DEPTH = 2
DEC_BATCH = 4
DEC_SEQ = 1024
PAST_LEN = 256
GRID_W = 64
GLA_HEADS = 4
GLA_DK = 64
GLA_DV = 128
GLA_KEY = GLA_HEADS * GLA_DK
GLA_VAL = GLA_HEADS * GLA_DV
GLA_GATE_RANK = 16
GLA_GATE_NORM = 16.0
GLA_CHUNK = 64
CONV_CH = 256
CONV_WIDTH = 31
NA_HEADS = 4
NA_HD = 64
NA_DIM = NA_HEADS * NA_HD
NA_WIN_ROWS = 8
NA_WIN_COLS = 16
N_EXPERTS = 64
TOP_K = 8
N_GROUPS = 8
TOPK_GROUPS = 4
EXPERT_DIM = 256
SHARED_DIM = 256
ROUTED_SCALE = 2.5
EPS = 1e-6

N_CTX_TOK = BATCH * SEQ
N_LAT_TOK = DEC_BATCH * DEC_SEQ
N_TOK = N_CTX_TOK + N_LAT_TOK
ROW_TILE = 256
N_CTX_TILES = N_CTX_TOK // ROW_TILE
TILES_PER_LAT_SEQ = DEC_SEQ // ROW_TILE
MOD_ROWS = 8
GLA_IN_W = 1664
GLA_LR_OFF = 2 * GLA_KEY + 2 * GLA_VAL
VMEM_LIMIT = 56 * 1024 * 1024

_BF = jnp.bfloat16
_F32 = jnp.float32


def _bf(x):
    return x.astype(_BF)


def _dot(a, b):
    return jnp.dot(a, b, preferred_element_type=_F32)


def _dot_nt(a, b):
    return lax.dot_general(a, b, (((1,), (1,)), ((), ())), preferred_element_type=_F32)


def _dot_tn(a, b):
    return lax.dot_general(a, b, (((0,), (0,)), ((), ())), preferred_element_type=_F32)


def _split2(x):
    hi = _bf(x)
    lo = _bf(x - hi.astype(_F32))
    return hi, lo


def _split3(x):
    hi = _bf(x)
    r = x - hi.astype(_F32)
    mid = _bf(r)
    lo = _bf(r - mid.astype(_F32))
    return hi, mid, lo


def _dot3(a, b, dot=_dot):
    a_hi, a_lo = _split2(a)
    b_hi, b_lo = _split2(b)
    return (dot(a_lo, b_hi) + dot(a_hi, b_lo)) + dot(a_hi, b_hi)


def _sigmoid(x):
    return 1.0 / (1.0 + jnp.exp(-x))


def _silu(x):
    return x * _sigmoid(x)


def _rms(x, w):
    return x * lax.rsqrt(jnp.mean(x * x, axis=-1, keepdims=True) + EPS) * w


def _params(sem):
    return pltpu.CompilerParams(dimension_semantics=sem, vmem_limit_bytes=VMEM_LIMIT)


def _mod_row(i):
    return jnp.where(i < N_CTX_TILES, 0, 1 + (i - N_CTX_TILES) // TILES_PER_LAT_SEQ)


def _ada_kernel(cv_ref, w_ref, b_ref, o_ref):
    o_ref[0] = _dot3(_silu(cv_ref[...]), w_ref[0]) + b_ref[0]


def _ada_mod(cvec, w_ada, b_ada):
    tn = 1024
    n_out = 6 * D_MODEL
    return pl.pallas_call(
        _ada_kernel,
        out_shape=jax.ShapeDtypeStruct((DEPTH, MOD_ROWS, n_out), _F32),
        grid=(DEPTH, n_out // tn),
        in_specs=[
            pl.BlockSpec((MOD_ROWS, D_MODEL), lambda l, j: (0, 0)),
            pl.BlockSpec((1, D_MODEL, tn), lambda l, j: (l, 0, j)),
            pl.BlockSpec((1, 1, tn), lambda l, j: (l, 0, j)),
        ],
        out_specs=pl.BlockSpec((1, MOD_ROWS, tn), lambda l, j: (l, 0, j)),
        compiler_params=_params(("arbitrary", "arbitrary")),
        name="ada_mod",
    )(cvec, w_ada, b_ada.reshape(DEPTH, 1, n_out))


def _inproj_kernel(xc_ref, xl_ref, nw_ref, mod_ref, wg_ref, wc_ref, wa_ref, g_ref, c_ref, a_ref):
    i = pl.program_id(0)
    row = _mod_row(i)
    sh = mod_ref[pl.ds(row, 1), 0:D_MODEL]
    sc = mod_ref[pl.ds(row, 1), D_MODEL:2 * D_MODEL]
    x = jnp.where(i < N_CTX_TILES, xc_ref[...], xl_ref[...])
    h = _bf(_rms(x, nw_ref[...]) * (1.0 + sc) + sh)
    g_ref[...] = _dot(h, wg_ref[...])
    c_ref[...] = _dot(h, wc_ref[...])
    a_ref[...] = _dot(h, wa_ref[...])


def _stream_tiles(lat_first_tile):
    ctx_tile = lambda i: (jnp.minimum(i, N_CTX_TILES - 1), 0)
    lat_tile = lambda i: (jnp.maximum(i - N_CTX_TILES, 0) + lat_first_tile, 0)
    return ctx_tile, lat_tile


def _in_proj(x_ctx, x_lat, lat_first_tile, norm_w, mod, w_gla, w_conv, w_na):
    full = lambda i: (0, 0)
    tile = lambda i: (i, 0)
    ctx_tile, lat_tile = _stream_tiles(lat_first_tile)
    return pl.pallas_call(
        _inproj_kernel,
        out_shape=(
            jax.ShapeDtypeStruct((N_TOK, GLA_IN_W), _F32),
            jax.ShapeDtypeStruct((N_TOK, 2 * CONV_CH), _F32),
            jax.ShapeDtypeStruct((N_TOK, 3 * NA_DIM), _F32),
        ),
        grid=(N_TOK // ROW_TILE,),
        in_specs=[
            pl.BlockSpec((ROW_TILE, D_MODEL), ctx_tile),
            pl.BlockSpec((ROW_TILE, D_MODEL), lat_tile),
            pl.BlockSpec((1, D_MODEL), full),
            pl.BlockSpec((MOD_ROWS, 6 * D_MODEL), full),
            pl.BlockSpec((D_MODEL, GLA_IN_W), full),
            pl.BlockSpec((D_MODEL, 2 * CONV_CH), full),
            pl.BlockSpec((D_MODEL, 3 * NA_DIM), full),
        ],
        out_specs=(
            pl.BlockSpec((ROW_TILE, GLA_IN_W), tile),
            pl.BlockSpec((ROW_TILE, 2 * CONV_CH), tile),
            pl.BlockSpec((ROW_TILE, 3 * NA_DIM), tile),
        ),
        compiler_params=_params(("arbitrary",)),
        name="in_proj",
    )(x_ctx, x_lat, norm_w, mod, w_gla, w_conv, w_na)


_GLA_LEVELS = (32, 16, 8, 4, 2, 1)
_N_EXP_BLOCKS = 2 + len(_GLA_LEVELS)
_N_MASKS = len(_GLA_LEVELS) + 1


def _gla_constants():
    cs = GLA_CHUNK
    r = np.arange(cs)
    i = r[:, None]
    c = r[None, :]
    w = np.zeros((2, _N_EXP_BLOCKS, cs, cs), np.float32)
    m = np.zeros((2, _N_MASKS, cs, cs), np.float32)
    w[0, 0] = c <= i
    w[0, 1] = c > i
    w[1, 0] = c >= i
    w[1, 1] = c < i
    for lv, half in enumerate(_GLA_LEVELS):
        mid = (r // (2 * half)) * (2 * half) + half
        mi = mid[:, None]
        second = (r >= mid)[:, None]
        same = (r[:, None] // (2 * half)) == (r[None, :] // (2 * half))
        w[0, 2 + lv] = np.where(second, (c >= mi) & (c <= i), (c > i) & (c <= mi - 1))
        w[1, 2 + lv] = np.where(second, (c >= mi) & (c <= i - 1), (c >= i) & (c <= mi - 1))
        m[0, lv] = same & (i >= mi) & (c < mi)
        m[1, lv] = same & (i < mi) & (c >= mi)
    m[:, _N_MASKS - 1] = np.eye(cs)
    w_all = w.reshape(2, _N_EXP_BLOCKS * cs, cs)
    lmask = np.tile(m, (1, 1, 1, GLA_HEADS))
    kr = np.arange(GLA_HEADS * cs)
    bdk = (kr[:, None] // cs) == (np.arange(GLA_KEY)[None, :] // GLA_DK)
    bdv = (kr[:, None] // cs) == (np.arange(GLA_VAL)[None, :] // GLA_DV)
    bdst = (np.arange(GLA_VAL)[:, None] // GLA_DV) == (np.arange(GLA_KEY)[None, :] // GLA_DK)
    return (jnp.asarray(w_all, _BF), jnp.asarray(lmask, _F32), jnp.asarray(bdk, _F32),
            jnp.asarray(bdv, _F32), jnp.asarray(bdst, _F32))


def _gla_kernel(g_ref, wz_ref, bz_ref, wall_ref, lmask_ref, bdk_ref, bdv_ref, bdst_ref, onorm_ref, st0_ref,
                o_ref, stfin_ref, la_scr, o_scr, st_scr, *, n):
    cs = GLA_CHUNK
    nc = n // cs
    gate_rows = 128

    def log_decays(t, carry):
        rows = pl.ds(pl.multiple_of(t * gate_rows, gate_rows), gate_rows)
        z = _dot3(g_ref[rows, GLA_LR_OFF:GLA_IN_W], wz_ref[...]) + bz_ref[...]
        la_scr[rows, :] = (jnp.minimum(z, 0.0) - jnp.log1p(jnp.exp(-jnp.abs(z)))) * (1.0 / GLA_GATE_NORM)
        return carry

    lax.fori_loop(0, n // gate_rows, log_decays, 0)

    def chunk(c, d):
        rows = pl.ds(pl.multiple_of(c * cs, cs), cs)
        q = g_ref[rows, 0:GLA_KEY] * (GLA_DK ** -0.5)
        k = g_ref[rows, GLA_KEY:2 * GLA_KEY]
        v = g_ref[rows, 2 * GLA_KEY:2 * GLA_KEY + GLA_VAL]
        la_hi, la_mid, la_lo = _split3(la_scr[rows, d * GLA_KEY:(d + 1) * GLA_KEY])
        w = wall_ref[d]
        f = jnp.exp((_dot(w, la_lo) + _dot(w, la_mid)) + _dot(w, la_hi))
        st = st_scr[d]
        o = _dot_nt(_bf(q * f[0:cs]), _bf(st))
        p = jnp.zeros((cs, GLA_HEADS * cs), _F32)
        for lv in range(_N_MASKS):
            if lv < len(_GLA_LEVELS):
                fl = f[(2 + lv) * cs:(3 + lv) * cs]
                ql, kl = q * fl, k * fl
            else:
                ql, kl = q, k
            k_bd = _bf(jnp.concatenate([kl] * GLA_HEADS, axis=0) * bdk_ref[...])
            p = p + lmask_ref[d, lv] * _dot_nt(_bf(ql), k_bd)
        v_bd = _bf(jnp.concatenate([v] * GLA_HEADS, axis=0) * bdv_ref[...])
        o_scr[d, rows, :] = o + _dot(_bf(p), v_bd)
        decay = f[cs - 1:cs] if d == 0 else f[0:1]
        st_scr[d] = st * decay + _dot_tn(_bf(v), _bf(k * f[cs:2 * cs])) * bdst_ref[...]

    st_scr[...] = st0_ref[0]

    def scan(i, carry):
        chunk(i, 0)
        chunk(nc - 1 - i, 1)
        return carry

    lax.fori_loop(0, nc, scan, 0)
    stfin_ref[0] = st_scr[...]

    def finish(c, carry):
        rows = pl.ds(pl.multiple_of(c * cs, cs), cs)
        o = o_scr[1, rows, :] + o_scr[0, rows, :]
        for h in range(GLA_HEADS):
            cols = slice(h * GLA_DV, (h + 1) * GLA_DV)
            gate = g_ref[rows, 2 * GLA_KEY + GLA_VAL + h * GLA_DV:2 * GLA_KEY + GLA_VAL + (h + 1) * GLA_DV]
            o_ref[rows, cols] = _rms(o[:, cols], onorm_ref[...]) * _silu(gate)
        return carry

    lax.fori_loop(0, nc, finish, 0)


def _gla(g_all, wz, bz, consts, onorm, st0, *, n, n_seq, first_block):
    w_all, lmask, bdk, bdv, bdst = consts
    full2 = lambda i: (0, 0)
    return pl.pallas_call(
        functools.partial(_gla_kernel, n=n),
        out_shape=(
            jax.ShapeDtypeStruct((n_seq * n, GLA_VAL), _F32),
            jax.ShapeDtypeStruct((n_seq, 2, GLA_VAL, GLA_KEY), _F32),
        ),
        grid=(n_seq,),
        in_specs=[
            pl.BlockSpec((n, GLA_IN_W), lambda i: (first_block + i, 0)),
            pl.BlockSpec(wz.shape, full2),
            pl.BlockSpec(bz.shape, full2),
            pl.BlockSpec(w_all.shape, lambda i: (0, 0, 0)),
            pl.BlockSpec(lmask.shape, lambda i: (0, 0, 0, 0)),
            pl.BlockSpec(bdk.shape, full2),
            pl.BlockSpec(bdv.shape, full2),
            pl.BlockSpec(bdst.shape, full2),
            pl.BlockSpec((1, GLA_DV), full2),
            pl.BlockSpec((1, 2, GLA_VAL, GLA_KEY), lambda i: (i, 0, 0, 0)),
        ],
        out_specs=(
            pl.BlockSpec((n, GLA_VAL), lambda i: (i, 0)),
            pl.BlockSpec((1, 2, GLA_VAL, GLA_KEY), lambda i: (i, 0, 0, 0)),
        ),
        scratch_shapes=[pltpu.VMEM((n, 2 * GLA_KEY), _F32), pltpu.VMEM((2, n, GLA_VAL), _F32),
                        pltpu.VMEM((2, GLA_VAL, GLA_KEY), _F32)],
        compiler_params=_params(("arbitrary",)),
        name=f"gla_{n}",
    )(g_all, wz, bz, w_all, lmask, bdk, bdv, bdst, onorm, st0)


def _state_to_blockdiag(s):
    b = s.shape[0]
    st = jnp.swapaxes(s, -1, -2)
    eye = jnp.eye(GLA_HEADS, dtype=s.dtype)
    out = st[:, :, :, :, None, :] * eye[None, None, :, None, :, None]
    return out.reshape(b, 2, GLA_VAL, GLA_KEY)


def _blockdiag_to_state(st):
    b = st.shape[0]
    s6 = st.reshape(b, 2, GLA_HEADS, GLA_DV, GLA_HEADS, GLA_DK)
    diag = jnp.stack([s6[:, :, h, :, h, :] for h in range(GLA_HEADS)], axis=2)
    return jnp.swapaxes(diag, -1, -2)


_CONV_PAD = 16
_CONV_ROWS = 128


def _conv_kernel(c_ref, dw_ref, dwb_ref, lng_ref, lnb_ref, pw_ref, o_ref, pad_scr, *, n):
    zeros = jnp.zeros((_CONV_PAD, CONV_CH), _F32)
    pad_scr[0:_CONV_PAD, :] = zeros
    pad_scr[_CONV_PAD + n:2 * _CONV_PAD + n, :] = zeros
    for r0 in range(0, n, _CONV_ROWS):
        a = c_ref[r0:r0 + _CONV_ROWS, :]
        pad_scr[_CONV_PAD + r0:_CONV_PAD + r0 + _CONV_ROWS, :] = a[:, :CONV_CH] * _sigmoid(a[:, CONV_CH:])
    half = CONV_WIDTH // 2
    for r0 in range(0, n, _CONV_ROWS):
        acc = jnp.zeros((_CONV_ROWS, CONV_CH), _F32)
        for w in range(CONV_WIDTH):
            s = _CONV_PAD + r0 + w - half
            acc = acc + pad_scr[s:s + _CONV_ROWS, :] * dw_ref[w:w + 1, :]
        acc = acc + dwb_ref[...]
        xc = acc - jnp.mean(acc, axis=-1, keepdims=True)
        y = xc * lax.rsqrt(jnp.mean(xc * xc, axis=-1, keepdims=True) + EPS) * lng_ref[...] + lnb_ref[...]
        o_ref[r0:r0 + _CONV_ROWS, :] = _dot(_bf(_silu(y)), pw_ref[...])


def _conv(c_all, dw, dwb, lng, lnb, pw, *, n, n_seq, first_block):
    full2 = lambda i: (0, 0)
    return pl.pallas_call(
        functools.partial(_conv_kernel, n=n),
        out_shape=jax.ShapeDtypeStruct((n_seq * n, CONV_CH), _F32),
        grid=(n_seq,),
        in_specs=[
            pl.BlockSpec((n, 2 * CONV_CH), lambda i: (first_block + i, 0)),
            pl.BlockSpec((CONV_WIDTH, CONV_CH), full2),
            pl.BlockSpec((1, CONV_CH), full2),
            pl.BlockSpec((1, CONV_CH), full2),
            pl.BlockSpec((1, CONV_CH), full2),
            pl.BlockSpec((CONV_CH, CONV_CH), full2),
        ],
        out_specs=pl.BlockSpec((n, CONV_CH), lambda i: (i, 0)),
        scratch_shapes=[pltpu.VMEM((n + 2 * _CONV_PAD, CONV_CH), _F32)],
        compiler_params=_params(("arbitrary",)),
        name=f"conv_{n}",
    )(c_all, dw, dwb, lng, lnb, pw)


def _softmax_rows(s):
    e = jnp.exp(s - jnp.max(s, axis=-1, keepdims=True))
    return e / jnp.sum(e, axis=-1, keepdims=True)


def _na_ctx_kernel(a_ref, qn_ref, kn_ref, o_ref, kc_ref, vc_ref):
    for h in range(NA_HEADS):
        cols = slice(h * NA_HD, (h + 1) * NA_HD)
        q = _rms(a_ref[:, h * NA_HD:(h + 1) * NA_HD], qn_ref[...])
        k = _rms(a_ref[:, NA_DIM + h * NA_HD:NA_DIM + (h + 1) * NA_HD], kn_ref[...])
        v = a_ref[:, 2 * NA_DIM + h * NA_HD:2 * NA_DIM + (h + 1) * NA_HD]
        kc_ref[0, h] = k
        vc_ref[0, h] = v
        p = _softmax_rows(_dot_nt(_bf(q), _bf(k)) * (NA_HD ** -0.5))
        o_ref[:, cols] = _dot(_bf(p), _bf(v))


def _na_ctx(a_all, qn, kn):
    full2 = lambda i: (0, 0)
    cache = jax.ShapeDtypeStruct((BATCH, NA_HEADS, SEQ, NA_HD), _F32)
    cache_spec = pl.BlockSpec((1, NA_HEADS, SEQ, NA_HD), lambda i: (i, 0, 0, 0))
    return pl.pallas_call(
        _na_ctx_kernel,
        out_shape=(jax.ShapeDtypeStruct((N_CTX_TOK, NA_DIM), _F32), cache, cache),
        grid=(BATCH,),
        in_specs=[
            pl.BlockSpec((SEQ, 3 * NA_DIM), lambda i: (i, 0)),
            pl.BlockSpec((1, NA_HD), full2),
            pl.BlockSpec((1, NA_HD), full2),
        ],
        out_specs=(pl.BlockSpec((SEQ, NA_DIM), lambda i: (i, 0)), cache_spec, cache_spec),
        compiler_params=_params(("arbitrary",)),
        name="na_ctx",
    )(a_all, qn, kn)


_NA_ROWS = DEC_SEQ // GRID_W
_NA_KEYS = NA_WIN_ROWS * GRID_W
_NA_VARIANTS = NA_WIN_ROWS
_NA_ROW_UNROLL = 8


_RPB_ROWS = 2 * NA_WIN_ROWS - 1
_RPB_COLS = 2 * NA_WIN_COLS - 1


def _na_bias_constants():
    qc = np.arange(GRID_W)[:, None]
    kc = np.arange(GRID_W)[None, :]
    shift = np.stack([(kc - qc + NA_WIN_COLS - 1) == co for co in range(_RPB_COLS)]).astype(np.float32)
    win_start = np.clip(qc - NA_WIN_COLS // 2, 0, GRID_W - NA_WIN_COLS)
    in_win = ((kc >= win_start) & (kc < win_start + NA_WIN_COLS)).astype(np.float32)
    return jnp.asarray(shift), jnp.asarray(in_win)


def _na_bias_kernel(rpb_ref, shift_ref, win_ref, o_ref):
    h = pl.program_id(0)
    in_win = win_ref[...] > 0.0
    tiles = []
    for ro in range(_RPB_ROWS):
        base = (h * _RPB_ROWS + ro) * _RPB_COLS
        acc = rpb_ref[base] * shift_ref[0]
        for co in range(1, _RPB_COLS):
            acc = acc + rpb_ref[base + co] * shift_ref[co]
        tiles.append(jnp.where(in_win, acc, -jnp.inf))
    for t in range(_NA_VARIANTS):
        o_ref[0, t] = jnp.concatenate([tiles[kr - t + NA_WIN_ROWS - 1] for kr in range(NA_WIN_ROWS)], axis=1)


def _na_bias_table(rpb):
    shift, in_win = _na_bias_constants()
    return pl.pallas_call(
        _na_bias_kernel,
        out_shape=jax.ShapeDtypeStruct((NA_HEADS, _NA_VARIANTS, GRID_W, _NA_KEYS), _F32),
        grid=(NA_HEADS,),
        in_specs=[
            pl.BlockSpec(memory_space=pltpu.SMEM),
            pl.BlockSpec(shift.shape, lambda h: (0, 0, 0)),
            pl.BlockSpec(in_win.shape, lambda h: (0, 0)),
        ],
        out_specs=pl.BlockSpec((1, _NA_VARIANTS, GRID_W, _NA_KEYS), lambda h: (h, 0, 0, 0)),
        compiler_params=_params(("arbitrary",)),
        name="na_bias",
    )(rpb.reshape(-1), shift, in_win)


def _na_lat_kernel(a_ref, kctx_ref, vctx_ref, bias_ref, qn_ref, kn_ref, o_ref, q_scr, k_scr, v_scr):
    scale = NA_HD ** -0.5
    for h in range(NA_HEADS):
        q_scr[...] = _bf(_rms(a_ref[:, h * NA_HD:(h + 1) * NA_HD], qn_ref[...]))
        k_scr[...] = _bf(_rms(a_ref[:, NA_DIM + h * NA_HD:NA_DIM + (h + 1) * NA_HD], kn_ref[...]))
        v_scr[...] = _bf(a_ref[:, 2 * NA_DIM + h * NA_HD:2 * NA_DIM + (h + 1) * NA_HD])
        k_ctx = _bf(kctx_ref[0, h])
        v_ctx = _bf(vctx_ref[0, h])

        def row(r):
            start = jnp.clip(r - NA_WIN_ROWS // 2, 0, _NA_ROWS - NA_WIN_ROWS)
            qrows = pl.ds(pl.multiple_of(r * GRID_W, GRID_W), GRID_W)
            krows = pl.ds(pl.multiple_of(start * GRID_W, GRID_W), _NA_KEYS)
            q = q_scr[qrows, :]
            s_loc = _dot_nt(q, k_scr[krows, :]) * scale + bias_ref[h, r - start]
            s_ctx = _dot_nt(q, k_ctx) * scale
            m = jnp.maximum(jnp.max(s_loc, axis=-1, keepdims=True), jnp.max(s_ctx, axis=-1, keepdims=True))
            e_loc = jnp.exp(s_loc - m)
            e_ctx = jnp.exp(s_ctx - m)
            inv = 1.0 / (jnp.sum(e_loc, axis=-1, keepdims=True) + jnp.sum(e_ctx, axis=-1, keepdims=True))
            o = _dot(_bf(e_loc * inv), v_scr[krows, :]) + _dot(_bf(e_ctx * inv), v_ctx)
            o_ref[qrows, h * NA_HD:(h + 1) * NA_HD] = o

        def row_group(g, carry):
            for dr in range(_NA_ROW_UNROLL):
                row(g * _NA_ROW_UNROLL + dr)
            return carry

        lax.fori_loop(0, _NA_ROWS // _NA_ROW_UNROLL, row_group, 0)


def _na_lat(a_all, k_ctx, v_ctx, bias, qn, kn):
    full2 = lambda i: (0, 0)
    ctx_spec = pl.BlockSpec((1, NA_HEADS, PAST_LEN, NA_HD), lambda i: (i, 0, 0, 0))
    return pl.pallas_call(
        _na_lat_kernel,
        out_shape=jax.ShapeDtypeStruct((N_LAT_TOK, NA_DIM), _F32),
        grid=(DEC_BATCH,),
        in_specs=[
            pl.BlockSpec((DEC_SEQ, 3 * NA_DIM), lambda i: (N_CTX_TOK // DEC_SEQ + i, 0)),
            ctx_spec,
            ctx_spec,
            pl.BlockSpec(bias.shape, lambda i: (0, 0, 0, 0)),
            pl.BlockSpec((1, NA_HD), full2),
            pl.BlockSpec((1, NA_HD), full2),
        ],
        out_specs=pl.BlockSpec((DEC_SEQ, NA_DIM), lambda i: (i, 0)),
        scratch_shapes=[pltpu.VMEM((DEC_SEQ, NA_HD), _BF)] * 3,
        compiler_params=_params(("arbitrary",)),
        name="na_lat",
    )(a_all, k_ctx, v_ctx, bias, qn, kn)


def _outproj_kernel(gc_ref, gl_ref, cc_ref, cl_ref, nc_ref, nl_ref, xc_ref, xl_ref, mod_ref, nw_ref, wo_ref,
                    rwt_ref, x1_ref, h2_ref, hr_ref, lg_ref):
    i = pl.program_id(0)
    is_ctx = i < N_CTX_TILES
    row = _mod_row(i)
    gla = jnp.where(is_ctx, gc_ref[...], gl_ref[...])
    conv = jnp.where(is_ctx, cc_ref[...], cl_ref[...])
    na = jnp.where(is_ctx, nc_ref[...], nl_ref[...])
    mix = (_dot(_bf(gla), wo_ref[0:GLA_VAL, :])
           + _dot(_bf(conv), wo_ref[GLA_VAL:GLA_VAL + CONV_CH, :])
           + _dot(_bf(na), wo_ref[GLA_VAL + CONV_CH:, :]))
    g1 = mod_ref[pl.ds(row, 1), 2 * D_MODEL:3 * D_MODEL]
    sh2 = mod_ref[pl.ds(row, 1), 3 * D_MODEL:4 * D_MODEL]
    sc2 = mod_ref[pl.ds(row, 1), 4 * D_MODEL:5 * D_MODEL]
    x1 = jnp.where(is_ctx, xc_ref[...], xl_ref[...]) + g1 * mix
    h2 = _rms(x1, nw_ref[...]) * (1.0 + sc2) + sh2
    x1_ref[...] = x1
    h2_ref[...] = _bf(h2)
    for j in range(_ROW_CHUNKS):
        hr_ref[pl.ds(j, ROW_TILE, stride=_ROW_CHUNKS), :] = h2[:, j * _LANE:(j + 1) * _LANE]
    lg_ref[...] = _dot3(rwt_ref[...], h2, dot=_dot_nt)


def _out_proj(gla_c, gla_l, conv_c, conv_l, na_c, na_l, x_ctx, x_lat, lat_first_tile, mod, norm_w, w_out,
              router_wt):
    full2 = lambda i: (0, 0)
    tile = lambda i: (i, 0)
    ctx_tile, lat_tile = _stream_tiles(0)
    _, x_lat_tile = _stream_tiles(lat_first_tile)
    return pl.pallas_call(
        _outproj_kernel,
        out_shape=(
            jax.ShapeDtypeStruct((N_TOK, D_MODEL), _F32),
            jax.ShapeDtypeStruct((N_TOK, D_MODEL), _BF),
            jax.ShapeDtypeStruct((N_TOK * _ROW_CHUNKS, _LANE), _F32),
            jax.ShapeDtypeStruct((N_EXPERTS, N_TOK), _F32),
        ),
        grid=(N_TOK // ROW_TILE,),
        in_specs=[
            pl.BlockSpec((ROW_TILE, GLA_VAL), ctx_tile),
            pl.BlockSpec((ROW_TILE, GLA_VAL), lat_tile),
            pl.BlockSpec((ROW_TILE, CONV_CH), ctx_tile),
            pl.BlockSpec((ROW_TILE, CONV_CH), lat_tile),
            pl.BlockSpec((ROW_TILE, NA_DIM), ctx_tile),
            pl.BlockSpec((ROW_TILE, NA_DIM), lat_tile),
            pl.BlockSpec((ROW_TILE, D_MODEL), ctx_tile),
            pl.BlockSpec((ROW_TILE, D_MODEL), x_lat_tile),
            pl.BlockSpec((MOD_ROWS, 6 * D_MODEL), full2),
            pl.BlockSpec((1, D_MODEL), full2),
            pl.BlockSpec((D_MODEL, D_MODEL), full2),
            pl.BlockSpec((N_EXPERTS, D_MODEL), full2),
        ],
        out_specs=(
            pl.BlockSpec((ROW_TILE, D_MODEL), tile),
            pl.BlockSpec((ROW_TILE, D_MODEL), tile),
            pl.BlockSpec((ROW_TILE * _ROW_CHUNKS, _LANE), tile),
            pl.BlockSpec((N_EXPERTS, ROW_TILE), lambda i: (0, i)),
        ),
        compiler_params=_params(("arbitrary",)),
        name="out_proj",
    )(gla_c, gla_l, conv_c, conv_l, na_c, na_l, x_ctx, x_lat, mod, norm_w, w_out, router_wt)


_PER_GROUP = N_EXPERTS // N_GROUPS
_ROUTE_TILE = 1024


def _first_max(x, idx, axes, sentinel):
    m = x
    for ax in axes:
        m = jnp.max(m, axis=ax, keepdims=True)
    first = jnp.where(x == m, idx, sentinel)
    for ax in axes:
        first = jnp.min(first, axis=ax, keepdims=True)
    return m, first


def _router_kernel(lg_ref, rb_ref, eid_ref, wts_ref, cnt_ref):
    t = lg_ref.shape[1]
    shape3 = (N_GROUPS, _PER_GROUP, t)
    scores = _sigmoid(lg_ref[...])
    biased = (scores + rb_ref[...]).reshape(shape3)
    scores = scores.reshape(shape3)
    neg = -jnp.inf
    in_grp = lax.broadcasted_iota(jnp.int32, shape3, 1)
    grp = lax.broadcasted_iota(jnp.int32, (N_GROUPS, 1, t), 0)
    expert = lax.broadcasted_iota(jnp.int32, shape3, 0) * _PER_GROUP + in_grp
    m1, i1 = _first_max(biased, in_grp, (1,), _PER_GROUP)
    m2 = jnp.max(jnp.where(in_grp == i1, neg, biased), axis=1, keepdims=True)
    gscore = m1 + m2
    keep = jnp.zeros((N_GROUPS, 1, t), _F32)
    for _ in range(TOPK_GROUPS):
        _, gi = _first_max(gscore, grp, (0,), N_GROUPS)
        hit = grp == gi
        keep = jnp.where(hit, 1.0, keep)
        gscore = jnp.where(hit, neg, gscore)
    masked = jnp.where(keep > 0.0, biased, neg)
    ids, picked = [], []
    hits = jnp.zeros(shape3, _F32)
    for _ in range(TOP_K):
        _, ei = _first_max(masked, expert, (1, 0), N_EXPERTS)
        hit = expert == ei
        hits = jnp.where(hit, 1.0, hits)
        sc = jnp.sum(jnp.sum(jnp.where(hit, scores, 0.0), axis=1, keepdims=True), axis=0, keepdims=True)
        ids.append(ei.reshape(1, t))
        picked.append(sc.reshape(1, t))
        masked = jnp.where(hit, neg, masked)
    picked = jnp.concatenate(picked, axis=0)
    den = jnp.sum(picked, axis=0, keepdims=True)
    eid_ref[...] = jnp.concatenate(ids, axis=0)
    wts_ref[...] = picked / den * ROUTED_SCALE
    cnt_ref[0] = jnp.sum(hits, axis=2, keepdims=True).reshape(N_EXPERTS, 1)


def _router(logits_t, router_bias):
    tile = lambda i: (0, i)
    return pl.pallas_call(
        _router_kernel,
        out_shape=(jax.ShapeDtypeStruct((TOP_K, N_TOK), jnp.int32), jax.ShapeDtypeStruct((TOP_K, N_TOK), _F32),
                   jax.ShapeDtypeStruct((N_TOK // _ROUTE_TILE, N_EXPERTS, 1), _F32)),
        grid=(N_TOK // _ROUTE_TILE,),
        in_specs=[
            pl.BlockSpec((N_EXPERTS, _ROUTE_TILE), tile),
            pl.BlockSpec((N_EXPERTS, 1), lambda i: (0, 0)),
        ],
        out_specs=(pl.BlockSpec((TOP_K, _ROUTE_TILE), tile), pl.BlockSpec((TOP_K, _ROUTE_TILE), tile),
                   pl.BlockSpec((1, N_EXPERTS, 1), lambda i: (i, 0, 0))),
        compiler_params=_params(("arbitrary",)),
        name="router",
    )(logits_t, router_bias)


_HALF_TOK = N_CTX_TOK
_N_HALVES = N_TOK // _HALF_TOK
_HALF_ASSIGN = _HALF_TOK * TOP_K
_GB = 256
_GB_MAX = _HALF_ASSIGN // _GB + N_EXPERTS
_LANE = 128
_ROW_CHUNKS = D_MODEL // _LANE
_GS = _GB + 8
_RMW_BATCH = 8
_TOK_BITS = 12
assert _HALF_TOK == 1 << _TOK_BITS


def _moe_plan(eid, wts, tile_counts):
    tok = jnp.arange(N_TOK, dtype=jnp.int32)
    key = (((tok >> _TOK_BITS) * N_EXPERTS)[None, :] + eid) * _HALF_TOK + (tok & (_HALF_TOK - 1))[None, :]
    key_s, w_s = lax.sort((key.reshape(-1), wts.reshape(-1)), num_keys=1)
    count = jnp.sum(tile_counts.reshape(_N_HALVES, -1, N_EXPERTS), axis=1).astype(jnp.int32)
    row_end = jnp.cumsum(count.reshape(-1)).reshape(_N_HALVES, N_EXPERTS)
    row_start = row_end - count
    nblk = (count + _GB - 1) // _GB
    blk_end = jnp.cumsum(nblk, axis=1)
    blk_start = blk_end - nblk
    b = jnp.arange(_GB_MAX, dtype=jnp.int32)
    n_used = blk_end[:, -1]
    b_eff = jnp.minimum(b[None, :], jnp.maximum(n_used - 1, 0)[:, None])
    grp = jnp.sum((b_eff[:, :, None] >= blk_end[:, None, :]).astype(jnp.int32), axis=-1)
    onehot = (grp[:, :, None] == jnp.arange(N_EXPERTS, dtype=jnp.int32)).astype(jnp.int32)
    pick = lambda v: jnp.sum(onehot * v[:, None, :], axis=-1)
    within = b[None, :] - pick(blk_start)
    start = pick(row_start) + within * _GB
    length = jnp.clip(pick(count) - within * _GB, 0, _GB)
    pad = jnp.zeros((_GB,), jnp.int32)
    return (jnp.concatenate([key_s, pad]), jnp.concatenate([w_s, pad.astype(_F32)]),
            grp.astype(jnp.int32), start.astype(jnp.int32), length.astype(jnp.int32), n_used.astype(jnp.int32))


def _gmm_kernel(be_ref, bstart_ref, blen_ref, nused_ref, key_ref, gate_ref,
                h_ref, wg_ref, wu_ref, wd_ref, acc_ref, xt_scr, yt_scr, wgb_scr, wub_scr, wdb_scr):
    b = pl.program_id(0)

    @pl.when(b == 0)
    def _():
        acc_ref[...] = jnp.zeros_like(acc_ref)
        xt_scr[...] = jnp.zeros_like(xt_scr)

    @pl.when((b == 0) | (be_ref[b] != be_ref[jnp.maximum(b - 1, 0)]))
    def _():
        wgb_scr[...] = _bf(wg_ref[0, 0])
        wub_scr[...] = _bf(wu_ref[0, 0])
        wdb_scr[...] = _bf(wd_ref[0, 0])

    @pl.when(b < nused_ref[0])
    def _():
        start = bstart_ref[b]
        length = blen_ref[b]
        for mi in range(_GB):
            tok = key_ref[start + mi] & (_HALF_TOK - 1)
            src = pl.ds(pl.multiple_of(tok * _ROW_CHUNKS, _ROW_CHUNKS), _ROW_CHUNKS)
            xt_scr[pl.ds(mi, _ROW_CHUNKS, stride=_GS), :] = h_ref[src, :]
        x = _bf(jnp.concatenate([xt_scr[j * _GS:j * _GS + _GB, :] for j in range(_ROW_CHUNKS)], axis=-1))
        act = _silu(_dot(x, wgb_scr[...])) * _dot(x, wub_scr[...])
        y = _dot(_bf(act), wdb_scr[...])
        for j in range(_ROW_CHUNKS):
            yt_scr[j * _GS:j * _GS + _GB, :] = y[:, j * _LANE:(j + 1) * _LANE]
        for m0 in range(0, _GB, _RMW_BATCH):
            pending = []
            for mi in range(m0, m0 + _RMW_BATCH):
                valid = mi < length
                tok = jnp.where(valid, key_ref[start + mi] & (_HALF_TOK - 1), _HALF_TOK)
                gate = jnp.where(valid, gate_ref[start + mi], 0.0)
                rows = pl.ds(pl.multiple_of(tok * _ROW_CHUNKS, _ROW_CHUNKS), _ROW_CHUNKS)
                pending.append((rows, acc_ref[rows, :] + gate * yt_scr[pl.ds(mi, _ROW_CHUNKS, stride=_GS), :]))
            for rows, val in pending:
                acc_ref[rows, :] = val


def _gmm(layer, half, plan, h_rows, wg, wu, wd):
    key_s, w_s, grp, start, length, n_used = plan
    expert = lambda b, be, *_: (layer, be[b], 0, 0)
    acc_rows = (_HALF_TOK + 1) * _ROW_CHUNKS
    grid_spec = pltpu.PrefetchScalarGridSpec(
        num_scalar_prefetch=6,
        grid=(_GB_MAX,),
        in_specs=[
            pl.BlockSpec((_HALF_TOK * _ROW_CHUNKS, _LANE), lambda b, *_: (half, 0), pipeline_mode=pl.Buffered(1)),
            pl.BlockSpec((1, 1, D_MODEL, EXPERT_DIM), expert),
            pl.BlockSpec((1, 1, D_MODEL, EXPERT_DIM), expert),
            pl.BlockSpec((1, 1, EXPERT_DIM, D_MODEL), expert),
        ],
        out_specs=pl.BlockSpec((acc_rows, _LANE), lambda b, *_: (0, 0), pipeline_mode=pl.Buffered(1)),
        scratch_shapes=[pltpu.VMEM((_ROW_CHUNKS * _GS, _LANE), _F32)] * 2
        + [pltpu.VMEM((D_MODEL, EXPERT_DIM), _BF)] * 2 + [pltpu.VMEM((EXPERT_DIM, D_MODEL), _BF)],
    )
    acc = pl.pallas_call(
        _gmm_kernel,
        out_shape=jax.ShapeDtypeStruct((acc_rows, _LANE), _F32),
        grid_spec=grid_spec,
        compiler_params=_params(("arbitrary",)),
        name="moe_experts",
    )(grp[half], start[half], length[half], n_used[half:half + 1], key_s, w_s, h_rows, wg, wu, wd)
    return acc


_FIN_TILE = 512


def _moe_finish_kernel(h_ref, rc_ref, rl_ref, x1_ref, mod_ref, sg_ref, su_ref, sd_ref, o_ref):
    t = pl.program_id(0)
    tiles_per_half = _HALF_TOK // _FIN_TILE
    is_ctx = t < tiles_per_half
    row = jnp.where(is_ctx, 0, 1 + (t - tiles_per_half) // (DEC_SEQ // _FIN_TILE))
    h = h_ref[...]
    shared = _dot(_bf(_silu(_dot(h, _bf(sg_ref[0]))) * _dot(h, _bf(su_ref[0]))), _bf(sd_ref[0]))
    routed = jnp.concatenate(
        [jnp.where(is_ctx, rc_ref[pl.ds(j, _FIN_TILE, stride=_ROW_CHUNKS), :],
                   rl_ref[pl.ds(j, _FIN_TILE, stride=_ROW_CHUNKS), :]) for j in range(_ROW_CHUNKS)], axis=-1)
    g2 = mod_ref[pl.ds(row, 1), 5 * D_MODEL:6 * D_MODEL]
    o_ref[...] = x1_ref[...] + g2 * (routed + shared)


def _moe_finish(layer, h2, routed_c, routed_l, x1, mod, sg, su, sd):
    tile = lambda t: (t, 0)
    tiles_per_half = _HALF_TOK // _FIN_TILE
    ctx_tile = lambda t: (jnp.minimum(t, tiles_per_half - 1), 0)
    lat_tile = lambda t: (jnp.maximum(t - tiles_per_half, 0), 0)
    shared = lambda t: (layer, 0, 0)
    return pl.pallas_call(
        _moe_finish_kernel,
        out_shape=jax.ShapeDtypeStruct((N_TOK, D_MODEL), _F32),
        grid=(N_TOK // _FIN_TILE,),
        in_specs=[
            pl.BlockSpec((_FIN_TILE, D_MODEL), tile),
            pl.BlockSpec((_FIN_TILE * _ROW_CHUNKS, _LANE), ctx_tile),
            pl.BlockSpec((_FIN_TILE * _ROW_CHUNKS, _LANE), lat_tile),
            pl.BlockSpec((_FIN_TILE, D_MODEL), tile),
            pl.BlockSpec((MOD_ROWS, 6 * D_MODEL), lambda t: (0, 0)),
            pl.BlockSpec((1, D_MODEL, SHARED_DIM), shared),
            pl.BlockSpec((1, D_MODEL, SHARED_DIM), shared),
            pl.BlockSpec((1, SHARED_DIM, D_MODEL), shared),
        ],
        out_specs=pl.BlockSpec((_FIN_TILE, D_MODEL), tile),
        compiler_params=_params(("arbitrary",)),
        name="moe_finish",
    )(h2, routed_c, routed_l, x1, mod, sg, su, sd)


def kernel(x_prompt, x_sample, state_gla, cache_na_k, cache_na_v, c, c_ctx, w_ada, b_ada, norm_mix, norm_ffn, w_in, gla_w_gate, gla_b_gate, gla_out_norm, conv_dw, conv_dw_b, conv_ln_g, conv_ln_b, conv_pw, na_q_norm, na_k_norm, na_rpb, w_out, router_w, router_bias, exp_w_gate, exp_w_up, exp_w_down, sh_w_gate, sh_w_up, sh_w_down):
    x_ctx, x_lat, x_lat_tile = x_prompt.reshape(N_CTX_TOK, D_MODEL), x_sample.reshape(N_LAT_TOK, D_MODEL), 0
    cvec = jnp.concatenate([c_ctx[None], c, jnp.zeros((MOD_ROWS - 1 - DEC_BATCH, D_MODEL), _F32)], axis=0)
    mod_all = _ada_mod(cvec, w_ada, b_ada)
    gla_consts = _gla_constants()
    zero_state = jnp.zeros((BATCH, 2, GLA_VAL, GLA_KEY), _F32)
    lat_first = N_CTX_TOK // DEC_SEQ

    states, keys, vals = [], [], []
    for l in range(DEPTH):
        mod = mod_all[l]
        wi = w_in[l]
        w_gla = _bf(jnp.pad(wi[:, :GLA_LR_OFF + 2 * GLA_GATE_RANK], ((0, 0), (0, GLA_IN_W - GLA_LR_OFF - 2 * GLA_GATE_RANK))))
        conv_off = GLA_LR_OFF + 2 * GLA_GATE_RANK
        w_conv = _bf(wi[:, conv_off:conv_off + 2 * CONV_CH])
        w_na = _bf(wi[:, conv_off + 2 * CONV_CH:])
        g_all, c_all, a_all = _in_proj(x_ctx, x_lat, x_lat_tile, norm_mix[l][None], mod, w_gla, w_conv, w_na)

        wz = jnp.zeros((GLA_IN_W - GLA_LR_OFF, 2 * GLA_KEY), _F32)
        wz = wz.at[:GLA_GATE_RANK, :GLA_KEY].set(gla_w_gate[l, 0])
        wz = wz.at[GLA_GATE_RANK:2 * GLA_GATE_RANK, GLA_KEY:].set(gla_w_gate[l, 1])
        bz = gla_b_gate[l].reshape(1, 2 * GLA_KEY)
        onorm = gla_out_norm[l][None]
        gla_c, st_c = _gla(g_all, wz, bz, gla_consts, onorm, zero_state, n=SEQ, n_seq=BATCH, first_block=0)
        gla_l, _ = _gla(g_all, wz, bz, gla_consts, onorm, _state_to_blockdiag(state_gla[:, l]),
                        n=DEC_SEQ, n_seq=DEC_BATCH, first_block=lat_first)

        conv_args = (conv_dw[l], conv_dw_b[l][None], conv_ln_g[l][None], conv_ln_b[l][None], _bf(conv_pw[l]))
        conv_c = _conv(c_all, *conv_args, n=SEQ, n_seq=BATCH, first_block=0)
        conv_l = _conv(c_all, *conv_args, n=DEC_SEQ, n_seq=DEC_BATCH, first_block=lat_first)

        qn, kn = na_q_norm[l][None], na_k_norm[l][None]
        na_c, k_l, v_l = _na_ctx(a_all, qn, kn)
        na_l = _na_lat(a_all, cache_na_k[:, l], cache_na_v[:, l], _na_bias_table(na_rpb[l]), qn, kn)

        x1, h2, h_rows, logits_t = _out_proj(gla_c, gla_l, conv_c, conv_l, na_c, na_l, x_ctx, x_lat, x_lat_tile, mod,
                                     norm_ffn[l][None], _bf(w_out[l]), router_w[l].T)
        plan = _moe_plan(*_router(logits_t, router_bias[l][:, None]))
        routed = [_gmm(l, half, plan, h_rows, exp_w_gate, exp_w_up, exp_w_down) for half in range(_N_HALVES)]
        x = _moe_finish(l, h2, routed[0], routed[1], x1, mod, sh_w_gate, sh_w_up, sh_w_down)
        x_ctx, x_lat, x_lat_tile = x, x, N_CTX_TILES

        states.append(_blockdiag_to_state(st_c))
        keys.append(k_l)
        vals.append(v_l)

    y_prompt = x[:N_CTX_TOK].reshape(BATCH, SEQ, D_MODEL)
    y_sample = x[N_CTX_TOK:].reshape(DEC_BATCH, DEC_SEQ, D_MODEL)
    return (y_prompt, y_sample, jnp.stack(states, axis=1), jnp.stack(keys, axis=1), jnp.stack(vals, axis=1))
```

```python
import functools

import numpy as np
import jax
import jax.numpy as jnp
from jax import lax
from jax.experimental import pallas as pl
from jax.experimental.pallas import tpu as pltpu

D_MODEL = 1024
BATCH = 16
SEQ = 256
DEPTH = 2
DEC_BATCH = 4
DEC_SEQ = 1024
PAST_LEN = 256
GRID_W = 64
GLA_HEADS = 4
GLA_DK = 64
GLA_DV = 128
GLA_KEY = GLA_HEADS * GLA_DK
GLA_VAL = GLA_HEADS * GLA_DV
GLA_GATE_RANK = 16
GLA_GATE_NORM = 16.0
GLA_CHUNK = 64
CONV_CH = 256
CONV_WIDTH = 31
NA_HEADS = 4
NA_HD = 64
NA_DIM = NA_HEADS * NA_HD
NA_WIN_ROWS = 8
NA_WIN_COLS = 16
N_EXPERTS = 64
TOP_K = 8
N_GROUPS = 8
TOPK_GROUPS = 4
EXPERT_DIM = 256
SHARED_DIM = 256
ROUTED_SCALE = 2.5
EPS = 1e-6

N_CTX_TOK = BATCH * SEQ
N_LAT_TOK = DEC_BATCH * DEC_SEQ
N_TOK = N_CTX_TOK + N_LAT_TOK
ROW_TILE = 256
N_CTX_TILES = N_CTX_TOK // ROW_TILE
TILES_PER_LAT_SEQ = DEC_SEQ // ROW_TILE
MOD_ROWS = 8
GLA_IN_W = 1664
GLA_LR_OFF = 2 * GLA_KEY + 2 * GLA_VAL
VMEM_LIMIT = 56 * 1024 * 1024

_BF = jnp.bfloat16
_F32 = jnp.float32


def _bf(x):
    return x.astype(_BF)


def _dot(a, b):
    return jnp.dot(a, b, preferred_element_type=_F32)


def _dot_nt(a, b):
    return lax.dot_general(a, b, (((1,), (1,)), ((), ())), preferred_element_type=_F32)


def _dot_tn(a, b):
    return lax.dot_general(a, b, (((0,), (0,)), ((), ())), preferred_element_type=_F32)


def _split2(x):
    hi = _bf(x)
    lo = _bf(x - hi.astype(_F32))
    return hi, lo


def _split3(x):
    hi = _bf(x)
    r = x - hi.astype(_F32)
    mid = _bf(r)
    lo = _bf(r - mid.astype(_F32))
    return hi, mid, lo


def _dot3(a, b, dot=_dot):
    a_hi, a_lo = _split2(a)
    b_hi, b_lo = _split2(b)
    return (dot(a_lo, b_hi) + dot(a_hi, b_lo)) + dot(a_hi, b_hi)


def _sigmoid(x):
    return 1.0 / (1.0 + jnp.exp(-x))


def _silu(x):
    return x * _sigmoid(x)


def _rms(x, w):
    return x * lax.rsqrt(jnp.mean(x * x, axis=-1, keepdims=True) + EPS) * w


def _params(sem):
    return pltpu.CompilerParams(dimension_semantics=sem, vmem_limit_bytes=VMEM_LIMIT)


def _mod_row(i):
    return jnp.where(i < N_CTX_TILES, 0, 1 + (i - N_CTX_TILES) // TILES_PER_LAT_SEQ)


def _ada_kernel(cv_ref, w_ref, b_ref, o_ref):
    o_ref[0] = _dot3(_silu(cv_ref[...]), w_ref[0]) + b_ref[0]


def _ada_mod(cvec, w_ada, b_ada):
    tn = 1024
    n_out = 6 * D_MODEL
    return pl.pallas_call(
        _ada_kernel,
        out_shape=jax.ShapeDtypeStruct((DEPTH, MOD_ROWS, n_out), _F32),
        grid=(DEPTH, n_out // tn),
        in_specs=[
            pl.BlockSpec((MOD_ROWS, D_MODEL), lambda l, j: (0, 0)),
            pl.BlockSpec((1, D_MODEL, tn), lambda l, j: (l, 0, j)),
            pl.BlockSpec((1, 1, tn), lambda l, j: (l, 0, j)),
        ],
        out_specs=pl.BlockSpec((1, MOD_ROWS, tn), lambda l, j: (l, 0, j)),
        compiler_params=_params(("arbitrary", "arbitrary")),
        name="ada_mod",
    )(cvec, w_ada, b_ada.reshape(DEPTH, 1, n_out))


def _inproj_kernel(xc_ref, xl_ref, nw_ref, mod_ref, wg_ref, wc_ref, wa_ref, g_ref, c_ref, a_ref):
    i = pl.program_id(0)
    row = _mod_row(i)
    sh = mod_ref[pl.ds(row, 1), 0:D_MODEL]
    sc = mod_ref[pl.ds(row, 1), D_MODEL:2 * D_MODEL]
    x = jnp.where(i < N_CTX_TILES, xc_ref[...], xl_ref[...])
    h = _bf(_rms(x, nw_ref[...]) * (1.0 + sc) + sh)
    g_ref[...] = _dot(h, wg_ref[...])
    c_ref[...] = _dot(h, wc_ref[...])
    a_ref[...] = _dot(h, wa_ref[...])


def _stream_tiles(lat_first_tile):
    ctx_tile = lambda i: (jnp.minimum(i, N_CTX_TILES - 1), 0)
    lat_tile = lambda i: (jnp.maximum(i - N_CTX_TILES, 0) + lat_first_tile, 0)
    return ctx_tile, lat_tile


def _in_proj(x_ctx, x_lat, lat_first_tile, norm_w, mod, w_gla, w_conv, w_na):
    full = lambda i: (0, 0)
    tile = lambda i: (i, 0)
    ctx_tile, lat_tile = _stream_tiles(lat_first_tile)
    return pl.pallas_call(
        _inproj_kernel,
        out_shape=(
            jax.ShapeDtypeStruct((N_TOK, GLA_IN_W), _F32),
            jax.ShapeDtypeStruct((N_TOK, 2 * CONV_CH), _F32),
            jax.ShapeDtypeStruct((N_TOK, 3 * NA_DIM), _F32),
        ),
        grid=(N_TOK // ROW_TILE,),
        in_specs=[
            pl.BlockSpec((ROW_TILE, D_MODEL), ctx_tile),
            pl.BlockSpec((ROW_TILE, D_MODEL), lat_tile),
            pl.BlockSpec((1, D_MODEL), full),
            pl.BlockSpec((MOD_ROWS, 6 * D_MODEL), full),
            pl.BlockSpec((D_MODEL, GLA_IN_W), full),
            pl.BlockSpec((D_MODEL, 2 * CONV_CH), full),
            pl.BlockSpec((D_MODEL, 3 * NA_DIM), full),
        ],
        out_specs=(
            pl.BlockSpec((ROW_TILE, GLA_IN_W), tile),
            pl.BlockSpec((ROW_TILE, 2 * CONV_CH), tile),
            pl.BlockSpec((ROW_TILE, 3 * NA_DIM), tile),
        ),
        compiler_params=_params(("arbitrary",)),
        name="in_proj",
    )(x_ctx, x_lat, norm_w, mod, w_gla, w_conv, w_na)


_GLA_LEVELS = (32, 16, 8, 4, 2, 1)
_N_EXP_BLOCKS = 2 + len(_GLA_LEVELS)
_N_MASKS = len(_GLA_LEVELS) + 1


def _gla_constants():
    cs = GLA_CHUNK
    r = np.arange(cs)
    i = r[:, None]
    c = r[None, :]
    w = np.zeros((2, _N_EXP_BLOCKS, cs, cs), np.float32)
    m = np.zeros((2, _N_MASKS, cs, cs), np.float32)
    w[0, 0] = c <= i
    w[0, 1] = c > i
    w[1, 0] = c >= i
    w[1, 1] = c < i
    for lv, half in enumerate(_GLA_LEVELS):
        mid = (r // (2 * half)) * (2 * half) + half
        mi = mid[:, None]
        second = (r >= mid)[:, None]
        same = (r[:, None] // (2 * half)) == (r[None, :] // (2 * half))
        w[0, 2 + lv] = np.where(second, (c >= mi) & (c <= i), (c > i) & (c <= mi - 1))
        w[1, 2 + lv] = np.where(second, (c >= mi) & (c <= i - 1), (c >= i) & (c <= mi - 1))
        m[0, lv] = same & (i >= mi) & (c < mi)
        m[1, lv] = same & (i < mi) & (c >= mi)
    m[:, _N_MASKS - 1] = np.eye(cs)
    w_all = w.reshape(2, _N_EXP_BLOCKS * cs, cs)
    lmask = np.tile(m, (1, 1, 1, GLA_HEADS))
    kr = np.arange(GLA_HEADS * cs)
    bdk = (kr[:, None] // cs) == (np.arange(GLA_KEY)[None, :] // GLA_DK)
    bdv = (kr[:, None] // cs) == (np.arange(GLA_VAL)[None, :] // GLA_DV)
    bdst = (np.arange(GLA_VAL)[:, None] // GLA_DV) == (np.arange(GLA_KEY)[None, :] // GLA_DK)
    return (jnp.asarray(w_all, _BF), jnp.asarray(lmask, _F32), jnp.asarray(bdk, _F32),
            jnp.asarray(bdv, _F32), jnp.asarray(bdst, _F32))


def _gla_kernel(g_ref, wz_ref, bz_ref, wall_ref, lmask_ref, bdk_ref, bdv_ref, bdst_ref, onorm_ref, st0_ref,
                o_ref, stfin_ref, la_scr, o_scr, st_scr, *, n):
    cs = GLA_CHUNK
    nc = n // cs
    gate_rows = 128

    def log_decays(t, carry):
        rows = pl.ds(pl.multiple_of(t * gate_rows, gate_rows), gate_rows)
        z = _dot3(g_ref[rows, GLA_LR_OFF:GLA_IN_W], wz_ref[...]) + bz_ref[...]
        la_scr[rows, :] = (jnp.minimum(z, 0.0) - jnp.log1p(jnp.exp(-jnp.abs(z)))) * (1.0 / GLA_GATE_NORM)
        return carry

    lax.fori_loop(0, n // gate_rows, log_decays, 0)

    def chunk(c, d):
        rows = pl.ds(pl.multiple_of(c * cs, cs), cs)
        q = g_ref[rows, 0:GLA_KEY] * (GLA_DK ** -0.5)
        k = g_ref[rows, GLA_KEY:2 * GLA_KEY]
        v = g_ref[rows, 2 * GLA_KEY:2 * GLA_KEY + GLA_VAL]
        la_hi, la_mid, la_lo = _split3(la_scr[rows, d * GLA_KEY:(d + 1) * GLA_KEY])
        w = wall_ref[d]
        f = jnp.exp((_dot(w, la_lo) + _dot(w, la_mid)) + _dot(w, la_hi))
        st = st_scr[d]
        o = _dot_nt(_bf(q * f[0:cs]), _bf(st))
        p = jnp.zeros((cs, GLA_HEADS * cs), _F32)
        for lv in range(_N_MASKS):
            if lv < len(_GLA_LEVELS):
                fl = f[(2 + lv) * cs:(3 + lv) * cs]
                ql, kl = q * fl, k * fl
            else:
                ql, kl = q, k
            k_bd = _bf(jnp.concatenate([kl] * GLA_HEADS, axis=0) * bdk_ref[...])
            p = p + lmask_ref[d, lv] * _dot_nt(_bf(ql), k_bd)
        v_bd = _bf(jnp.concatenate([v] * GLA_HEADS, axis=0) * bdv_ref[...])
        o_scr[d, rows, :] = o + _dot(_bf(p), v_bd)
        decay = f[cs - 1:cs] if d == 0 else f[0:1]
        st_scr[d] = st * decay + _dot_tn(_bf(v), _bf(k * f[cs:2 * cs])) * bdst_ref[...]

    st_scr[...] = st0_ref[0]

    def scan(i, carry):
        chunk(i, 0)
        chunk(nc - 1 - i, 1)
        return carry

    lax.fori_loop(0, nc, scan, 0)
    stfin_ref[0] = st_scr[...]

    def finish(c, carry):
        rows = pl.ds(pl.multiple_of(c * cs, cs), cs)
        o = o_scr[1, rows, :] + o_scr[0, rows, :]
        for h in range(GLA_HEADS):
            cols = slice(h * GLA_DV, (h + 1) * GLA_DV)
            gate = g_ref[rows, 2 * GLA_KEY + GLA_VAL + h * GLA_DV:2 * GLA_KEY + GLA_VAL + (h + 1) * GLA_DV]
            o_ref[rows, cols] = _rms(o[:, cols], onorm_ref[...]) * _silu(gate)
        return carry

    lax.fori_loop(0, nc, finish, 0)


def _gla(g_all, wz, bz, consts, onorm, st0, *, n, n_seq, first_block):
    w_all, lmask, bdk, bdv, bdst = consts
    full2 = lambda i: (0, 0)
    return pl.pallas_call(
        functools.partial(_gla_kernel, n=n),
        out_shape=(
            jax.ShapeDtypeStruct((n_seq * n, GLA_VAL), _F32),
            jax.ShapeDtypeStruct((n_seq, 2, GLA_VAL, GLA_KEY), _F32),
        ),
        grid=(n_seq,),
        in_specs=[
            pl.BlockSpec((n, GLA_IN_W), lambda i: (first_block + i, 0)),
            pl.BlockSpec(wz.shape, full2),
            pl.BlockSpec(bz.shape, full2),
            pl.BlockSpec(w_all.shape, lambda i: (0, 0, 0)),
            pl.BlockSpec(lmask.shape, lambda i: (0, 0, 0, 0)),
            pl.BlockSpec(bdk.shape, full2),
            pl.BlockSpec(bdv.shape, full2),
            pl.BlockSpec(bdst.shape, full2),
            pl.BlockSpec((1, GLA_DV), full2),
            pl.BlockSpec((1, 2, GLA_VAL, GLA_KEY), lambda i: (i, 0, 0, 0)),
        ],
        out_specs=(
            pl.BlockSpec((n, GLA_VAL), lambda i: (i, 0)),
            pl.BlockSpec((1, 2, GLA_VAL, GLA_KEY), lambda i: (i, 0, 0, 0)),
        ),
        scratch_shapes=[pltpu.VMEM((n, 2 * GLA_KEY), _F32), pltpu.VMEM((2, n, GLA_VAL), _F32),
                        pltpu.VMEM((2, GLA_VAL, GLA_KEY), _F32)],
        compiler_params=_params(("arbitrary",)),
        name=f"gla_{n}",
    )(g_all, wz, bz, w_all, lmask, bdk, bdv, bdst, onorm, st0)


def _state_to_blockdiag(s):
    b = s.shape[0]
    st = jnp.swapaxes(s, -1, -2)
    eye = jnp.eye(GLA_HEADS, dtype=s.dtype)
    out = st[:, :, :, :, None, :] * eye[None, None, :, None, :, None]
    return out.reshape(b, 2, GLA_VAL, GLA_KEY)


def _blockdiag_to_state(st):
    b = st.shape[0]
    s6 = st.reshape(b, 2, GLA_HEADS, GLA_DV, GLA_HEADS, GLA_DK)
    diag = jnp.stack([s6[:, :, h, :, h, :] for h in range(GLA_HEADS)], axis=2)
    return jnp.swapaxes(diag, -1, -2)


_CONV_PAD = 16
_CONV_ROWS = 128


def _conv_kernel(c_ref, dw_ref, dwb_ref, lng_ref, lnb_ref, pw_ref, o_ref, pad_scr, *, n):
    zeros = jnp.zeros((_CONV_PAD, CONV_CH), _F32)
    pad_scr[0:_CONV_PAD, :] = zeros
    pad_scr[_CONV_PAD + n:2 * _CONV_PAD + n, :] = zeros
    for r0 in range(0, n, _CONV_ROWS):
        a = c_ref[r0:r0 + _CONV_ROWS, :]
        pad_scr[_CONV_PAD + r0:_CONV_PAD + r0 + _CONV_ROWS, :] = a[:, :CONV_CH] * _sigmoid(a[:, CONV_CH:])
    half = CONV_WIDTH // 2
    for r0 in range(0, n, _CONV_ROWS):
        acc = jnp.zeros((_CONV_ROWS, CONV_CH), _F32)
        for w in range(CONV_WIDTH):
            s = _CONV_PAD + r0 + w - half
            acc = acc + pad_scr[s:s + _CONV_ROWS, :] * dw_ref[w:w + 1, :]
        acc = acc + dwb_ref[...]
        xc = acc - jnp.mean(acc, axis=-1, keepdims=True)
        y = xc * lax.rsqrt(jnp.mean(xc * xc, axis=-1, keepdims=True) + EPS) * lng_ref[...] + lnb_ref[...]
        o_ref[r0:r0 + _CONV_ROWS, :] = _dot(_bf(_silu(y)), pw_ref[...])


def _conv(c_all, dw, dwb, lng, lnb, pw, *, n, n_seq, first_block):
    full2 = lambda i: (0, 0)
    return pl.pallas_call(
        functools.partial(_conv_kernel, n=n),
        out_shape=jax.ShapeDtypeStruct((n_seq * n, CONV_CH), _F32),
        grid=(n_seq,),
        in_specs=[
            pl.BlockSpec((n, 2 * CONV_CH), lambda i: (first_block + i, 0)),
            pl.BlockSpec((CONV_WIDTH, CONV_CH), full2),
            pl.BlockSpec((1, CONV_CH), full2),
            pl.BlockSpec((1, CONV_CH), full2),
            pl.BlockSpec((1, CONV_CH), full2),
            pl.BlockSpec((CONV_CH, CONV_CH), full2),
        ],
        out_specs=pl.BlockSpec((n, CONV_CH), lambda i: (i, 0)),
        scratch_shapes=[pltpu.VMEM((n + 2 * _CONV_PAD, CONV_CH), _F32)],
        compiler_params=_params(("arbitrary",)),
        name=f"conv_{n}",
    )(c_all, dw, dwb, lng, lnb, pw)


def _softmax_rows(s):
    e = jnp.exp(s - jnp.max(s, axis=-1, keepdims=True))
    return e / jnp.sum(e, axis=-1, keepdims=True)


def _na_ctx_kernel(a_ref, qn_ref, kn_ref, o_ref, kc_ref, vc_ref):
    for h in range(NA_HEADS):
        cols = slice(h * NA_HD, (h + 1) * NA_HD)
        q = _rms(a_ref[:, h * NA_HD:(h + 1) * NA_HD], qn_ref[...])
        k = _rms(a_ref[:, NA_DIM + h * NA_HD:NA_DIM + (h + 1) * NA_HD], kn_ref[...])
        v = a_ref[:, 2 * NA_DIM + h * NA_HD:2 * NA_DIM + (h + 1) * NA_HD]
        kc_ref[0, h] = k
        vc_ref[0, h] = v
        p = _softmax_rows(_dot_nt(_bf(q), _bf(k)) * (NA_HD ** -0.5))
        o_ref[:, cols] = _dot(_bf(p), _bf(v))


def _na_ctx(a_all, qn, kn):
    full2 = lambda i: (0, 0)
    cache = jax.ShapeDtypeStruct((BATCH, NA_HEADS, SEQ, NA_HD), _F32)
    cache_spec = pl.BlockSpec((1, NA_HEADS, SEQ, NA_HD), lambda i: (i, 0, 0, 0))
    return pl.pallas_call(
        _na_ctx_kernel,
        out_shape=(jax.ShapeDtypeStruct((N_CTX_TOK, NA_DIM), _F32), cache, cache),
        grid=(BATCH,),
        in_specs=[
            pl.BlockSpec((SEQ, 3 * NA_DIM), lambda i: (i, 0)),
            pl.BlockSpec((1, NA_HD), full2),
            pl.BlockSpec((1, NA_HD), full2),
        ],
        out_specs=(pl.BlockSpec((SEQ, NA_DIM), lambda i: (i, 0)), cache_spec, cache_spec),
        compiler_params=_params(("arbitrary",)),
        name="na_ctx",
    )(a_all, qn, kn)


_NA_ROWS = DEC_SEQ // GRID_W
_NA_KEYS = NA_WIN_ROWS * GRID_W
_NA_VARIANTS = NA_WIN_ROWS
_NA_ROW_UNROLL = 8


_RPB_ROWS = 2 * NA_WIN_ROWS - 1
_RPB_COLS = 2 * NA_WIN_COLS - 1


def _na_bias_constants():
    qc = np.arange(GRID_W)[:, None]
    kc = np.arange(GRID_W)[None, :]
    shift = np.stack([(kc - qc + NA_WIN_COLS - 1) == co for co in range(_RPB_COLS)]).astype(np.float32)
    win_start = np.clip(qc - NA_WIN_COLS // 2, 0, GRID_W - NA_WIN_COLS)
    in_win = ((kc >= win_start) & (kc < win_start + NA_WIN_COLS)).astype(np.float32)
    return jnp.asarray(shift), jnp.asarray(in_win)


def _na_bias_kernel(rpb_ref, shift_ref, win_ref, o_ref):
    h = pl.program_id(0)
    in_win = win_ref[...] > 0.0
    tiles = []
    for ro in range(_RPB_ROWS):
        base = (h * _RPB_ROWS + ro) * _RPB_COLS
        acc = rpb_ref[base] * shift_ref[0]
        for co in range(1, _RPB_COLS):
            acc = acc + rpb_ref[base + co] * shift_ref[co]
        tiles.append(jnp.where(in_win, acc, -jnp.inf))
    for t in range(_NA_VARIANTS):
        o_ref[0, t] = jnp.concatenate([tiles[kr - t + NA_WIN_ROWS - 1] for kr in range(NA_WIN_ROWS)], axis=1)


def _na_bias_table(rpb):
    shift, in_win = _na_bias_constants()
    return pl.pallas_call(
        _na_bias_kernel,
        out_shape=jax.ShapeDtypeStruct((NA_HEADS, _NA_VARIANTS, GRID_W, _NA_KEYS), _F32),
        grid=(NA_HEADS,),
        in_specs=[
            pl.BlockSpec(memory_space=pltpu.SMEM),
            pl.BlockSpec(shift.shape, lambda h: (0, 0, 0)),
            pl.BlockSpec(in_win.shape, lambda h: (0, 0)),
        ],
        out_specs=pl.BlockSpec((1, _NA_VARIANTS, GRID_W, _NA_KEYS), lambda h: (h, 0, 0, 0)),
        compiler_params=_params(("arbitrary",)),
        name="na_bias",
    )(rpb.reshape(-1), shift, in_win)


def _na_lat_kernel(a_ref, kctx_ref, vctx_ref, bias_ref, qn_ref, kn_ref, o_ref, q_scr, k_scr, v_scr):
    scale = NA_HD ** -0.5
    for h in range(NA_HEADS):
        q_scr[...] = _bf(_rms(a_ref[:, h * NA_HD:(h + 1) * NA_HD], qn_ref[...]))
        k_scr[...] = _bf(_rms(a_ref[:, NA_DIM + h * NA_HD:NA_DIM + (h + 1) * NA_HD], kn_ref[...]))
        v_scr[...] = _bf(a_ref[:, 2 * NA_DIM + h * NA_HD:2 * NA_DIM + (h + 1) * NA_HD])
        k_ctx = _bf(kctx_ref[0, h])
        v_ctx = _bf(vctx_ref[0, h])

        def row(r):
            start = jnp.clip(r - NA_WIN_ROWS // 2, 0, _NA_ROWS - NA_WIN_ROWS)
            qrows = pl.ds(pl.multiple_of(r * GRID_W, GRID_W), GRID_W)
            krows = pl.ds(pl.multiple_of(start * GRID_W, GRID_W), _NA_KEYS)
            q = q_scr[qrows, :]
            s_loc = _dot_nt(q, k_scr[krows, :]) * scale + bias_ref[h, r - start]
            s_ctx = _dot_nt(q, k_ctx) * scale
            m = jnp.maximum(jnp.max(s_loc, axis=-1, keepdims=True), jnp.max(s_ctx, axis=-1, keepdims=True))
            e_loc = jnp.exp(s_loc - m)
            e_ctx = jnp.exp(s_ctx - m)
            inv = 1.0 / (jnp.sum(e_loc, axis=-1, keepdims=True) + jnp.sum(e_ctx, axis=-1, keepdims=True))
            o = _dot(_bf(e_loc * inv), v_scr[krows, :]) + _dot(_bf(e_ctx * inv), v_ctx)
            o_ref[qrows, h * NA_HD:(h + 1) * NA_HD] = o

        def row_group(g, carry):
            for dr in range(_NA_ROW_UNROLL):
                row(g * _NA_ROW_UNROLL + dr)
            return carry

        lax.fori_loop(0, _NA_ROWS // _NA_ROW_UNROLL, row_group, 0)


def _na_lat(a_all, k_ctx, v_ctx, bias, qn, kn):
    full2 = lambda i: (0, 0)
    ctx_spec = pl.BlockSpec((1, NA_HEADS, PAST_LEN, NA_HD), lambda i: (i, 0, 0, 0))
    return pl.pallas_call(
        _na_lat_kernel,
        out_shape=jax.ShapeDtypeStruct((N_LAT_TOK, NA_DIM), _F32),
        grid=(DEC_BATCH,),
        in_specs=[
            pl.BlockSpec((DEC_SEQ, 3 * NA_DIM), lambda i: (N_CTX_TOK // DEC_SEQ + i, 0)),
            ctx_spec,
            ctx_spec,
            pl.BlockSpec(bias.shape, lambda i: (0, 0, 0, 0)),
            pl.BlockSpec((1, NA_HD), full2),
            pl.BlockSpec((1, NA_HD), full2),
        ],
        out_specs=pl.BlockSpec((DEC_SEQ, NA_DIM), lambda i: (i, 0)),
        scratch_shapes=[pltpu.VMEM((DEC_SEQ, NA_HD), _BF)] * 3,
        compiler_params=_params(("arbitrary",)),
        name="na_lat",
    )(a_all, k_ctx, v_ctx, bias, qn, kn)


def _outproj_kernel(gc_ref, gl_ref, cc_ref, cl_ref, nc_ref, nl_ref, xc_ref, xl_ref, mod_ref, nw_ref, wo_ref,
                    rwt_ref, x1_ref, h2_ref, hr_ref, lg_ref):
    i = pl.program_id(0)
    is_ctx = i < N_CTX_TILES
    row = _mod_row(i)
    gla = jnp.where(is_ctx, gc_ref[...], gl_ref[...])
    conv = jnp.where(is_ctx, cc_ref[...], cl_ref[...])
    na = jnp.where(is_ctx, nc_ref[...], nl_ref[...])
    mix = (_dot(_bf(gla), wo_ref[0:GLA_VAL, :])
           + _dot(_bf(conv), wo_ref[GLA_VAL:GLA_VAL + CONV_CH, :])
           + _dot(_bf(na), wo_ref[GLA_VAL + CONV_CH:, :]))
    g1 = mod_ref[pl.ds(row, 1), 2 * D_MODEL:3 * D_MODEL]
    sh2 = mod_ref[pl.ds(row, 1), 3 * D_MODEL:4 * D_MODEL]
    sc2 = mod_ref[pl.ds(row, 1), 4 * D_MODEL:5 * D_MODEL]
    x1 = jnp.where(is_ctx, xc_ref[...], xl_ref[...]) + g1 * mix
    h2 = _rms(x1, nw_ref[...]) * (1.0 + sc2) + sh2
    x1_ref[...] = x1
    h2_ref[...] = _bf(h2)
    for j in range(_ROW_CHUNKS):
        hr_ref[pl.ds(j, ROW_TILE, stride=_ROW_CHUNKS), :] = h2[:, j * _LANE:(j + 1) * _LANE]
    lg_ref[...] = _dot3(rwt_ref[...], h2, dot=_dot_nt)


def _out_proj(gla_c, gla_l, conv_c, conv_l, na_c, na_l, x_ctx, x_lat, lat_first_tile, mod, norm_w, w_out,
              router_wt):
    full2 = lambda i: (0, 0)
    tile = lambda i: (i, 0)
    ctx_tile, lat_tile = _stream_tiles(0)
    _, x_lat_tile = _stream_tiles(lat_first_tile)
    return pl.pallas_call(
        _outproj_kernel,
        out_shape=(
            jax.ShapeDtypeStruct((N_TOK, D_MODEL), _F32),
            jax.ShapeDtypeStruct((N_TOK, D_MODEL), _BF),
            jax.ShapeDtypeStruct((N_TOK * _ROW_CHUNKS, _LANE), _F32),
            jax.ShapeDtypeStruct((N_EXPERTS, N_TOK), _F32),
        ),
        grid=(N_TOK // ROW_TILE,),
        in_specs=[
            pl.BlockSpec((ROW_TILE, GLA_VAL), ctx_tile),
            pl.BlockSpec((ROW_TILE, GLA_VAL), lat_tile),
            pl.BlockSpec((ROW_TILE, CONV_CH), ctx_tile),
            pl.BlockSpec((ROW_TILE, CONV_CH), lat_tile),
            pl.BlockSpec((ROW_TILE, NA_DIM), ctx_tile),
            pl.BlockSpec((ROW_TILE, NA_DIM), lat_tile),
            pl.BlockSpec((ROW_TILE, D_MODEL), ctx_tile),
            pl.BlockSpec((ROW_TILE, D_MODEL), x_lat_tile),
            pl.BlockSpec((MOD_ROWS, 6 * D_MODEL), full2),
            pl.BlockSpec((1, D_MODEL), full2),
            pl.BlockSpec((D_MODEL, D_MODEL), full2),
            pl.BlockSpec((N_EXPERTS, D_MODEL), full2),
        ],
        out_specs=(
            pl.BlockSpec((ROW_TILE, D_MODEL), tile),
            pl.BlockSpec((ROW_TILE, D_MODEL), tile),
            pl.BlockSpec((ROW_TILE * _ROW_CHUNKS, _LANE), tile),
            pl.BlockSpec((N_EXPERTS, ROW_TILE), lambda i: (0, i)),
        ),
        compiler_params=_params(("arbitrary",)),
        name="out_proj",
    )(gla_c, gla_l, conv_c, conv_l, na_c, na_l, x_ctx, x_lat, mod, norm_w, w_out, router_wt)


_PER_GROUP = N_EXPERTS // N_GROUPS
_ROUTE_TILE = 1024


def _first_max(x, idx, axes, sentinel):
    m = x
    for ax in axes:
        m = jnp.max(m, axis=ax, keepdims=True)
    first = jnp.where(x == m, idx, sentinel)
    for ax in axes:
        first = jnp.min(first, axis=ax, keepdims=True)
    return m, first


def _router_kernel(lg_ref, rb_ref, eid_ref, wts_ref, cnt_ref):
    t = lg_ref.shape[1]
    shape3 = (N_GROUPS, _PER_GROUP, t)
    scores = _sigmoid(lg_ref[...])
    biased = (scores + rb_ref[...]).reshape(shape3)
    scores = scores.reshape(shape3)
    neg = -jnp.inf
    in_grp = lax.broadcasted_iota(jnp.int32, shape3, 1)
    grp = lax.broadcasted_iota(jnp.int32, (N_GROUPS, 1, t), 0)
    expert = lax.broadcasted_iota(jnp.int32, shape3, 0) * _PER_GROUP + in_grp
    m1, i1 = _first_max(biased, in_grp, (1,), _PER_GROUP)
    m2 = jnp.max(jnp.where(in_grp == i1, neg, biased), axis=1, keepdims=True)
    gscore = m1 + m2
    keep = jnp.zeros((N_GROUPS, 1, t), _F32)
    for _ in range(TOPK_GROUPS):
        _, gi = _first_max(gscore, grp, (0,), N_GROUPS)
        hit = grp == gi
        keep = jnp.where(hit, 1.0, keep)
        gscore = jnp.where(hit, neg, gscore)
    masked = jnp.where(keep > 0.0, biased, neg)
    ids, picked = [], []
    hits = jnp.zeros(shape3, _F32)
    for _ in range(TOP_K):
        _, ei = _first_max(masked, expert, (1, 0), N_EXPERTS)
        hit = expert == ei
        hits = jnp.where(hit, 1.0, hits)
        sc = jnp.sum(jnp.sum(jnp.where(hit, scores, 0.0), axis=1, keepdims=True), axis=0, keepdims=True)
        ids.append(ei.reshape(1, t))
        picked.append(sc.reshape(1, t))
        masked = jnp.where(hit, neg, masked)
    picked = jnp.concatenate(picked, axis=0)
    den = jnp.sum(picked, axis=0, keepdims=True)
    eid_ref[...] = jnp.concatenate(ids, axis=0)
    wts_ref[...] = picked / den * ROUTED_SCALE
    cnt_ref[0] = jnp.sum(hits, axis=2, keepdims=True).reshape(N_EXPERTS, 1)


def _router(logits_t, router_bias):
    tile = lambda i: (0, i)
    return pl.pallas_call(
        _router_kernel,
        out_shape=(jax.ShapeDtypeStruct((TOP_K, N_TOK), jnp.int32), jax.ShapeDtypeStruct((TOP_K, N_TOK), _F32),
                   jax.ShapeDtypeStruct((N_TOK // _ROUTE_TILE, N_EXPERTS, 1), _F32)),
        grid=(N_TOK // _ROUTE_TILE,),
        in_specs=[
            pl.BlockSpec((N_EXPERTS, _ROUTE_TILE), tile),
            pl.BlockSpec((N_EXPERTS, 1), lambda i: (0, 0)),
        ],
        out_specs=(pl.BlockSpec((TOP_K, _ROUTE_TILE), tile), pl.BlockSpec((TOP_K, _ROUTE_TILE), tile),
                   pl.BlockSpec((1, N_EXPERTS, 1), lambda i: (i, 0, 0))),
        compiler_params=_params(("arbitrary",)),
        name="router",
    )(logits_t, router_bias)


_HALF_TOK = N_CTX_TOK
_N_HALVES = N_TOK // _HALF_TOK
_HALF_ASSIGN = _HALF_TOK * TOP_K
_GB = 256
_GB_MAX = _HALF_ASSIGN // _GB + N_EXPERTS
_LANE = 128
_ROW_CHUNKS = D_MODEL // _LANE
_GS = _GB + 8
_RMW_BATCH = 8
_TOK_BITS = 12
assert _HALF_TOK == 1 << _TOK_BITS


def _moe_plan(eid, wts, tile_counts):
    tok = jnp.arange(N_TOK, dtype=jnp.int32)
    key = (((tok >> _TOK_BITS) * N_EXPERTS)[None, :] + eid) * _HALF_TOK + (tok & (_HALF_TOK - 1))[None, :]
    key_s, w_s = lax.sort((key.reshape(-1), wts.reshape(-1)), num_keys=1)
    count = jnp.sum(tile_counts.reshape(_N_HALVES, -1, N_EXPERTS), axis=1).astype(jnp.int32)
    row_end = jnp.cumsum(count.reshape(-1)).reshape(_N_HALVES, N_EXPERTS)
    row_start = row_end - count
    nblk = (count + _GB - 1) // _GB
    blk_end = jnp.cumsum(nblk, axis=1)
    blk_start = blk_end - nblk
    b = jnp.arange(_GB_MAX, dtype=jnp.int32)
    n_used = blk_end[:, -1]
    b_eff = jnp.minimum(b[None, :], jnp.maximum(n_used - 1, 0)[:, None])
    grp = jnp.sum((b_eff[:, :, None] >= blk_end[:, None, :]).astype(jnp.int32), axis=-1)
    onehot = (grp[:, :, None] == jnp.arange(N_EXPERTS, dtype=jnp.int32)).astype(jnp.int32)
    pick = lambda v: jnp.sum(onehot * v[:, None, :], axis=-1)
    within = b[None, :] - pick(blk_start)
    start = pick(row_start) + within * _GB
    length = jnp.clip(pick(count) - within * _GB, 0, _GB)
    pad = jnp.zeros((_GB,), jnp.int32)
    return (jnp.concatenate([key_s, pad]), jnp.concatenate([w_s, pad.astype(_F32)]),
            grp.astype(jnp.int32), start.astype(jnp.int32), length.astype(jnp.int32), n_used.astype(jnp.int32))


def _gmm_kernel(be_ref, bstart_ref, blen_ref, nused_ref, key_ref, gate_ref,
                h_ref, wg_ref, wu_ref, wd_ref, acc_ref, xt_scr, yt_scr, wgb_scr, wub_scr, wdb_scr):
    s = pl.program_id(0)
    n_used = nused_ref[0]
    last = _GB_MAX - 1

    @pl.when(s == 0)
    def _():
        acc_ref[...] = jnp.zeros_like(acc_ref)
        xt_scr[...] = jnp.zeros_like(xt_scr)
        yt_scr[...] = jnp.zeros_like(yt_scr)

    fb = jnp.clip(s - 1, 0, last)

    @pl.when((s == 0) | (be_ref[fb] != be_ref[jnp.clip(s - 2, 0, last)]))
    def _():
        wgb_scr[...] = _bf(wg_ref[0, 0])
        wub_scr[...] = _bf(wu_ref[0, 0])
        wdb_scr[...] = _bf(wd_ref[0, 0])

    slot = s % 2
    other = 1 - slot

    cb = jnp.clip(s - 2, 0, last)
    c_live = (s >= 2) & (cb < n_used)
    c_start = jnp.where(c_live, bstart_ref[cb], 0)
    c_len = jnp.where(c_live, blen_ref[cb], 0)
    for m0 in range(0, _GB, _RMW_BATCH):
        pending = []
        for mi in range(m0, m0 + _RMW_BATCH):
            valid = mi < c_len
            tok = jnp.where(valid, key_ref[c_start + mi] & (_HALF_TOK - 1), _HALF_TOK)
            gate = jnp.where(valid, gate_ref[c_start + mi], 0.0)
            rows = pl.ds(pl.multiple_of(tok * _ROW_CHUNKS, _ROW_CHUNKS), _ROW_CHUNKS)
            pending.append((rows, acc_ref[rows, :] + gate * yt_scr[slot, pl.ds(mi, _ROW_CHUNKS, stride=_GS), :]))
        for rows, val in pending:
            acc_ref[rows, :] = val

    x = _bf(jnp.concatenate([xt_scr[other, j * _GS:j * _GS + _GB, :] for j in range(_ROW_CHUNKS)], axis=-1))
    act = _silu(_dot(x, wgb_scr[...])) * _dot(x, wub_scr[...])
    y = _dot(_bf(act), wdb_scr[...])
    for j in range(_ROW_CHUNKS):
        yt_scr[other, j * _GS:j * _GS + _GB, :] = y[:, j * _LANE:(j + 1) * _LANE]

    gb = jnp.minimum(s, last)
    g_start = jnp.where(gb < n_used, bstart_ref[gb], 0)
    for mi in range(_GB):
        tok = key_ref[g_start + mi] & (_HALF_TOK - 1)
        src = pl.ds(pl.multiple_of(tok * _ROW_CHUNKS, _ROW_CHUNKS), _ROW_CHUNKS)
        xt_scr[slot, pl.ds(mi, _ROW_CHUNKS, stride=_GS), :] = h_ref[src, :]


def _gmm(layer, half, plan, h_rows, wg, wu, wd):
    key_s, w_s, grp, start, length, n_used = plan
    expert = lambda s, be, *_: (layer, be[jnp.clip(s - 1, 0, _GB_MAX - 1)], 0, 0)
    acc_rows = (_HALF_TOK + 1) * _ROW_CHUNKS
    grid_spec = pltpu.PrefetchScalarGridSpec(
        num_scalar_prefetch=6,
        grid=(_GB_MAX + 2,),
        in_specs=[
            pl.BlockSpec((_HALF_TOK * _ROW_CHUNKS, _LANE), lambda b, *_: (half, 0), pipeline_mode=pl.Buffered(1)),
            pl.BlockSpec((1, 1, D_MODEL, EXPERT_DIM), expert),
            pl.BlockSpec((1, 1, D_MODEL, EXPERT_DIM), expert),
            pl.BlockSpec((1, 1, EXPERT_DIM, D_MODEL), expert),
        ],
        out_specs=pl.BlockSpec((acc_rows, _LANE), lambda b, *_: (0, 0), pipeline_mode=pl.Buffered(1)),
        scratch_shapes=[pltpu.VMEM((2, _ROW_CHUNKS * _GS, _LANE), _F32)] * 2
        + [pltpu.VMEM((D_MODEL, EXPERT_DIM), _BF)] * 2 + [pltpu.VMEM((EXPERT_DIM, D_MODEL), _BF)],
    )
    acc = pl.pallas_call(
        _gmm_kernel,
        out_shape=jax.ShapeDtypeStruct((acc_rows, _LANE), _F32),
        grid_spec=grid_spec,
        compiler_params=_params(("arbitrary",)),
        name="moe_experts",
    )(grp[half], start[half], length[half], n_used[half:half + 1], key_s, w_s, h_rows, wg, wu, wd)
    return acc


_FIN_TILE = 512


def _moe_finish_kernel(h_ref, rc_ref, rl_ref, x1_ref, mod_ref, sg_ref, su_ref, sd_ref, o_ref):
    t = pl.program_id(0)
    tiles_per_half = _HALF_TOK // _FIN_TILE
    is_ctx = t < tiles_per_half
    row = jnp.where(is_ctx, 0, 1 + (t - tiles_per_half) // (DEC_SEQ // _FIN_TILE))
    h = h_ref[...]
    shared = _dot(_bf(_silu(_dot(h, _bf(sg_ref[0]))) * _dot(h, _bf(su_ref[0]))), _bf(sd_ref[0]))
    routed = jnp.concatenate(
        [jnp.where(is_ctx, rc_ref[pl.ds(j, _FIN_TILE, stride=_ROW_CHUNKS), :],
                   rl_ref[pl.ds(j, _FIN_TILE, stride=_ROW_CHUNKS), :]) for j in range(_ROW_CHUNKS)], axis=-1)
    g2 = mod_ref[pl.ds(row, 1), 5 * D_MODEL:6 * D_MODEL]
    o_ref[...] = x1_ref[...] + g2 * (routed + shared)


def _moe_finish(layer, h2, routed_c, routed_l, x1, mod, sg, su, sd):
    tile = lambda t: (t, 0)
    tiles_per_half = _HALF_TOK // _FIN_TILE
    ctx_tile = lambda t: (jnp.minimum(t, tiles_per_half - 1), 0)
    lat_tile = lambda t: (jnp.maximum(t - tiles_per_half, 0), 0)
    shared = lambda t: (layer, 0, 0)
    return pl.pallas_call(
        _moe_finish_kernel,
        out_shape=jax.ShapeDtypeStruct((N_TOK, D_MODEL), _F32),
        grid=(N_TOK // _FIN_TILE,),
        in_specs=[
            pl.BlockSpec((_FIN_TILE, D_MODEL), tile),
            pl.BlockSpec((_FIN_TILE * _ROW_CHUNKS, _LANE), ctx_tile),
            pl.BlockSpec((_FIN_TILE * _ROW_CHUNKS, _LANE), lat_tile),
            pl.BlockSpec((_FIN_TILE, D_MODEL), tile),
            pl.BlockSpec((MOD_ROWS, 6 * D_MODEL), lambda t: (0, 0)),
            pl.BlockSpec((1, D_MODEL, SHARED_DIM), shared),
            pl.BlockSpec((1, D_MODEL, SHARED_DIM), shared),
            pl.BlockSpec((1, SHARED_DIM, D_MODEL), shared),
        ],
        out_specs=pl.BlockSpec((_FIN_TILE, D_MODEL), tile),
        compiler_params=_params(("arbitrary",)),
        name="moe_finish",
    )(h2, routed_c, routed_l, x1, mod, sg, su, sd)


def kernel(x_prompt, x_sample, state_gla, cache_na_k, cache_na_v, c, c_ctx, w_ada, b_ada, norm_mix, norm_ffn, w_in, gla_w_gate, gla_b_gate, gla_out_norm, conv_dw, conv_dw_b, conv_ln_g, conv_ln_b, conv_pw, na_q_norm, na_k_norm, na_rpb, w_out, router_w, router_bias, exp_w_gate, exp_w_up, exp_w_down, sh_w_gate, sh_w_up, sh_w_down):
    x_ctx, x_lat, x_lat_tile = x_prompt.reshape(N_CTX_TOK, D_MODEL), x_sample.reshape(N_LAT_TOK, D_MODEL), 0
    cvec = jnp.concatenate([c_ctx[None], c, jnp.zeros((MOD_ROWS - 1 - DEC_BATCH, D_MODEL), _F32)], axis=0)
    mod_all = _ada_mod(cvec, w_ada, b_ada)
    gla_consts = _gla_constants()
    zero_state = jnp.zeros((BATCH, 2, GLA_VAL, GLA_KEY), _F32)
    lat_first = N_CTX_TOK // DEC_SEQ

    states, keys, vals = [], [], []
    for l in range(DEPTH):
        mod = mod_all[l]
        wi = w_in[l]
        w_gla = _bf(jnp.pad(wi[:, :GLA_LR_OFF + 2 * GLA_GATE_RANK], ((0, 0), (0, GLA_IN_W - GLA_LR_OFF - 2 * GLA_GATE_RANK))))
        conv_off = GLA_LR_OFF + 2 * GLA_GATE_RANK
        w_conv = _bf(wi[:, conv_off:conv_off + 2 * CONV_CH])
        w_na = _bf(wi[:, conv_off + 2 * CONV_CH:])
        g_all, c_all, a_all = _in_proj(x_ctx, x_lat, x_lat_tile, norm_mix[l][None], mod, w_gla, w_conv, w_na)

        wz = jnp.zeros((GLA_IN_W - GLA_LR_OFF, 2 * GLA_KEY), _F32)
        wz = wz.at[:GLA_GATE_RANK, :GLA_KEY].set(gla_w_gate[l, 0])
        wz = wz.at[GLA_GATE_RANK:2 * GLA_GATE_RANK, GLA_KEY:].set(gla_w_gate[l, 1])
        bz = gla_b_gate[l].reshape(1, 2 * GLA_KEY)
        onorm = gla_out_norm[l][None]
        gla_c, st_c = _gla(g_all, wz, bz, gla_consts, onorm, zero_state, n=SEQ, n_seq=BATCH, first_block=0)
        gla_l, _ = _gla(g_all, wz, bz, gla_consts, onorm, _state_to_blockdiag(state_gla[:, l]),
                        n=DEC_SEQ, n_seq=DEC_BATCH, first_block=lat_first)

        conv_args = (conv_dw[l], conv_dw_b[l][None], conv_ln_g[l][None], conv_ln_b[l][None], _bf(conv_pw[l]))
        conv_c = _conv(c_all, *conv_args, n=SEQ, n_seq=BATCH, first_block=0)
        conv_l = _conv(c_all, *conv_args, n=DEC_SEQ, n_seq=DEC_BATCH, first_block=lat_first)

        qn, kn = na_q_norm[l][None], na_k_norm[l][None]
        na_c, k_l, v_l = _na_ctx(a_all, qn, kn)
        na_l = _na_lat(a_all, cache_na_k[:, l], cache_na_v[:, l], _na_bias_table(na_rpb[l]), qn, kn)

        x1, h2, h_rows, logits_t = _out_proj(gla_c, gla_l, conv_c, conv_l, na_c, na_l, x_ctx, x_lat, x_lat_tile, mod,
                                     norm_ffn[l][None], _bf(w_out[l]), router_w[l].T)
        plan = _moe_plan(*_router(logits_t, router_bias[l][:, None]))
        routed = [_gmm(l, half, plan, h_rows, exp_w_gate, exp_w_up, exp_w_down) for half in range(_N_HALVES)]
        x = _moe_finish(l, h2, routed[0], routed[1], x1, mod, sh_w_gate, sh_w_up, sh_w_down)
        x_ctx, x_lat, x_lat_tile = x, x, N_CTX_TILES

        states.append(_blockdiag_to_state(st_c))
        keys.append(k_l)
        vals.append(v_l)

    y_prompt = x[:N_CTX_TOK].reshape(BATCH, SEQ, D_MODEL)
    y_sample = x[N_CTX_TOK:].reshape(DEC_BATCH, DEC_SEQ, D_MODEL)
    return (y_prompt, y_sample, jnp.stack(states, axis=1), jnp.stack(keys, axis=1), jnp.stack(vals, axis=1))
```

```python
import functools

import numpy as np
import jax
import jax.numpy as jnp
from jax import lax
from jax.experimental import pallas as pl
from jax.experimental.pallas import tpu as pltpu

D_MODEL = 1024
BATCH = 16
SEQ = 256
DEPTH = 2
DEC_BATCH = 4
DEC_SEQ = 1024
PAST_LEN = 256
GRID_W = 64
GLA_HEADS = 4
GLA_DK = 64
GLA_DV = 128
GLA_KEY = GLA_HEADS * GLA_DK
GLA_VAL = GLA_HEADS * GLA_DV
GLA_GATE_RANK = 16
GLA_GATE_NORM = 16.0
GLA_CHUNK = 64
CONV_CH = 256
CONV_WIDTH = 31
NA_HEADS = 4
NA_HD = 64
NA_DIM = NA_HEADS * NA_HD
NA_WIN_ROWS = 8
NA_WIN_COLS = 16
N_EXPERTS = 64
TOP_K = 8
N_GROUPS = 8
TOPK_GROUPS = 4
EXPERT_DIM = 256
SHARED_DIM = 256
ROUTED_SCALE = 2.5
EPS = 1e-6

N_CTX_TOK = BATCH * SEQ
N_LAT_TOK = DEC_BATCH * DEC_SEQ
N_TOK = N_CTX_TOK + N_LAT_TOK
ROW_TILE = 256
N_CTX_TILES = N_CTX_TOK // ROW_TILE
TILES_PER_LAT_SEQ = DEC_SEQ // ROW_TILE
MOD_ROWS = 8
GLA_IN_W = 1664
GLA_LR_OFF = 2 * GLA_KEY + 2 * GLA_VAL
VMEM_LIMIT = 56 * 1024 * 1024

_BF = jnp.bfloat16
_F32 = jnp.float32


def _bf(x):
    return x.astype(_BF)


def _dot(a, b):
    return jnp.dot(a, b, preferred_element_type=_F32)


def _dot_nt(a, b):
    return lax.dot_general(a, b, (((1,), (1,)), ((), ())), preferred_element_type=_F32)


def _dot_tn(a, b):
    return lax.dot_general(a, b, (((0,), (0,)), ((), ())), preferred_element_type=_F32)


def _split2(x):
    hi = _bf(x)
    lo = _bf(x - hi.astype(_F32))
    return hi, lo


def _split3(x):
    hi = _bf(x)
    r = x - hi.astype(_F32)
    mid = _bf(r)
    lo = _bf(r - mid.astype(_F32))
    return hi, mid, lo


def _dot3(a, b, dot=_dot):
    a_hi, a_lo = _split2(a)
    b_hi, b_lo = _split2(b)
    return (dot(a_lo, b_hi) + dot(a_hi, b_lo)) + dot(a_hi, b_hi)


def _sigmoid(x):
    return 1.0 / (1.0 + jnp.exp(-x))


def _silu(x):
    return x * _sigmoid(x)


def _rms(x, w):
    return x * lax.rsqrt(jnp.mean(x * x, axis=-1, keepdims=True) + EPS) * w


def _params(sem):
    return pltpu.CompilerParams(dimension_semantics=sem, vmem_limit_bytes=VMEM_LIMIT)


def _mod_row(i):
    return jnp.where(i < N_CTX_TILES, 0, 1 + (i - N_CTX_TILES) // TILES_PER_LAT_SEQ)


def _ada_kernel(cv_ref, w_ref, b_ref, o_ref):
    o_ref[0] = _dot3(_silu(cv_ref[...]), w_ref[0]) + b_ref[0]


def _ada_mod(cvec, w_ada, b_ada):
    tn = 1024
    n_out = 6 * D_MODEL
    return pl.pallas_call(
        _ada_kernel,
        out_shape=jax.ShapeDtypeStruct((DEPTH, MOD_ROWS, n_out), _F32),
        grid=(DEPTH, n_out // tn),
        in_specs=[
            pl.BlockSpec((MOD_ROWS, D_MODEL), lambda l, j: (0, 0)),
            pl.BlockSpec((1, D_MODEL, tn), lambda l, j: (l, 0, j)),
            pl.BlockSpec((1, 1, tn), lambda l, j: (l, 0, j)),
        ],
        out_specs=pl.BlockSpec((1, MOD_ROWS, tn), lambda l, j: (l, 0, j)),
        compiler_params=_params(("arbitrary", "arbitrary")),
        name="ada_mod",
    )(cvec, w_ada, b_ada.reshape(DEPTH, 1, n_out))


def _inproj_kernel(xc_ref, xl_ref, nw_ref, mod_ref, wg_ref, wc_ref, wa_ref, g_ref, c_ref, a_ref):
    i = pl.program_id(0)
    row = _mod_row(i)
    sh = mod_ref[pl.ds(row, 1), 0:D_MODEL]
    sc = mod_ref[pl.ds(row, 1), D_MODEL:2 * D_MODEL]
    x = jnp.where(i < N_CTX_TILES, xc_ref[...], xl_ref[...])
    h = _bf(_rms(x, nw_ref[...]) * (1.0 + sc) + sh)
    g_ref[...] = _dot(h, wg_ref[...])
    c_ref[...] = _dot(h, wc_ref[...])
    a_ref[...] = _dot(h, wa_ref[...])


def _stream_tiles(lat_first_tile):
    ctx_tile = lambda i: (jnp.minimum(i, N_CTX_TILES - 1), 0)
    lat_tile = lambda i: (jnp.maximum(i - N_CTX_TILES, 0) + lat_first_tile, 0)
    return ctx_tile, lat_tile


def _in_proj(x_ctx, x_lat, lat_first_tile, norm_w, mod, w_gla, w_conv, w_na):
    full = lambda i: (0, 0)
    tile = lambda i: (i, 0)
    ctx_tile, lat_tile = _stream_tiles(lat_first_tile)
    return pl.pallas_call(
        _inproj_kernel,
        out_shape=(
            jax.ShapeDtypeStruct((N_TOK, GLA_IN_W), _F32),
            jax.ShapeDtypeStruct((N_TOK, 2 * CONV_CH), _F32),
            jax.ShapeDtypeStruct((N_TOK, 3 * NA_DIM), _F32),
        ),
        grid=(N_TOK // ROW_TILE,),
        in_specs=[
            pl.BlockSpec((ROW_TILE, D_MODEL), ctx_tile),
            pl.BlockSpec((ROW_TILE, D_MODEL), lat_tile),
            pl.BlockSpec((1, D_MODEL), full),
            pl.BlockSpec((MOD_ROWS, 6 * D_MODEL), full),
            pl.BlockSpec((D_MODEL, GLA_IN_W), full),
            pl.BlockSpec((D_MODEL, 2 * CONV_CH), full),
            pl.BlockSpec((D_MODEL, 3 * NA_DIM), full),
        ],
        out_specs=(
            pl.BlockSpec((ROW_TILE, GLA_IN_W), tile),
            pl.BlockSpec((ROW_TILE, 2 * CONV_CH), tile),
            pl.BlockSpec((ROW_TILE, 3 * NA_DIM), tile),
        ),
        compiler_params=_params(("arbitrary",)),
        name="in_proj",
    )(x_ctx, x_lat, norm_w, mod, w_gla, w_conv, w_na)


_GLA_LEVELS = (32, 16, 8, 4, 2, 1)
_N_EXP_BLOCKS = 2 + len(_GLA_LEVELS)
_N_MASKS = len(_GLA_LEVELS) + 1


def _gla_constants():
    cs = GLA_CHUNK
    r = np.arange(cs)
    i = r[:, None]
    c = r[None, :]
    w = np.zeros((2, _N_EXP_BLOCKS, cs, cs), np.float32)
    m = np.zeros((2, _N_MASKS, cs, cs), np.float32)
    w[0, 0] = c <= i
    w[0, 1] = c > i
    w[1, 0] = c >= i
    w[1, 1] = c < i
    for lv, half in enumerate(_GLA_LEVELS):
        mid = (r // (2 * half)) * (2 * half) + half
        mi = mid[:, None]
        second = (r >= mid)[:, None]
        same = (r[:, None] // (2 * half)) == (r[None, :] // (2 * half))
        w[0, 2 + lv] = np.where(second, (c >= mi) & (c <= i), (c > i) & (c <= mi - 1))
        w[1, 2 + lv] = np.where(second, (c >= mi) & (c <= i - 1), (c >= i) & (c <= mi - 1))
        m[0, lv] = same & (i >= mi) & (c < mi)
        m[1, lv] = same & (i < mi) & (c >= mi)
    m[:, _N_MASKS - 1] = np.eye(cs)
    w_all = w.reshape(2, _N_EXP_BLOCKS * cs, cs)
    lmask = np.tile(m, (1, 1, 1, GLA_HEADS))
    kr = np.arange(GLA_HEADS * cs)
    bdk = (kr[:, None] // cs) == (np.arange(GLA_KEY)[None, :] // GLA_DK)
    bdv = (kr[:, None] // cs) == (np.arange(GLA_VAL)[None, :] // GLA_DV)
    bdst = (np.arange(GLA_VAL)[:, None] // GLA_DV) == (np.arange(GLA_KEY)[None, :] // GLA_DK)
    return (jnp.asarray(w_all, _BF), jnp.asarray(lmask, _F32), jnp.asarray(bdk, _F32),
            jnp.asarray(bdv, _F32), jnp.asarray(bdst, _F32))


def _gla_kernel(g_ref, wz_ref, bz_ref, wall_ref, lmask_ref, bdk_ref, bdv_ref, bdst_ref, onorm_ref, st0_ref,
                o_ref, stfin_ref, la_scr, o_scr, st_scr, *, n):
    cs = GLA_CHUNK
    nc = n // cs
    gate_rows = 128

    def log_decays(t, carry):
        rows = pl.ds(pl.multiple_of(t * gate_rows, gate_rows), gate_rows)
        z = _dot3(g_ref[rows, GLA_LR_OFF:GLA_IN_W], wz_ref[...]) + bz_ref[...]
        la_scr[rows, :] = (jnp.minimum(z, 0.0) - jnp.log1p(jnp.exp(-jnp.abs(z)))) * (1.0 / GLA_GATE_NORM)
        return carry

    lax.fori_loop(0, n // gate_rows, log_decays, 0)

    def chunk(c, d):
        rows = pl.ds(pl.multiple_of(c * cs, cs), cs)
        q = g_ref[rows, 0:GLA_KEY] * (GLA_DK ** -0.5)
        k = g_ref[rows, GLA_KEY:2 * GLA_KEY]
        v = g_ref[rows, 2 * GLA_KEY:2 * GLA_KEY + GLA_VAL]
        la_hi, la_mid, la_lo = _split3(la_scr[rows, d * GLA_KEY:(d + 1) * GLA_KEY])
        w = wall_ref[d]
        f = jnp.exp((_dot(w, la_lo) + _dot(w, la_mid)) + _dot(w, la_hi))
        st = st_scr[d]
        o = _dot_nt(_bf(q * f[0:cs]), _bf(st))
        p = jnp.zeros((cs, GLA_HEADS * cs), _F32)
        for lv in range(_N_MASKS):
            if lv < len(_GLA_LEVELS):
                fl = f[(2 + lv) * cs:(3 + lv) * cs]
                ql, kl = q * fl, k * fl
            else:
                ql, kl = q, k
            k_bd = _bf(jnp.concatenate([kl] * GLA_HEADS, axis=0) * bdk_ref[...])
            p = p + lmask_ref[d, lv] * _dot_nt(_bf(ql), k_bd)
        v_bd = _bf(jnp.concatenate([v] * GLA_HEADS, axis=0) * bdv_ref[...])
        o_scr[d, rows, :] = o + _dot(_bf(p), v_bd)
        decay = f[cs - 1:cs] if d == 0 else f[0:1]
        st_scr[d] = st * decay + _dot_tn(_bf(v), _bf(k * f[cs:2 * cs])) * bdst_ref[...]

    st_scr[...] = st0_ref[0]

    def scan(i, carry):
        chunk(i, 0)
        chunk(nc - 1 - i, 1)
        return carry

    lax.fori_loop(0, nc, scan, 0)
    stfin_ref[0] = st_scr[...]

    def finish(c, carry):
        rows = pl.ds(pl.multiple_of(c * cs, cs), cs)
        o = o_scr[1, rows, :] + o_scr[0, rows, :]
        for h in range(GLA_HEADS):
            cols = slice(h * GLA_DV, (h + 1) * GLA_DV)
            gate = g_ref[rows, 2 * GLA_KEY + GLA_VAL + h * GLA_DV:2 * GLA_KEY + GLA_VAL + (h + 1) * GLA_DV]
            o_ref[rows, cols] = _rms(o[:, cols], onorm_ref[...]) * _silu(gate)
        return carry

    lax.fori_loop(0, nc, finish, 0)


def _gla(g_all, wz, bz, consts, onorm, st0, *, n, n_seq, first_block):
    w_all, lmask, bdk, bdv, bdst = consts
    full2 = lambda i: (0, 0)
    return pl.pallas_call(
        functools.partial(_gla_kernel, n=n),
        out_shape=(
            jax.ShapeDtypeStruct((n_seq * n, GLA_VAL), _F32),
            jax.ShapeDtypeStruct((n_seq, 2, GLA_VAL, GLA_KEY), _F32),
        ),
        grid=(n_seq,),
        in_specs=[
            pl.BlockSpec((n, GLA_IN_W), lambda i: (first_block + i, 0)),
            pl.BlockSpec(wz.shape, full2),
            pl.BlockSpec(bz.shape, full2),
            pl.BlockSpec(w_all.shape, lambda i: (0, 0, 0)),
            pl.BlockSpec(lmask.shape, lambda i: (0, 0, 0, 0)),
            pl.BlockSpec(bdk.shape, full2),
            pl.BlockSpec(bdv.shape, full2),
            pl.BlockSpec(bdst.shape, full2),
            pl.BlockSpec((1, GLA_DV), full2),
            pl.BlockSpec((1, 2, GLA_VAL, GLA_KEY), lambda i: (i, 0, 0, 0)),
        ],
        out_specs=(
            pl.BlockSpec((n, GLA_VAL), lambda i: (i, 0)),
            pl.BlockSpec((1, 2, GLA_VAL, GLA_KEY), lambda i: (i, 0, 0, 0)),
        ),
        scratch_shapes=[pltpu.VMEM((n, 2 * GLA_KEY), _F32), pltpu.VMEM((2, n, GLA_VAL), _F32),
                        pltpu.VMEM((2, GLA_VAL, GLA_KEY), _F32)],
        compiler_params=_params(("arbitrary",)),
        name=f"gla_{n}",
    )(g_all, wz, bz, w_all, lmask, bdk, bdv, bdst, onorm, st0)


def _state_to_blockdiag(s):
    b = s.shape[0]
    st = jnp.swapaxes(s, -1, -2)
    eye = jnp.eye(GLA_HEADS, dtype=s.dtype)
    out = st[:, :, :, :, None, :] * eye[None, None, :, None, :, None]
    return out.reshape(b, 2, GLA_VAL, GLA_KEY)


def _blockdiag_to_state(st):
    b = st.shape[0]
    s6 = st.reshape(b, 2, GLA_HEADS, GLA_DV, GLA_HEADS, GLA_DK)
    diag = jnp.stack([s6[:, :, h, :, h, :] for h in range(GLA_HEADS)], axis=2)
    return jnp.swapaxes(diag, -1, -2)


_CONV_PAD = 16
_CONV_ROWS = 128


def _conv_kernel(c_ref, dw_ref, dwb_ref, lng_ref, lnb_ref, pw_ref, o_ref, pad_scr, *, n):
    zeros = jnp.zeros((_CONV_PAD, CONV_CH), _F32)
    pad_scr[0:_CONV_PAD, :] = zeros
    pad_scr[_CONV_PAD + n:2 * _CONV_PAD + n, :] = zeros
    for r0 in range(0, n, _CONV_ROWS):
        a = c_ref[r0:r0 + _CONV_ROWS, :]
        pad_scr[_CONV_PAD + r0:_CONV_PAD + r0 + _CONV_ROWS, :] = a[:, :CONV_CH] * _sigmoid(a[:, CONV_CH:])
    half = CONV_WIDTH // 2
    for r0 in range(0, n, _CONV_ROWS):
        acc = jnp.zeros((_CONV_ROWS, CONV_CH), _F32)
        for w in range(CONV_WIDTH):
            s = _CONV_PAD + r0 + w - half
            acc = acc + pad_scr[s:s + _CONV_ROWS, :] * dw_ref[w:w + 1, :]
        acc = acc + dwb_ref[...]
        xc = acc - jnp.mean(acc, axis=-1, keepdims=True)
        y = xc * lax.rsqrt(jnp.mean(xc * xc, axis=-1, keepdims=True) + EPS) * lng_ref[...] + lnb_ref[...]
        o_ref[r0:r0 + _CONV_ROWS, :] = _dot(_bf(_silu(y)), pw_ref[...])


def _conv(c_all, dw, dwb, lng, lnb, pw, *, n, n_seq, first_block):
    full2 = lambda i: (0, 0)
    return pl.pallas_call(
        functools.partial(_conv_kernel, n=n),
        out_shape=jax.ShapeDtypeStruct((n_seq * n, CONV_CH), _F32),
        grid=(n_seq,),
        in_specs=[
            pl.BlockSpec((n, 2 * CONV_CH), lambda i: (first_block + i, 0)),
            pl.BlockSpec((CONV_WIDTH, CONV_CH), full2),
            pl.BlockSpec((1, CONV_CH), full2),
            pl.BlockSpec((1, CONV_CH), full2),
            pl.BlockSpec((1, CONV_CH), full2),
            pl.BlockSpec((CONV_CH, CONV_CH), full2),
        ],
        out_specs=pl.BlockSpec((n, CONV_CH), lambda i: (i, 0)),
        scratch_shapes=[pltpu.VMEM((n + 2 * _CONV_PAD, CONV_CH), _F32)],
        compiler_params=_params(("arbitrary",)),
        name=f"conv_{n}",
    )(c_all, dw, dwb, lng, lnb, pw)


def _softmax_rows(s):
    e = jnp.exp(s - jnp.max(s, axis=-1, keepdims=True))
    return e / jnp.sum(e, axis=-1, keepdims=True)


def _na_ctx_kernel(a_ref, qn_ref, kn_ref, o_ref, kc_ref, vc_ref):
    for h in range(NA_HEADS):
        cols = slice(h * NA_HD, (h + 1) * NA_HD)
        q = _rms(a_ref[:, h * NA_HD:(h + 1) * NA_HD], qn_ref[...])
        k = _rms(a_ref[:, NA_DIM + h * NA_HD:NA_DIM + (h + 1) * NA_HD], kn_ref[...])
        v = a_ref[:, 2 * NA_DIM + h * NA_HD:2 * NA_DIM + (h + 1) * NA_HD]
        kc_ref[0, h] = k
        vc_ref[0, h] = v
        p = _softmax_rows(_dot_nt(_bf(q), _bf(k)) * (NA_HD ** -0.5))
        o_ref[:, cols] = _dot(_bf(p), _bf(v))


def _na_ctx(a_all, qn, kn):
    full2 = lambda i: (0, 0)
    cache = jax.ShapeDtypeStruct((BATCH, NA_HEADS, SEQ, NA_HD), _F32)
    cache_spec = pl.BlockSpec((1, NA_HEADS, SEQ, NA_HD), lambda i: (i, 0, 0, 0))
    return pl.pallas_call(
        _na_ctx_kernel,
        out_shape=(jax.ShapeDtypeStruct((N_CTX_TOK, NA_DIM), _F32), cache, cache),
        grid=(BATCH,),
        in_specs=[
            pl.BlockSpec((SEQ, 3 * NA_DIM), lambda i: (i, 0)),
            pl.BlockSpec((1, NA_HD), full2),
            pl.BlockSpec((1, NA_HD), full2),
        ],
        out_specs=(pl.BlockSpec((SEQ, NA_DIM), lambda i: (i, 0)), cache_spec, cache_spec),
        compiler_params=_params(("arbitrary",)),
        name="na_ctx",
    )(a_all, qn, kn)


_NA_ROWS = DEC_SEQ // GRID_W
_NA_KEYS = NA_WIN_ROWS * GRID_W
_NA_VARIANTS = NA_WIN_ROWS
_NA_ROW_UNROLL = 8


_RPB_ROWS = 2 * NA_WIN_ROWS - 1
_RPB_COLS = 2 * NA_WIN_COLS - 1


def _na_bias_constants():
    qc = np.arange(GRID_W)[:, None]
    kc = np.arange(GRID_W)[None, :]
    shift = np.stack([(kc - qc + NA_WIN_COLS - 1) == co for co in range(_RPB_COLS)]).astype(np.float32)
    win_start = np.clip(qc - NA_WIN_COLS // 2, 0, GRID_W - NA_WIN_COLS)
    in_win = ((kc >= win_start) & (kc < win_start + NA_WIN_COLS)).astype(np.float32)
    return jnp.asarray(shift), jnp.asarray(in_win)


def _na_bias_kernel(rpb_ref, shift_ref, win_ref, o_ref):
    h = pl.program_id(0)
    in_win = win_ref[...] > 0.0
    tiles = []
    for ro in range(_RPB_ROWS):
        base = (h * _RPB_ROWS + ro) * _RPB_COLS
        acc = rpb_ref[base] * shift_ref[0]
        for co in range(1, _RPB_COLS):
            acc = acc + rpb_ref[base + co] * shift_ref[co]
        tiles.append(jnp.where(in_win, acc, -jnp.inf))
    for t in range(_NA_VARIANTS):
        o_ref[0, t] = jnp.concatenate([tiles[kr - t + NA_WIN_ROWS - 1] for kr in range(NA_WIN_ROWS)], axis=1)


def _na_bias_table(rpb):
    shift, in_win = _na_bias_constants()
    return pl.pallas_call(
        _na_bias_kernel,
        out_shape=jax.ShapeDtypeStruct((NA_HEADS, _NA_VARIANTS, GRID_W, _NA_KEYS), _F32),
        grid=(NA_HEADS,),
        in_specs=[
            pl.BlockSpec(memory_space=pltpu.SMEM),
            pl.BlockSpec(shift.shape, lambda h: (0, 0, 0)),
            pl.BlockSpec(in_win.shape, lambda h: (0, 0)),
        ],
        out_specs=pl.BlockSpec((1, _NA_VARIANTS, GRID_W, _NA_KEYS), lambda h: (h, 0, 0, 0)),
        compiler_params=_params(("arbitrary",)),
        name="na_bias",
    )(rpb.reshape(-1), shift, in_win)


def _na_lat_kernel(a_ref, kctx_ref, vctx_ref, bias_ref, qn_ref, kn_ref, o_ref, q_scr, k_scr, v_scr):
    scale = NA_HD ** -0.5
    for h in range(NA_HEADS):
        q_scr[...] = _bf(_rms(a_ref[:, h * NA_HD:(h + 1) * NA_HD], qn_ref[...]))
        k_scr[...] = _bf(_rms(a_ref[:, NA_DIM + h * NA_HD:NA_DIM + (h + 1) * NA_HD], kn_ref[...]))
        v_scr[...] = _bf(a_ref[:, 2 * NA_DIM + h * NA_HD:2 * NA_DIM + (h + 1) * NA_HD])
        k_ctx = _bf(kctx_ref[0, h])
        v_ctx = _bf(vctx_ref[0, h])

        def row(r):
            start = jnp.clip(r - NA_WIN_ROWS // 2, 0, _NA_ROWS - NA_WIN_ROWS)
            qrows = pl.ds(pl.multiple_of(r * GRID_W, GRID_W), GRID_W)
            krows = pl.ds(pl.multiple_of(start * GRID_W, GRID_W), _NA_KEYS)
            q = q_scr[qrows, :]
            s_loc = _dot_nt(q, k_scr[krows, :]) * scale + bias_ref[h, r - start]
            s_ctx = _dot_nt(q, k_ctx) * scale
            m = jnp.maximum(jnp.max(s_loc, axis=-1, keepdims=True), jnp.max(s_ctx, axis=-1, keepdims=True))
            e_loc = jnp.exp(s_loc - m)
            e_ctx = jnp.exp(s_ctx - m)
            inv = 1.0 / (jnp.sum(e_loc, axis=-1, keepdims=True) + jnp.sum(e_ctx, axis=-1, keepdims=True))
            o = _dot(_bf(e_loc * inv), v_scr[krows, :]) + _dot(_bf(e_ctx * inv), v_ctx)
            o_ref[qrows, h * NA_HD:(h + 1) * NA_HD] = o

        def row_group(g, carry):
            for dr in range(_NA_ROW_UNROLL):
                row(g * _NA_ROW_UNROLL + dr)
            return carry

        lax.fori_loop(0, _NA_ROWS // _NA_ROW_UNROLL, row_group, 0)


def _na_lat(a_all, k_ctx, v_ctx, bias, qn, kn):
    full2 = lambda i: (0, 0)
    ctx_spec = pl.BlockSpec((1, NA_HEADS, PAST_LEN, NA_HD), lambda i: (i, 0, 0, 0))
    return pl.pallas_call(
        _na_lat_kernel,
        out_shape=jax.ShapeDtypeStruct((N_LAT_TOK, NA_DIM), _F32),
        grid=(DEC_BATCH,),
        in_specs=[
            pl.BlockSpec((DEC_SEQ, 3 * NA_DIM), lambda i: (N_CTX_TOK // DEC_SEQ + i, 0)),
            ctx_spec,
            ctx_spec,
            pl.BlockSpec(bias.shape, lambda i: (0, 0, 0, 0)),
            pl.BlockSpec((1, NA_HD), full2),
            pl.BlockSpec((1, NA_HD), full2),
        ],
        out_specs=pl.BlockSpec((DEC_SEQ, NA_DIM), lambda i: (i, 0)),
        scratch_shapes=[pltpu.VMEM((DEC_SEQ, NA_HD), _BF)] * 3,
        compiler_params=_params(("arbitrary",)),
        name="na_lat",
    )(a_all, k_ctx, v_ctx, bias, qn, kn)


def _outproj_kernel(gc_ref, gl_ref, cc_ref, cl_ref, nc_ref, nl_ref, xc_ref, xl_ref, mod_ref, nw_ref, wo_ref,
                    rwt_ref, x1_ref, h2_ref, hr_ref, lg_ref):
    i = pl.program_id(0)
    is_ctx = i < N_CTX_TILES
    row = _mod_row(i)
    gla = jnp.where(is_ctx, gc_ref[...], gl_ref[...])
    conv = jnp.where(is_ctx, cc_ref[...], cl_ref[...])
    na = jnp.where(is_ctx, nc_ref[...], nl_ref[...])
    mix = (_dot(_bf(gla), wo_ref[0:GLA_VAL, :])
           + _dot(_bf(conv), wo_ref[GLA_VAL:GLA_VAL + CONV_CH, :])
           + _dot(_bf(na), wo_ref[GLA_VAL + CONV_CH:, :]))
    g1 = mod_ref[pl.ds(row, 1), 2 * D_MODEL:3 * D_MODEL]
    sh2 = mod_ref[pl.ds(row, 1), 3 * D_MODEL:4 * D_MODEL]
    sc2 = mod_ref[pl.ds(row, 1), 4 * D_MODEL:5 * D_MODEL]
    x1 = jnp.where(is_ctx, xc_ref[...], xl_ref[...]) + g1 * mix
    h2 = _rms(x1, nw_ref[...]) * (1.0 + sc2) + sh2
    x1_ref[...] = x1
    h2_ref[...] = _bf(h2)
    for j in range(_ROW_CHUNKS):
        hr_ref[pl.ds(j, ROW_TILE, stride=_ROW_CHUNKS), :] = h2[:, j * _LANE:(j + 1) * _LANE]
    lg_ref[...] = _dot3(rwt_ref[...], h2, dot=_dot_nt)


def _out_proj(gla_c, gla_l, conv_c, conv_l, na_c, na_l, x_ctx, x_lat, lat_first_tile, mod, norm_w, w_out,
              router_wt):
    full2 = lambda i: (0, 0)
    tile = lambda i: (i, 0)
    ctx_tile, lat_tile = _stream_tiles(0)
    _, x_lat_tile = _stream_tiles(lat_first_tile)
    return pl.pallas_call(
        _outproj_kernel,
        out_shape=(
            jax.ShapeDtypeStruct((N_TOK, D_MODEL), _F32),
            jax.ShapeDtypeStruct((N_TOK, D_MODEL), _BF),
            jax.ShapeDtypeStruct((N_TOK * _ROW_CHUNKS, _LANE), _F32),
            jax.ShapeDtypeStruct((N_EXPERTS, N_TOK), _F32),
        ),
        grid=(N_TOK // ROW_TILE,),
        in_specs=[
            pl.BlockSpec((ROW_TILE, GLA_VAL), ctx_tile),
            pl.BlockSpec((ROW_TILE, GLA_VAL), lat_tile),
            pl.BlockSpec((ROW_TILE, CONV_CH), ctx_tile),
            pl.BlockSpec((ROW_TILE, CONV_CH), lat_tile),
            pl.BlockSpec((ROW_TILE, NA_DIM), ctx_tile),
            pl.BlockSpec((ROW_TILE, NA_DIM), lat_tile),
            pl.BlockSpec((ROW_TILE, D_MODEL), ctx_tile),
            pl.BlockSpec((ROW_TILE, D_MODEL), x_lat_tile),
            pl.BlockSpec((MOD_ROWS, 6 * D_MODEL), full2),
            pl.BlockSpec((1, D_MODEL), full2),
            pl.BlockSpec((D_MODEL, D_MODEL), full2),
            pl.BlockSpec((N_EXPERTS, D_MODEL), full2),
        ],
        out_specs=(
            pl.BlockSpec((ROW_TILE, D_MODEL), tile),
            pl.BlockSpec((ROW_TILE, D_MODEL), tile),
            pl.BlockSpec((ROW_TILE * _ROW_CHUNKS, _LANE), tile),
            pl.BlockSpec((N_EXPERTS, ROW_TILE), lambda i: (0, i)),
        ),
        compiler_params=_params(("arbitrary",)),
        name="out_proj",
    )(gla_c, gla_l, conv_c, conv_l, na_c, na_l, x_ctx, x_lat, mod, norm_w, w_out, router_wt)


_PER_GROUP = N_EXPERTS // N_GROUPS
_ROUTE_TILE = 1024


def _first_max(x, idx, axes, sentinel):
    m = x
    for ax in axes:
        m = jnp.max(m, axis=ax, keepdims=True)
    first = jnp.where(x == m, idx, sentinel)
    for ax in axes:
        first = jnp.min(first, axis=ax, keepdims=True)
    return m, first


def _router_kernel(lg_ref, rb_ref, eid_ref, wts_ref, cnt_ref):
    t = lg_ref.shape[1]
    shape3 = (N_GROUPS, _PER_GROUP, t)
    scores = _sigmoid(lg_ref[...])
    biased = (scores + rb_ref[...]).reshape(shape3)
    scores = scores.reshape(shape3)
    neg = -jnp.inf
    in_grp = lax.broadcasted_iota(jnp.int32, shape3, 1)
    grp = lax.broadcasted_iota(jnp.int32, (N_GROUPS, 1, t), 0)
    expert = lax.broadcasted_iota(jnp.int32, shape3, 0) * _PER_GROUP + in_grp
    m1, i1 = _first_max(biased, in_grp, (1,), _PER_GROUP)
    m2 = jnp.max(jnp.where(in_grp == i1, neg, biased), axis=1, keepdims=True)
    gscore = m1 + m2
    keep = jnp.zeros((N_GROUPS, 1, t), _F32)
    for _ in range(TOPK_GROUPS):
        _, gi = _first_max(gscore, grp, (0,), N_GROUPS)
        hit = grp == gi
        keep = jnp.where(hit, 1.0, keep)
        gscore = jnp.where(hit, neg, gscore)
    masked = jnp.where(keep > 0.0, biased, neg)
    ids, picked = [], []
    hits = jnp.zeros(shape3, _F32)
    for _ in range(TOP_K):
        _, ei = _first_max(masked, expert, (1, 0), N_EXPERTS)
        hit = expert == ei
        hits = jnp.where(hit, 1.0, hits)
        sc = jnp.sum(jnp.sum(jnp.where(hit, scores, 0.0), axis=1, keepdims=True), axis=0, keepdims=True)
        ids.append(ei.reshape(1, t))
        picked.append(sc.reshape(1, t))
        masked = jnp.where(hit, neg, masked)
    picked = jnp.concatenate(picked, axis=0)
    den = jnp.sum(picked, axis=0, keepdims=True)
    eid_ref[...] = jnp.concatenate(ids, axis=0)
    wts_ref[...] = picked / den * ROUTED_SCALE
    cnt_ref[0] = jnp.sum(hits, axis=2, keepdims=True).reshape(N_EXPERTS, 1)


def _router(logits_t, router_bias):
    tile = lambda i: (0, i)
    return pl.pallas_call(
        _router_kernel,
        out_shape=(jax.ShapeDtypeStruct((TOP_K, N_TOK), jnp.int32), jax.ShapeDtypeStruct((TOP_K, N_TOK), _F32),
                   jax.ShapeDtypeStruct((N_TOK // _ROUTE_TILE, N_EXPERTS, 1), _F32)),
        grid=(N_TOK // _ROUTE_TILE,),
        in_specs=[
            pl.BlockSpec((N_EXPERTS, _ROUTE_TILE), tile),
            pl.BlockSpec((N_EXPERTS, 1), lambda i: (0, 0)),
        ],
        out_specs=(pl.BlockSpec((TOP_K, _ROUTE_TILE), tile), pl.BlockSpec((TOP_K, _ROUTE_TILE), tile),
                   pl.BlockSpec((1, N_EXPERTS, 1), lambda i: (i, 0, 0))),
        compiler_params=_params(("arbitrary",)),
        name="router",
    )(logits_t, router_bias)


_HALF_TOK = N_CTX_TOK
_N_HALVES = N_TOK // _HALF_TOK
_HALF_ASSIGN = _HALF_TOK * TOP_K
_GB = 256
_GB_MAX = _HALF_ASSIGN // _GB + N_EXPERTS
_LANE = 128
_ROW_CHUNKS = D_MODEL // _LANE
_GS = _GB + 1
_STAGE_ROWS = (_ROW_CHUNKS * _GS + 7) // 8 * 8
_RMW_BATCH = 8
_TOK_BITS = 12
assert _HALF_TOK == 1 << _TOK_BITS
_N_GROUPS_ALL = _N_HALVES * N_EXPERTS
_GROUP_PAD = _GB - 1
_HALF_SORTED = _HALF_ASSIGN + N_EXPERTS * _GROUP_PAD


def _moe_plan(eid, wts, tile_counts):
    tok = jnp.arange(N_TOK, dtype=jnp.int32)
    grp_of = ((tok >> _TOK_BITS) * N_EXPERTS)[None, :] + eid
    key = (grp_of << (_TOK_BITS + 1)) + (tok & (_HALF_TOK - 1))[None, :]
    pad_key = np.repeat(np.arange(_N_GROUPS_ALL, dtype=np.int32) << (_TOK_BITS + 1), _GROUP_PAD) + _HALF_TOK
    keys = jnp.concatenate([key.reshape(-1), jnp.asarray(pad_key)])
    gates = jnp.concatenate([wts.reshape(-1), jnp.zeros((pad_key.size,), _F32)])
    key_s, gate_s = lax.sort((keys, gates), num_keys=1)
    row_off = (key_s & (2 * _HALF_TOK - 1)) * _ROW_CHUNKS
    count = jnp.sum(tile_counts.reshape(_N_HALVES, -1, N_EXPERTS), axis=1).astype(jnp.int32)
    seg = count + _GROUP_PAD
    seg_start = jnp.cumsum(seg, axis=1) - seg
    nblk = (count + _GB - 1) // _GB
    blk_end = jnp.cumsum(nblk, axis=1)
    blk_start = blk_end - nblk
    n_used = blk_end[:, -1]
    b = jnp.minimum(jnp.arange(_GB_MAX, dtype=jnp.int32)[None, :], n_used[:, None] - 1)
    grp = jnp.sum((b[:, :, None] >= blk_end[:, None, :]).astype(jnp.int32), axis=-1)
    onehot = (grp[:, :, None] == jnp.arange(N_EXPERTS, dtype=jnp.int32)).astype(jnp.int32)
    pick = lambda v: jnp.sum(onehot * v[:, None, :], axis=-1)
    start = pick(seg_start) + (b - pick(blk_start)) * _GB
    return (row_off.reshape(_N_HALVES, _HALF_SORTED), gate_s.reshape(_N_HALVES, _HALF_SORTED),
            grp.astype(jnp.int32), start.astype(jnp.int32), n_used.astype(jnp.int32))


def _gmm_kernel(be_ref, bstart_ref, nused_ref, off_ref, gate_ref,
                h_ref, wg_ref, wu_ref, wd_ref, acc_ref, xt_scr, yt_scr, wgb_scr, wub_scr, wdb_scr):
    s = pl.program_id(0)
    n_used = nused_ref[0]
    last = _GB_MAX - 1

    @pl.when(s == 0)
    def _():
        acc_ref[...] = jnp.zeros_like(acc_ref)
        xt_scr[...] = jnp.zeros_like(xt_scr)
        yt_scr[...] = jnp.zeros_like(yt_scr)

    fb = jnp.clip(s - 1, 0, last)

    @pl.when((s == 0) | (be_ref[fb] != be_ref[jnp.clip(s - 2, 0, last)]))
    def _():
        wgb_scr[...] = _bf(wg_ref[0, 0])
        wub_scr[...] = _bf(wu_ref[0, 0])
        wdb_scr[...] = _bf(wd_ref[0, 0])

    @pl.when(s < n_used + 2)
    def _():
        slot = s % 2
        other = 1 - slot

        c_start = bstart_ref[jnp.clip(s - 2, 0, last)]
        for m0 in range(0, _GB, _RMW_BATCH):
            pending = []
            for mi in range(m0, m0 + _RMW_BATCH):
                rows = pl.ds(pl.multiple_of(off_ref[c_start + mi], _ROW_CHUNKS), _ROW_CHUNKS)
                contrib = gate_ref[c_start + mi] * yt_scr[slot, pl.ds(mi, _ROW_CHUNKS, stride=_GS), :]
                pending.append((rows, acc_ref[rows, :] + contrib))
            for rows, val in pending:
                acc_ref[rows, :] = val

        x = _bf(jnp.concatenate([xt_scr[other, j * _GS:j * _GS + _GB, :] for j in range(_ROW_CHUNKS)], axis=-1))
        act = _silu(_dot(x, wgb_scr[...])) * _dot(x, wub_scr[...])
        y = _dot(_bf(act), wdb_scr[...])
        for j in range(_ROW_CHUNKS):
            yt_scr[other, j * _GS:j * _GS + _GB, :] = y[:, j * _LANE:(j + 1) * _LANE]

        g_start = bstart_ref[jnp.minimum(s, last)]
        for mi in range(_GB):
            tok_row = off_ref[g_start + mi] & (_HALF_TOK * _ROW_CHUNKS - 1)
            src = pl.ds(pl.multiple_of(tok_row, _ROW_CHUNKS), _ROW_CHUNKS)
            xt_scr[slot, pl.ds(mi, _ROW_CHUNKS, stride=_GS), :] = h_ref[src, :]


def _gmm(layer, half, plan, h_rows, wg, wu, wd):
    row_off, gate_s, grp, start, n_used = plan
    expert = lambda s, be, *_: (layer, be[jnp.clip(s - 1, 0, _GB_MAX - 1)], 0, 0)
    acc_rows = (_HALF_TOK + 1) * _ROW_CHUNKS
    grid_spec = pltpu.PrefetchScalarGridSpec(
        num_scalar_prefetch=5,
        grid=(_GB_MAX + 2,),
        in_specs=[
            pl.BlockSpec((_HALF_TOK * _ROW_CHUNKS, _LANE), lambda b, *_: (half, 0), pipeline_mode=pl.Buffered(1)),
            pl.BlockSpec((1, 1, D_MODEL, EXPERT_DIM), expert),
            pl.BlockSpec((1, 1, D_MODEL, EXPERT_DIM), expert),
            pl.BlockSpec((1, 1, EXPERT_DIM, D_MODEL), expert),
        ],
        out_specs=pl.BlockSpec((acc_rows, _LANE), lambda b, *_: (0, 0), pipeline_mode=pl.Buffered(1)),
        scratch_shapes=[pltpu.VMEM((2, _STAGE_ROWS, _LANE), _F32)] * 2
        + [pltpu.VMEM((D_MODEL, EXPERT_DIM), _BF)] * 2 + [pltpu.VMEM((EXPERT_DIM, D_MODEL), _BF)],
    )
    acc = pl.pallas_call(
        _gmm_kernel,
        out_shape=jax.ShapeDtypeStruct((acc_rows, _LANE), _F32),
        grid_spec=grid_spec,
        compiler_params=_params(("arbitrary",)),
        name="moe_experts",
    )(grp[half], start[half], n_used[half:half + 1], row_off[half], gate_s[half], h_rows, wg, wu, wd)
    return acc


_FIN_TILE = 512


def _moe_finish_kernel(h_ref, rc_ref, rl_ref, x1_ref, mod_ref, sg_ref, su_ref, sd_ref, o_ref):
    t = pl.program_id(0)
    tiles_per_half = _HALF_TOK // _FIN_TILE
    is_ctx = t < tiles_per_half
    row = jnp.where(is_ctx, 0, 1 + (t - tiles_per_half) // (DEC_SEQ // _FIN_TILE))
    h = h_ref[...]
    shared = _dot(_bf(_silu(_dot(h, _bf(sg_ref[0]))) * _dot(h, _bf(su_ref[0]))), _bf(sd_ref[0]))
    routed = jnp.concatenate(
        [jnp.where(is_ctx, rc_ref[pl.ds(j, _FIN_TILE, stride=_ROW_CHUNKS), :],
                   rl_ref[pl.ds(j, _FIN_TILE, stride=_ROW_CHUNKS), :]) for j in range(_ROW_CHUNKS)], axis=-1)
    g2 = mod_ref[pl.ds(row, 1), 5 * D_MODEL:6 * D_MODEL]
    o_ref[...] = x1_ref[...] + g2 * (routed + shared)


def _moe_finish(layer, h2, routed_c, routed_l, x1, mod, sg, su, sd):
    tile = lambda t: (t, 0)
    tiles_per_half = _HALF_TOK // _FIN_TILE
    ctx_tile = lambda t: (jnp.minimum(t, tiles_per_half - 1), 0)
    lat_tile = lambda t: (jnp.maximum(t - tiles_per_half, 0), 0)
    shared = lambda t: (layer, 0, 0)
    return pl.pallas_call(
        _moe_finish_kernel,
        out_shape=jax.ShapeDtypeStruct((N_TOK, D_MODEL), _F32),
        grid=(N_TOK // _FIN_TILE,),
        in_specs=[
            pl.BlockSpec((_FIN_TILE, D_MODEL), tile),
            pl.BlockSpec((_FIN_TILE * _ROW_CHUNKS, _LANE), ctx_tile),
            pl.BlockSpec((_FIN_TILE * _ROW_CHUNKS, _LANE), lat_tile),
            pl.BlockSpec((_FIN_TILE, D_MODEL), tile),
            pl.BlockSpec((MOD_ROWS, 6 * D_MODEL), lambda t: (0, 0)),
            pl.BlockSpec((1, D_MODEL, SHARED_DIM), shared),
            pl.BlockSpec((1, D_MODEL, SHARED_DIM), shared),
            pl.BlockSpec((1, SHARED_DIM, D_MODEL), shared),
        ],
        out_specs=pl.BlockSpec((_FIN_TILE, D_MODEL), tile),
        compiler_params=_params(("arbitrary",)),
        name="moe_finish",
    )(h2, routed_c, routed_l, x1, mod, sg, su, sd)


def kernel(x_prompt, x_sample, state_gla, cache_na_k, cache_na_v, c, c_ctx, w_ada, b_ada, norm_mix, norm_ffn, w_in, gla_w_gate, gla_b_gate, gla_out_norm, conv_dw, conv_dw_b, conv_ln_g, conv_ln_b, conv_pw, na_q_norm, na_k_norm, na_rpb, w_out, router_w, router_bias, exp_w_gate, exp_w_up, exp_w_down, sh_w_gate, sh_w_up, sh_w_down):
    x_ctx, x_lat, x_lat_tile = x_prompt.reshape(N_CTX_TOK, D_MODEL), x_sample.reshape(N_LAT_TOK, D_MODEL), 0
    cvec = jnp.concatenate([c_ctx[None], c, jnp.zeros((MOD_ROWS - 1 - DEC_BATCH, D_MODEL), _F32)], axis=0)
    mod_all = _ada_mod(cvec, w_ada, b_ada)
    gla_consts = _gla_constants()
    zero_state = jnp.zeros((BATCH, 2, GLA_VAL, GLA_KEY), _F32)
    lat_first = N_CTX_TOK // DEC_SEQ

    states, keys, vals = [], [], []
    for l in range(DEPTH):
        mod = mod_all[l]
        wi = w_in[l]
        w_gla = _bf(jnp.pad(wi[:, :GLA_LR_OFF + 2 * GLA_GATE_RANK], ((0, 0), (0, GLA_IN_W - GLA_LR_OFF - 2 * GLA_GATE_RANK))))
        conv_off = GLA_LR_OFF + 2 * GLA_GATE_RANK
        w_conv = _bf(wi[:, conv_off:conv_off + 2 * CONV_CH])
        w_na = _bf(wi[:, conv_off + 2 * CONV_CH:])
        g_all, c_all, a_all = _in_proj(x_ctx, x_lat, x_lat_tile, norm_mix[l][None], mod, w_gla, w_conv, w_na)

        wz = jnp.zeros((GLA_IN_W - GLA_LR_OFF, 2 * GLA_KEY), _F32)
        wz = wz.at[:GLA_GATE_RANK, :GLA_KEY].set(gla_w_gate[l, 0])
        wz = wz.at[GLA_GATE_RANK:2 * GLA_GATE_RANK, GLA_KEY:].set(gla_w_gate[l, 1])
        bz = gla_b_gate[l].reshape(1, 2 * GLA_KEY)
        onorm = gla_out_norm[l][None]
        gla_c, st_c = _gla(g_all, wz, bz, gla_consts, onorm, zero_state, n=SEQ, n_seq=BATCH, first_block=0)
        gla_l, _ = _gla(g_all, wz, bz, gla_consts, onorm, _state_to_blockdiag(state_gla[:, l]),
                        n=DEC_SEQ, n_seq=DEC_BATCH, first_block=lat_first)

        conv_args = (conv_dw[l], conv_dw_b[l][None], conv_ln_g[l][None], conv_ln_b[l][None], _bf(conv_pw[l]))
        conv_c = _conv(c_all, *conv_args, n=SEQ, n_seq=BATCH, first_block=0)
        conv_l = _conv(c_all, *conv_args, n=DEC_SEQ, n_seq=DEC_BATCH, first_block=lat_first)

        qn, kn = na_q_norm[l][None], na_k_norm[l][None]
        na_c, k_l, v_l = _na_ctx(a_all, qn, kn)
        na_l = _na_lat(a_all, cache_na_k[:, l], cache_na_v[:, l], _na_bias_table(na_rpb[l]), qn, kn)

        x1, h2, h_rows, logits_t = _out_proj(gla_c, gla_l, conv_c, conv_l, na_c, na_l, x_ctx, x_lat, x_lat_tile, mod,
                                     norm_ffn[l][None], _bf(w_out[l]), router_w[l].T)
        plan = _moe_plan(*_router(logits_t, router_bias[l][:, None]))
        routed = [_gmm(l, half, plan, h_rows, exp_w_gate, exp_w_up, exp_w_down) for half in range(_N_HALVES)]
        x = _moe_finish(l, h2, routed[0], routed[1], x1, mod, sh_w_gate, sh_w_up, sh_w_down)
        x_ctx, x_lat, x_lat_tile = x, x, N_CTX_TILES

        states.append(_blockdiag_to_state(st_c))
        keys.append(k_l)
        vals.append(v_l)

    y_prompt = x[:N_CTX_TOK].reshape(BATCH, SEQ, D_MODEL)
    y_sample = x[N_CTX_TOK:].reshape(DEC_BATCH, DEC_SEQ, D_MODEL)
    return (y_prompt, y_sample, jnp.stack(states, axis=1), jnp.stack(keys, axis=1), jnp.stack(vals, axis=1))
```

```python
import functools

import numpy as np
import jax
import jax.numpy as jnp
from jax import lax
from jax.experimental import pallas as pl
from jax.experimental.pallas import tpu as pltpu

D_MODEL = 1024
BATCH = 16
SEQ = 256
DEPTH = 2
DEC_BATCH = 4
DEC_SEQ = 1024
PAST_LEN = 256
GRID_W = 64
GLA_HEADS = 4
GLA_DK = 64
GLA_DV = 128
GLA_KEY = GLA_HEADS * GLA_DK
GLA_VAL = GLA_HEADS * GLA_DV
GLA_GATE_RANK = 16
GLA_GATE_NORM = 16.0
GLA_CHUNK = 64
CONV_CH = 256
CONV_WIDTH = 31
NA_HEADS = 4
NA_HD = 64
NA_DIM = NA_HEADS * NA_HD
NA_WIN_ROWS = 8
NA_WIN_COLS = 16
N_EXPERTS = 64
TOP_K = 8
N_GROUPS = 8
TOPK_GROUPS = 4
EXPERT_DIM = 256
SHARED_DIM = 256
ROUTED_SCALE = 2.5
EPS = 1e-6

N_CTX_TOK = BATCH * SEQ
N_LAT_TOK = DEC_BATCH * DEC_SEQ
N_TOK = N_CTX_TOK + N_LAT_TOK
ROW_TILE = 256
N_CTX_TILES = N_CTX_TOK // ROW_TILE
TILES_PER_LAT_SEQ = DEC_SEQ // ROW_TILE
MOD_ROWS = 8
GLA_IN_W = 1664
GLA_LR_OFF = 2 * GLA_KEY + 2 * GLA_VAL
VMEM_LIMIT = 56 * 1024 * 1024

_BF = jnp.bfloat16
_F32 = jnp.float32


def _bf(x):
    return x.astype(_BF)


def _dot(a, b):
    return jnp.dot(a, b, preferred_element_type=_F32)


def _dot_nt(a, b):
    return lax.dot_general(a, b, (((1,), (1,)), ((), ())), preferred_element_type=_F32)


def _dot_tn(a, b):
    return lax.dot_general(a, b, (((0,), (0,)), ((), ())), preferred_element_type=_F32)


def _split2(x):
    hi = _bf(x)
    lo = _bf(x - hi.astype(_F32))
    return hi, lo


def _split3(x):
    hi = _bf(x)
    r = x - hi.astype(_F32)
    mid = _bf(r)
    lo = _bf(r - mid.astype(_F32))
    return hi, mid, lo


def _dot3(a, b, dot=_dot):
    a_hi, a_lo = _split2(a)
    b_hi, b_lo = _split2(b)
    return (dot(a_lo, b_hi) + dot(a_hi, b_lo)) + dot(a_hi, b_hi)


def _sigmoid(x):
    return 1.0 / (1.0 + jnp.exp(-x))


def _silu(x):
    return x * _sigmoid(x)


def _rms(x, w):
    return x * lax.rsqrt(jnp.mean(x * x, axis=-1, keepdims=True) + EPS) * w


def _params(sem):
    return pltpu.CompilerParams(dimension_semantics=sem, vmem_limit_bytes=VMEM_LIMIT)


def _mod_row(i):
    return jnp.where(i < N_CTX_TILES, 0, 1 + (i - N_CTX_TILES) // TILES_PER_LAT_SEQ)


def _ada_kernel(cv_ref, w_ref, b_ref, o_ref):
    o_ref[0] = _dot3(_silu(cv_ref[...]), w_ref[0]) + b_ref[0]


def _ada_mod(cvec, w_ada, b_ada):
    tn = 1024
    n_out = 6 * D_MODEL
    return pl.pallas_call(
        _ada_kernel,
        out_shape=jax.ShapeDtypeStruct((DEPTH, MOD_ROWS, n_out), _F32),
        grid=(DEPTH, n_out // tn),
        in_specs=[
            pl.BlockSpec((MOD_ROWS, D_MODEL), lambda l, j: (0, 0)),
            pl.BlockSpec((1, D_MODEL, tn), lambda l, j: (l, 0, j)),
            pl.BlockSpec((1, 1, tn), lambda l, j: (l, 0, j)),
        ],
        out_specs=pl.BlockSpec((1, MOD_ROWS, tn), lambda l, j: (l, 0, j)),
        compiler_params=_params(("arbitrary", "arbitrary")),
        name="ada_mod",
    )(cvec, w_ada, b_ada.reshape(DEPTH, 1, n_out))


def _inproj_kernel(xc_ref, xl_ref, nw_ref, mod_ref, wg_ref, wc_ref, wa_ref, g_ref, c_ref, a_ref):
    i = pl.program_id(0)
    row = _mod_row(i)
    sh = mod_ref[pl.ds(row, 1), 0:D_MODEL]
    sc = mod_ref[pl.ds(row, 1), D_MODEL:2 * D_MODEL]
    x = jnp.where(i < N_CTX_TILES, xc_ref[...], xl_ref[...])
    h = _bf(_rms(x, nw_ref[...]) * (1.0 + sc) + sh)
    g_ref[...] = _dot(h, wg_ref[...])
    c_ref[...] = _dot(h, wc_ref[...])
    a_ref[...] = _dot(h, wa_ref[...])


def _stream_tiles(lat_first_tile):
    ctx_tile = lambda i: (jnp.minimum(i, N_CTX_TILES - 1), 0)
    lat_tile = lambda i: (jnp.maximum(i - N_CTX_TILES, 0) + lat_first_tile, 0)
    return ctx_tile, lat_tile


def _in_proj(x_ctx, x_lat, lat_first_tile, norm_w, mod, w_gla, w_conv, w_na):
    full = lambda i: (0, 0)
    tile = lambda i: (i, 0)
    ctx_tile, lat_tile = _stream_tiles(lat_first_tile)
    return pl.pallas_call(
        _inproj_kernel,
        out_shape=(
            jax.ShapeDtypeStruct((N_TOK, GLA_IN_W), _F32),
            jax.ShapeDtypeStruct((N_TOK, 2 * CONV_CH), _F32),
            jax.ShapeDtypeStruct((N_TOK, 3 * NA_DIM), _F32),
        ),
        grid=(N_TOK // ROW_TILE,),
        in_specs=[
            pl.BlockSpec((ROW_TILE, D_MODEL), ctx_tile),
            pl.BlockSpec((ROW_TILE, D_MODEL), lat_tile),
            pl.BlockSpec((1, D_MODEL), full),
            pl.BlockSpec((MOD_ROWS, 6 * D_MODEL), full),
            pl.BlockSpec((D_MODEL, GLA_IN_W), full),
            pl.BlockSpec((D_MODEL, 2 * CONV_CH), full),
            pl.BlockSpec((D_MODEL, 3 * NA_DIM), full),
        ],
        out_specs=(
            pl.BlockSpec((ROW_TILE, GLA_IN_W), tile),
            pl.BlockSpec((ROW_TILE, 2 * CONV_CH), tile),
            pl.BlockSpec((ROW_TILE, 3 * NA_DIM), tile),
        ),
        compiler_params=_params(("arbitrary",)),
        name="in_proj",
    )(x_ctx, x_lat, norm_w, mod, w_gla, w_conv, w_na)


_GLA_LEVELS = (32, 16, 8, 4, 2, 1)
_N_EXP_BLOCKS = 2 + len(_GLA_LEVELS)
_N_MASKS = len(_GLA_LEVELS) + 1


def _gla_constants():
    cs = GLA_CHUNK
    r = np.arange(cs)
    i = r[:, None]
    c = r[None, :]
    w = np.zeros((2, _N_EXP_BLOCKS, cs, cs), np.float32)
    m = np.zeros((2, _N_MASKS, cs, cs), np.float32)
    w[0, 0] = c <= i
    w[0, 1] = c > i
    w[1, 0] = c >= i
    w[1, 1] = c < i
    for lv, half in enumerate(_GLA_LEVELS):
        mid = (r // (2 * half)) * (2 * half) + half
        mi = mid[:, None]
        second = (r >= mid)[:, None]
        same = (r[:, None] // (2 * half)) == (r[None, :] // (2 * half))
        w[0, 2 + lv] = np.where(second, (c >= mi) & (c <= i), (c > i) & (c <= mi - 1))
        w[1, 2 + lv] = np.where(second, (c >= mi) & (c <= i - 1), (c >= i) & (c <= mi - 1))
        m[0, lv] = same & (i >= mi) & (c < mi)
        m[1, lv] = same & (i < mi) & (c >= mi)
    m[:, _N_MASKS - 1] = np.eye(cs)
    w_all = w.reshape(2, _N_EXP_BLOCKS * cs, cs)
    lmask = np.tile(m, (1, 1, 1, GLA_HEADS))
    return jnp.asarray(w_all, _BF), jnp.asarray(lmask, _F32)


def _gla_kernel(g_ref, wz_ref, bz_ref, wall_ref, lmask_ref, onorm_ref, st0_ref,
                o_ref, stfin_ref, la_scr, o_scr, st_scr, *, n):
    cs = GLA_CHUNK
    nc = n // cs
    gate_rows = 128
    key_head = lax.broadcasted_iota(jnp.int32, (1, GLA_KEY), 1) // GLA_DK
    head_lanes = [(key_head == h).astype(_F32) for h in range(GLA_HEADS)]
    zero_v = jnp.zeros((cs, GLA_DV), _F32)

    def log_decays(t, carry):
        rows = pl.ds(pl.multiple_of(t * gate_rows, gate_rows), gate_rows)
        z = _dot3(g_ref[rows, GLA_LR_OFF:GLA_IN_W], wz_ref[...]) + bz_ref[...]
        la_scr[rows, :] = (jnp.minimum(z, 0.0) - jnp.log1p(jnp.exp(-jnp.abs(z)))) * (1.0 / GLA_GATE_NORM)
        return carry

    lax.fori_loop(0, n // gate_rows, log_decays, 0)

    def chunk(c, d):
        rows = pl.ds(pl.multiple_of(c * cs, cs), cs)
        q = g_ref[rows, 0:GLA_KEY] * (GLA_DK ** -0.5)
        k = g_ref[rows, GLA_KEY:2 * GLA_KEY]
        v = g_ref[rows, 2 * GLA_KEY:2 * GLA_KEY + GLA_VAL]
        la_hi, la_mid, la_lo = _split3(la_scr[rows, d * GLA_KEY:(d + 1) * GLA_KEY])
        w = wall_ref[d]
        f = jnp.exp((_dot(w, la_lo) + _dot(w, la_mid)) + _dot(w, la_hi))
        st = st_scr[d]
        o = _dot_nt(_bf(q * f[0:cs]), _bf(st))
        p = jnp.zeros((cs, GLA_HEADS * cs), _F32)
        for lv in range(_N_MASKS):
            if lv < len(_GLA_LEVELS):
                fl = f[(2 + lv) * cs:(3 + lv) * cs]
                ql, kl = q * fl, k * fl
            else:
                ql, kl = q, k
            k_bd = _bf(jnp.concatenate([kl * m for m in head_lanes], axis=0))
            p = p + lmask_ref[d, lv] * _dot_nt(_bf(ql), k_bd)
        v_bd = _bf(jnp.concatenate(
            [jnp.concatenate([v[:, g * GLA_DV:(g + 1) * GLA_DV] if g == h else zero_v for g in range(GLA_HEADS)], axis=1)
             for h in range(GLA_HEADS)], axis=0))
        o_scr[d, rows, :] = o + _dot(_bf(p), v_bd)
        decay = f[cs - 1:cs] if d == 0 else f[0:1]
        u_t = _dot_tn(_bf(v), _bf(k * f[cs:2 * cs]))
        u_t = jnp.concatenate([u_t[h * GLA_DV:(h + 1) * GLA_DV] * head_lanes[h] for h in range(GLA_HEADS)], axis=0)
        st_scr[d] = st * decay + u_t

    st_scr[...] = st0_ref[0]

    def scan(i, carry):
        chunk(i, 0)
        chunk(nc - 1 - i, 1)
        return carry

    lax.fori_loop(0, nc, scan, 0)
    stfin_ref[0] = st_scr[...]

    def finish(c, carry):
        rows = pl.ds(pl.multiple_of(c * cs, cs), cs)
        o = o_scr[1, rows, :] + o_scr[0, rows, :]
        for h in range(GLA_HEADS):
            cols = slice(h * GLA_DV, (h + 1) * GLA_DV)
            gate = g_ref[rows, 2 * GLA_KEY + GLA_VAL + h * GLA_DV:2 * GLA_KEY + GLA_VAL + (h + 1) * GLA_DV]
            o_ref[rows, cols] = _rms(o[:, cols], onorm_ref[...]) * _silu(gate)
        return carry

    lax.fori_loop(0, nc, finish, 0)


def _gla(g_all, wz, bz, consts, onorm, st0, *, n, n_seq, first_block):
    w_all, lmask = consts
    full2 = lambda i: (0, 0)
    return pl.pallas_call(
        functools.partial(_gla_kernel, n=n),
        out_shape=(
            jax.ShapeDtypeStruct((n_seq * n, GLA_VAL), _F32),
            jax.ShapeDtypeStruct((n_seq, 2, GLA_VAL, GLA_KEY), _F32),
        ),
        grid=(n_seq,),
        in_specs=[
            pl.BlockSpec((n, GLA_IN_W), lambda i: (first_block + i, 0)),
            pl.BlockSpec(wz.shape, full2),
            pl.BlockSpec(bz.shape, full2),
            pl.BlockSpec(w_all.shape, lambda i: (0, 0, 0)),
            pl.BlockSpec(lmask.shape, lambda i: (0, 0, 0, 0)),
            pl.BlockSpec((1, GLA_DV), full2),
            pl.BlockSpec((1, 2, GLA_VAL, GLA_KEY), lambda i: (i, 0, 0, 0)),
        ],
        out_specs=(
            pl.BlockSpec((n, GLA_VAL), lambda i: (i, 0)),
            pl.BlockSpec((1, 2, GLA_VAL, GLA_KEY), lambda i: (i, 0, 0, 0)),
        ),
        scratch_shapes=[pltpu.VMEM((n, 2 * GLA_KEY), _F32), pltpu.VMEM((2, n, GLA_VAL), _F32),
                        pltpu.VMEM((2, GLA_VAL, GLA_KEY), _F32)],
        compiler_params=_params(("arbitrary",)),
        name=f"gla_{n}",
    )(g_all, wz, bz, w_all, lmask, onorm, st0)


def _state_to_blockdiag(s):
    b = s.shape[0]
    st = jnp.swapaxes(s, -1, -2)
    eye = jnp.eye(GLA_HEADS, dtype=s.dtype)
    out = st[:, :, :, :, None, :] * eye[None, None, :, None, :, None]
    return out.reshape(b, 2, GLA_VAL, GLA_KEY)


def _blockdiag_to_state(st):
    b = st.shape[0]
    s6 = st.reshape(b, 2, GLA_HEADS, GLA_DV, GLA_HEADS, GLA_DK)
    diag = jnp.stack([s6[:, :, h, :, h, :] for h in range(GLA_HEADS)], axis=2)
    return jnp.swapaxes(diag, -1, -2)


_CONV_PAD = 16
_CONV_ROWS = 128


def _conv_kernel(c_ref, dw_ref, dwb_ref, lng_ref, lnb_ref, pw_ref, o_ref, pad_scr, *, n):
    zeros = jnp.zeros((_CONV_PAD, CONV_CH), _F32)
    pad_scr[0:_CONV_PAD, :] = zeros
    pad_scr[_CONV_PAD + n:2 * _CONV_PAD + n, :] = zeros
    for r0 in range(0, n, _CONV_ROWS):
        a = c_ref[r0:r0 + _CONV_ROWS, :]
        pad_scr[_CONV_PAD + r0:_CONV_PAD + r0 + _CONV_ROWS, :] = a[:, :CONV_CH] * _sigmoid(a[:, CONV_CH:])
    half = CONV_WIDTH // 2
    for r0 in range(0, n, _CONV_ROWS):
        acc = jnp.zeros((_CONV_ROWS, CONV_CH), _F32)
        for w in range(CONV_WIDTH):
            s = _CONV_PAD + r0 + w - half
            acc = acc + pad_scr[s:s + _CONV_ROWS, :] * dw_ref[w:w + 1, :]
        acc = acc + dwb_ref[...]
        xc = acc - jnp.mean(acc, axis=-1, keepdims=True)
        y = xc * lax.rsqrt(jnp.mean(xc * xc, axis=-1, keepdims=True) + EPS) * lng_ref[...] + lnb_ref[...]
        o_ref[r0:r0 + _CONV_ROWS, :] = _dot(_bf(_silu(y)), pw_ref[...])


def _conv(c_all, dw, dwb, lng, lnb, pw, *, n, n_seq, first_block):
    full2 = lambda i: (0, 0)
    return pl.pallas_call(
        functools.partial(_conv_kernel, n=n),
        out_shape=jax.ShapeDtypeStruct((n_seq * n, CONV_CH), _F32),
        grid=(n_seq,),
        in_specs=[
            pl.BlockSpec((n, 2 * CONV_CH), lambda i: (first_block + i, 0)),
            pl.BlockSpec((CONV_WIDTH, CONV_CH), full2),
            pl.BlockSpec((1, CONV_CH), full2),
            pl.BlockSpec((1, CONV_CH), full2),
            pl.BlockSpec((1, CONV_CH), full2),
            pl.BlockSpec((CONV_CH, CONV_CH), full2),
        ],
        out_specs=pl.BlockSpec((n, CONV_CH), lambda i: (i, 0)),
        scratch_shapes=[pltpu.VMEM((n + 2 * _CONV_PAD, CONV_CH), _F32)],
        compiler_params=_params(("arbitrary",)),
        name=f"conv_{n}",
    )(c_all, dw, dwb, lng, lnb, pw)


def _softmax_rows(s):
    e = jnp.exp(s - jnp.max(s, axis=-1, keepdims=True))
    return e / jnp.sum(e, axis=-1, keepdims=True)


def _na_ctx_kernel(a_ref, qn_ref, kn_ref, o_ref, kc_ref, vc_ref):
    for h in range(NA_HEADS):
        cols = slice(h * NA_HD, (h + 1) * NA_HD)
        q = _rms(a_ref[:, h * NA_HD:(h + 1) * NA_HD], qn_ref[...])
        k = _rms(a_ref[:, NA_DIM + h * NA_HD:NA_DIM + (h + 1) * NA_HD], kn_ref[...])
        v = a_ref[:, 2 * NA_DIM + h * NA_HD:2 * NA_DIM + (h + 1) * NA_HD]
        kc_ref[0, h] = k
        vc_ref[0, h] = v
        p = _softmax_rows(_dot_nt(_bf(q), _bf(k)) * (NA_HD ** -0.5))
        o_ref[:, cols] = _dot(_bf(p), _bf(v))


def _na_ctx(a_all, qn, kn):
    full2 = lambda i: (0, 0)
    cache = jax.ShapeDtypeStruct((BATCH, NA_HEADS, SEQ, NA_HD), _F32)
    cache_spec = pl.BlockSpec((1, NA_HEADS, SEQ, NA_HD), lambda i: (i, 0, 0, 0))
    return pl.pallas_call(
        _na_ctx_kernel,
        out_shape=(jax.ShapeDtypeStruct((N_CTX_TOK, NA_DIM), _F32), cache, cache),
        grid=(BATCH,),
        in_specs=[
            pl.BlockSpec((SEQ, 3 * NA_DIM), lambda i: (i, 0)),
            pl.BlockSpec((1, NA_HD), full2),
            pl.BlockSpec((1, NA_HD), full2),
        ],
        out_specs=(pl.BlockSpec((SEQ, NA_DIM), lambda i: (i, 0)), cache_spec, cache_spec),
        compiler_params=_params(("arbitrary",)),
        name="na_ctx",
    )(a_all, qn, kn)


_NA_ROWS = DEC_SEQ // GRID_W
_NA_KEYS = NA_WIN_ROWS * GRID_W
_NA_VARIANTS = NA_WIN_ROWS
_NA_ROW_UNROLL = 8


_RPB_ROWS = 2 * NA_WIN_ROWS - 1
_RPB_COLS = 2 * NA_WIN_COLS - 1


def _na_bias_constants():
    qc = np.arange(GRID_W)[:, None]
    kc = np.arange(GRID_W)[None, :]
    shift = np.stack([(kc - qc + NA_WIN_COLS - 1) == co for co in range(_RPB_COLS)]).astype(np.float32)
    win_start = np.clip(qc - NA_WIN_COLS // 2, 0, GRID_W - NA_WIN_COLS)
    in_win = ((kc >= win_start) & (kc < win_start + NA_WIN_COLS)).astype(np.float32)
    return jnp.asarray(shift), jnp.asarray(in_win)


def _na_bias_kernel(rpb_ref, shift_ref, win_ref, o_ref):
    h = pl.program_id(0)
    in_win = win_ref[...] > 0.0
    tiles = []
    for ro in range(_RPB_ROWS):
        base = (h * _RPB_ROWS + ro) * _RPB_COLS
        acc = rpb_ref[base] * shift_ref[0]
        for co in range(1, _RPB_COLS):
            acc = acc + rpb_ref[base + co] * shift_ref[co]
        tiles.append(jnp.where(in_win, acc, -jnp.inf))
    for t in range(_NA_VARIANTS):
        o_ref[0, t] = jnp.concatenate([tiles[kr - t + NA_WIN_ROWS - 1] for kr in range(NA_WIN_ROWS)], axis=1)


def _na_bias_table(rpb):
    shift, in_win = _na_bias_constants()
    return pl.pallas_call(
        _na_bias_kernel,
        out_shape=jax.ShapeDtypeStruct((NA_HEADS, _NA_VARIANTS, GRID_W, _NA_KEYS), _F32),
        grid=(NA_HEADS,),
        in_specs=[
            pl.BlockSpec(memory_space=pltpu.SMEM),
            pl.BlockSpec(shift.shape, lambda h: (0, 0, 0)),
            pl.BlockSpec(in_win.shape, lambda h: (0, 0)),
        ],
        out_specs=pl.BlockSpec((1, _NA_VARIANTS, GRID_W, _NA_KEYS), lambda h: (h, 0, 0, 0)),
        compiler_params=_params(("arbitrary",)),
        name="na_bias",
    )(rpb.reshape(-1), shift, in_win)


def _na_lat_kernel(a_ref, kctx_ref, vctx_ref, bias_ref, qn_ref, kn_ref, o_ref, q_scr, k_scr, v_scr):
    scale = NA_HD ** -0.5
    for h in range(NA_HEADS):
        q_scr[...] = _bf(_rms(a_ref[:, h * NA_HD:(h + 1) * NA_HD], qn_ref[...]))
        k_scr[...] = _bf(_rms(a_ref[:, NA_DIM + h * NA_HD:NA_DIM + (h + 1) * NA_HD], kn_ref[...]))
        v_scr[...] = _bf(a_ref[:, 2 * NA_DIM + h * NA_HD:2 * NA_DIM + (h + 1) * NA_HD])
        k_ctx = _bf(kctx_ref[0, h])
        v_ctx = _bf(vctx_ref[0, h])

        def row(r):
            start = jnp.clip(r - NA_WIN_ROWS // 2, 0, _NA_ROWS - NA_WIN_ROWS)
            qrows = pl.ds(pl.multiple_of(r * GRID_W, GRID_W), GRID_W)
            krows = pl.ds(pl.multiple_of(start * GRID_W, GRID_W), _NA_KEYS)
            q = q_scr[qrows, :]
            s_loc = _dot_nt(q, k_scr[krows, :]) * scale + bias_ref[h, r - start]
            s_ctx = _dot_nt(q, k_ctx) * scale
            m = jnp.maximum(jnp.max(s_loc, axis=-1, keepdims=True), jnp.max(s_ctx, axis=-1, keepdims=True))
            e_loc = jnp.exp(s_loc - m)
            e_ctx = jnp.exp(s_ctx - m)
            inv = 1.0 / (jnp.sum(e_loc, axis=-1, keepdims=True) + jnp.sum(e_ctx, axis=-1, keepdims=True))
            o = _dot(_bf(e_loc * inv), v_scr[krows, :]) + _dot(_bf(e_ctx * inv), v_ctx)
            o_ref[qrows, h * NA_HD:(h + 1) * NA_HD] = o

        def row_group(g, carry):
            for dr in range(_NA_ROW_UNROLL):
                row(g * _NA_ROW_UNROLL + dr)
            return carry

        lax.fori_loop(0, _NA_ROWS // _NA_ROW_UNROLL, row_group, 0)


def _na_lat(a_all, k_ctx, v_ctx, bias, qn, kn):
    full2 = lambda i: (0, 0)
    ctx_spec = pl.BlockSpec((1, NA_HEADS, PAST_LEN, NA_HD), lambda i: (i, 0, 0, 0))
    return pl.pallas_call(
        _na_lat_kernel,
        out_shape=jax.ShapeDtypeStruct((N_LAT_TOK, NA_DIM), _F32),
        grid=(DEC_BATCH,),
        in_specs=[
            pl.BlockSpec((DEC_SEQ, 3 * NA_DIM), lambda i: (N_CTX_TOK // DEC_SEQ + i, 0)),
            ctx_spec,
            ctx_spec,
            pl.BlockSpec(bias.shape, lambda i: (0, 0, 0, 0)),
            pl.BlockSpec((1, NA_HD), full2),
            pl.BlockSpec((1, NA_HD), full2),
        ],
        out_specs=pl.BlockSpec((DEC_SEQ, NA_DIM), lambda i: (i, 0)),
        scratch_shapes=[pltpu.VMEM((DEC_SEQ, NA_HD), _BF)] * 3,
        compiler_params=_params(("arbitrary",)),
        name="na_lat",
    )(a_all, k_ctx, v_ctx, bias, qn, kn)


def _outproj_kernel(gc_ref, gl_ref, cc_ref, cl_ref, nc_ref, nl_ref, xc_ref, xl_ref, mod_ref, nw_ref, wo_ref,
                    rwt_ref, x1_ref, h2_ref, hr_ref, lg_ref):
    i = pl.program_id(0)
    is_ctx = i < N_CTX_TILES
    row = _mod_row(i)
    gla = jnp.where(is_ctx, gc_ref[...], gl_ref[...])
    conv = jnp.where(is_ctx, cc_ref[...], cl_ref[...])
    na = jnp.where(is_ctx, nc_ref[...], nl_ref[...])
    mix = (_dot(_bf(gla), wo_ref[0:GLA_VAL, :])
           + _dot(_bf(conv), wo_ref[GLA_VAL:GLA_VAL + CONV_CH, :])
           + _dot(_bf(na), wo_ref[GLA_VAL + CONV_CH:, :]))
    g1 = mod_ref[pl.ds(row, 1), 2 * D_MODEL:3 * D_MODEL]
    sh2 = mod_ref[pl.ds(row, 1), 3 * D_MODEL:4 * D_MODEL]
    sc2 = mod_ref[pl.ds(row, 1), 4 * D_MODEL:5 * D_MODEL]
    x1 = jnp.where(is_ctx, xc_ref[...], xl_ref[...]) + g1 * mix
    h2 = _rms(x1, nw_ref[...]) * (1.0 + sc2) + sh2
    x1_ref[...] = x1
    h2_ref[...] = _bf(h2)
    for j in range(_ROW_CHUNKS):
        hr_ref[pl.ds(j, ROW_TILE, stride=_ROW_CHUNKS), :] = h2[:, j * _LANE:(j + 1) * _LANE]
    lg_ref[...] = _dot3(rwt_ref[...], h2, dot=_dot_nt)


def _out_proj(gla_c, gla_l, conv_c, conv_l, na_c, na_l, x_ctx, x_lat, lat_first_tile, mod, norm_w, w_out,
              router_wt):
    full2 = lambda i: (0, 0)
    tile = lambda i: (i, 0)
    ctx_tile, lat_tile = _stream_tiles(0)
    _, x_lat_tile = _stream_tiles(lat_first_tile)
    return pl.pallas_call(
        _outproj_kernel,
        out_shape=(
            jax.ShapeDtypeStruct((N_TOK, D_MODEL), _F32),
            jax.ShapeDtypeStruct((N_TOK, D_MODEL), _BF),
            jax.ShapeDtypeStruct((N_TOK * _ROW_CHUNKS, _LANE), _F32),
            jax.ShapeDtypeStruct((N_EXPERTS, N_TOK), _F32),
        ),
        grid=(N_TOK // ROW_TILE,),
        in_specs=[
            pl.BlockSpec((ROW_TILE, GLA_VAL), ctx_tile),
            pl.BlockSpec((ROW_TILE, GLA_VAL), lat_tile),
            pl.BlockSpec((ROW_TILE, CONV_CH), ctx_tile),
            pl.BlockSpec((ROW_TILE, CONV_CH), lat_tile),
            pl.BlockSpec((ROW_TILE, NA_DIM), ctx_tile),
            pl.BlockSpec((ROW_TILE, NA_DIM), lat_tile),
            pl.BlockSpec((ROW_TILE, D_MODEL), ctx_tile),
            pl.BlockSpec((ROW_TILE, D_MODEL), x_lat_tile),
            pl.BlockSpec((MOD_ROWS, 6 * D_MODEL), full2),
            pl.BlockSpec((1, D_MODEL), full2),
            pl.BlockSpec((D_MODEL, D_MODEL), full2),
            pl.BlockSpec((N_EXPERTS, D_MODEL), full2),
        ],
        out_specs=(
            pl.BlockSpec((ROW_TILE, D_MODEL), tile),
            pl.BlockSpec((ROW_TILE, D_MODEL), tile),
            pl.BlockSpec((ROW_TILE * _ROW_CHUNKS, _LANE), tile),
            pl.BlockSpec((N_EXPERTS, ROW_TILE), lambda i: (0, i)),
        ),
        compiler_params=_params(("arbitrary",)),
        name="out_proj",
    )(gla_c, gla_l, conv_c, conv_l, na_c, na_l, x_ctx, x_lat, mod, norm_w, w_out, router_wt)


_PER_GROUP = N_EXPERTS // N_GROUPS
_ROUTE_TILE = 1024


def _first_max(x, idx, axes, sentinel):
    m = x
    for ax in axes:
        m = jnp.max(m, axis=ax, keepdims=True)
    first = jnp.where(x == m, idx, sentinel)
    for ax in axes:
        first = jnp.min(first, axis=ax, keepdims=True)
    return m, first


def _router_kernel(lg_ref, rb_ref, eid_ref, wts_ref, cnt_ref):
    t = lg_ref.shape[1]
    shape3 = (N_GROUPS, _PER_GROUP, t)
    scores = _sigmoid(lg_ref[...])
    biased = (scores + rb_ref[...]).reshape(shape3)
    scores = scores.reshape(shape3)
    neg = -jnp.inf
    in_grp = lax.broadcasted_iota(jnp.int32, shape3, 1)
    grp = lax.broadcasted_iota(jnp.int32, (N_GROUPS, 1, t), 0)
    expert = lax.broadcasted_iota(jnp.int32, shape3, 0) * _PER_GROUP + in_grp
    m1, i1 = _first_max(biased, in_grp, (1,), _PER_GROUP)
    m2 = jnp.max(jnp.where(in_grp == i1, neg, biased), axis=1, keepdims=True)
    gscore = m1 + m2
    keep = jnp.zeros((N_GROUPS, 1, t), _F32)
    for _ in range(TOPK_GROUPS):
        _, gi = _first_max(gscore, grp, (0,), N_GROUPS)
        hit = grp == gi
        keep = jnp.where(hit, 1.0, keep)
        gscore = jnp.where(hit, neg, gscore)
    masked = jnp.where(keep > 0.0, biased, neg)
    ids, picked = [], []
    hits = jnp.zeros(shape3, _F32)
    for _ in range(TOP_K):
        _, ei = _first_max(masked, expert, (1, 0), N_EXPERTS)
        hit = expert == ei
        hits = jnp.where(hit, 1.0, hits)
        sc = jnp.sum(jnp.sum(jnp.where(hit, scores, 0.0), axis=1, keepdims=True), axis=0, keepdims=True)
        ids.append(ei.reshape(1, t))
        picked.append(sc.reshape(1, t))
        masked = jnp.where(hit, neg, masked)
    picked = jnp.concatenate(picked, axis=0)
    den = jnp.sum(picked, axis=0, keepdims=True)
    eid_ref[...] = jnp.concatenate(ids, axis=0)
    wts_ref[...] = picked / den * ROUTED_SCALE
    cnt_ref[0] = jnp.sum(hits, axis=2, keepdims=True).reshape(N_EXPERTS, 1)


def _router(logits_t, router_bias):
    tile = lambda i: (0, i)
    return pl.pallas_call(
        _router_kernel,
        out_shape=(jax.ShapeDtypeStruct((TOP_K, N_TOK), jnp.int32), jax.ShapeDtypeStruct((TOP_K, N_TOK), _F32),
                   jax.ShapeDtypeStruct((N_TOK // _ROUTE_TILE, N_EXPERTS, 1), _F32)),
        grid=(N_TOK // _ROUTE_TILE,),
        in_specs=[
            pl.BlockSpec((N_EXPERTS, _ROUTE_TILE), tile),
            pl.BlockSpec((N_EXPERTS, 1), lambda i: (0, 0)),
        ],
        out_specs=(pl.BlockSpec((TOP_K, _ROUTE_TILE), tile), pl.BlockSpec((TOP_K, _ROUTE_TILE), tile),
                   pl.BlockSpec((1, N_EXPERTS, 1), lambda i: (i, 0, 0))),
        compiler_params=_params(("arbitrary",)),
        name="router",
    )(logits_t, router_bias)


_HALF_TOK = N_CTX_TOK
_N_HALVES = N_TOK // _HALF_TOK
_HALF_ASSIGN = _HALF_TOK * TOP_K
_GB = 256
_GB_MAX = _HALF_ASSIGN // _GB + N_EXPERTS
_LANE = 128
_ROW_CHUNKS = D_MODEL // _LANE
_GS = _GB + 1
_STAGE_ROWS = (_ROW_CHUNKS * _GS + 7) // 8 * 8
_RMW_BATCH = 8
_TOK_BITS = 12
assert _HALF_TOK == 1 << _TOK_BITS
_FFN_SPLIT = 2


def _moe_plan(eid, wts, tile_counts):
    tok = jnp.arange(N_TOK, dtype=jnp.int32)
    key = (((tok >> _TOK_BITS) * N_EXPERTS)[None, :] + eid) * _HALF_TOK + (tok & (_HALF_TOK - 1))[None, :]
    key_s, gate_s = lax.sort((key.reshape(-1), wts.reshape(-1)), num_keys=1)
    row_off = (key_s & (_HALF_TOK - 1)) * _ROW_CHUNKS
    count = jnp.sum(tile_counts.reshape(_N_HALVES, -1, N_EXPERTS), axis=1).astype(jnp.int32)
    row_end = jnp.cumsum(count.reshape(-1)).reshape(_N_HALVES, N_EXPERTS)
    row_start = row_end - count
    nblk = (count + _GB - 1) // _GB
    blk_end = jnp.cumsum(nblk, axis=1)
    blk_start = blk_end - nblk
    n_used = blk_end[:, -1]
    b = jnp.minimum(jnp.arange(_GB_MAX, dtype=jnp.int32)[None, :], n_used[:, None] - 1)
    grp = jnp.sum((b[:, :, None] >= blk_end[:, None, :]).astype(jnp.int32), axis=-1)
    onehot = (grp[:, :, None] == jnp.arange(N_EXPERTS, dtype=jnp.int32)).astype(jnp.int32)
    pick = lambda v: jnp.sum(onehot * v[:, None, :], axis=-1)
    within = b - pick(blk_start)
    start = pick(row_start) + within * _GB
    length = jnp.clip(pick(count) - within * _GB, 0, _GB)
    pad = jnp.zeros((_GB,), jnp.int32)
    return (jnp.concatenate([row_off, pad]), jnp.concatenate([gate_s, pad.astype(_F32)]),
            grp.astype(jnp.int32), start.astype(jnp.int32), length.astype(jnp.int32), n_used.astype(jnp.int32))


def _gmm_kernel(be_ref, bstart_ref, blen_ref, nused_ref, off_ref, gate_ref,
                h_ref, wg_ref, wu_ref, wd_ref, acc_ref, xt_scr, yt_scr, wgb_scr, wub_scr, wdb_scr):
    s = pl.program_id(0)
    n_used = nused_ref[0]
    last = _GB_MAX - 1

    @pl.when(s == 0)
    def _():
        acc_ref[...] = jnp.zeros_like(acc_ref)
        xt_scr[...] = jnp.zeros_like(xt_scr)
        yt_scr[...] = jnp.zeros_like(yt_scr)

    fb = jnp.clip(s - 1, 0, last)

    @pl.when((s == 0) | (be_ref[fb] != be_ref[jnp.clip(s - 2, 0, last)]))
    def _():
        wgb_scr[...] = _bf(wg_ref[0, 0])
        wub_scr[...] = _bf(wu_ref[0, 0])
        wdb_scr[...] = _bf(wd_ref[0, 0])

    @pl.when(s < n_used + 2)
    def _():
        slot = s % 2
        other = 1 - slot

        cb = jnp.clip(s - 2, 0, last)
        c_start = bstart_ref[cb]
        c_len = blen_ref[cb]
        for m0 in range(0, _GB, _RMW_BATCH):
            pending = []
            for mi in range(m0, m0 + _RMW_BATCH):
                valid = mi < c_len
                row = jnp.where(valid, off_ref[c_start + mi], _HALF_TOK * _ROW_CHUNKS)
                gate = jnp.where(valid, gate_ref[c_start + mi], 0.0)
                rows = pl.ds(pl.multiple_of(row, _ROW_CHUNKS), _ROW_CHUNKS)
                contrib = gate * yt_scr[slot, pl.ds(mi, _ROW_CHUNKS, stride=_GS), :]
                pending.append((rows, acc_ref[rows, :] + contrib))
            for rows, val in pending:
                acc_ref[rows, :] = val

        part = _GB // _FFN_SPLIT
        for p in range(_FFN_SPLIT):
            x = _bf(jnp.concatenate([xt_scr[other, j * _GS + p * part:j * _GS + (p + 1) * part, :]
                                     for j in range(_ROW_CHUNKS)], axis=-1))
            act = _silu(_dot(x, wgb_scr[...])) * _dot(x, wub_scr[...])
            y = _dot(_bf(act), wdb_scr[...])
            for j in range(_ROW_CHUNKS):
                yt_scr[other, j * _GS + p * part:j * _GS + (p + 1) * part, :] = y[:, j * _LANE:(j + 1) * _LANE]

        g_start = bstart_ref[jnp.minimum(s, last)]
        for mi in range(_GB):
            src = pl.ds(pl.multiple_of(off_ref[g_start + mi], _ROW_CHUNKS), _ROW_CHUNKS)
            xt_scr[slot, pl.ds(mi, _ROW_CHUNKS, stride=_GS), :] = h_ref[src, :]


def _gmm(layer, half, plan, h_rows, wg, wu, wd):
    row_off, gate_s, grp, start, length, n_used = plan
    expert = lambda s, be, *_: (layer, be[jnp.clip(s - 1, 0, _GB_MAX - 1)], 0, 0)
    acc_rows = (_HALF_TOK + 1) * _ROW_CHUNKS
    grid_spec = pltpu.PrefetchScalarGridSpec(
        num_scalar_prefetch=6,
        grid=(_GB_MAX + 2,),
        in_specs=[
            pl.BlockSpec((_HALF_TOK * _ROW_CHUNKS, _LANE), lambda b, *_: (half, 0), pipeline_mode=pl.Buffered(1)),
            pl.BlockSpec((1, 1, D_MODEL, EXPERT_DIM), expert),
            pl.BlockSpec((1, 1, D_MODEL, EXPERT_DIM), expert),
            pl.BlockSpec((1, 1, EXPERT_DIM, D_MODEL), expert),
        ],
        out_specs=pl.BlockSpec((acc_rows, _LANE), lambda b, *_: (0, 0), pipeline_mode=pl.Buffered(1)),
        scratch_shapes=[pltpu.VMEM((2, _STAGE_ROWS, _LANE), _F32)] * 2
        + [pltpu.VMEM((D_MODEL, EXPERT_DIM), _BF)] * 2 + [pltpu.VMEM((EXPERT_DIM, D_MODEL), _BF)],
    )
    acc = pl.pallas_call(
        _gmm_kernel,
        out_shape=jax.ShapeDtypeStruct((acc_rows, _LANE), _F32),
        grid_spec=grid_spec,
        compiler_params=_params(("arbitrary",)),
        name="moe_experts",
    )(grp[half], start[half], length[half], n_used[half:half + 1], row_off, gate_s, h_rows, wg, wu, wd)
    return acc


_FIN_TILE = 512


def _moe_finish_kernel(h_ref, rc_ref, rl_ref, x1_ref, mod_ref, sg_ref, su_ref, sd_ref, o_ref):
    t = pl.program_id(0)
    tiles_per_half = _HALF_TOK // _FIN_TILE
    is_ctx = t < tiles_per_half
    row = jnp.where(is_ctx, 0, 1 + (t - tiles_per_half) // (DEC_SEQ // _FIN_TILE))
    h = h_ref[...]
    shared = _dot(_bf(_silu(_dot(h, _bf(sg_ref[0]))) * _dot(h, _bf(su_ref[0]))), _bf(sd_ref[0]))
    routed = jnp.concatenate(
        [jnp.where(is_ctx, rc_ref[pl.ds(j, _FIN_TILE, stride=_ROW_CHUNKS), :],
                   rl_ref[pl.ds(j, _FIN_TILE, stride=_ROW_CHUNKS), :]) for j in range(_ROW_CHUNKS)], axis=-1)
    g2 = mod_ref[pl.ds(row, 1), 5 * D_MODEL:6 * D_MODEL]
    o_ref[...] = x1_ref[...] + g2 * (routed + shared)


def _moe_finish(layer, h2, routed_c, routed_l, x1, mod, sg, su, sd):
    tile = lambda t: (t, 0)
    tiles_per_half = _HALF_TOK // _FIN_TILE
    ctx_tile = lambda t: (jnp.minimum(t, tiles_per_half - 1), 0)
    lat_tile = lambda t: (jnp.maximum(t - tiles_per_half, 0), 0)
    shared = lambda t: (layer, 0, 0)
    return pl.pallas_call(
        _moe_finish_kernel,
        out_shape=jax.ShapeDtypeStruct((N_TOK, D_MODEL), _F32),
        grid=(N_TOK // _FIN_TILE,),
        in_specs=[
            pl.BlockSpec((_FIN_TILE, D_MODEL), tile),
            pl.BlockSpec((_FIN_TILE * _ROW_CHUNKS, _LANE), ctx_tile),
            pl.BlockSpec((_FIN_TILE * _ROW_CHUNKS, _LANE), lat_tile),
            pl.BlockSpec((_FIN_TILE, D_MODEL), tile),
            pl.BlockSpec((MOD_ROWS, 6 * D_MODEL), lambda t: (0, 0)),
            pl.BlockSpec((1, D_MODEL, SHARED_DIM), shared),
            pl.BlockSpec((1, D_MODEL, SHARED_DIM), shared),
            pl.BlockSpec((1, SHARED_DIM, D_MODEL), shared),
        ],
        out_specs=pl.BlockSpec((_FIN_TILE, D_MODEL), tile),
        compiler_params=_params(("arbitrary",)),
        name="moe_finish",
    )(h2, routed_c, routed_l, x1, mod, sg, su, sd)


def kernel(x_prompt, x_sample, state_gla, cache_na_k, cache_na_v, c, c_ctx, w_ada, b_ada, norm_mix, norm_ffn, w_in, gla_w_gate, gla_b_gate, gla_out_norm, conv_dw, conv_dw_b, conv_ln_g, conv_ln_b, conv_pw, na_q_norm, na_k_norm, na_rpb, w_out, router_w, router_bias, exp_w_gate, exp_w_up, exp_w_down, sh_w_gate, sh_w_up, sh_w_down):
    x_ctx, x_lat, x_lat_tile = x_prompt.reshape(N_CTX_TOK, D_MODEL), x_sample.reshape(N_LAT_TOK, D_MODEL), 0
    cvec = jnp.concatenate([c_ctx[None], c, jnp.zeros((MOD_ROWS - 1 - DEC_BATCH, D_MODEL), _F32)], axis=0)
    mod_all = _ada_mod(cvec, w_ada, b_ada)
    gla_consts = _gla_constants()
    zero_state = jnp.zeros((BATCH, 2, GLA_VAL, GLA_KEY), _F32)
    lat_first = N_CTX_TOK // DEC_SEQ

    states, keys, vals = [], [], []
    for l in range(DEPTH):
        mod = mod_all[l]
        wi = w_in[l]
        w_gla = _bf(jnp.pad(wi[:, :GLA_LR_OFF + 2 * GLA_GATE_RANK], ((0, 0), (0, GLA_IN_W - GLA_LR_OFF - 2 * GLA_GATE_RANK))))
        conv_off = GLA_LR_OFF + 2 * GLA_GATE_RANK
        w_conv = _bf(wi[:, conv_off:conv_off + 2 * CONV_CH])
        w_na = _bf(wi[:, conv_off + 2 * CONV_CH:])
        g_all, c_all, a_all = _in_proj(x_ctx, x_lat, x_lat_tile, norm_mix[l][None], mod, w_gla, w_conv, w_na)

        wz = jnp.zeros((GLA_IN_W - GLA_LR_OFF, 2 * GLA_KEY), _F32)
        wz = wz.at[:GLA_GATE_RANK, :GLA_KEY].set(gla_w_gate[l, 0])
        wz = wz.at[GLA_GATE_RANK:2 * GLA_GATE_RANK, GLA_KEY:].set(gla_w_gate[l, 1])
        bz = gla_b_gate[l].reshape(1, 2 * GLA_KEY)
        onorm = gla_out_norm[l][None]
        gla_c, st_c = _gla(g_all, wz, bz, gla_consts, onorm, zero_state, n=SEQ, n_seq=BATCH, first_block=0)
        gla_l, _ = _gla(g_all, wz, bz, gla_consts, onorm, _state_to_blockdiag(state_gla[:, l]),
                        n=DEC_SEQ, n_seq=DEC_BATCH, first_block=lat_first)

        conv_args = (conv_dw[l], conv_dw_b[l][None], conv_ln_g[l][None], conv_ln_b[l][None], _bf(conv_pw[l]))
        conv_c = _conv(c_all, *conv_args, n=SEQ, n_seq=BATCH, first_block=0)
        conv_l = _conv(c_all, *conv_args, n=DEC_SEQ, n_seq=DEC_BATCH, first_block=lat_first)

        qn, kn = na_q_norm[l][None], na_k_norm[l][None]
        na_c, k_l, v_l = _na_ctx(a_all, qn, kn)
        na_l = _na_lat(a_all, cache_na_k[:, l], cache_na_v[:, l], _na_bias_table(na_rpb[l]), qn, kn)

        x1, h2, h_rows, logits_t = _out_proj(gla_c, gla_l, conv_c, conv_l, na_c, na_l, x_ctx, x_lat, x_lat_tile, mod,
                                     norm_ffn[l][None], _bf(w_out[l]), router_w[l].T)
        plan = _moe_plan(*_router(logits_t, router_bias[l][:, None]))
        routed = [_gmm(l, half, plan, h_rows, exp_w_gate, exp_w_up, exp_w_down) for half in range(_N_HALVES)]
        x = _moe_finish(l, h2, routed[0], routed[1], x1, mod, sh_w_gate, sh_w_up, sh_w_down)
        x_ctx, x_lat, x_lat_tile = x, x, N_CTX_TILES

        states.append(_blockdiag_to_state(st_c))
        keys.append(k_l)
        vals.append(v_l)

    y_prompt = x[:N_CTX_TOK].reshape(BATCH, SEQ, D_MODEL)
    y_sample = x[N_CTX_TOK:].reshape(DEC_BATCH, DEC_SEQ, D_MODEL)
    return (y_prompt, y_sample, jnp.stack(states, axis=1), jnp.stack(keys, axis=1), jnp.stack(vals, axis=1))
```

```python
import functools

import numpy as np
import jax
import jax.numpy as jnp
from jax import lax
from jax.experimental import pallas as pl
from jax.experimental.pallas import tpu as pltpu

D_MODEL = 1024
BATCH = 16
SEQ = 256
DEPTH = 2
DEC_BATCH = 4
DEC_SEQ = 1024
PAST_LEN = 256
GRID_W = 64
GLA_HEADS = 4
GLA_DK = 64
GLA_DV = 128
GLA_KEY = GLA_HEADS * GLA_DK
GLA_VAL = GLA_HEADS * GLA_DV
GLA_GATE_RANK = 16
GLA_GATE_NORM = 16.0
GLA_CHUNK = 64
CONV_CH = 256
CONV_WIDTH = 31
NA_HEADS = 4
NA_HD = 64
NA_DIM = NA_HEADS * NA_HD
NA_WIN_ROWS = 8
NA_WIN_COLS = 16
N_EXPERTS = 64
TOP_K = 8
N_GROUPS = 8
TOPK_GROUPS = 4
EXPERT_DIM = 256
SHARED_DIM = 256
ROUTED_SCALE = 2.5
EPS = 1e-6

N_CTX_TOK = BATCH * SEQ
N_LAT_TOK = DEC_BATCH * DEC_SEQ
N_TOK = N_CTX_TOK + N_LAT_TOK
ROW_TILE = 512
N_CTX_TILES = N_CTX_TOK // ROW_TILE
TILES_PER_LAT_SEQ = DEC_SEQ // ROW_TILE
MOD_ROWS = 8
GLA_IN_W = 1664
GLA_LR_OFF = 2 * GLA_KEY + 2 * GLA_VAL
VMEM_LIMIT = 56 * 1024 * 1024

_BF = jnp.bfloat16
_F32 = jnp.float32


def _bf(x):
    return x.astype(_BF)


def _dot(a, b):
    return jnp.dot(a, b, preferred_element_type=_F32)


def _dot_nt(a, b):
    return lax.dot_general(a, b, (((1,), (1,)), ((), ())), preferred_element_type=_F32)


def _dot_tn(a, b):
    return lax.dot_general(a, b, (((0,), (0,)), ((), ())), preferred_element_type=_F32)


def _split2(x):
    hi = _bf(x)
    lo = _bf(x - hi.astype(_F32))
    return hi, lo


def _split3(x):
    hi = _bf(x)
    r = x - hi.astype(_F32)
    mid = _bf(r)
    lo = _bf(r - mid.astype(_F32))
    return hi, mid, lo


def _dot3(a, b, dot=_dot):
    a_hi, a_lo = _split2(a)
    b_hi, b_lo = _split2(b)
    return (dot(a_lo, b_hi) + dot(a_hi, b_lo)) + dot(a_hi, b_hi)


def _sigmoid(x):
    return 1.0 / (1.0 + jnp.exp(-x))


def _silu(x):
    return x * _sigmoid(x)


def _rms(x, w):
    return x * lax.rsqrt(jnp.mean(x * x, axis=-1, keepdims=True) + EPS) * w


def _params(sem):
    return pltpu.CompilerParams(dimension_semantics=sem, vmem_limit_bytes=VMEM_LIMIT)


def _mod_row(i):
    return jnp.where(i < N_CTX_TILES, 0, 1 + (i - N_CTX_TILES) // TILES_PER_LAT_SEQ)


def _ada_kernel(cv_ref, w_ref, b_ref, o_ref):
    o_ref[0] = _dot3(_silu(cv_ref[...]), w_ref[0]) + b_ref[0]


def _ada_mod(cvec, w_ada, b_ada):
    tn = 1024
    n_out = 6 * D_MODEL
    return pl.pallas_call(
        _ada_kernel,
        out_shape=jax.ShapeDtypeStruct((DEPTH, MOD_ROWS, n_out), _F32),
        grid=(DEPTH, n_out // tn),
        in_specs=[
            pl.BlockSpec((MOD_ROWS, D_MODEL), lambda l, j: (0, 0)),
            pl.BlockSpec((1, D_MODEL, tn), lambda l, j: (l, 0, j)),
            pl.BlockSpec((1, 1, tn), lambda l, j: (l, 0, j)),
        ],
        out_specs=pl.BlockSpec((1, MOD_ROWS, tn), lambda l, j: (l, 0, j)),
        compiler_params=_params(("arbitrary", "arbitrary")),
        name="ada_mod",
    )(cvec, w_ada, b_ada.reshape(DEPTH, 1, n_out))


def _inproj_kernel(xc_ref, xl_ref, nw_ref, mod_ref, wg_ref, wc_ref, wa_ref, g_ref, c_ref, a_ref):
    i = pl.program_id(0)
    row = _mod_row(i)
    sh = mod_ref[pl.ds(row, 1), 0:D_MODEL]
    sc = mod_ref[pl.ds(row, 1), D_MODEL:2 * D_MODEL]
    x = jnp.where(i < N_CTX_TILES, xc_ref[...], xl_ref[...])
    h = _bf(_rms(x, nw_ref[...]) * (1.0 + sc) + sh)
    g_ref[...] = _dot(h, wg_ref[...])
    c_ref[...] = _dot(h, wc_ref[...])
    a_ref[...] = _dot(h, wa_ref[...])


def _stream_tiles(lat_first_tile):
    ctx_tile = lambda i: (jnp.minimum(i, N_CTX_TILES - 1), 0)
    lat_tile = lambda i: (jnp.maximum(i - N_CTX_TILES, 0) + lat_first_tile, 0)
    return ctx_tile, lat_tile


def _in_proj(x_ctx, x_lat, lat_first_tile, norm_w, mod, w_gla, w_conv, w_na):
    full = lambda i: (0, 0)
    tile = lambda i: (i, 0)
    ctx_tile, lat_tile = _stream_tiles(lat_first_tile)
    return pl.pallas_call(
        _inproj_kernel,
        out_shape=(
            jax.ShapeDtypeStruct((N_TOK, GLA_IN_W), _F32),
            jax.ShapeDtypeStruct((N_TOK, 2 * CONV_CH), _F32),
            jax.ShapeDtypeStruct((N_TOK, 3 * NA_DIM), _F32),
        ),
        grid=(N_TOK // ROW_TILE,),
        in_specs=[
            pl.BlockSpec((ROW_TILE, D_MODEL), ctx_tile),
            pl.BlockSpec((ROW_TILE, D_MODEL), lat_tile),
            pl.BlockSpec((1, D_MODEL), full),
            pl.BlockSpec((MOD_ROWS, 6 * D_MODEL), full),
            pl.BlockSpec((D_MODEL, GLA_IN_W), full),
            pl.BlockSpec((D_MODEL, 2 * CONV_CH), full),
            pl.BlockSpec((D_MODEL, 3 * NA_DIM), full),
        ],
        out_specs=(
            pl.BlockSpec((ROW_TILE, GLA_IN_W), tile),
            pl.BlockSpec((ROW_TILE, 2 * CONV_CH), tile),
            pl.BlockSpec((ROW_TILE, 3 * NA_DIM), tile),
        ),
        compiler_params=_params(("arbitrary",)),
        name="in_proj",
    )(x_ctx, x_lat, norm_w, mod, w_gla, w_conv, w_na)


_GLA_LEVELS = (32, 16, 8, 4, 2, 1)
_N_EXP_BLOCKS = 2 + len(_GLA_LEVELS)
_N_MASKS = len(_GLA_LEVELS) + 1


def _gla_constants():
    cs = GLA_CHUNK
    r = np.arange(cs)
    i = r[:, None]
    c = r[None, :]
    w = np.zeros((2, _N_EXP_BLOCKS, cs, cs), np.float32)
    m = np.zeros((2, _N_MASKS, cs, cs), np.float32)
    w[0, 0] = c <= i
    w[0, 1] = c > i
    w[1, 0] = c >= i
    w[1, 1] = c < i
    for lv, half in enumerate(_GLA_LEVELS):
        mid = (r // (2 * half)) * (2 * half) + half
        mi = mid[:, None]
        second = (r >= mid)[:, None]
        same = (r[:, None] // (2 * half)) == (r[None, :] // (2 * half))
        w[0, 2 + lv] = np.where(second, (c >= mi) & (c <= i), (c > i) & (c <= mi - 1))
        w[1, 2 + lv] = np.where(second, (c >= mi) & (c <= i - 1), (c >= i) & (c <= mi - 1))
        m[0, lv] = same & (i >= mi) & (c < mi)
        m[1, lv] = same & (i < mi) & (c >= mi)
    m[:, _N_MASKS - 1] = np.eye(cs)
    w_all = w.reshape(2, _N_EXP_BLOCKS * cs, cs)
    lmask = np.tile(m, (1, 1, 1, GLA_HEADS))
    return jnp.asarray(w_all, _BF), jnp.asarray(lmask, _F32)


def _gla_kernel(g_ref, wz_ref, bz_ref, wall_ref, lmask_ref, onorm_ref, st0_ref,
                o_ref, stfin_ref, la_scr, o_scr, st_scr, *, n):
    cs = GLA_CHUNK
    nc = n // cs
    gate_rows = 128
    key_head = lax.broadcasted_iota(jnp.int32, (1, GLA_KEY), 1) // GLA_DK
    head_lanes = [(key_head == h).astype(_F32) for h in range(GLA_HEADS)]
    head_lanes_bf = [_bf(m) for m in head_lanes]
    zero_v = jnp.zeros((cs, GLA_DV), _BF)

    def log_decays(t, carry):
        rows = pl.ds(pl.multiple_of(t * gate_rows, gate_rows), gate_rows)
        z = _dot3(g_ref[rows, GLA_LR_OFF:GLA_IN_W], wz_ref[...]) + bz_ref[...]
        la_scr[rows, :] = (jnp.minimum(z, 0.0) - jnp.log1p(jnp.exp(-jnp.abs(z)))) * (1.0 / GLA_GATE_NORM)
        return carry

    lax.fori_loop(0, n // gate_rows, log_decays, 0)

    def chunk(c, d):
        rows = pl.ds(pl.multiple_of(c * cs, cs), cs)
        q = g_ref[rows, 0:GLA_KEY] * (GLA_DK ** -0.5)
        k = g_ref[rows, GLA_KEY:2 * GLA_KEY]
        v = g_ref[rows, 2 * GLA_KEY:2 * GLA_KEY + GLA_VAL]
        la_hi, la_mid, la_lo = _split3(la_scr[rows, d * GLA_KEY:(d + 1) * GLA_KEY])
        w = wall_ref[d]
        f = jnp.exp((_dot(w, la_lo) + _dot(w, la_mid)) + _dot(w, la_hi))
        st = st_scr[d]
        o = _dot_nt(_bf(q * f[0:cs]), _bf(st))
        p = jnp.zeros((cs, GLA_HEADS * cs), _F32)
        for lv in range(_N_MASKS):
            if lv < len(_GLA_LEVELS):
                fl = f[(2 + lv) * cs:(3 + lv) * cs]
                ql, kl = q * fl, k * fl
            else:
                ql, kl = q, k
            kl = _bf(kl)
            k_bd = jnp.concatenate([kl * m for m in head_lanes_bf], axis=0)
            p = p + lmask_ref[d, lv] * _dot_nt(_bf(ql), k_bd)
        vb = _bf(v)
        v_bd = jnp.concatenate(
            [jnp.concatenate([vb[:, g * GLA_DV:(g + 1) * GLA_DV] if g == h else zero_v for g in range(GLA_HEADS)], axis=1)
             for h in range(GLA_HEADS)], axis=0)
        o_scr[d, rows, :] = o + _dot(_bf(p), v_bd)
        decay = f[cs - 1:cs] if d == 0 else f[0:1]
        u_t = _dot_tn(vb, _bf(k * f[cs:2 * cs]))
        u_t = jnp.concatenate([u_t[h * GLA_DV:(h + 1) * GLA_DV] * head_lanes[h] for h in range(GLA_HEADS)], axis=0)
        st_scr[d] = st * decay + u_t

    st_scr[...] = st0_ref[0]

    def scan(i, carry):
        chunk(i, 0)
        chunk(nc - 1 - i, 1)
        return carry

    lax.fori_loop(0, nc, scan, 0)
    stfin_ref[0] = st_scr[...]

    def finish(c, carry):
        rows = pl.ds(pl.multiple_of(c * cs, cs), cs)
        o = o_scr[1, rows, :] + o_scr[0, rows, :]
        for h in range(GLA_HEADS):
            cols = slice(h * GLA_DV, (h + 1) * GLA_DV)
            gate = g_ref[rows, 2 * GLA_KEY + GLA_VAL + h * GLA_DV:2 * GLA_KEY + GLA_VAL + (h + 1) * GLA_DV]
            o_ref[rows, cols] = _rms(o[:, cols], onorm_ref[...]) * _silu(gate)
        return carry

    lax.fori_loop(0, nc, finish, 0)


def _gla(g_all, wz, bz, consts, onorm, st0, *, n, n_seq, first_block):
    w_all, lmask = consts
    full2 = lambda i: (0, 0)
    return pl.pallas_call(
        functools.partial(_gla_kernel, n=n),
        out_shape=(
            jax.ShapeDtypeStruct((n_seq * n, GLA_VAL), _F32),
            jax.ShapeDtypeStruct((n_seq, 2, GLA_VAL, GLA_KEY), _F32),
        ),
        grid=(n_seq,),
        in_specs=[
            pl.BlockSpec((n, GLA_IN_W), lambda i: (first_block + i, 0)),
            pl.BlockSpec(wz.shape, full2),
            pl.BlockSpec(bz.shape, full2),
            pl.BlockSpec(w_all.shape, lambda i: (0, 0, 0)),
            pl.BlockSpec(lmask.shape, lambda i: (0, 0, 0, 0)),
            pl.BlockSpec((1, GLA_DV), full2),
            pl.BlockSpec((1, 2, GLA_VAL, GLA_KEY), lambda i: (i, 0, 0, 0)),
        ],
        out_specs=(
            pl.BlockSpec((n, GLA_VAL), lambda i: (i, 0)),
            pl.BlockSpec((1, 2, GLA_VAL, GLA_KEY), lambda i: (i, 0, 0, 0)),
        ),
        scratch_shapes=[pltpu.VMEM((n, 2 * GLA_KEY), _F32), pltpu.VMEM((2, n, GLA_VAL), _F32),
                        pltpu.VMEM((2, GLA_VAL, GLA_KEY), _F32)],
        compiler_params=_params(("arbitrary",)),
        name=f"gla_{n}",
    )(g_all, wz, bz, w_all, lmask, onorm, st0)


def _state_to_blockdiag(s):
    b = s.shape[0]
    st = jnp.swapaxes(s, -1, -2)
    eye = jnp.eye(GLA_HEADS, dtype=s.dtype)
    out = st[:, :, :, :, None, :] * eye[None, None, :, None, :, None]
    return out.reshape(b, 2, GLA_VAL, GLA_KEY)


def _blockdiag_to_state(st):
    b = st.shape[0]
    s6 = st.reshape(b, 2, GLA_HEADS, GLA_DV, GLA_HEADS, GLA_DK)
    diag = jnp.stack([s6[:, :, h, :, h, :] for h in range(GLA_HEADS)], axis=2)
    return jnp.swapaxes(diag, -1, -2)


_CONV_PAD = 16
_CONV_ROWS = 128


def _conv_kernel(c_ref, dw_ref, dwb_ref, lng_ref, lnb_ref, pw_ref, o_ref, pad_scr, *, n):
    zeros = jnp.zeros((_CONV_PAD, CONV_CH), _F32)
    pad_scr[0:_CONV_PAD, :] = zeros
    pad_scr[_CONV_PAD + n:2 * _CONV_PAD + n, :] = zeros
    for r0 in range(0, n, _CONV_ROWS):
        a = c_ref[r0:r0 + _CONV_ROWS, :]
        pad_scr[_CONV_PAD + r0:_CONV_PAD + r0 + _CONV_ROWS, :] = a[:, :CONV_CH] * _sigmoid(a[:, CONV_CH:])
    half = CONV_WIDTH // 2
    for r0 in range(0, n, _CONV_ROWS):
        acc = jnp.zeros((_CONV_ROWS // 8, 8, CONV_CH), _F32)
        for w in range(CONV_WIDTH):
            s = _CONV_PAD + r0 + w - half
            acc = acc + pad_scr[s:s + _CONV_ROWS, :].reshape(_CONV_ROWS // 8, 8, CONV_CH) * dw_ref[w][None]
        acc = acc.reshape(_CONV_ROWS, CONV_CH) + dwb_ref[...]
        xc = acc - jnp.mean(acc, axis=-1, keepdims=True)
        y = xc * lax.rsqrt(jnp.mean(xc * xc, axis=-1, keepdims=True) + EPS) * lng_ref[...] + lnb_ref[...]
        o_ref[r0:r0 + _CONV_ROWS, :] = _dot(_bf(_silu(y)), pw_ref[...])


def _conv(c_all, dw, dwb, lng, lnb, pw, *, n, n_seq, first_block):
    full2 = lambda i: (0, 0)
    return pl.pallas_call(
        functools.partial(_conv_kernel, n=n),
        out_shape=jax.ShapeDtypeStruct((n_seq * n, CONV_CH), _F32),
        grid=(n_seq,),
        in_specs=[
            pl.BlockSpec((n, 2 * CONV_CH), lambda i: (first_block + i, 0)),
            pl.BlockSpec((CONV_WIDTH, 8, CONV_CH), lambda i: (0, 0, 0)),
            pl.BlockSpec((1, CONV_CH), full2),
            pl.BlockSpec((1, CONV_CH), full2),
            pl.BlockSpec((1, CONV_CH), full2),
            pl.BlockSpec((CONV_CH, CONV_CH), full2),
        ],
        out_specs=pl.BlockSpec((n, CONV_CH), lambda i: (i, 0)),
        scratch_shapes=[pltpu.VMEM((n + 2 * _CONV_PAD, CONV_CH), _F32)],
        compiler_params=_params(("arbitrary",)),
        name=f"conv_{n}",
    )(c_all, dw, dwb, lng, lnb, pw)


def _softmax_rows(s):
    e = jnp.exp(s - jnp.max(s, axis=-1, keepdims=True))
    return e / jnp.sum(e, axis=-1, keepdims=True)


def _na_ctx_kernel(a_ref, qn_ref, kn_ref, o_ref, kc_ref, vc_ref):
    for h in range(NA_HEADS):
        cols = slice(h * NA_HD, (h + 1) * NA_HD)
        q = _rms(a_ref[:, h * NA_HD:(h + 1) * NA_HD], qn_ref[...])
        k = _rms(a_ref[:, NA_DIM + h * NA_HD:NA_DIM + (h + 1) * NA_HD], kn_ref[...])
        v = a_ref[:, 2 * NA_DIM + h * NA_HD:2 * NA_DIM + (h + 1) * NA_HD]
        kc_ref[0, h] = k
        vc_ref[0, h] = v
        p = _softmax_rows(_dot_nt(_bf(q), _bf(k)) * (NA_HD ** -0.5))
        o_ref[:, cols] = _dot(_bf(p), _bf(v))


def _na_ctx(a_all, qn, kn):
    full2 = lambda i: (0, 0)
    cache = jax.ShapeDtypeStruct((BATCH, NA_HEADS, SEQ, NA_HD), _F32)
    cache_spec = pl.BlockSpec((1, NA_HEADS, SEQ, NA_HD), lambda i: (i, 0, 0, 0))
    return pl.pallas_call(
        _na_ctx_kernel,
        out_shape=(jax.ShapeDtypeStruct((N_CTX_TOK, NA_DIM), _F32), cache, cache),
        grid=(BATCH,),
        in_specs=[
            pl.BlockSpec((SEQ, 3 * NA_DIM), lambda i: (i, 0)),
            pl.BlockSpec((1, NA_HD), full2),
            pl.BlockSpec((1, NA_HD), full2),
        ],
        out_specs=(pl.BlockSpec((SEQ, NA_DIM), lambda i: (i, 0)), cache_spec, cache_spec),
        compiler_params=_params(("arbitrary",)),
        name="na_ctx",
    )(a_all, qn, kn)


_NA_ROWS = DEC_SEQ // GRID_W
_NA_KEYS = NA_WIN_ROWS * GRID_W
_NA_VARIANTS = NA_WIN_ROWS
_NA_ROW_UNROLL = 8


_RPB_ROWS = 2 * NA_WIN_ROWS - 1
_RPB_COLS = 2 * NA_WIN_COLS - 1


def _na_bias_constants():
    qc = np.arange(GRID_W)[:, None]
    kc = np.arange(GRID_W)[None, :]
    shift = np.stack([(kc - qc + NA_WIN_COLS - 1) == co for co in range(_RPB_COLS)]).astype(np.float32)
    win_start = np.clip(qc - NA_WIN_COLS // 2, 0, GRID_W - NA_WIN_COLS)
    in_win = ((kc >= win_start) & (kc < win_start + NA_WIN_COLS)).astype(np.float32)
    return jnp.asarray(shift), jnp.asarray(in_win)


def _na_bias_kernel(rpb_ref, shift_ref, win_ref, o_ref):
    h = pl.program_id(0)
    in_win = win_ref[...] > 0.0
    tiles = []
    for ro in range(_RPB_ROWS):
        base = (h * _RPB_ROWS + ro) * _RPB_COLS
        acc = rpb_ref[base] * shift_ref[0]
        for co in range(1, _RPB_COLS):
            acc = acc + rpb_ref[base + co] * shift_ref[co]
        tiles.append(jnp.where(in_win, acc, -jnp.inf))
    for t in range(_NA_VARIANTS):
        o_ref[0, t] = jnp.concatenate([tiles[kr - t + NA_WIN_ROWS - 1] for kr in range(NA_WIN_ROWS)], axis=1)


def _na_bias_table(rpb):
    shift, in_win = _na_bias_constants()
    return pl.pallas_call(
        _na_bias_kernel,
        out_shape=jax.ShapeDtypeStruct((NA_HEADS, _NA_VARIANTS, GRID_W, _NA_KEYS), _F32),
        grid=(NA_HEADS,),
        in_specs=[
            pl.BlockSpec(memory_space=pltpu.SMEM),
            pl.BlockSpec(shift.shape, lambda h: (0, 0, 0)),
            pl.BlockSpec(in_win.shape, lambda h: (0, 0)),
        ],
        out_specs=pl.BlockSpec((1, _NA_VARIANTS, GRID_W, _NA_KEYS), lambda h: (h, 0, 0, 0)),
        compiler_params=_params(("arbitrary",)),
        name="na_bias",
    )(rpb.reshape(-1), shift, in_win)


def _na_lat_kernel(a_ref, kctx_ref, vctx_ref, bias_ref, qn_ref, kn_ref, o_ref, q_scr, k_scr, v_scr):
    scale = NA_HD ** -0.5
    for h in range(NA_HEADS):
        q_scr[...] = _bf(_rms(a_ref[:, h * NA_HD:(h + 1) * NA_HD], qn_ref[...]))
        k_scr[...] = _bf(_rms(a_ref[:, NA_DIM + h * NA_HD:NA_DIM + (h + 1) * NA_HD], kn_ref[...]))
        v_scr[...] = _bf(a_ref[:, 2 * NA_DIM + h * NA_HD:2 * NA_DIM + (h + 1) * NA_HD])
        k_ctx = _bf(kctx_ref[0, h])
        v_ctx = _bf(vctx_ref[0, h])

        def row(r):
            start = jnp.clip(r - NA_WIN_ROWS // 2, 0, _NA_ROWS - NA_WIN_ROWS)
            qrows = pl.ds(pl.multiple_of(r * GRID_W, GRID_W), GRID_W)
            krows = pl.ds(pl.multiple_of(start * GRID_W, GRID_W), _NA_KEYS)
            q = q_scr[qrows, :]
            s_loc = _dot_nt(q, k_scr[krows, :]) * scale + bias_ref[h, r - start]
            s_ctx = _dot_nt(q, k_ctx) * scale
            m = jnp.maximum(jnp.max(s_loc, axis=-1, keepdims=True), jnp.max(s_ctx, axis=-1, keepdims=True))
            e_loc = jnp.exp(s_loc - m)
            e_ctx = jnp.exp(s_ctx - m)
            inv = 1.0 / (jnp.sum(e_loc, axis=-1, keepdims=True) + jnp.sum(e_ctx, axis=-1, keepdims=True))
            o = _dot(_bf(e_loc * inv), v_scr[krows, :]) + _dot(_bf(e_ctx * inv), v_ctx)
            o_ref[qrows, h * NA_HD:(h + 1) * NA_HD] = o

        def row_group(g, carry):
            for dr in range(_NA_ROW_UNROLL):
                row(g * _NA_ROW_UNROLL + dr)
            return carry

        lax.fori_loop(0, _NA_ROWS // _NA_ROW_UNROLL, row_group, 0)


def _na_lat(a_all, k_ctx, v_ctx, bias, qn, kn):
    full2 = lambda i: (0, 0)
    ctx_spec = pl.BlockSpec((1, NA_HEADS, PAST_LEN, NA_HD), lambda i: (i, 0, 0, 0))
    return pl.pallas_call(
        _na_lat_kernel,
        out_shape=jax.ShapeDtypeStruct((N_LAT_TOK, NA_DIM), _F32),
        grid=(DEC_BATCH,),
        in_specs=[
            pl.BlockSpec((DEC_SEQ, 3 * NA_DIM), lambda i: (N_CTX_TOK // DEC_SEQ + i, 0)),
            ctx_spec,
            ctx_spec,
            pl.BlockSpec(bias.shape, lambda i: (0, 0, 0, 0)),
            pl.BlockSpec((1, NA_HD), full2),
            pl.BlockSpec((1, NA_HD), full2),
        ],
        out_specs=pl.BlockSpec((DEC_SEQ, NA_DIM), lambda i: (i, 0)),
        scratch_shapes=[pltpu.VMEM((DEC_SEQ, NA_HD), _BF)] * 3,
        compiler_params=_params(("arbitrary",)),
        name="na_lat",
    )(a_all, k_ctx, v_ctx, bias, qn, kn)


def _outproj_kernel(gc_ref, gl_ref, cc_ref, cl_ref, nc_ref, nl_ref, xc_ref, xl_ref, mod_ref, nw_ref, wo_ref,
                    rwt_ref, x1_ref, h2_ref, hr_ref, lg_ref):
    i = pl.program_id(0)
    is_ctx = i < N_CTX_TILES
    row = _mod_row(i)
    gla = jnp.where(is_ctx, gc_ref[...], gl_ref[...])
    conv = jnp.where(is_ctx, cc_ref[...], cl_ref[...])
    na = jnp.where(is_ctx, nc_ref[...], nl_ref[...])
    mix = (_dot(_bf(gla), wo_ref[0:GLA_VAL, :])
           + _dot(_bf(conv), wo_ref[GLA_VAL:GLA_VAL + CONV_CH, :])
           + _dot(_bf(na), wo_ref[GLA_VAL + CONV_CH:, :]))
    g1 = mod_ref[pl.ds(row, 1), 2 * D_MODEL:3 * D_MODEL]
    sh2 = mod_ref[pl.ds(row, 1), 3 * D_MODEL:4 * D_MODEL]
    sc2 = mod_ref[pl.ds(row, 1), 4 * D_MODEL:5 * D_MODEL]
    x1 = jnp.where(is_ctx, xc_ref[...], xl_ref[...]) + g1 * mix
    h2 = _rms(x1, nw_ref[...]) * (1.0 + sc2) + sh2
    x1_ref[...] = x1
    h2_ref[...] = _bf(h2)
    for j in range(_ROW_CHUNKS):
        hr_ref[pl.ds(j, ROW_TILE, stride=_ROW_CHUNKS), :] = h2[:, j * _LANE:(j + 1) * _LANE]
    lg_ref[...] = _dot3(rwt_ref[...], h2, dot=_dot_nt)


def _out_proj(gla_c, gla_l, conv_c, conv_l, na_c, na_l, x_ctx, x_lat, lat_first_tile, mod, norm_w, w_out,
              router_wt):
    full2 = lambda i: (0, 0)
    tile = lambda i: (i, 0)
    ctx_tile, lat_tile = _stream_tiles(0)
    _, x_lat_tile = _stream_tiles(lat_first_tile)
    return pl.pallas_call(
        _outproj_kernel,
        out_shape=(
            jax.ShapeDtypeStruct((N_TOK, D_MODEL), _F32),
            jax.ShapeDtypeStruct((N_TOK, D_MODEL), _BF),
            jax.ShapeDtypeStruct((N_TOK * _ROW_CHUNKS, _LANE), _F32),
            jax.ShapeDtypeStruct((N_EXPERTS, N_TOK), _F32),
        ),
        grid=(N_TOK // ROW_TILE,),
        in_specs=[
            pl.BlockSpec((ROW_TILE, GLA_VAL), ctx_tile),
            pl.BlockSpec((ROW_TILE, GLA_VAL), lat_tile),
            pl.BlockSpec((ROW_TILE, CONV_CH), ctx_tile),
            pl.BlockSpec((ROW_TILE, CONV_CH), lat_tile),
            pl.BlockSpec((ROW_TILE, NA_DIM), ctx_tile),
            pl.BlockSpec((ROW_TILE, NA_DIM), lat_tile),
            pl.BlockSpec((ROW_TILE, D_MODEL), ctx_tile),
            pl.BlockSpec((ROW_TILE, D_MODEL), x_lat_tile),
            pl.BlockSpec((MOD_ROWS, 6 * D_MODEL), full2),
            pl.BlockSpec((1, D_MODEL), full2),
            pl.BlockSpec((D_MODEL, D_MODEL), full2),
            pl.BlockSpec((N_EXPERTS, D_MODEL), full2),
        ],
        out_specs=(
            pl.BlockSpec((ROW_TILE, D_MODEL), tile),
            pl.BlockSpec((ROW_TILE, D_MODEL), tile),
            pl.BlockSpec((ROW_TILE * _ROW_CHUNKS, _LANE), tile),
            pl.BlockSpec((N_EXPERTS, ROW_TILE), lambda i: (0, i)),
        ),
        compiler_params=_params(("arbitrary",)),
        name="out_proj",
    )(gla_c, gla_l, conv_c, conv_l, na_c, na_l, x_ctx, x_lat, mod, norm_w, w_out, router_wt)


_PER_GROUP = N_EXPERTS // N_GROUPS
_ROUTE_TILE = 1024


def _first_max(x, idx, axes, sentinel):
    m = x
    for ax in axes:
        m = jnp.max(m, axis=ax, keepdims=True)
    first = jnp.where(x == m, idx, sentinel)
    for ax in axes:
        first = jnp.min(first, axis=ax, keepdims=True)
    return m, first


def _router_kernel(lg_ref, rb_ref, eid_ref, wts_ref, cnt_ref):
    t = lg_ref.shape[1]
    shape3 = (N_GROUPS, _PER_GROUP, t)
    scores = _sigmoid(lg_ref[...])
    biased = (scores + rb_ref[...]).reshape(shape3)
    scores = scores.reshape(shape3)
    neg = -jnp.inf
    in_grp = lax.broadcasted_iota(jnp.int32, shape3, 1)
    grp = lax.broadcasted_iota(jnp.int32, (N_GROUPS, 1, t), 0)
    expert = lax.broadcasted_iota(jnp.int32, shape3, 0) * _PER_GROUP + in_grp
    m1, i1 = _first_max(biased, in_grp, (1,), _PER_GROUP)
    m2 = jnp.max(jnp.where(in_grp == i1, neg, biased), axis=1, keepdims=True)
    gscore = m1 + m2
    keep = jnp.zeros((N_GROUPS, 1, t), _F32)
    for _ in range(TOPK_GROUPS):
        _, gi = _first_max(gscore, grp, (0,), N_GROUPS)
        hit = grp == gi
        keep = jnp.where(hit, 1.0, keep)
        gscore = jnp.where(hit, neg, gscore)
    masked = jnp.where(keep > 0.0, biased, neg)
    ids, picked = [], []
    hits = jnp.zeros(shape3, _F32)
    for _ in range(TOP_K):
        _, ei = _first_max(masked, expert, (1, 0), N_EXPERTS)
        hit = expert == ei
        hits = jnp.where(hit, 1.0, hits)
        sc = jnp.sum(jnp.sum(jnp.where(hit, scores, 0.0), axis=1, keepdims=True), axis=0, keepdims=True)
        ids.append(ei.reshape(1, t))
        picked.append(sc.reshape(1, t))
        masked = jnp.where(hit, neg, masked)
    picked = jnp.concatenate(picked, axis=0)
    den = jnp.sum(picked, axis=0, keepdims=True)
    eid_ref[...] = jnp.concatenate(ids, axis=0)
    wts_ref[...] = picked / den * ROUTED_SCALE
    cnt_ref[0] = jnp.sum(hits, axis=2, keepdims=True).reshape(N_EXPERTS, 1)


def _router(logits_t, router_bias):
    tile = lambda i: (0, i)
    return pl.pallas_call(
        _router_kernel,
        out_shape=(jax.ShapeDtypeStruct((TOP_K, N_TOK), jnp.int32), jax.ShapeDtypeStruct((TOP_K, N_TOK), _F32),
                   jax.ShapeDtypeStruct((N_TOK // _ROUTE_TILE, N_EXPERTS, 1), _F32)),
        grid=(N_TOK // _ROUTE_TILE,),
        in_specs=[
            pl.BlockSpec((N_EXPERTS, _ROUTE_TILE), tile),
            pl.BlockSpec((N_EXPERTS, 1), lambda i: (0, 0)),
        ],
        out_specs=(pl.BlockSpec((TOP_K, _ROUTE_TILE), tile), pl.BlockSpec((TOP_K, _ROUTE_TILE), tile),
                   pl.BlockSpec((1, N_EXPERTS, 1), lambda i: (i, 0, 0))),
        compiler_params=_params(("arbitrary",)),
        name="router",
    )(logits_t, router_bias)


_HALF_TOK = N_CTX_TOK
_N_HALVES = N_TOK // _HALF_TOK
_HALF_ASSIGN = _HALF_TOK * TOP_K
_GB = 256
_GB_MAX = _HALF_ASSIGN // _GB + N_EXPERTS
_LANE = 128
_ROW_CHUNKS = D_MODEL // _LANE
_GS = _GB + 1
_STAGE_ROWS = (_ROW_CHUNKS * _GS + 7) // 8 * 8
_RMW_BATCH = 8
_TOK_BITS = 12
assert _HALF_TOK == 1 << _TOK_BITS
_FFN_SPLIT = 2


def _moe_plan(eid, wts, tile_counts):
    tok = jnp.arange(N_TOK, dtype=jnp.int32)
    key = (((tok >> _TOK_BITS) * N_EXPERTS)[None, :] + eid) * _HALF_TOK + (tok & (_HALF_TOK - 1))[None, :]
    key_s, gate_s = lax.sort((key.reshape(-1), wts.reshape(-1)), num_keys=1)
    row_off = (key_s & (_HALF_TOK - 1)) * _ROW_CHUNKS
    count = jnp.sum(tile_counts.reshape(_N_HALVES, -1, N_EXPERTS), axis=1).astype(jnp.int32)
    row_end = jnp.cumsum(count.reshape(-1)).reshape(_N_HALVES, N_EXPERTS)
    row_start = row_end - count
    nblk = (count + _GB - 1) // _GB
    blk_end = jnp.cumsum(nblk, axis=1)
    blk_start = blk_end - nblk
    n_used = blk_end[:, -1]
    b = jnp.minimum(jnp.arange(_GB_MAX, dtype=jnp.int32)[None, :], n_used[:, None] - 1)
    grp = jnp.sum((b[:, :, None] >= blk_end[:, None, :]).astype(jnp.int32), axis=-1)
    onehot = (grp[:, :, None] == jnp.arange(N_EXPERTS, dtype=jnp.int32)).astype(jnp.int32)
    pick = lambda v: jnp.sum(onehot * v[:, None, :], axis=-1)
    within = b - pick(blk_start)
    start = pick(row_start) + within * _GB
    length = jnp.clip(pick(count) - within * _GB, 0, _GB)
    pad = jnp.zeros((_GB,), jnp.int32)
    return (jnp.concatenate([row_off, pad]), jnp.concatenate([gate_s, pad.astype(_F32)]),
            grp.astype(jnp.int32), start.astype(jnp.int32), length.astype(jnp.int32), n_used.astype(jnp.int32))


def _gmm_kernel(be_ref, bstart_ref, blen_ref, nused_ref, off_ref, gate_ref,
                h_ref, wg_ref, wu_ref, wd_ref, acc_ref, xt_scr, yt_scr, wgb_scr, wub_scr, wdb_scr):
    s = pl.program_id(0)
    n_used = nused_ref[0]
    last = _GB_MAX - 1

    @pl.when(s == 0)
    def _():
        acc_ref[...] = jnp.zeros_like(acc_ref)
        xt_scr[...] = jnp.zeros_like(xt_scr)
        yt_scr[...] = jnp.zeros_like(yt_scr)

    fb = jnp.clip(s - 1, 0, last)

    @pl.when((s == 0) | (be_ref[fb] != be_ref[jnp.clip(s - 2, 0, last)]))
    def _():
        wgb_scr[...] = _bf(wg_ref[0, 0])
        wub_scr[...] = _bf(wu_ref[0, 0])
        wdb_scr[...] = _bf(wd_ref[0, 0])

    @pl.when(s < n_used + 2)
    def _():
        slot = s % 2
        other = 1 - slot

        cb = jnp.clip(s - 2, 0, last)
        c_start = bstart_ref[cb]
        c_len = blen_ref[cb]
        for m0 in range(0, _GB, _RMW_BATCH):
            pending = []
            for mi in range(m0, m0 + _RMW_BATCH):
                valid = mi < c_len
                row = jnp.where(valid, off_ref[c_start + mi], _HALF_TOK * _ROW_CHUNKS)
                gate = jnp.where(valid, gate_ref[c_start + mi], 0.0)
                rows = pl.ds(pl.multiple_of(row, _ROW_CHUNKS), _ROW_CHUNKS)
                contrib = gate * yt_scr[slot, pl.ds(mi, _ROW_CHUNKS, stride=_GS), :]
                pending.append((rows, acc_ref[rows, :] + contrib))
            for rows, val in pending:
                acc_ref[rows, :] = val

        part = _GB // _FFN_SPLIT
        for p in range(_FFN_SPLIT):
            x = _bf(jnp.concatenate([xt_scr[other, j * _GS + p * part:j * _GS + (p + 1) * part, :]
                                     for j in range(_ROW_CHUNKS)], axis=-1))
            act = _silu(_dot(x, wgb_scr[...])) * _dot(x, wub_scr[...])
            y = _dot(_bf(act), wdb_scr[...])
            for j in range(_ROW_CHUNKS):
                yt_scr[other, j * _GS + p * part:j * _GS + (p + 1) * part, :] = y[:, j * _LANE:(j + 1) * _LANE]

        g_start = bstart_ref[jnp.minimum(s, last)]
        for mi in range(_GB):
            src = pl.ds(pl.multiple_of(off_ref[g_start + mi], _ROW_CHUNKS), _ROW_CHUNKS)
            xt_scr[slot, pl.ds(mi, _ROW_CHUNKS, stride=_GS), :] = h_ref[src, :]


def _gmm(layer, half, plan, h_rows, wg, wu, wd):
    row_off, gate_s, grp, start, length, n_used = plan
    expert = lambda s, be, *_: (layer, be[jnp.clip(s - 1, 0, _GB_MAX - 1)], 0, 0)
    acc_rows = (_HALF_TOK + 1) * _ROW_CHUNKS
    grid_spec = pltpu.PrefetchScalarGridSpec(
        num_scalar_prefetch=6,
        grid=(_GB_MAX + 2,),
        in_specs=[
            pl.BlockSpec((_HALF_TOK * _ROW_CHUNKS, _LANE), lambda b, *_: (half, 0), pipeline_mode=pl.Buffered(1)),
            pl.BlockSpec((1, 1, D_MODEL, EXPERT_DIM), expert),
            pl.BlockSpec((1, 1, D_MODEL, EXPERT_DIM), expert),
            pl.BlockSpec((1, 1, EXPERT_DIM, D_MODEL), expert),
        ],
        out_specs=pl.BlockSpec((acc_rows, _LANE), lambda b, *_: (0, 0), pipeline_mode=pl.Buffered(1)),
        scratch_shapes=[pltpu.VMEM((2, _STAGE_ROWS, _LANE), _F32)] * 2
        + [pltpu.VMEM((D_MODEL, EXPERT_DIM), _BF)] * 2 + [pltpu.VMEM((EXPERT_DIM, D_MODEL), _BF)],
    )
    acc = pl.pallas_call(
        _gmm_kernel,
        out_shape=jax.ShapeDtypeStruct((acc_rows, _LANE), _F32),
        grid_spec=grid_spec,
        compiler_params=_params(("arbitrary",)),
        name="moe_experts",
    )(grp[half], start[half], length[half], n_used[half:half + 1], row_off, gate_s, h_rows, wg, wu, wd)
    return acc


_FIN_TILE = 512


def _moe_finish_kernel(h_ref, rc_ref, rl_ref, x1_ref, mod_ref, sg_ref, su_ref, sd_ref, o_ref):
    t = pl.program_id(0)
    tiles_per_half = _HALF_TOK // _FIN_TILE
    is_ctx = t < tiles_per_half
    row = jnp.where(is_ctx, 0, 1 + (t - tiles_per_half) // (DEC_SEQ // _FIN_TILE))
    h = h_ref[...]
    shared = _dot(_bf(_silu(_dot(h, _bf(sg_ref[0]))) * _dot(h, _bf(su_ref[0]))), _bf(sd_ref[0]))
    routed = jnp.concatenate(
        [jnp.where(is_ctx, rc_ref[pl.ds(j, _FIN_TILE, stride=_ROW_CHUNKS), :],
                   rl_ref[pl.ds(j, _FIN_TILE, stride=_ROW_CHUNKS), :]) for j in range(_ROW_CHUNKS)], axis=-1)
    g2 = mod_ref[pl.ds(row, 1), 5 * D_MODEL:6 * D_MODEL]
    o_ref[...] = x1_ref[...] + g2 * (routed + shared)


def _moe_finish(layer, h2, routed_c, routed_l, x1, mod, sg, su, sd):
    tile = lambda t: (t, 0)
    tiles_per_half = _HALF_TOK // _FIN_TILE
    ctx_tile = lambda t: (jnp.minimum(t, tiles_per_half - 1), 0)
    lat_tile = lambda t: (jnp.maximum(t - tiles_per_half, 0), 0)
    shared = lambda t: (layer, 0, 0)
    return pl.pallas_call(
        _moe_finish_kernel,
        out_shape=jax.ShapeDtypeStruct((N_TOK, D_MODEL), _F32),
        grid=(N_TOK // _FIN_TILE,),
        in_specs=[
            pl.BlockSpec((_FIN_TILE, D_MODEL), tile),
            pl.BlockSpec((_FIN_TILE * _ROW_CHUNKS, _LANE), ctx_tile),
            pl.BlockSpec((_FIN_TILE * _ROW_CHUNKS, _LANE), lat_tile),
            pl.BlockSpec((_FIN_TILE, D_MODEL), tile),
            pl.BlockSpec((MOD_ROWS, 6 * D_MODEL), lambda t: (0, 0)),
            pl.BlockSpec((1, D_MODEL, SHARED_DIM), shared),
            pl.BlockSpec((1, D_MODEL, SHARED_DIM), shared),
            pl.BlockSpec((1, SHARED_DIM, D_MODEL), shared),
        ],
        out_specs=pl.BlockSpec((_FIN_TILE, D_MODEL), tile),
        compiler_params=_params(("arbitrary",)),
        name="moe_finish",
    )(h2, routed_c, routed_l, x1, mod, sg, su, sd)


def kernel(x_prompt, x_sample, state_gla, cache_na_k, cache_na_v, c, c_ctx, w_ada, b_ada, norm_mix, norm_ffn, w_in, gla_w_gate, gla_b_gate, gla_out_norm, conv_dw, conv_dw_b, conv_ln_g, conv_ln_b, conv_pw, na_q_norm, na_k_norm, na_rpb, w_out, router_w, router_bias, exp_w_gate, exp_w_up, exp_w_down, sh_w_gate, sh_w_up, sh_w_down):
    x_ctx, x_lat, x_lat_tile = x_prompt.reshape(N_CTX_TOK, D_MODEL), x_sample.reshape(N_LAT_TOK, D_MODEL), 0
    cvec = jnp.concatenate([c_ctx[None], c, jnp.zeros((MOD_ROWS - 1 - DEC_BATCH, D_MODEL), _F32)], axis=0)
    mod_all = _ada_mod(cvec, w_ada, b_ada)
    gla_consts = _gla_constants()
    zero_state = jnp.zeros((BATCH, 2, GLA_VAL, GLA_KEY), _F32)
    lat_first = N_CTX_TOK // DEC_SEQ

    states, keys, vals = [], [], []
    for l in range(DEPTH):
        mod = mod_all[l]
        wi = w_in[l]
        w_gla = _bf(jnp.pad(wi[:, :GLA_LR_OFF + 2 * GLA_GATE_RANK], ((0, 0), (0, GLA_IN_W - GLA_LR_OFF - 2 * GLA_GATE_RANK))))
        conv_off = GLA_LR_OFF + 2 * GLA_GATE_RANK
        w_conv = _bf(wi[:, conv_off:conv_off + 2 * CONV_CH])
        w_na = _bf(wi[:, conv_off + 2 * CONV_CH:])
        g_all, c_all, a_all = _in_proj(x_ctx, x_lat, x_lat_tile, norm_mix[l][None], mod, w_gla, w_conv, w_na)

        wz = jnp.zeros((GLA_IN_W - GLA_LR_OFF, 2 * GLA_KEY), _F32)
        wz = wz.at[:GLA_GATE_RANK, :GLA_KEY].set(gla_w_gate[l, 0])
        wz = wz.at[GLA_GATE_RANK:2 * GLA_GATE_RANK, GLA_KEY:].set(gla_w_gate[l, 1])
        bz = gla_b_gate[l].reshape(1, 2 * GLA_KEY)
        onorm = gla_out_norm[l][None]
        gla_c, st_c = _gla(g_all, wz, bz, gla_consts, onorm, zero_state, n=SEQ, n_seq=BATCH, first_block=0)
        gla_l, _ = _gla(g_all, wz, bz, gla_consts, onorm, _state_to_blockdiag(state_gla[:, l]),
                        n=DEC_SEQ, n_seq=DEC_BATCH, first_block=lat_first)

        conv_args = (jnp.broadcast_to(conv_dw[l][:, None, :], (CONV_WIDTH, 8, CONV_CH)), conv_dw_b[l][None],
                     conv_ln_g[l][None], conv_ln_b[l][None], _bf(conv_pw[l]))
        conv_c = _conv(c_all, *conv_args, n=SEQ, n_seq=BATCH, first_block=0)
        conv_l = _conv(c_all, *conv_args, n=DEC_SEQ, n_seq=DEC_BATCH, first_block=lat_first)

        qn, kn = na_q_norm[l][None], na_k_norm[l][None]
        na_c, k_l, v_l = _na_ctx(a_all, qn, kn)
        na_l = _na_lat(a_all, cache_na_k[:, l], cache_na_v[:, l], _na_bias_table(na_rpb[l]), qn, kn)

        x1, h2, h_rows, logits_t = _out_proj(gla_c, gla_l, conv_c, conv_l, na_c, na_l, x_ctx, x_lat, x_lat_tile, mod,
                                     norm_ffn[l][None], _bf(w_out[l]), router_w[l].T)
        plan = _moe_plan(*_router(logits_t, router_bias[l][:, None]))
        routed = [_gmm(l, half, plan, h_rows, exp_w_gate, exp_w_up, exp_w_down) for half in range(_N_HALVES)]
        x = _moe_finish(l, h2, routed[0], routed[1], x1, mod, sh_w_gate, sh_w_up, sh_w_down)
        x_ctx, x_lat, x_lat_tile = x, x, N_CTX_TILES

        states.append(_blockdiag_to_state(st_c))
        keys.append(k_l)
        vals.append(v_l)

    y_prompt = x[:N_CTX_TOK].reshape(BATCH, SEQ, D_MODEL)
    y_sample = x[N_CTX_TOK:].reshape(DEC_BATCH, DEC_SEQ, D_MODEL)
    return (y_prompt, y_sample, jnp.stack(states, axis=1), jnp.stack(keys, axis=1), jnp.stack(vals, axis=1))
```

```python
import functools

import numpy as np
import jax
import jax.numpy as jnp
from jax import lax
from jax.experimental import pallas as pl
from jax.experimental.pallas import tpu as pltpu

D_MODEL = 1024
BATCH = 16
SEQ = 256
DEPTH = 2
DEC_BATCH = 4
DEC_SEQ = 1024
PAST_LEN = 256
GRID_W = 64
GLA_HEADS = 4
GLA_DK = 64
GLA_DV = 128
GLA_KEY = GLA_HEADS * GLA_DK
GLA_VAL = GLA_HEADS * GLA_DV
GLA_GATE_RANK = 16
GLA_GATE_NORM = 16.0
GLA_CHUNK = 64
CONV_CH = 256
CONV_WIDTH = 31
NA_HEADS = 4
NA_HD = 64
NA_DIM = NA_HEADS * NA_HD
NA_WIN_ROWS = 8
NA_WIN_COLS = 16
N_EXPERTS = 64
TOP_K = 8
N_GROUPS = 8
TOPK_GROUPS = 4
EXPERT_DIM = 256
SHARED_DIM = 256
ROUTED_SCALE = 2.5
EPS = 1e-6

N_CTX_TOK = BATCH * SEQ
N_LAT_TOK = DEC_BATCH * DEC_SEQ
N_TOK = N_CTX_TOK + N_LAT_TOK
ROW_TILE = 512
N_CTX_TILES = N_CTX_TOK // ROW_TILE
TILES_PER_LAT_SEQ = DEC_SEQ // ROW_TILE
MOD_ROWS = 8
GLA_IN_W = 1664
GLA_LR_OFF = 2 * GLA_KEY + 2 * GLA_VAL
VMEM_LIMIT = 56 * 1024 * 1024

_BF = jnp.bfloat16
_F32 = jnp.float32


def _bf(x):
    return x.astype(_BF)


def _dot(a, b):
    return jnp.dot(a, b, preferred_element_type=_F32)


def _dot_nt(a, b):
    return lax.dot_general(a, b, (((1,), (1,)), ((), ())), preferred_element_type=_F32)


def _dot_tn(a, b):
    return lax.dot_general(a, b, (((0,), (0,)), ((), ())), preferred_element_type=_F32)


def _split2(x):
    hi = _bf(x)
    lo = _bf(x - hi.astype(_F32))
    return hi, lo


def _split3(x):
    hi = _bf(x)
    r = x - hi.astype(_F32)
    mid = _bf(r)
    lo = _bf(r - mid.astype(_F32))
    return hi, mid, lo


def _dot3(a, b, dot=_dot):
    a_hi, a_lo = _split2(a)
    b_hi, b_lo = _split2(b)
    return (dot(a_lo, b_hi) + dot(a_hi, b_lo)) + dot(a_hi, b_hi)


def _sigmoid(x):
    return 1.0 / (1.0 + jnp.exp(-x))


def _silu(x):
    return x * _sigmoid(x)


def _rms(x, w):
    return x * lax.rsqrt(jnp.mean(x * x, axis=-1, keepdims=True) + EPS) * w


def _params(sem):
    return pltpu.CompilerParams(dimension_semantics=sem, vmem_limit_bytes=VMEM_LIMIT)


def _mod_row(i):
    return jnp.where(i < N_CTX_TILES, 0, 1 + (i - N_CTX_TILES) // TILES_PER_LAT_SEQ)


def _ada_kernel(cv_ref, w_ref, b_ref, o_ref):
    o_ref[0] = _dot3(_silu(cv_ref[...]), w_ref[0]) + b_ref[0]


def _ada_mod(cvec, w_ada, b_ada):
    tn = 1024
    n_out = 6 * D_MODEL
    return pl.pallas_call(
        _ada_kernel,
        out_shape=jax.ShapeDtypeStruct((DEPTH, MOD_ROWS, n_out), _F32),
        grid=(DEPTH, n_out // tn),
        in_specs=[
            pl.BlockSpec((MOD_ROWS, D_MODEL), lambda l, j: (0, 0)),
            pl.BlockSpec((1, D_MODEL, tn), lambda l, j: (l, 0, j)),
            pl.BlockSpec((1, 1, tn), lambda l, j: (l, 0, j)),
        ],
        out_specs=pl.BlockSpec((1, MOD_ROWS, tn), lambda l, j: (l, 0, j)),
        compiler_params=_params(("arbitrary", "arbitrary")),
        name="ada_mod",
    )(cvec, w_ada, b_ada.reshape(DEPTH, 1, n_out))


def _inproj_kernel(xc_ref, xl_ref, nw_ref, mod_ref, wg_ref, wc_ref, wa_ref, g_ref, c_ref, a_ref):
    i = pl.program_id(0)
    row = _mod_row(i)
    sh = mod_ref[pl.ds(row, 1), 0:D_MODEL]
    sc = mod_ref[pl.ds(row, 1), D_MODEL:2 * D_MODEL]
    x = jnp.where(i < N_CTX_TILES, xc_ref[...], xl_ref[...])
    h = _bf(_rms(x, nw_ref[...]) * (1.0 + sc) + sh)
    g_ref[...] = _dot(h, wg_ref[...])
    c_ref[...] = _dot(h, wc_ref[...])
    a_ref[...] = _dot(h, wa_ref[...])


def _stream_tiles(lat_first_tile):
    ctx_tile = lambda i: (jnp.minimum(i, N_CTX_TILES - 1), 0)
    lat_tile = lambda i: (jnp.maximum(i - N_CTX_TILES, 0) + lat_first_tile, 0)
    return ctx_tile, lat_tile


def _in_proj(x_ctx, x_lat, lat_first_tile, norm_w, mod, w_gla, w_conv, w_na):
    full = lambda i: (0, 0)
    tile = lambda i: (i, 0)
    ctx_tile, lat_tile = _stream_tiles(lat_first_tile)
    return pl.pallas_call(
        _inproj_kernel,
        out_shape=(
            jax.ShapeDtypeStruct((N_TOK, GLA_IN_W), _F32),
            jax.ShapeDtypeStruct((N_TOK, 2 * CONV_CH), _F32),
            jax.ShapeDtypeStruct((N_TOK, 3 * NA_DIM), _F32),
        ),
        grid=(N_TOK // ROW_TILE,),
        in_specs=[
            pl.BlockSpec((ROW_TILE, D_MODEL), ctx_tile),
            pl.BlockSpec((ROW_TILE, D_MODEL), lat_tile),
            pl.BlockSpec((1, D_MODEL), full),
            pl.BlockSpec((MOD_ROWS, 6 * D_MODEL), full),
            pl.BlockSpec((D_MODEL, GLA_IN_W), full),
            pl.BlockSpec((D_MODEL, 2 * CONV_CH), full),
            pl.BlockSpec((D_MODEL, 3 * NA_DIM), full),
        ],
        out_specs=(
            pl.BlockSpec((ROW_TILE, GLA_IN_W), tile),
            pl.BlockSpec((ROW_TILE, 2 * CONV_CH), tile),
            pl.BlockSpec((ROW_TILE, 3 * NA_DIM), tile),
        ),
        compiler_params=_params(("arbitrary",)),
        name="in_proj",
    )(x_ctx, x_lat, norm_w, mod, w_gla, w_conv, w_na)


_GLA_LEVELS = (32, 16, 8, 4, 2, 1)
_N_EXP_BLOCKS = 2 + len(_GLA_LEVELS)
_N_MASKS = len(_GLA_LEVELS) + 1


def _gla_constants():
    cs = GLA_CHUNK
    r = np.arange(cs)
    i = r[:, None]
    c = r[None, :]
    w = np.zeros((2, _N_EXP_BLOCKS, cs, cs), np.float32)
    m = np.zeros((2, _N_MASKS, cs, cs), np.float32)
    w[0, 0] = c <= i
    w[0, 1] = c > i
    w[1, 0] = c >= i
    w[1, 1] = c < i
    for lv, half in enumerate(_GLA_LEVELS):
        mid = (r // (2 * half)) * (2 * half) + half
        mi = mid[:, None]
        second = (r >= mid)[:, None]
        same = (r[:, None] // (2 * half)) == (r[None, :] // (2 * half))
        w[0, 2 + lv] = np.where(second, (c >= mi) & (c <= i), (c > i) & (c <= mi - 1))
        w[1, 2 + lv] = np.where(second, (c >= mi) & (c <= i - 1), (c >= i) & (c <= mi - 1))
        m[0, lv] = same & (i >= mi) & (c < mi)
        m[1, lv] = same & (i < mi) & (c >= mi)
    m[:, _N_MASKS - 1] = np.eye(cs)
    w_all = w.reshape(2, _N_EXP_BLOCKS * cs, cs)
    lmask = np.tile(m, (1, 1, 1, GLA_HEADS))
    return jnp.asarray(w_all, _BF), jnp.asarray(lmask, _F32)


def _gla_kernel(g_ref, wz_ref, bz_ref, wall_ref, lmask_ref, onorm_ref, st0_ref,
                o_ref, stfin_ref, la_scr, o_scr, st_scr, *, n):
    cs = GLA_CHUNK
    nc = n // cs
    gate_rows = 128
    key_head = lax.broadcasted_iota(jnp.int32, (1, GLA_KEY), 1) // GLA_DK
    head_lanes = [(key_head == h).astype(_F32) for h in range(GLA_HEADS)]
    head_lanes_bf = [_bf(m) for m in head_lanes]
    zero_v = jnp.zeros((cs, GLA_DV), _BF)

    def log_decays(t, carry):
        rows = pl.ds(pl.multiple_of(t * gate_rows, gate_rows), gate_rows)
        z = _dot3(g_ref[rows, GLA_LR_OFF:GLA_IN_W], wz_ref[...]) + bz_ref[...]
        la_scr[rows, :] = (jnp.minimum(z, 0.0) - jnp.log1p(jnp.exp(-jnp.abs(z)))) * (1.0 / GLA_GATE_NORM)
        return carry

    lax.fori_loop(0, n // gate_rows, log_decays, 0)

    def chunk(c, d):
        rows = pl.ds(pl.multiple_of(c * cs, cs), cs)
        q = g_ref[rows, 0:GLA_KEY] * (GLA_DK ** -0.5)
        k = g_ref[rows, GLA_KEY:2 * GLA_KEY]
        v = g_ref[rows, 2 * GLA_KEY:2 * GLA_KEY + GLA_VAL]
        la_hi, la_mid, la_lo = _split3(la_scr[rows, d * GLA_KEY:(d + 1) * GLA_KEY])
        w = wall_ref[d]
        f = jnp.exp((_dot(w, la_lo) + _dot(w, la_mid)) + _dot(w, la_hi))
        st = st_scr[d]
        o = _dot_nt(_bf(q * f[0:cs]), _bf(st))
        p = jnp.zeros((cs, GLA_HEADS * cs), _F32)
        for lv in range(_N_MASKS):
            if lv < len(_GLA_LEVELS):
                fl = f[(2 + lv) * cs:(3 + lv) * cs]
                ql, kl = q * fl, k * fl
            else:
                ql, kl = q, k
            kl = _bf(kl)
            k_bd = jnp.concatenate([kl * m for m in head_lanes_bf], axis=0)
            p = p + lmask_ref[d, lv] * _dot_nt(_bf(ql), k_bd)
        vb = _bf(v)
        v_bd = jnp.concatenate(
            [jnp.concatenate([vb[:, g * GLA_DV:(g + 1) * GLA_DV] if g == h else zero_v for g in range(GLA_HEADS)], axis=1)
             for h in range(GLA_HEADS)], axis=0)
        o_scr[d, rows, :] = o + _dot(_bf(p), v_bd)
        decay = f[cs - 1:cs] if d == 0 else f[0:1]
        u_t = _dot_tn(vb, _bf(k * f[cs:2 * cs]))
        u_t = jnp.concatenate([u_t[h * GLA_DV:(h + 1) * GLA_DV] * head_lanes[h] for h in range(GLA_HEADS)], axis=0)
        st_scr[d] = st * decay + u_t

    st_scr[...] = st0_ref[0]

    def scan(i, carry):
        chunk(i, 0)
        chunk(nc - 1 - i, 1)
        return carry

    lax.fori_loop(0, nc, scan, 0)
    stfin_ref[0] = st_scr[...]

    def finish(c, carry):
        rows = pl.ds(pl.multiple_of(c * cs, cs), cs)
        o = o_scr[1, rows, :] + o_scr[0, rows, :]
        for h in range(GLA_HEADS):
            cols = slice(h * GLA_DV, (h + 1) * GLA_DV)
            gate = g_ref[rows, 2 * GLA_KEY + GLA_VAL + h * GLA_DV:2 * GLA_KEY + GLA_VAL + (h + 1) * GLA_DV]
            o_ref[rows, cols] = _rms(o[:, cols], onorm_ref[...]) * _silu(gate)
        return carry

    lax.fori_loop(0, nc, finish, 0)


def _gla(g_all, wz, bz, consts, onorm, st0, *, n, n_seq, first_block):
    w_all, lmask = consts
    full2 = lambda i: (0, 0)
    return pl.pallas_call(
        functools.partial(_gla_kernel, n=n),
        out_shape=(
            jax.ShapeDtypeStruct((n_seq * n, GLA_VAL), _F32),
            jax.ShapeDtypeStruct((n_seq, 2, GLA_VAL, GLA_KEY), _F32),
        ),
        grid=(n_seq,),
        in_specs=[
            pl.BlockSpec((n, GLA_IN_W), lambda i: (first_block + i, 0)),
            pl.BlockSpec(wz.shape, full2),
            pl.BlockSpec(bz.shape, full2),
            pl.BlockSpec(w_all.shape, lambda i: (0, 0, 0)),
            pl.BlockSpec(lmask.shape, lambda i: (0, 0, 0, 0)),
            pl.BlockSpec((1, GLA_DV), full2),
            pl.BlockSpec((1, 2, GLA_VAL, GLA_KEY), lambda i: (i, 0, 0, 0)),
        ],
        out_specs=(
            pl.BlockSpec((n, GLA_VAL), lambda i: (i, 0)),
            pl.BlockSpec((1, 2, GLA_VAL, GLA_KEY), lambda i: (i, 0, 0, 0)),
        ),
        scratch_shapes=[pltpu.VMEM((n, 2 * GLA_KEY), _F32), pltpu.VMEM((2, n, GLA_VAL), _F32),
                        pltpu.VMEM((2, GLA_VAL, GLA_KEY), _F32)],
        compiler_params=_params(("arbitrary",)),
        name=f"gla_{n}",
    )(g_all, wz, bz, w_all, lmask, onorm, st0)


def _state_to_blockdiag(s):
    b = s.shape[0]
    st = jnp.swapaxes(s, -1, -2)
    eye = jnp.eye(GLA_HEADS, dtype=s.dtype)
    out = st[:, :, :, :, None, :] * eye[None, None, :, None, :, None]
    return out.reshape(b, 2, GLA_VAL, GLA_KEY)


def _blockdiag_to_state(st):
    b = st.shape[0]
    s6 = st.reshape(b, 2, GLA_HEADS, GLA_DV, GLA_HEADS, GLA_DK)
    diag = jnp.stack([s6[:, :, h, :, h, :] for h in range(GLA_HEADS)], axis=2)
    return jnp.swapaxes(diag, -1, -2)


_CONV_PAD = 16
_CONV_ROWS = 128


def _conv_kernel(c_ref, dw_ref, dwb_ref, lng_ref, lnb_ref, pw_ref, o_ref, pad_scr, *, n):
    zeros = jnp.zeros((_CONV_PAD, CONV_CH), _F32)
    pad_scr[0:_CONV_PAD, :] = zeros
    pad_scr[_CONV_PAD + n:2 * _CONV_PAD + n, :] = zeros
    for r0 in range(0, n, _CONV_ROWS):
        a = c_ref[r0:r0 + _CONV_ROWS, :]
        pad_scr[_CONV_PAD + r0:_CONV_PAD + r0 + _CONV_ROWS, :] = a[:, :CONV_CH] * _sigmoid(a[:, CONV_CH:])
    half = CONV_WIDTH // 2
    for r0 in range(0, n, _CONV_ROWS):
        acc = jnp.zeros((_CONV_ROWS // 8, 8, CONV_CH), _F32)
        for w in range(CONV_WIDTH):
            s = _CONV_PAD + r0 + w - half
            acc = acc + pad_scr[s:s + _CONV_ROWS, :].reshape(_CONV_ROWS // 8, 8, CONV_CH) * dw_ref[w][None]
        acc = acc.reshape(_CONV_ROWS, CONV_CH) + dwb_ref[...]
        xc = acc - jnp.mean(acc, axis=-1, keepdims=True)
        y = xc * lax.rsqrt(jnp.mean(xc * xc, axis=-1, keepdims=True) + EPS) * lng_ref[...] + lnb_ref[...]
        o_ref[r0:r0 + _CONV_ROWS, :] = _dot(_bf(_silu(y)), pw_ref[...])


def _conv(c_all, dw, dwb, lng, lnb, pw, *, n, n_seq, first_block):
    full2 = lambda i: (0, 0)
    return pl.pallas_call(
        functools.partial(_conv_kernel, n=n),
        out_shape=jax.ShapeDtypeStruct((n_seq * n, CONV_CH), _F32),
        grid=(n_seq,),
        in_specs=[
            pl.BlockSpec((n, 2 * CONV_CH), lambda i: (first_block + i, 0)),
            pl.BlockSpec((CONV_WIDTH, 8, CONV_CH), lambda i: (0, 0, 0)),
            pl.BlockSpec((1, CONV_CH), full2),
            pl.BlockSpec((1, CONV_CH), full2),
            pl.BlockSpec((1, CONV_CH), full2),
            pl.BlockSpec((CONV_CH, CONV_CH), full2),
        ],
        out_specs=pl.BlockSpec((n, CONV_CH), lambda i: (i, 0)),
        scratch_shapes=[pltpu.VMEM((n + 2 * _CONV_PAD, CONV_CH), _F32)],
        compiler_params=_params(("arbitrary",)),
        name=f"conv_{n}",
    )(c_all, dw, dwb, lng, lnb, pw)


def _softmax_rows(s):
    e = jnp.exp(s - jnp.max(s, axis=-1, keepdims=True))
    return e / jnp.sum(e, axis=-1, keepdims=True)


def _na_ctx_kernel(a_ref, qn_ref, kn_ref, o_ref, kc_ref, vc_ref):
    for h in range(NA_HEADS):
        cols = slice(h * NA_HD, (h + 1) * NA_HD)
        q = _rms(a_ref[:, h * NA_HD:(h + 1) * NA_HD], qn_ref[...])
        k = _rms(a_ref[:, NA_DIM + h * NA_HD:NA_DIM + (h + 1) * NA_HD], kn_ref[...])
        v = a_ref[:, 2 * NA_DIM + h * NA_HD:2 * NA_DIM + (h + 1) * NA_HD]
        kc_ref[0, h] = k
        vc_ref[0, h] = v
        p = _softmax_rows(_dot_nt(_bf(q), _bf(k)) * (NA_HD ** -0.5))
        o_ref[:, cols] = _dot(_bf(p), _bf(v))


def _na_ctx(a_all, qn, kn):
    full2 = lambda i: (0, 0)
    cache = jax.ShapeDtypeStruct((BATCH, NA_HEADS, SEQ, NA_HD), _F32)
    cache_spec = pl.BlockSpec((1, NA_HEADS, SEQ, NA_HD), lambda i: (i, 0, 0, 0))
    return pl.pallas_call(
        _na_ctx_kernel,
        out_shape=(jax.ShapeDtypeStruct((N_CTX_TOK, NA_DIM), _F32), cache, cache),
        grid=(BATCH,),
        in_specs=[
            pl.BlockSpec((SEQ, 3 * NA_DIM), lambda i: (i, 0)),
            pl.BlockSpec((1, NA_HD), full2),
            pl.BlockSpec((1, NA_HD), full2),
        ],
        out_specs=(pl.BlockSpec((SEQ, NA_DIM), lambda i: (i, 0)), cache_spec, cache_spec),
        compiler_params=_params(("arbitrary",)),
        name="na_ctx",
    )(a_all, qn, kn)


_NA_ROWS = DEC_SEQ // GRID_W
_NA_KEYS = NA_WIN_ROWS * GRID_W
_NA_VARIANTS = NA_WIN_ROWS


_RPB_ROWS = 2 * NA_WIN_ROWS - 1
_RPB_COLS = 2 * NA_WIN_COLS - 1


def _na_bias_constants():
    qc = np.arange(GRID_W)[:, None]
    kc = np.arange(GRID_W)[None, :]
    shift = np.stack([(kc - qc + NA_WIN_COLS - 1) == co for co in range(_RPB_COLS)]).astype(np.float32)
    win_start = np.clip(qc - NA_WIN_COLS // 2, 0, GRID_W - NA_WIN_COLS)
    in_win = ((kc >= win_start) & (kc < win_start + NA_WIN_COLS)).astype(np.float32)
    return jnp.asarray(shift), jnp.asarray(in_win)


def _na_bias_kernel(rpb_ref, shift_ref, win_ref, o_ref):
    h = pl.program_id(0)
    in_win = win_ref[...] > 0.0
    tiles = []
    for ro in range(_RPB_ROWS):
        base = (h * _RPB_ROWS + ro) * _RPB_COLS
        acc = rpb_ref[base] * shift_ref[0]
        for co in range(1, _RPB_COLS):
            acc = acc + rpb_ref[base + co] * shift_ref[co]
        tiles.append(jnp.where(in_win, acc, -jnp.inf))
    for t in range(_NA_VARIANTS):
        o_ref[0, t] = jnp.concatenate([tiles[kr - t + NA_WIN_ROWS - 1] for kr in range(NA_WIN_ROWS)], axis=1)


def _na_bias_table(rpb):
    shift, in_win = _na_bias_constants()
    return pl.pallas_call(
        _na_bias_kernel,
        out_shape=jax.ShapeDtypeStruct((NA_HEADS, _NA_VARIANTS, GRID_W, _NA_KEYS), _F32),
        grid=(NA_HEADS,),
        in_specs=[
            pl.BlockSpec(memory_space=pltpu.SMEM),
            pl.BlockSpec(shift.shape, lambda h: (0, 0, 0)),
            pl.BlockSpec(in_win.shape, lambda h: (0, 0)),
        ],
        out_specs=pl.BlockSpec((1, _NA_VARIANTS, GRID_W, _NA_KEYS), lambda h: (h, 0, 0, 0)),
        compiler_params=_params(("arbitrary",)),
        name="na_bias",
    )(rpb.reshape(-1), shift, in_win)


def _na_row_groups():
    start = lambda r: min(max(r - NA_WIN_ROWS // 2, 0), _NA_ROWS - NA_WIN_ROWS)
    groups, r = [], 0
    while r < _NA_ROWS:
        n = 1
        while r + n < _NA_ROWS and start(r + n) == start(r):
            n += 1
        groups.append((r, n, start(r)))
        r += n
    return groups


def _na_lat_kernel(a_ref, kctx_ref, vctx_ref, bias_ref, qn_ref, kn_ref, o_ref,
                   q_scr, k_scr, v_scr, sctx_scr, pctx_scr):
    scale = NA_HD ** -0.5
    for h in range(NA_HEADS):
        cols = slice(h * NA_HD, (h + 1) * NA_HD)
        q_scr[...] = _bf(_rms(a_ref[:, h * NA_HD:(h + 1) * NA_HD], qn_ref[...]))
        k_scr[...] = _bf(_rms(a_ref[:, NA_DIM + h * NA_HD:NA_DIM + (h + 1) * NA_HD], kn_ref[...]))
        v_scr[...] = _bf(a_ref[:, 2 * NA_DIM + h * NA_HD:2 * NA_DIM + (h + 1) * NA_HD])
        sctx_scr[...] = _dot_nt(q_scr[...], _bf(kctx_ref[0, h])) * scale
        for r0, nr, ks in _na_row_groups():
            qrows = slice(r0 * GRID_W, (r0 + nr) * GRID_W)
            krows = slice(ks * GRID_W, ks * GRID_W + _NA_KEYS)
            bias = jnp.concatenate([bias_ref[h, r - ks] for r in range(r0, r0 + nr)], axis=0)
            s_loc = _dot_nt(q_scr[qrows, :], k_scr[krows, :]) * scale + bias
            s_ctx = sctx_scr[qrows, :]
            m = jnp.maximum(jnp.max(s_loc, axis=-1, keepdims=True), jnp.max(s_ctx, axis=-1, keepdims=True))
            e_loc = jnp.exp(s_loc - m)
            e_ctx = jnp.exp(s_ctx - m)
            inv = 1.0 / (jnp.sum(e_loc, axis=-1, keepdims=True) + jnp.sum(e_ctx, axis=-1, keepdims=True))
            pctx_scr[qrows, :] = _bf(e_ctx * inv)
            o_ref[qrows, cols] = _dot(_bf(e_loc * inv), v_scr[krows, :])
        o_ref[:, cols] = o_ref[:, cols] + _dot(pctx_scr[...], _bf(vctx_ref[0, h]))


def _na_lat(a_all, k_ctx, v_ctx, bias, qn, kn):
    full2 = lambda i: (0, 0)
    ctx_spec = pl.BlockSpec((1, NA_HEADS, PAST_LEN, NA_HD), lambda i: (i, 0, 0, 0))
    return pl.pallas_call(
        _na_lat_kernel,
        out_shape=jax.ShapeDtypeStruct((N_LAT_TOK, NA_DIM), _F32),
        grid=(DEC_BATCH,),
        in_specs=[
            pl.BlockSpec((DEC_SEQ, 3 * NA_DIM), lambda i: (N_CTX_TOK // DEC_SEQ + i, 0)),
            ctx_spec,
            ctx_spec,
            pl.BlockSpec(bias.shape, lambda i: (0, 0, 0, 0)),
            pl.BlockSpec((1, NA_HD), full2),
            pl.BlockSpec((1, NA_HD), full2),
        ],
        out_specs=pl.BlockSpec((DEC_SEQ, NA_DIM), lambda i: (i, 0)),
        scratch_shapes=[pltpu.VMEM((DEC_SEQ, NA_HD), _BF)] * 3
        + [pltpu.VMEM((DEC_SEQ, PAST_LEN), _F32), pltpu.VMEM((DEC_SEQ, PAST_LEN), _BF)],
        compiler_params=_params(("arbitrary",)),
        name="na_lat",
    )(a_all, k_ctx, v_ctx, bias, qn, kn)


def _outproj_kernel(gc_ref, gl_ref, cc_ref, cl_ref, nc_ref, nl_ref, xc_ref, xl_ref, mod_ref, nw_ref, wo_ref,
                    rwt_ref, x1_ref, h2_ref, hr_ref, lg_ref):
    i = pl.program_id(0)
    is_ctx = i < N_CTX_TILES
    row = _mod_row(i)
    gla = jnp.where(is_ctx, gc_ref[...], gl_ref[...])
    conv = jnp.where(is_ctx, cc_ref[...], cl_ref[...])
    na = jnp.where(is_ctx, nc_ref[...], nl_ref[...])
    mix = (_dot(_bf(gla), wo_ref[0:GLA_VAL, :])
           + _dot(_bf(conv), wo_ref[GLA_VAL:GLA_VAL + CONV_CH, :])
           + _dot(_bf(na), wo_ref[GLA_VAL + CONV_CH:, :]))
    g1 = mod_ref[pl.ds(row, 1), 2 * D_MODEL:3 * D_MODEL]
    sh2 = mod_ref[pl.ds(row, 1), 3 * D_MODEL:4 * D_MODEL]
    sc2 = mod_ref[pl.ds(row, 1), 4 * D_MODEL:5 * D_MODEL]
    x1 = jnp.where(is_ctx, xc_ref[...], xl_ref[...]) + g1 * mix
    h2 = _rms(x1, nw_ref[...]) * (1.0 + sc2) + sh2
    x1_ref[...] = x1
    h2_ref[...] = _bf(h2)
    for j in range(_ROW_CHUNKS):
        hr_ref[pl.ds(j, ROW_TILE, stride=_ROW_CHUNKS), :] = h2[:, j * _LANE:(j + 1) * _LANE]
    lg_ref[...] = _dot3(rwt_ref[...], h2, dot=_dot_nt)


def _out_proj(gla_c, gla_l, conv_c, conv_l, na_c, na_l, x_ctx, x_lat, lat_first_tile, mod, norm_w, w_out,
              router_wt):
    full2 = lambda i: (0, 0)
    tile = lambda i: (i, 0)
    ctx_tile, lat_tile = _stream_tiles(0)
    _, x_lat_tile = _stream_tiles(lat_first_tile)
    return pl.pallas_call(
        _outproj_kernel,
        out_shape=(
            jax.ShapeDtypeStruct((N_TOK, D_MODEL), _F32),
            jax.ShapeDtypeStruct((N_TOK, D_MODEL), _BF),
            jax.ShapeDtypeStruct((N_TOK * _ROW_CHUNKS, _LANE), _F32),
            jax.ShapeDtypeStruct((N_EXPERTS, N_TOK), _F32),
        ),
        grid=(N_TOK // ROW_TILE,),
        in_specs=[
            pl.BlockSpec((ROW_TILE, GLA_VAL), ctx_tile),
            pl.BlockSpec((ROW_TILE, GLA_VAL), lat_tile),
            pl.BlockSpec((ROW_TILE, CONV_CH), ctx_tile),
            pl.BlockSpec((ROW_TILE, CONV_CH), lat_tile),
            pl.BlockSpec((ROW_TILE, NA_DIM), ctx_tile),
            pl.BlockSpec((ROW_TILE, NA_DIM), lat_tile),
            pl.BlockSpec((ROW_TILE, D_MODEL), ctx_tile),
            pl.BlockSpec((ROW_TILE, D_MODEL), x_lat_tile),
            pl.BlockSpec((MOD_ROWS, 6 * D_MODEL), full2),
            pl.BlockSpec((1, D_MODEL), full2),
            pl.BlockSpec((D_MODEL, D_MODEL), full2),
            pl.BlockSpec((N_EXPERTS, D_MODEL), full2),
        ],
        out_specs=(
            pl.BlockSpec((ROW_TILE, D_MODEL), tile),
            pl.BlockSpec((ROW_TILE, D_MODEL), tile),
            pl.BlockSpec((ROW_TILE * _ROW_CHUNKS, _LANE), tile),
            pl.BlockSpec((N_EXPERTS, ROW_TILE), lambda i: (0, i)),
        ),
        compiler_params=_params(("arbitrary",)),
        name="out_proj",
    )(gla_c, gla_l, conv_c, conv_l, na_c, na_l, x_ctx, x_lat, mod, norm_w, w_out, router_wt)


_PER_GROUP = N_EXPERTS // N_GROUPS
_ROUTE_TILE = 1024


def _first_max(x, idx, axes, sentinel):
    m = x
    for ax in axes:
        m = jnp.max(m, axis=ax, keepdims=True)
    first = jnp.where(x == m, idx, sentinel)
    for ax in axes:
        first = jnp.min(first, axis=ax, keepdims=True)
    return m, first


def _router_kernel(lg_ref, rb_ref, eid_ref, wts_ref, cnt_ref):
    t = lg_ref.shape[1]
    shape3 = (N_GROUPS, _PER_GROUP, t)
    scores = _sigmoid(lg_ref[...])
    biased = (scores + rb_ref[...]).reshape(shape3)
    scores = scores.reshape(shape3)
    neg = -jnp.inf
    in_grp = lax.broadcasted_iota(jnp.int32, shape3, 1)
    grp = lax.broadcasted_iota(jnp.int32, (N_GROUPS, 1, t), 0)
    expert = lax.broadcasted_iota(jnp.int32, shape3, 0) * _PER_GROUP + in_grp
    m1, i1 = _first_max(biased, in_grp, (1,), _PER_GROUP)
    m2 = jnp.max(jnp.where(in_grp == i1, neg, biased), axis=1, keepdims=True)
    gscore = m1 + m2
    keep = jnp.zeros((N_GROUPS, 1, t), _F32)
    for _ in range(TOPK_GROUPS):
        _, gi = _first_max(gscore, grp, (0,), N_GROUPS)
        hit = grp == gi
        keep = jnp.where(hit, 1.0, keep)
        gscore = jnp.where(hit, neg, gscore)
    masked = jnp.where(keep > 0.0, biased, neg)
    ids, picked = [], []
    hits = jnp.zeros(shape3, _F32)
    for _ in range(TOP_K):
        _, ei = _first_max(masked, expert, (1, 0), N_EXPERTS)
        hit = expert == ei
        hits = jnp.where(hit, 1.0, hits)
        sc = jnp.sum(jnp.sum(jnp.where(hit, scores, 0.0), axis=1, keepdims=True), axis=0, keepdims=True)
        ids.append(ei.reshape(1, t))
        picked.append(sc.reshape(1, t))
        masked = jnp.where(hit, neg, masked)
    picked = jnp.concatenate(picked, axis=0)
    den = jnp.sum(picked, axis=0, keepdims=True)
    eid_ref[...] = jnp.concatenate(ids, axis=0)
    wts_ref[...] = picked / den * ROUTED_SCALE
    cnt_ref[0] = jnp.sum(hits, axis=2, keepdims=True).reshape(N_EXPERTS, 1)


def _router(logits_t, router_bias):
    tile = lambda i: (0, i)
    return pl.pallas_call(
        _router_kernel,
        out_shape=(jax.ShapeDtypeStruct((TOP_K, N_TOK), jnp.int32), jax.ShapeDtypeStruct((TOP_K, N_TOK), _F32),
                   jax.ShapeDtypeStruct((N_TOK // _ROUTE_TILE, N_EXPERTS, 1), _F32)),
        grid=(N_TOK // _ROUTE_TILE,),
        in_specs=[
            pl.BlockSpec((N_EXPERTS, _ROUTE_TILE), tile),
            pl.BlockSpec((N_EXPERTS, 1), lambda i: (0, 0)),
        ],
        out_specs=(pl.BlockSpec((TOP_K, _ROUTE_TILE), tile), pl.BlockSpec((TOP_K, _ROUTE_TILE), tile),
                   pl.BlockSpec((1, N_EXPERTS, 1), lambda i: (i, 0, 0))),
        compiler_params=_params(("arbitrary",)),
        name="router",
    )(logits_t, router_bias)


_HALF_TOK = N_CTX_TOK
_N_HALVES = N_TOK // _HALF_TOK
_HALF_ASSIGN = _HALF_TOK * TOP_K
_GB = 256
_GB_MAX = _HALF_ASSIGN // _GB + N_EXPERTS
_LANE = 128
_ROW_CHUNKS = D_MODEL // _LANE
_GS = _GB + 1
_STAGE_ROWS = (_ROW_CHUNKS * _GS + 7) // 8 * 8
_RMW_BATCH = 8
_TOK_BITS = 12
assert _HALF_TOK == 1 << _TOK_BITS
_FFN_SPLIT = 2


def _moe_plan(eid, wts, tile_counts):
    tok = jnp.arange(N_TOK, dtype=jnp.int32)
    key = (((tok >> _TOK_BITS) * N_EXPERTS)[None, :] + eid) * _HALF_TOK + (tok & (_HALF_TOK - 1))[None, :]
    key_s, gate_s = lax.sort((key.reshape(-1), wts.reshape(-1)), num_keys=1)
    row_off = (key_s & (_HALF_TOK - 1)) * _ROW_CHUNKS
    count = jnp.sum(tile_counts.reshape(_N_HALVES, -1, N_EXPERTS), axis=1).astype(jnp.int32)
    row_end = jnp.cumsum(count.reshape(-1)).reshape(_N_HALVES, N_EXPERTS)
    row_start = row_end - count
    nblk = (count + _GB - 1) // _GB
    blk_end = jnp.cumsum(nblk, axis=1)
    blk_start = blk_end - nblk
    n_used = blk_end[:, -1]
    b = jnp.minimum(jnp.arange(_GB_MAX, dtype=jnp.int32)[None, :], n_used[:, None] - 1)
    grp = jnp.sum((b[:, :, None] >= blk_end[:, None, :]).astype(jnp.int32), axis=-1)
    onehot = (grp[:, :, None] == jnp.arange(N_EXPERTS, dtype=jnp.int32)).astype(jnp.int32)
    pick = lambda v: jnp.sum(onehot * v[:, None, :], axis=-1)
    within = b - pick(blk_start)
    start = pick(row_start) + within * _GB
    length = jnp.clip(pick(count) - within * _GB, 0, _GB)
    pad = jnp.zeros((_GB,), jnp.int32)
    return (jnp.concatenate([row_off, pad]), jnp.concatenate([gate_s, pad.astype(_F32)]),
            grp.astype(jnp.int32), start.astype(jnp.int32), length.astype(jnp.int32), n_used.astype(jnp.int32))


def _gmm_kernel(be_ref, bstart_ref, blen_ref, nused_ref, off_ref, gate_ref,
                h_ref, wg_ref, wu_ref, wd_ref, acc_ref, xt_scr, yt_scr, wgb_scr, wub_scr, wdb_scr):
    s = pl.program_id(0)
    n_used = nused_ref[0]
    last = _GB_MAX - 1

    @pl.when(s == 0)
    def _():
        acc_ref[...] = jnp.zeros_like(acc_ref)
        xt_scr[...] = jnp.zeros_like(xt_scr)
        yt_scr[...] = jnp.zeros_like(yt_scr)

    fb = jnp.clip(s - 1, 0, last)

    @pl.when((s == 0) | (be_ref[fb] != be_ref[jnp.clip(s - 2, 0, last)]))
    def _():
        wgb_scr[...] = _bf(wg_ref[0, 0])
        wub_scr[...] = _bf(wu_ref[0, 0])
        wdb_scr[...] = _bf(wd_ref[0, 0])

    @pl.when(s < n_used + 2)
    def _():
        slot = s % 2
        other = 1 - slot

        cb = jnp.clip(s - 2, 0, last)
        c_start = bstart_ref[cb]
        c_len = blen_ref[cb]
        for m0 in range(0, _GB, _RMW_BATCH):
            pending = []
            for mi in range(m0, m0 + _RMW_BATCH):
                valid = mi < c_len
                row = jnp.where(valid, off_ref[c_start + mi], _HALF_TOK * _ROW_CHUNKS)
                gate = jnp.where(valid, gate_ref[c_start + mi], 0.0)
                rows = pl.ds(pl.multiple_of(row, _ROW_CHUNKS), _ROW_CHUNKS)
                contrib = gate * yt_scr[slot, pl.ds(mi, _ROW_CHUNKS, stride=_GS), :]
                pending.append((rows, acc_ref[rows, :] + contrib))
            for rows, val in pending:
                acc_ref[rows, :] = val

        part = _GB // _FFN_SPLIT
        for p in range(_FFN_SPLIT):
            x = _bf(jnp.concatenate([xt_scr[other, j * _GS + p * part:j * _GS + (p + 1) * part, :]
                                     for j in range(_ROW_CHUNKS)], axis=-1))
            act = _silu(_dot(x, wgb_scr[...])) * _dot(x, wub_scr[...])
            y = _dot(_bf(act), wdb_scr[...])
            for j in range(_ROW_CHUNKS):
                yt_scr[other, j * _GS + p * part:j * _GS + (p + 1) * part, :] = y[:, j * _LANE:(j + 1) * _LANE]

        g_start = bstart_ref[jnp.minimum(s, last)]
        for mi in range(_GB):
            src = pl.ds(pl.multiple_of(off_ref[g_start + mi], _ROW_CHUNKS), _ROW_CHUNKS)
            xt_scr[slot, pl.ds(mi, _ROW_CHUNKS, stride=_GS), :] = h_ref[src, :]


def _gmm(layer, half, plan, h_rows, wg, wu, wd):
    row_off, gate_s, grp, start, length, n_used = plan
    expert = lambda s, be, *_: (layer, be[jnp.clip(s - 1, 0, _GB_MAX - 1)], 0, 0)
    acc_rows = (_HALF_TOK + 1) * _ROW_CHUNKS
    grid_spec = pltpu.PrefetchScalarGridSpec(
        num_scalar_prefetch=6,
        grid=(_GB_MAX + 2,),
        in_specs=[
            pl.BlockSpec((_HALF_TOK * _ROW_CHUNKS, _LANE), lambda b, *_: (half, 0), pipeline_mode=pl.Buffered(1)),
            pl.BlockSpec((1, 1, D_MODEL, EXPERT_DIM), expert),
            pl.BlockSpec((1, 1, D_MODEL, EXPERT_DIM), expert),
            pl.BlockSpec((1, 1, EXPERT_DIM, D_MODEL), expert),
        ],
        out_specs=pl.BlockSpec((acc_rows, _LANE), lambda b, *_: (0, 0), pipeline_mode=pl.Buffered(1)),
        scratch_shapes=[pltpu.VMEM((2, _STAGE_ROWS, _LANE), _F32)] * 2
        + [pltpu.VMEM((D_MODEL, EXPERT_DIM), _BF)] * 2 + [pltpu.VMEM((EXPERT_DIM, D_MODEL), _BF)],
    )
    acc = pl.pallas_call(
        _gmm_kernel,
        out_shape=jax.ShapeDtypeStruct((acc_rows, _LANE), _F32),
        grid_spec=grid_spec,
        compiler_params=_params(("arbitrary",)),
        name="moe_experts",
    )(grp[half], start[half], length[half], n_used[half:half + 1], row_off, gate_s, h_rows, wg, wu, wd)
    return acc


_FIN_TILE = 512


def _moe_finish_kernel(h_ref, rc_ref, rl_ref, x1_ref, mod_ref, sg_ref, su_ref, sd_ref, o_ref):
    t = pl.program_id(0)
    tiles_per_half = _HALF_TOK // _FIN_TILE
    is_ctx = t < tiles_per_half
    row = jnp.where(is_ctx, 0, 1 + (t - tiles_per_half) // (DEC_SEQ // _FIN_TILE))
    h = h_ref[...]
    shared = _dot(_bf(_silu(_dot(h, _bf(sg_ref[0]))) * _dot(h, _bf(su_ref[0]))), _bf(sd_ref[0]))
    routed = jnp.concatenate(
        [jnp.where(is_ctx, rc_ref[pl.ds(j, _FIN_TILE, stride=_ROW_CHUNKS), :],
                   rl_ref[pl.ds(j, _FIN_TILE, stride=_ROW_CHUNKS), :]) for j in range(_ROW_CHUNKS)], axis=-1)
    g2 = mod_ref[pl.ds(row, 1), 5 * D_MODEL:6 * D_MODEL]
    o_ref[...] = x1_ref[...] + g2 * (routed + shared)


def _moe_finish(layer, h2, routed_c, routed_l, x1, mod, sg, su, sd):
    tile = lambda t: (t, 0)
    tiles_per_half = _HALF_TOK // _FIN_TILE
    ctx_tile = lambda t: (jnp.minimum(t, tiles_per_half - 1), 0)
    lat_tile = lambda t: (jnp.maximum(t - tiles_per_half, 0), 0)
    shared = lambda t: (layer, 0, 0)
    return pl.pallas_call(
        _moe_finish_kernel,
        out_shape=jax.ShapeDtypeStruct((N_TOK, D_MODEL), _F32),
        grid=(N_TOK // _FIN_TILE,),
        in_specs=[
            pl.BlockSpec((_FIN_TILE, D_MODEL), tile),
            pl.BlockSpec((_FIN_TILE * _ROW_CHUNKS, _LANE), ctx_tile),
            pl.BlockSpec((_FIN_TILE * _ROW_CHUNKS, _LANE), lat_tile),
            pl.BlockSpec((_FIN_TILE, D_MODEL), tile),
            pl.BlockSpec((MOD_ROWS, 6 * D_MODEL), lambda t: (0, 0)),
            pl.BlockSpec((1, D_MODEL, SHARED_DIM), shared),
            pl.BlockSpec((1, D_MODEL, SHARED_DIM), shared),
            pl.BlockSpec((1, SHARED_DIM, D_MODEL), shared),
        ],
        out_specs=pl.BlockSpec((_FIN_TILE, D_MODEL), tile),
        compiler_params=_params(("arbitrary",)),
        name="moe_finish",
    )(h2, routed_c, routed_l, x1, mod, sg, su, sd)


def kernel(x_prompt, x_sample, state_gla, cache_na_k, cache_na_v, c, c_ctx, w_ada, b_ada, norm_mix, norm_ffn, w_in, gla_w_gate, gla_b_gate, gla_out_norm, conv_dw, conv_dw_b, conv_ln_g, conv_ln_b, conv_pw, na_q_norm, na_k_norm, na_rpb, w_out, router_w, router_bias, exp_w_gate, exp_w_up, exp_w_down, sh_w_gate, sh_w_up, sh_w_down):
    x_ctx, x_lat, x_lat_tile = x_prompt.reshape(N_CTX_TOK, D_MODEL), x_sample.reshape(N_LAT_TOK, D_MODEL), 0
    cvec = jnp.concatenate([c_ctx[None], c, jnp.zeros((MOD_ROWS - 1 - DEC_BATCH, D_MODEL), _F32)], axis=0)
    mod_all = _ada_mod(cvec, w_ada, b_ada)
    gla_consts = _gla_constants()
    zero_state = jnp.zeros((BATCH, 2, GLA_VAL, GLA_KEY), _F32)
    lat_first = N_CTX_TOK // DEC_SEQ

    states, keys, vals = [], [], []
    for l in range(DEPTH):
        mod = mod_all[l]
        wi = w_in[l]
        w_gla = _bf(jnp.pad(wi[:, :GLA_LR_OFF + 2 * GLA_GATE_RANK], ((0, 0), (0, GLA_IN_W - GLA_LR_OFF - 2 * GLA_GATE_RANK))))
        conv_off = GLA_LR_OFF + 2 * GLA_GATE_RANK
        w_conv = _bf(wi[:, conv_off:conv_off + 2 * CONV_CH])
        w_na = _bf(wi[:, conv_off + 2 * CONV_CH:])
        g_all, c_all, a_all = _in_proj(x_ctx, x_lat, x_lat_tile, norm_mix[l][None], mod, w_gla, w_conv, w_na)

        wz = jnp.zeros((GLA_IN_W - GLA_LR_OFF, 2 * GLA_KEY), _F32)
        wz = wz.at[:GLA_GATE_RANK, :GLA_KEY].set(gla_w_gate[l, 0])
        wz = wz.at[GLA_GATE_RANK:2 * GLA_GATE_RANK, GLA_KEY:].set(gla_w_gate[l, 1])
        bz = gla_b_gate[l].reshape(1, 2 * GLA_KEY)
        onorm = gla_out_norm[l][None]
        gla_c, st_c = _gla(g_all, wz, bz, gla_consts, onorm, zero_state, n=SEQ, n_seq=BATCH, first_block=0)
        gla_l, _ = _gla(g_all, wz, bz, gla_consts, onorm, _state_to_blockdiag(state_gla[:, l]),
                        n=DEC_SEQ, n_seq=DEC_BATCH, first_block=lat_first)

        conv_args = (jnp.broadcast_to(conv_dw[l][:, None, :], (CONV_WIDTH, 8, CONV_CH)), conv_dw_b[l][None],
                     conv_ln_g[l][None], conv_ln_b[l][None], _bf(conv_pw[l]))
        conv_c = _conv(c_all, *conv_args, n=SEQ, n_seq=BATCH, first_block=0)
        conv_l = _conv(c_all, *conv_args, n=DEC_SEQ, n_seq=DEC_BATCH, first_block=lat_first)

        qn, kn = na_q_norm[l][None], na_k_norm[l][None]
        na_c, k_l, v_l = _na_ctx(a_all, qn, kn)
        na_l = _na_lat(a_all, cache_na_k[:, l], cache_na_v[:, l], _na_bias_table(na_rpb[l]), qn, kn)

        x1, h2, h_rows, logits_t = _out_proj(gla_c, gla_l, conv_c, conv_l, na_c, na_l, x_ctx, x_lat, x_lat_tile, mod,
                                     norm_ffn[l][None], _bf(w_out[l]), router_w[l].T)
        plan = _moe_plan(*_router(logits_t, router_bias[l][:, None]))
        routed = [_gmm(l, half, plan, h_rows, exp_w_gate, exp_w_up, exp_w_down) for half in range(_N_HALVES)]
        x = _moe_finish(l, h2, routed[0], routed[1], x1, mod, sh_w_gate, sh_w_up, sh_w_down)
        x_ctx, x_lat, x_lat_tile = x, x, N_CTX_TILES

        states.append(_blockdiag_to_state(st_c))
        keys.append(k_l)
        vals.append(v_l)

    y_prompt = x[:N_CTX_TOK].reshape(BATCH, SEQ, D_MODEL)
    y_sample = x[N_CTX_TOK:].reshape(DEC_BATCH, DEC_SEQ, D_MODEL)
    return (y_prompt, y_sample, jnp.stack(states, axis=1), jnp.stack(keys, axis=1), jnp.stack(vals, axis=1))
```

```python
import functools

import numpy as np
import jax
import jax.numpy as jnp
from jax import lax
from jax.experimental import pallas as pl
from jax.experimental.pallas import tpu as pltpu

D_MODEL = 1024
BATCH = 16
SEQ = 256
DEPTH = 2
DEC_BATCH = 4
DEC_SEQ = 1024
PAST_LEN = 256
GRID_W = 64
GLA_HEADS = 4
GLA_DK = 64
GLA_DV = 128
GLA_KEY = GLA_HEADS * GLA_DK
GLA_VAL = GLA_HEADS * GLA_DV
GLA_GATE_RANK = 16
GLA_GATE_NORM = 16.0
GLA_CHUNK = 64
CONV_CH = 256
CONV_WIDTH = 31
NA_HEADS = 4
NA_HD = 64
NA_DIM = NA_HEADS * NA_HD
NA_WIN_ROWS = 8
NA_WIN_COLS = 16
N_EXPERTS = 64
TOP_K = 8
N_GROUPS = 8
TOPK_GROUPS = 4
EXPERT_DIM = 256
SHARED_DIM = 256
ROUTED_SCALE = 2.5
EPS = 1e-6

N_CTX_TOK = BATCH * SEQ
N_LAT_TOK = DEC_BATCH * DEC_SEQ
N_TOK = N_CTX_TOK + N_LAT_TOK
ROW_TILE = 512
N_CTX_TILES = N_CTX_TOK // ROW_TILE
TILES_PER_LAT_SEQ = DEC_SEQ // ROW_TILE
MOD_ROWS = 8
GLA_IN_W = 1664
GLA_LR_OFF = 2 * GLA_KEY + 2 * GLA_VAL
VMEM_LIMIT = 56 * 1024 * 1024

_BF = jnp.bfloat16
_F32 = jnp.float32


def _bf(x):
    return x.astype(_BF)


def _dot(a, b):
    return jnp.dot(a, b, preferred_element_type=_F32)


def _dot_nt(a, b):
    return lax.dot_general(a, b, (((1,), (1,)), ((), ())), preferred_element_type=_F32)


def _dot_tn(a, b):
    return lax.dot_general(a, b, (((0,), (0,)), ((), ())), preferred_element_type=_F32)


def _split2(x):
    hi = _bf(x)
    lo = _bf(x - hi.astype(_F32))
    return hi, lo


def _split3(x):
    hi = _bf(x)
    r = x - hi.astype(_F32)
    mid = _bf(r)
    lo = _bf(r - mid.astype(_F32))
    return hi, mid, lo


def _dot3(a, b, dot=_dot):
    a_hi, a_lo = _split2(a)
    b_hi, b_lo = _split2(b)
    return (dot(a_lo, b_hi) + dot(a_hi, b_lo)) + dot(a_hi, b_hi)


def _sigmoid(x):
    return 1.0 / (1.0 + jnp.exp(-x))


def _silu(x):
    return x * _sigmoid(x)


def _rms(x, w):
    return x * lax.rsqrt(jnp.mean(x * x, axis=-1, keepdims=True) + EPS) * w


def _params(sem):
    return pltpu.CompilerParams(dimension_semantics=sem, vmem_limit_bytes=VMEM_LIMIT)


def _mod_row(i):
    return jnp.where(i < N_CTX_TILES, 0, 1 + (i - N_CTX_TILES) // TILES_PER_LAT_SEQ)


def _ada_kernel(cv_ref, w_ref, b_ref, o_ref):
    o_ref[0] = _dot3(_silu(cv_ref[...]), w_ref[0]) + b_ref[0]


def _ada_mod(cvec, w_ada, b_ada):
    tn = 1024
    n_out = 6 * D_MODEL
    return pl.pallas_call(
        _ada_kernel,
        out_shape=jax.ShapeDtypeStruct((DEPTH, MOD_ROWS, n_out), _F32),
        grid=(DEPTH, n_out // tn),
        in_specs=[
            pl.BlockSpec((MOD_ROWS, D_MODEL), lambda l, j: (0, 0)),
            pl.BlockSpec((1, D_MODEL, tn), lambda l, j: (l, 0, j)),
            pl.BlockSpec((1, 1, tn), lambda l, j: (l, 0, j)),
        ],
        out_specs=pl.BlockSpec((1, MOD_ROWS, tn), lambda l, j: (l, 0, j)),
        compiler_params=_params(("arbitrary", "arbitrary")),
        name="ada_mod",
    )(cvec, w_ada, b_ada.reshape(DEPTH, 1, n_out))


def _inproj_kernel(xc_ref, xl_ref, nw_ref, mod_ref, wg_ref, wc_ref, wa_ref, g_ref, c_ref, a_ref):
    i = pl.program_id(0)
    row = _mod_row(i)
    sh = mod_ref[pl.ds(row, 1), 0:D_MODEL]
    sc = mod_ref[pl.ds(row, 1), D_MODEL:2 * D_MODEL]
    x = jnp.where(i < N_CTX_TILES, xc_ref[...], xl_ref[...])
    h = _bf(_rms(x, nw_ref[...]) * (1.0 + sc) + sh)
    g_ref[...] = _dot(h, wg_ref[...])
    c_ref[...] = _dot(h, wc_ref[...])
    a_ref[...] = _dot(h, wa_ref[...])


def _stream_tiles(lat_first_tile):
    ctx_tile = lambda i: (jnp.minimum(i, N_CTX_TILES - 1), 0)
    lat_tile = lambda i: (jnp.maximum(i - N_CTX_TILES, 0) + lat_first_tile, 0)
    return ctx_tile, lat_tile


def _in_proj(x_ctx, x_lat, lat_first_tile, norm_w, mod, w_gla, w_conv, w_na):
    full = lambda i: (0, 0)
    tile = lambda i: (i, 0)
    ctx_tile, lat_tile = _stream_tiles(lat_first_tile)
    return pl.pallas_call(
        _inproj_kernel,
        out_shape=(
            jax.ShapeDtypeStruct((N_TOK, GLA_IN_W), _F32),
            jax.ShapeDtypeStruct((N_TOK, 2 * CONV_CH), _F32),
            jax.ShapeDtypeStruct((N_TOK, 3 * NA_DIM), _F32),
        ),
        grid=(N_TOK // ROW_TILE,),
        in_specs=[
            pl.BlockSpec((ROW_TILE, D_MODEL), ctx_tile),
            pl.BlockSpec((ROW_TILE, D_MODEL), lat_tile),
            pl.BlockSpec((1, D_MODEL), full),
            pl.BlockSpec((MOD_ROWS, 6 * D_MODEL), full),
            pl.BlockSpec((D_MODEL, GLA_IN_W), full),
            pl.BlockSpec((D_MODEL, 2 * CONV_CH), full),
            pl.BlockSpec((D_MODEL, 3 * NA_DIM), full),
        ],
        out_specs=(
            pl.BlockSpec((ROW_TILE, GLA_IN_W), tile),
            pl.BlockSpec((ROW_TILE, 2 * CONV_CH), tile),
            pl.BlockSpec((ROW_TILE, 3 * NA_DIM), tile),
        ),
        compiler_params=_params(("arbitrary",)),
        name="in_proj",
    )(x_ctx, x_lat, norm_w, mod, w_gla, w_conv, w_na)


_GLA_LEVELS = (32, 16, 8, 4, 2, 1)
_N_EXP_BLOCKS = 2 + len(_GLA_LEVELS)
_N_MASKS = len(_GLA_LEVELS) + 1


def _gla_constants():
    cs = GLA_CHUNK
    r = np.arange(cs)
    i = r[:, None]
    c = r[None, :]
    w = np.zeros((2, _N_EXP_BLOCKS, cs, cs), np.float32)
    m = np.zeros((2, _N_MASKS, cs, cs), np.float32)
    w[0, 0] = c <= i
    w[0, 1] = c > i
    w[1, 0] = c >= i
    w[1, 1] = c < i
    for lv, half in enumerate(_GLA_LEVELS):
        mid = (r // (2 * half)) * (2 * half) + half
        mi = mid[:, None]
        second = (r >= mid)[:, None]
        same = (r[:, None] // (2 * half)) == (r[None, :] // (2 * half))
        w[0, 2 + lv] = np.where(second, (c >= mi) & (c <= i), (c > i) & (c <= mi - 1))
        w[1, 2 + lv] = np.where(second, (c >= mi) & (c <= i - 1), (c >= i) & (c <= mi - 1))
        m[0, lv] = same & (i >= mi) & (c < mi)
        m[1, lv] = same & (i < mi) & (c >= mi)
    m[:, _N_MASKS - 1] = np.eye(cs)
    w_all = w.reshape(2, _N_EXP_BLOCKS * cs, cs)
    lmask = np.tile(m, (1, 1, 1, GLA_HEADS))
    return jnp.asarray(w_all, _BF), jnp.asarray(lmask, _F32)


def _gla_kernel(g_ref, wz_ref, bz_ref, wall_ref, lmask_ref, onorm_ref, st0_ref,
                o_ref, stfin_ref, la_scr, o_scr, st_scr, *, n):
    cs = GLA_CHUNK
    nc = n // cs
    gate_rows = 128
    key_head = lax.broadcasted_iota(jnp.int32, (1, GLA_KEY), 1) // GLA_DK
    head_lanes = [(key_head == h).astype(_F32) for h in range(GLA_HEADS)]
    head_lanes_bf = [_bf(m) for m in head_lanes]
    zero_v = jnp.zeros((cs, GLA_DV), _BF)

    def log_decays(t, carry):
        rows = pl.ds(pl.multiple_of(t * gate_rows, gate_rows), gate_rows)
        z = _dot3(g_ref[rows, GLA_LR_OFF:GLA_IN_W], wz_ref[...]) + bz_ref[...]
        la_scr[rows, :] = (jnp.minimum(z, 0.0) - jnp.log1p(jnp.exp(-jnp.abs(z)))) * (1.0 / GLA_GATE_NORM)
        return carry

    lax.fori_loop(0, n // gate_rows, log_decays, 0)

    def chunk(c, d):
        rows = pl.ds(pl.multiple_of(c * cs, cs), cs)
        q = g_ref[rows, 0:GLA_KEY] * (GLA_DK ** -0.5)
        k = g_ref[rows, GLA_KEY:2 * GLA_KEY]
        v = g_ref[rows, 2 * GLA_KEY:2 * GLA_KEY + GLA_VAL]
        la_hi, la_mid, la_lo = _split3(la_scr[rows, d * GLA_KEY:(d + 1) * GLA_KEY])
        w = wall_ref[d]
        f = jnp.exp((_dot(w, la_lo) + _dot(w, la_mid)) + _dot(w, la_hi))
        st = st_scr[d]
        o = _dot_nt(_bf(q * f[0:cs]), _bf(st))
        p = jnp.zeros((cs, GLA_HEADS * cs), _F32)
        for lv in range(_N_MASKS):
            if lv < len(_GLA_LEVELS):
                fl = f[(2 + lv) * cs:(3 + lv) * cs]
                ql, kl = q * fl, k * fl
            else:
                ql, kl = q, k
            kl = _bf(kl)
            k_bd = jnp.concatenate([kl * m for m in head_lanes_bf], axis=0)
            p = p + lmask_ref[d, lv] * _dot_nt(_bf(ql), k_bd)
        vb = _bf(v)
        v_bd = jnp.concatenate(
            [jnp.concatenate([vb[:, g * GLA_DV:(g + 1) * GLA_DV] if g == h else zero_v for g in range(GLA_HEADS)], axis=1)
             for h in range(GLA_HEADS)], axis=0)
        o_scr[d, rows, :] = o + _dot(_bf(p), v_bd)
        decay = f[cs - 1:cs] if d == 0 else f[0:1]
        u_t = _dot_tn(vb, _bf(k * f[cs:2 * cs]))
        u_t = jnp.concatenate([u_t[h * GLA_DV:(h + 1) * GLA_DV] * head_lanes[h] for h in range(GLA_HEADS)], axis=0)
        st_scr[d] = st * decay + u_t

    st_scr[...] = st0_ref[0]

    def scan(i, carry):
        chunk(i, 0)
        chunk(nc - 1 - i, 1)
        return carry

    lax.fori_loop(0, nc, scan, 0)
    stfin_ref[0] = st_scr[...]

    def finish(c, carry):
        rows = pl.ds(pl.multiple_of(c * cs, cs), cs)
        o = o_scr[1, rows, :] + o_scr[0, rows, :]
        for h in range(GLA_HEADS):
            cols = slice(h * GLA_DV, (h + 1) * GLA_DV)
            gate = g_ref[rows, 2 * GLA_KEY + GLA_VAL + h * GLA_DV:2 * GLA_KEY + GLA_VAL + (h + 1) * GLA_DV]
            o_ref[rows, cols] = _rms(o[:, cols], onorm_ref[...]) * _silu(gate)
        return carry

    lax.fori_loop(0, nc, finish, 0)


def _gla(g_all, wz, bz, consts, onorm, st0, *, n, n_seq, first_block):
    w_all, lmask = consts
    full2 = lambda i: (0, 0)
    return pl.pallas_call(
        functools.partial(_gla_kernel, n=n),
        out_shape=(
            jax.ShapeDtypeStruct((n_seq * n, GLA_VAL), _F32),
            jax.ShapeDtypeStruct((n_seq, 2, GLA_VAL, GLA_KEY), _F32),
        ),
        grid=(n_seq,),
        in_specs=[
            pl.BlockSpec((n, GLA_IN_W), lambda i: (first_block + i, 0)),
            pl.BlockSpec(wz.shape, full2),
            pl.BlockSpec(bz.shape, full2),
            pl.BlockSpec(w_all.shape, lambda i: (0, 0, 0)),
            pl.BlockSpec(lmask.shape, lambda i: (0, 0, 0, 0)),
            pl.BlockSpec((1, GLA_DV), full2),
            pl.BlockSpec((1, 2, GLA_VAL, GLA_KEY), lambda i: (i, 0, 0, 0)),
        ],
        out_specs=(
            pl.BlockSpec((n, GLA_VAL), lambda i: (i, 0)),
            pl.BlockSpec((1, 2, GLA_VAL, GLA_KEY), lambda i: (i, 0, 0, 0)),
        ),
        scratch_shapes=[pltpu.VMEM((n, 2 * GLA_KEY), _F32), pltpu.VMEM((2, n, GLA_VAL), _F32),
                        pltpu.VMEM((2, GLA_VAL, GLA_KEY), _F32)],
        compiler_params=_params(("arbitrary",)),
        name=f"gla_{n}",
    )(g_all, wz, bz, w_all, lmask, onorm, st0)


def _state_to_blockdiag(s):
    b = s.shape[0]
    st = jnp.swapaxes(s, -1, -2)
    eye = jnp.eye(GLA_HEADS, dtype=s.dtype)
    out = st[:, :, :, :, None, :] * eye[None, None, :, None, :, None]
    return out.reshape(b, 2, GLA_VAL, GLA_KEY)


def _blockdiag_to_state(st):
    b = st.shape[0]
    s6 = st.reshape(b, 2, GLA_HEADS, GLA_DV, GLA_HEADS, GLA_DK)
    diag = jnp.stack([s6[:, :, h, :, h, :] for h in range(GLA_HEADS)], axis=2)
    return jnp.swapaxes(diag, -1, -2)


_CONV_PAD = 16
_CONV_ROWS = 128


def _conv_kernel(c_ref, dw_ref, dwb_ref, lng_ref, lnb_ref, pw_ref, o_ref, pad_scr, *, n):
    zeros = jnp.zeros((_CONV_PAD, CONV_CH), _F32)
    pad_scr[0:_CONV_PAD, :] = zeros
    pad_scr[_CONV_PAD + n:2 * _CONV_PAD + n, :] = zeros
    for r0 in range(0, n, _CONV_ROWS):
        a = c_ref[r0:r0 + _CONV_ROWS, :]
        pad_scr[_CONV_PAD + r0:_CONV_PAD + r0 + _CONV_ROWS, :] = a[:, :CONV_CH] * _sigmoid(a[:, CONV_CH:])
    half = CONV_WIDTH // 2
    for r0 in range(0, n, _CONV_ROWS):
        acc = jnp.zeros((_CONV_ROWS // 8, 8, CONV_CH), _F32)
        for w in range(CONV_WIDTH):
            s = _CONV_PAD + r0 + w - half
            acc = acc + pad_scr[s:s + _CONV_ROWS, :].reshape(_CONV_ROWS // 8, 8, CONV_CH) * dw_ref[w][None]
        acc = acc.reshape(_CONV_ROWS, CONV_CH) + dwb_ref[...]
        xc = acc - jnp.mean(acc, axis=-1, keepdims=True)
        y = xc * lax.rsqrt(jnp.mean(xc * xc, axis=-1, keepdims=True) + EPS) * lng_ref[...] + lnb_ref[...]
        o_ref[r0:r0 + _CONV_ROWS, :] = _dot(_bf(_silu(y)), pw_ref[...])


def _conv(c_all, dw, dwb, lng, lnb, pw, *, n, n_seq, first_block):
    full2 = lambda i: (0, 0)
    return pl.pallas_call(
        functools.partial(_conv_kernel, n=n),
        out_shape=jax.ShapeDtypeStruct((n_seq * n, CONV_CH), _F32),
        grid=(n_seq,),
        in_specs=[
            pl.BlockSpec((n, 2 * CONV_CH), lambda i: (first_block + i, 0)),
            pl.BlockSpec((CONV_WIDTH, 8, CONV_CH), lambda i: (0, 0, 0)),
            pl.BlockSpec((1, CONV_CH), full2),
            pl.BlockSpec((1, CONV_CH), full2),
            pl.BlockSpec((1, CONV_CH), full2),
            pl.BlockSpec((CONV_CH, CONV_CH), full2),
        ],
        out_specs=pl.BlockSpec((n, CONV_CH), lambda i: (i, 0)),
        scratch_shapes=[pltpu.VMEM((n + 2 * _CONV_PAD, CONV_CH), _F32)],
        compiler_params=_params(("arbitrary",)),
        name=f"conv_{n}",
    )(c_all, dw, dwb, lng, lnb, pw)


def _softmax_rows(s):
    e = jnp.exp(s - jnp.max(s, axis=-1, keepdims=True))
    return e / jnp.sum(e, axis=-1, keepdims=True)


def _na_ctx_kernel(a_ref, qn_ref, kn_ref, o_ref, kc_ref, vc_ref):
    for h in range(NA_HEADS):
        cols = slice(h * NA_HD, (h + 1) * NA_HD)
        q = _rms(a_ref[:, h * NA_HD:(h + 1) * NA_HD], qn_ref[...])
        k = _rms(a_ref[:, NA_DIM + h * NA_HD:NA_DIM + (h + 1) * NA_HD], kn_ref[...])
        v = a_ref[:, 2 * NA_DIM + h * NA_HD:2 * NA_DIM + (h + 1) * NA_HD]
        kc_ref[0, h] = k
        vc_ref[0, h] = v
        p = _softmax_rows(_dot_nt(_bf(q), _bf(k)) * (NA_HD ** -0.5))
        o_ref[:, cols] = _dot(_bf(p), _bf(v))


def _na_ctx(a_all, qn, kn):
    full2 = lambda i: (0, 0)
    cache = jax.ShapeDtypeStruct((BATCH, NA_HEADS, SEQ, NA_HD), _F32)
    cache_spec = pl.BlockSpec((1, NA_HEADS, SEQ, NA_HD), lambda i: (i, 0, 0, 0))
    return pl.pallas_call(
        _na_ctx_kernel,
        out_shape=(jax.ShapeDtypeStruct((N_CTX_TOK, NA_DIM), _F32), cache, cache),
        grid=(BATCH,),
        in_specs=[
            pl.BlockSpec((SEQ, 3 * NA_DIM), lambda i: (i, 0)),
            pl.BlockSpec((1, NA_HD), full2),
            pl.BlockSpec((1, NA_HD), full2),
        ],
        out_specs=(pl.BlockSpec((SEQ, NA_DIM), lambda i: (i, 0)), cache_spec, cache_spec),
        compiler_params=_params(("arbitrary",)),
        name="na_ctx",
    )(a_all, qn, kn)


_NA_ROWS = DEC_SEQ // GRID_W
_NA_KEYS = NA_WIN_ROWS * GRID_W
_NA_VARIANTS = NA_WIN_ROWS


_RPB_ROWS = 2 * NA_WIN_ROWS - 1
_RPB_COLS = 2 * NA_WIN_COLS - 1


def _na_bias_constants():
    qc = np.arange(GRID_W)[:, None]
    kc = np.arange(GRID_W)[None, :]
    shift = np.stack([(kc - qc + NA_WIN_COLS - 1) == co for co in range(_RPB_COLS)]).astype(np.float32)
    win_start = np.clip(qc - NA_WIN_COLS // 2, 0, GRID_W - NA_WIN_COLS)
    in_win = ((kc >= win_start) & (kc < win_start + NA_WIN_COLS)).astype(np.float32)
    return jnp.asarray(shift), jnp.asarray(in_win)


def _na_bias_kernel(rpb_ref, shift_ref, win_ref, o_ref):
    h = pl.program_id(0)
    in_win = win_ref[...] > 0.0
    tiles = []
    for ro in range(_RPB_ROWS):
        base = (h * _RPB_ROWS + ro) * _RPB_COLS
        acc = rpb_ref[base] * shift_ref[0]
        for co in range(1, _RPB_COLS):
            acc = acc + rpb_ref[base + co] * shift_ref[co]
        tiles.append(jnp.where(in_win, acc, -jnp.inf))
    for t in range(_NA_VARIANTS):
        o_ref[0, t] = jnp.concatenate([tiles[kr - t + NA_WIN_ROWS - 1] for kr in range(NA_WIN_ROWS)], axis=1)


def _na_bias_table(rpb):
    shift, in_win = _na_bias_constants()
    return pl.pallas_call(
        _na_bias_kernel,
        out_shape=jax.ShapeDtypeStruct((NA_HEADS, _NA_VARIANTS, GRID_W, _NA_KEYS), _F32),
        grid=(NA_HEADS,),
        in_specs=[
            pl.BlockSpec(memory_space=pltpu.SMEM),
            pl.BlockSpec(shift.shape, lambda h: (0, 0, 0)),
            pl.BlockSpec(in_win.shape, lambda h: (0, 0)),
        ],
        out_specs=pl.BlockSpec((1, _NA_VARIANTS, GRID_W, _NA_KEYS), lambda h: (h, 0, 0, 0)),
        compiler_params=_params(("arbitrary",)),
        name="na_bias",
    )(rpb.reshape(-1), shift, in_win)


def _na_row_groups():
    start = lambda r: min(max(r - NA_WIN_ROWS // 2, 0), _NA_ROWS - NA_WIN_ROWS)
    groups, r = [], 0
    while r < _NA_ROWS:
        n = 1
        while r + n < _NA_ROWS and start(r + n) == start(r):
            n += 1
        groups.append((r, n, start(r)))
        r += n
    return groups


def _na_lat_kernel(a_ref, kctx_ref, vctx_ref, bias_ref, qn_ref, kn_ref, o_ref,
                   q_scr, k_scr, v_scr, sctx_scr, pctx_scr):
    scale = NA_HD ** -0.5
    for h in range(NA_HEADS):
        cols = slice(h * NA_HD, (h + 1) * NA_HD)
        q_scr[...] = _bf(_rms(a_ref[:, h * NA_HD:(h + 1) * NA_HD], qn_ref[...]))
        k_scr[...] = _bf(_rms(a_ref[:, NA_DIM + h * NA_HD:NA_DIM + (h + 1) * NA_HD], kn_ref[...]))
        v_scr[...] = _bf(a_ref[:, 2 * NA_DIM + h * NA_HD:2 * NA_DIM + (h + 1) * NA_HD])
        sctx_scr[...] = _dot_nt(q_scr[...], _bf(kctx_ref[0, h])) * scale
        for r0, nr, ks in _na_row_groups():
            qrows = slice(r0 * GRID_W, (r0 + nr) * GRID_W)
            krows = slice(ks * GRID_W, ks * GRID_W + _NA_KEYS)
            bias = jnp.concatenate([bias_ref[h, r - ks] for r in range(r0, r0 + nr)], axis=0)
            s_loc = _dot_nt(q_scr[qrows, :], k_scr[krows, :]) * scale + bias
            s_ctx = sctx_scr[qrows, :]
            m = jnp.maximum(jnp.max(s_loc, axis=-1, keepdims=True), jnp.max(s_ctx, axis=-1, keepdims=True))
            e_loc = jnp.exp(s_loc - m)
            e_ctx = jnp.exp(s_ctx - m)
            inv = 1.0 / (jnp.sum(e_loc, axis=-1, keepdims=True) + jnp.sum(e_ctx, axis=-1, keepdims=True))
            pctx_scr[qrows, :] = _bf(e_ctx * inv)
            o_ref[qrows, cols] = _dot(_bf(e_loc * inv), v_scr[krows, :])
        o_ref[:, cols] = o_ref[:, cols] + _dot(pctx_scr[...], _bf(vctx_ref[0, h]))


def _na_lat(a_all, k_ctx, v_ctx, bias, qn, kn):
    full2 = lambda i: (0, 0)
    ctx_spec = pl.BlockSpec((1, NA_HEADS, PAST_LEN, NA_HD), lambda i: (i, 0, 0, 0))
    return pl.pallas_call(
        _na_lat_kernel,
        out_shape=jax.ShapeDtypeStruct((N_LAT_TOK, NA_DIM), _F32),
        grid=(DEC_BATCH,),
        in_specs=[
            pl.BlockSpec((DEC_SEQ, 3 * NA_DIM), lambda i: (N_CTX_TOK // DEC_SEQ + i, 0)),
            ctx_spec,
            ctx_spec,
            pl.BlockSpec(bias.shape, lambda i: (0, 0, 0, 0)),
            pl.BlockSpec((1, NA_HD), full2),
            pl.BlockSpec((1, NA_HD), full2),
        ],
        out_specs=pl.BlockSpec((DEC_SEQ, NA_DIM), lambda i: (i, 0)),
        scratch_shapes=[pltpu.VMEM((DEC_SEQ, NA_HD), _BF)] * 3
        + [pltpu.VMEM((DEC_SEQ, PAST_LEN), _F32), pltpu.VMEM((DEC_SEQ, PAST_LEN), _BF)],
        compiler_params=_params(("arbitrary",)),
        name="na_lat",
    )(a_all, k_ctx, v_ctx, bias, qn, kn)


def _outproj_kernel(gc_ref, gl_ref, cc_ref, cl_ref, nc_ref, nl_ref, xc_ref, xl_ref, mod_ref, nw_ref, wo_ref,
                    rwt_ref, x1_ref, h2_ref, hr_ref, lg_ref):
    i = pl.program_id(0)
    is_ctx = i < N_CTX_TILES
    row = _mod_row(i)
    gla = jnp.where(is_ctx, gc_ref[...], gl_ref[...])
    conv = jnp.where(is_ctx, cc_ref[...], cl_ref[...])
    na = jnp.where(is_ctx, nc_ref[...], nl_ref[...])
    mix = (_dot(_bf(gla), wo_ref[0:GLA_VAL, :])
           + _dot(_bf(conv), wo_ref[GLA_VAL:GLA_VAL + CONV_CH, :])
           + _dot(_bf(na), wo_ref[GLA_VAL + CONV_CH:, :]))
    g1 = mod_ref[pl.ds(row, 1), 2 * D_MODEL:3 * D_MODEL]
    sh2 = mod_ref[pl.ds(row, 1), 3 * D_MODEL:4 * D_MODEL]
    sc2 = mod_ref[pl.ds(row, 1), 4 * D_MODEL:5 * D_MODEL]
    x1 = jnp.where(is_ctx, xc_ref[...], xl_ref[...]) + g1 * mix
    h2 = _rms(x1, nw_ref[...]) * (1.0 + sc2) + sh2
    x1_ref[...] = x1
    h2_ref[...] = _bf(h2)
    for j in range(_ROW_CHUNKS):
        hr_ref[pl.ds(j, ROW_TILE, stride=_ROW_CHUNKS), :] = h2[:, j * _LANE:(j + 1) * _LANE]
    lg_ref[...] = _dot3(rwt_ref[...], h2, dot=_dot_nt)


def _out_proj(gla_c, gla_l, conv_c, conv_l, na_c, na_l, x_ctx, x_lat, lat_first_tile, mod, norm_w, w_out,
              router_wt):
    full2 = lambda i: (0, 0)
    tile = lambda i: (i, 0)
    ctx_tile, lat_tile = _stream_tiles(0)
    _, x_lat_tile = _stream_tiles(lat_first_tile)
    return pl.pallas_call(
        _outproj_kernel,
        out_shape=(
            jax.ShapeDtypeStruct((N_TOK, D_MODEL), _F32),
            jax.ShapeDtypeStruct((N_TOK, D_MODEL), _BF),
            jax.ShapeDtypeStruct((N_TOK * _ROW_CHUNKS, _LANE), _F32),
            jax.ShapeDtypeStruct((N_EXPERTS, N_TOK), _F32),
        ),
        grid=(N_TOK // ROW_TILE,),
        in_specs=[
            pl.BlockSpec((ROW_TILE, GLA_VAL), ctx_tile),
            pl.BlockSpec((ROW_TILE, GLA_VAL), lat_tile),
            pl.BlockSpec((ROW_TILE, CONV_CH), ctx_tile),
            pl.BlockSpec((ROW_TILE, CONV_CH), lat_tile),
            pl.BlockSpec((ROW_TILE, NA_DIM), ctx_tile),
            pl.BlockSpec((ROW_TILE, NA_DIM), lat_tile),
            pl.BlockSpec((ROW_TILE, D_MODEL), ctx_tile),
            pl.BlockSpec((ROW_TILE, D_MODEL), x_lat_tile),
            pl.BlockSpec((MOD_ROWS, 6 * D_MODEL), full2),
            pl.BlockSpec((1, D_MODEL), full2),
            pl.BlockSpec((D_MODEL, D_MODEL), full2),
            pl.BlockSpec((N_EXPERTS, D_MODEL), full2),
        ],
        out_specs=(
            pl.BlockSpec((ROW_TILE, D_MODEL), tile),
            pl.BlockSpec((ROW_TILE, D_MODEL), tile),
            pl.BlockSpec((ROW_TILE * _ROW_CHUNKS, _LANE), tile),
            pl.BlockSpec((N_EXPERTS, ROW_TILE), lambda i: (0, i)),
        ),
        compiler_params=_params(("arbitrary",)),
        name="out_proj",
    )(gla_c, gla_l, conv_c, conv_l, na_c, na_l, x_ctx, x_lat, mod, norm_w, w_out, router_wt)


_PER_GROUP = N_EXPERTS // N_GROUPS
_ROUTE_TILE = 1024


def _first_max(x, idx, axes, sentinel):
    m = x
    for ax in axes:
        m = jnp.max(m, axis=ax, keepdims=True)
    first = jnp.where(x == m, idx, sentinel)
    for ax in axes:
        first = jnp.min(first, axis=ax, keepdims=True)
    return m, first


def _router_kernel(lg_ref, rb_ref, eid_ref, wts_ref, cnt_ref):
    t = lg_ref.shape[1]
    shape3 = (N_GROUPS, _PER_GROUP, t)
    scores = _sigmoid(lg_ref[...])
    biased = (scores + rb_ref[...]).reshape(shape3)
    scores = scores.reshape(shape3)
    neg = -jnp.inf
    in_grp = lax.broadcasted_iota(jnp.int32, shape3, 1)
    grp = lax.broadcasted_iota(jnp.int32, (N_GROUPS, 1, t), 0)
    expert = lax.broadcasted_iota(jnp.int32, shape3, 0) * _PER_GROUP + in_grp
    m1, i1 = _first_max(biased, in_grp, (1,), _PER_GROUP)
    m2 = jnp.max(jnp.where(in_grp == i1, neg, biased), axis=1, keepdims=True)
    gscore = m1 + m2
    keep = jnp.zeros((N_GROUPS, 1, t), _F32)
    for _ in range(TOPK_GROUPS):
        _, gi = _first_max(gscore, grp, (0,), N_GROUPS)
        hit = grp == gi
        keep = jnp.where(hit, 1.0, keep)
        gscore = jnp.where(hit, neg, gscore)
    masked = jnp.where(keep > 0.0, biased, neg)
    ids, picked = [], []
    hits = jnp.zeros(shape3, _F32)
    for _ in range(TOP_K):
        _, ei = _first_max(masked, expert, (1, 0), N_EXPERTS)
        hit = expert == ei
        hits = jnp.where(hit, 1.0, hits)
        sc = jnp.sum(jnp.sum(jnp.where(hit, scores, 0.0), axis=1, keepdims=True), axis=0, keepdims=True)
        ids.append(ei.reshape(1, t))
        picked.append(sc.reshape(1, t))
        masked = jnp.where(hit, neg, masked)
    picked = jnp.concatenate(picked, axis=0)
    den = jnp.sum(picked, axis=0, keepdims=True)
    eid_ref[...] = jnp.concatenate(ids, axis=0)
    wts_ref[...] = picked / den * ROUTED_SCALE
    cnt_ref[0] = jnp.sum(hits, axis=2, keepdims=True).reshape(N_EXPERTS, 1)


def _router(logits_t, router_bias):
    tile = lambda i: (0, i)
    return pl.pallas_call(
        _router_kernel,
        out_shape=(jax.ShapeDtypeStruct((TOP_K, N_TOK), jnp.int32), jax.ShapeDtypeStruct((TOP_K, N_TOK), _F32),
                   jax.ShapeDtypeStruct((N_TOK // _ROUTE_TILE, N_EXPERTS, 1), _F32)),
        grid=(N_TOK // _ROUTE_TILE,),
        in_specs=[
            pl.BlockSpec((N_EXPERTS, _ROUTE_TILE), tile),
            pl.BlockSpec((N_EXPERTS, 1), lambda i: (0, 0)),
        ],
        out_specs=(pl.BlockSpec((TOP_K, _ROUTE_TILE), tile), pl.BlockSpec((TOP_K, _ROUTE_TILE), tile),
                   pl.BlockSpec((1, N_EXPERTS, 1), lambda i: (i, 0, 0))),
        compiler_params=_params(("arbitrary",)),
        name="router",
    )(logits_t, router_bias)


_HALF_TOK = N_CTX_TOK
_N_HALVES = N_TOK // _HALF_TOK
_HALF_ASSIGN = _HALF_TOK * TOP_K
_GB = 512
_GB_MAX = _HALF_ASSIGN // _GB + N_EXPERTS
_LANE = 128
_ROW_CHUNKS = D_MODEL // _LANE
_GS = _GB + 1
_STAGE_ROWS = (_ROW_CHUNKS * _GS + 7) // 8 * 8
_RMW_BATCH = 8
_TOK_BITS = 12
assert _HALF_TOK == 1 << _TOK_BITS
_FFN_SPLIT = 1


def _moe_plan(eid, wts, tile_counts):
    tok = jnp.arange(N_TOK, dtype=jnp.int32)
    key = (((tok >> _TOK_BITS) * N_EXPERTS)[None, :] + eid) * _HALF_TOK + (tok & (_HALF_TOK - 1))[None, :]
    key_s, gate_s = lax.sort((key.reshape(-1), wts.reshape(-1)), num_keys=1)
    row_off = (key_s & (_HALF_TOK - 1)) * _ROW_CHUNKS
    count = jnp.sum(tile_counts.reshape(_N_HALVES, -1, N_EXPERTS), axis=1).astype(jnp.int32)
    row_end = jnp.cumsum(count.reshape(-1)).reshape(_N_HALVES, N_EXPERTS)
    row_start = row_end - count
    nblk = (count + _GB - 1) // _GB
    blk_end = jnp.cumsum(nblk, axis=1)
    blk_start = blk_end - nblk
    n_used = blk_end[:, -1]
    b = jnp.minimum(jnp.arange(_GB_MAX, dtype=jnp.int32)[None, :], n_used[:, None] - 1)
    grp = jnp.sum((b[:, :, None] >= blk_end[:, None, :]).astype(jnp.int32), axis=-1)
    onehot = (grp[:, :, None] == jnp.arange(N_EXPERTS, dtype=jnp.int32)).astype(jnp.int32)
    pick = lambda v: jnp.sum(onehot * v[:, None, :], axis=-1)
    within = b - pick(blk_start)
    start = pick(row_start) + within * _GB
    length = jnp.clip(pick(count) - within * _GB, 0, _GB)
    pad = jnp.zeros((_GB,), jnp.int32)
    return (jnp.concatenate([row_off, pad]), jnp.concatenate([gate_s, pad.astype(_F32)]),
            grp.astype(jnp.int32), start.astype(jnp.int32), length.astype(jnp.int32), n_used.astype(jnp.int32))


def _gmm_kernel(be_ref, bstart_ref, blen_ref, nused_ref, off_ref, gate_ref,
                h_ref, wg_ref, wu_ref, wd_ref, acc_ref, xt_scr, yt_scr, wgb_scr, wub_scr, wdb_scr):
    s = pl.program_id(0)
    n_used = nused_ref[0]
    last = _GB_MAX - 1

    @pl.when(s == 0)
    def _():
        acc_ref[...] = jnp.zeros_like(acc_ref)
        xt_scr[...] = jnp.zeros_like(xt_scr)
        yt_scr[...] = jnp.zeros_like(yt_scr)

    fb = jnp.clip(s - 1, 0, last)

    @pl.when((s == 0) | (be_ref[fb] != be_ref[jnp.clip(s - 2, 0, last)]))
    def _():
        wgb_scr[...] = _bf(wg_ref[0, 0])
        wub_scr[...] = _bf(wu_ref[0, 0])
        wdb_scr[...] = _bf(wd_ref[0, 0])

    @pl.when(s < n_used + 2)
    def _():
        slot = s % 2
        other = 1 - slot

        cb = jnp.clip(s - 2, 0, last)
        c_start = bstart_ref[cb]
        c_len = blen_ref[cb]
        for m0 in range(0, _GB, _RMW_BATCH):
            pending = []
            for mi in range(m0, m0 + _RMW_BATCH):
                valid = mi < c_len
                row = jnp.where(valid, off_ref[c_start + mi], _HALF_TOK * _ROW_CHUNKS)
                gate = jnp.where(valid, gate_ref[c_start + mi], 0.0)
                rows = pl.ds(pl.multiple_of(row, _ROW_CHUNKS), _ROW_CHUNKS)
                contrib = gate * yt_scr[slot, pl.ds(mi, _ROW_CHUNKS, stride=_GS), :]
                pending.append((rows, acc_ref[rows, :] + contrib))
            for rows, val in pending:
                acc_ref[rows, :] = val

        part = _GB // _FFN_SPLIT
        for p in range(_FFN_SPLIT):
            x = _bf(jnp.concatenate([xt_scr[other, j * _GS + p * part:j * _GS + (p + 1) * part, :]
                                     for j in range(_ROW_CHUNKS)], axis=-1))
            act = _silu(_dot(x, wgb_scr[...])) * _dot(x, wub_scr[...])
            y = _dot(_bf(act), wdb_scr[...])
            for j in range(_ROW_CHUNKS):
                yt_scr[other, j * _GS + p * part:j * _GS + (p + 1) * part, :] = y[:, j * _LANE:(j + 1) * _LANE]

        g_start = bstart_ref[jnp.minimum(s, last)]
        for mi in range(_GB):
            src = pl.ds(pl.multiple_of(off_ref[g_start + mi], _ROW_CHUNKS), _ROW_CHUNKS)
            xt_scr[slot, pl.ds(mi, _ROW_CHUNKS, stride=_GS), :] = h_ref[src, :]


def _gmm(layer, half, plan, h_rows, wg, wu, wd):
    row_off, gate_s, grp, start, length, n_used = plan
    expert = lambda s, be, *_: (layer, be[jnp.clip(s - 1, 0, _GB_MAX - 1)], 0, 0)
    acc_rows = (_HALF_TOK + 1) * _ROW_CHUNKS
    grid_spec = pltpu.PrefetchScalarGridSpec(
        num_scalar_prefetch=6,
        grid=(_GB_MAX + 2,),
        in_specs=[
            pl.BlockSpec((_HALF_TOK * _ROW_CHUNKS, _LANE), lambda b, *_: (half, 0), pipeline_mode=pl.Buffered(1)),
            pl.BlockSpec((1, 1, D_MODEL, EXPERT_DIM), expert),
            pl.BlockSpec((1, 1, D_MODEL, EXPERT_DIM), expert),
            pl.BlockSpec((1, 1, EXPERT_DIM, D_MODEL), expert),
        ],
        out_specs=pl.BlockSpec((acc_rows, _LANE), lambda b, *_: (0, 0), pipeline_mode=pl.Buffered(1)),
        scratch_shapes=[pltpu.VMEM((2, _STAGE_ROWS, _LANE), _F32)] * 2
        + [pltpu.VMEM((D_MODEL, EXPERT_DIM), _BF)] * 2 + [pltpu.VMEM((EXPERT_DIM, D_MODEL), _BF)],
    )
    acc = pl.pallas_call(
        _gmm_kernel,
        out_shape=jax.ShapeDtypeStruct((acc_rows, _LANE), _F32),
        grid_spec=grid_spec,
        compiler_params=_params(("arbitrary",)),
        name="moe_experts",
    )(grp[half], start[half], length[half], n_used[half:half + 1], row_off, gate_s, h_rows, wg, wu, wd)
    return acc


_FIN_TILE = 512


def _moe_finish_kernel(h_ref, rc_ref, rl_ref, x1_ref, mod_ref, sg_ref, su_ref, sd_ref, o_ref):
    t = pl.program_id(0)
    tiles_per_half = _HALF_TOK // _FIN_TILE
    is_ctx = t < tiles_per_half
    row = jnp.where(is_ctx, 0, 1 + (t - tiles_per_half) // (DEC_SEQ // _FIN_TILE))
    h = h_ref[...]
    shared = _dot(_bf(_silu(_dot(h, _bf(sg_ref[0]))) * _dot(h, _bf(su_ref[0]))), _bf(sd_ref[0]))
    routed = jnp.concatenate(
        [jnp.where(is_ctx, rc_ref[pl.ds(j, _FIN_TILE, stride=_ROW_CHUNKS), :],
                   rl_ref[pl.ds(j, _FIN_TILE, stride=_ROW_CHUNKS), :]) for j in range(_ROW_CHUNKS)], axis=-1)
    g2 = mod_ref[pl.ds(row, 1), 5 * D_MODEL:6 * D_MODEL]
    o_ref[...] = x1_ref[...] + g2 * (routed + shared)


def _moe_finish(layer, h2, routed_c, routed_l, x1, mod, sg, su, sd):
    tile = lambda t: (t, 0)
    tiles_per_half = _HALF_TOK // _FIN_TILE
    ctx_tile = lambda t: (jnp.minimum(t, tiles_per_half - 1), 0)
    lat_tile = lambda t: (jnp.maximum(t - tiles_per_half, 0), 0)
    shared = lambda t: (layer, 0, 0)
    return pl.pallas_call(
        _moe_finish_kernel,
        out_shape=jax.ShapeDtypeStruct((N_TOK, D_MODEL), _F32),
        grid=(N_TOK // _FIN_TILE,),
        in_specs=[
            pl.BlockSpec((_FIN_TILE, D_MODEL), tile),
            pl.BlockSpec((_FIN_TILE * _ROW_CHUNKS, _LANE), ctx_tile),
            pl.BlockSpec((_FIN_TILE * _ROW_CHUNKS, _LANE), lat_tile),
            pl.BlockSpec((_FIN_TILE, D_MODEL), tile),
            pl.BlockSpec((MOD_ROWS, 6 * D_MODEL), lambda t: (0, 0)),
            pl.BlockSpec((1, D_MODEL, SHARED_DIM), shared),
            pl.BlockSpec((1, D_MODEL, SHARED_DIM), shared),
            pl.BlockSpec((1, SHARED_DIM, D_MODEL), shared),
        ],
        out_specs=pl.BlockSpec((_FIN_TILE, D_MODEL), tile),
        compiler_params=_params(("arbitrary",)),
        name="moe_finish",
    )(h2, routed_c, routed_l, x1, mod, sg, su, sd)


def kernel(x_prompt, x_sample, state_gla, cache_na_k, cache_na_v, c, c_ctx, w_ada, b_ada, norm_mix, norm_ffn, w_in, gla_w_gate, gla_b_gate, gla_out_norm, conv_dw, conv_dw_b, conv_ln_g, conv_ln_b, conv_pw, na_q_norm, na_k_norm, na_rpb, w_out, router_w, router_bias, exp_w_gate, exp_w_up, exp_w_down, sh_w_gate, sh_w_up, sh_w_down):
    x_ctx, x_lat, x_lat_tile = x_prompt.reshape(N_CTX_TOK, D_MODEL), x_sample.reshape(N_LAT_TOK, D_MODEL), 0
    cvec = jnp.concatenate([c_ctx[None], c, jnp.zeros((MOD_ROWS - 1 - DEC_BATCH, D_MODEL), _F32)], axis=0)
    mod_all = _ada_mod(cvec, w_ada, b_ada)
    gla_consts = _gla_constants()
    zero_state = jnp.zeros((BATCH, 2, GLA_VAL, GLA_KEY), _F32)
    lat_first = N_CTX_TOK // DEC_SEQ

    states, keys, vals = [], [], []
    for l in range(DEPTH):
        mod = mod_all[l]
        wi = w_in[l]
        w_gla = _bf(jnp.pad(wi[:, :GLA_LR_OFF + 2 * GLA_GATE_RANK], ((0, 0), (0, GLA_IN_W - GLA_LR_OFF - 2 * GLA_GATE_RANK))))
        conv_off = GLA_LR_OFF + 2 * GLA_GATE_RANK
        w_conv = _bf(wi[:, conv_off:conv_off + 2 * CONV_CH])
        w_na = _bf(wi[:, conv_off + 2 * CONV_CH:])
        g_all, c_all, a_all = _in_proj(x_ctx, x_lat, x_lat_tile, norm_mix[l][None], mod, w_gla, w_conv, w_na)

        wz = jnp.zeros((GLA_IN_W - GLA_LR_OFF, 2 * GLA_KEY), _F32)
        wz = wz.at[:GLA_GATE_RANK, :GLA_KEY].set(gla_w_gate[l, 0])
        wz = wz.at[GLA_GATE_RANK:2 * GLA_GATE_RANK, GLA_KEY:].set(gla_w_gate[l, 1])
        bz = gla_b_gate[l].reshape(1, 2 * GLA_KEY)
        onorm = gla_out_norm[l][None]
        gla_c, st_c = _gla(g_all, wz, bz, gla_consts, onorm, zero_state, n=SEQ, n_seq=BATCH, first_block=0)
        gla_l, _ = _gla(g_all, wz, bz, gla_consts, onorm, _state_to_blockdiag(state_gla[:, l]),
                        n=DEC_SEQ, n_seq=DEC_BATCH, first_block=lat_first)

        conv_args = (jnp.broadcast_to(conv_dw[l][:, None, :], (CONV_WIDTH, 8, CONV_CH)), conv_dw_b[l][None],
                     conv_ln_g[l][None], conv_ln_b[l][None], _bf(conv_pw[l]))
        conv_c = _conv(c_all, *conv_args, n=SEQ, n_seq=BATCH, first_block=0)
        conv_l = _conv(c_all, *conv_args, n=DEC_SEQ, n_seq=DEC_BATCH, first_block=lat_first)

        qn, kn = na_q_norm[l][None], na_k_norm[l][None]
        na_c, k_l, v_l = _na_ctx(a_all, qn, kn)
        na_l = _na_lat(a_all, cache_na_k[:, l], cache_na_v[:, l], _na_bias_table(na_rpb[l]), qn, kn)

        x1, h2, h_rows, logits_t = _out_proj(gla_c, gla_l, conv_c, conv_l, na_c, na_l, x_ctx, x_lat, x_lat_tile, mod,
                                     norm_ffn[l][None], _bf(w_out[l]), router_w[l].T)
        plan = _moe_plan(*_router(logits_t, router_bias[l][:, None]))
        routed = [_gmm(l, half, plan, h_rows, exp_w_gate, exp_w_up, exp_w_down) for half in range(_N_HALVES)]
        x = _moe_finish(l, h2, routed[0], routed[1], x1, mod, sh_w_gate, sh_w_up, sh_w_down)
        x_ctx, x_lat, x_lat_tile = x, x, N_CTX_TILES

        states.append(_blockdiag_to_state(st_c))
        keys.append(k_l)
        vals.append(v_l)

    y_prompt = x[:N_CTX_TOK].reshape(BATCH, SEQ, D_MODEL)
    y_sample = x[N_CTX_TOK:].reshape(DEC_BATCH, DEC_SEQ, D_MODEL)
    return (y_prompt, y_sample, jnp.stack(states, axis=1), jnp.stack(keys, axis=1), jnp.stack(vals, axis=1))
```

```python
import functools

import numpy as np
import jax
import jax.numpy as jnp
from jax import lax
from jax.experimental import pallas as pl
from jax.experimental.pallas import tpu as pltpu

D_MODEL = 1024
BATCH = 16
SEQ = 256
DEPTH = 2
DEC_BATCH = 4
DEC_SEQ = 1024
PAST_LEN = 256
GRID_W = 64
GLA_HEADS = 4
GLA_DK = 64
GLA_DV = 128
GLA_KEY = GLA_HEADS * GLA_DK
GLA_VAL = GLA_HEADS * GLA_DV
GLA_GATE_RANK = 16
GLA_GATE_NORM = 16.0
GLA_CHUNK = 64
CONV_CH = 256
CONV_WIDTH = 31
NA_HEADS = 4
NA_HD = 64
NA_DIM = NA_HEADS * NA_HD
NA_WIN_ROWS = 8
NA_WIN_COLS = 16
N_EXPERTS = 64
TOP_K = 8
N_GROUPS = 8
TOPK_GROUPS = 4
EXPERT_DIM = 256
SHARED_DIM = 256
ROUTED_SCALE = 2.5
EPS = 1e-6

N_CTX_TOK = BATCH * SEQ
N_LAT_TOK = DEC_BATCH * DEC_SEQ
N_TOK = N_CTX_TOK + N_LAT_TOK
ROW_TILE = 512
N_CTX_TILES = N_CTX_TOK // ROW_TILE
TILES_PER_LAT_SEQ = DEC_SEQ // ROW_TILE
MOD_ROWS = 8
GLA_IN_W = 1664
GLA_LR_OFF = 2 * GLA_KEY + 2 * GLA_VAL
VMEM_LIMIT = 56 * 1024 * 1024

_BF = jnp.bfloat16
_F32 = jnp.float32


def _bf(x):
    return x.astype(_BF)


def _dot(a, b):
    return jnp.dot(a, b, preferred_element_type=_F32)


def _dot_nt(a, b):
    return lax.dot_general(a, b, (((1,), (1,)), ((), ())), preferred_element_type=_F32)


def _dot_tn(a, b):
    return lax.dot_general(a, b, (((0,), (0,)), ((), ())), preferred_element_type=_F32)


def _split2(x):
    hi = _bf(x)
    lo = _bf(x - hi.astype(_F32))
    return hi, lo


def _split3(x):
    hi = _bf(x)
    r = x - hi.astype(_F32)
    mid = _bf(r)
    lo = _bf(r - mid.astype(_F32))
    return hi, mid, lo


def _dot3(a, b, dot=_dot):
    a_hi, a_lo = _split2(a)
    b_hi, b_lo = _split2(b)
    return (dot(a_lo, b_hi) + dot(a_hi, b_lo)) + dot(a_hi, b_hi)


def _sigmoid(x):
    return 1.0 / (1.0 + jnp.exp(-x))


def _silu(x):
    return x * _sigmoid(x)


def _rms(x, w):
    return x * lax.rsqrt(jnp.mean(x * x, axis=-1, keepdims=True) + EPS) * w


def _params(sem):
    return pltpu.CompilerParams(dimension_semantics=sem, vmem_limit_bytes=VMEM_LIMIT)


def _mod_row(i):
    return jnp.where(i < N_CTX_TILES, 0, 1 + (i - N_CTX_TILES) // TILES_PER_LAT_SEQ)


def _ada_kernel(cv_ref, w_ref, b_ref, o_ref):
    o_ref[0] = _dot3(_silu(cv_ref[...]), w_ref[0]) + b_ref[0]


def _ada_mod(cvec, w_ada, b_ada):
    tn = 1024
    n_out = 6 * D_MODEL
    return pl.pallas_call(
        _ada_kernel,
        out_shape=jax.ShapeDtypeStruct((DEPTH, MOD_ROWS, n_out), _F32),
        grid=(DEPTH, n_out // tn),
        in_specs=[
            pl.BlockSpec((MOD_ROWS, D_MODEL), lambda l, j: (0, 0)),
            pl.BlockSpec((1, D_MODEL, tn), lambda l, j: (l, 0, j)),
            pl.BlockSpec((1, 1, tn), lambda l, j: (l, 0, j)),
        ],
        out_specs=pl.BlockSpec((1, MOD_ROWS, tn), lambda l, j: (l, 0, j)),
        compiler_params=_params(("arbitrary", "arbitrary")),
        name="ada_mod",
    )(cvec, w_ada, b_ada.reshape(DEPTH, 1, n_out))


def _inproj_kernel(xc_ref, xl_ref, nw_ref, mod_ref, wg_ref, wc_ref, wa_ref, g_ref, c_ref, a_ref):
    i = pl.program_id(0)
    row = _mod_row(i)
    sh = mod_ref[pl.ds(row, 1), 0:D_MODEL]
    sc = mod_ref[pl.ds(row, 1), D_MODEL:2 * D_MODEL]
    x = jnp.where(i < N_CTX_TILES, xc_ref[...], xl_ref[...])
    h = _bf(_rms(x, nw_ref[...]) * (1.0 + sc) + sh)
    g_ref[...] = _dot(h, wg_ref[...])
    c_ref[...] = _dot(h, wc_ref[...])
    a_ref[...] = _dot(h, wa_ref[...])


def _stream_tiles(lat_first_tile):
    ctx_tile = lambda i: (jnp.minimum(i, N_CTX_TILES - 1), 0)
    lat_tile = lambda i: (jnp.maximum(i - N_CTX_TILES, 0) + lat_first_tile, 0)
    return ctx_tile, lat_tile


def _in_proj(x_ctx, x_lat, lat_first_tile, norm_w, mod, w_gla, w_conv, w_na):
    full = lambda i: (0, 0)
    tile = lambda i: (i, 0)
    ctx_tile, lat_tile = _stream_tiles(lat_first_tile)
    return pl.pallas_call(
        _inproj_kernel,
        out_shape=(
            jax.ShapeDtypeStruct((N_TOK, GLA_IN_W), _F32),
            jax.ShapeDtypeStruct((N_TOK, 2 * CONV_CH), _F32),
            jax.ShapeDtypeStruct((N_TOK, 3 * NA_DIM), _F32),
        ),
        grid=(N_TOK // ROW_TILE,),
        in_specs=[
            pl.BlockSpec((ROW_TILE, D_MODEL), ctx_tile),
            pl.BlockSpec((ROW_TILE, D_MODEL), lat_tile),
            pl.BlockSpec((1, D_MODEL), full),
            pl.BlockSpec((MOD_ROWS, 6 * D_MODEL), full),
            pl.BlockSpec((D_MODEL, GLA_IN_W), full),
            pl.BlockSpec((D_MODEL, 2 * CONV_CH), full),
            pl.BlockSpec((D_MODEL, 3 * NA_DIM), full),
        ],
        out_specs=(
            pl.BlockSpec((ROW_TILE, GLA_IN_W), tile),
            pl.BlockSpec((ROW_TILE, 2 * CONV_CH), tile),
            pl.BlockSpec((ROW_TILE, 3 * NA_DIM), tile),
        ),
        compiler_params=_params(("arbitrary",)),
        name="in_proj",
    )(x_ctx, x_lat, norm_w, mod, w_gla, w_conv, w_na)


_GLA_LEVELS = (32, 16, 8, 4, 2, 1)
_N_EXP_BLOCKS = 2 + len(_GLA_LEVELS)
_N_MASKS = len(_GLA_LEVELS) + 1


def _gla_constants():
    cs = GLA_CHUNK
    r = np.arange(cs)
    i = r[:, None]
    c = r[None, :]
    w = np.zeros((2, _N_EXP_BLOCKS, cs, cs), np.float32)
    m = np.zeros((2, _N_MASKS, cs, cs), np.float32)
    w[0, 0] = c <= i
    w[0, 1] = c > i
    w[1, 0] = c >= i
    w[1, 1] = c < i
    for lv, half in enumerate(_GLA_LEVELS):
        mid = (r // (2 * half)) * (2 * half) + half
        mi = mid[:, None]
        second = (r >= mid)[:, None]
        same = (r[:, None] // (2 * half)) == (r[None, :] // (2 * half))
        w[0, 2 + lv] = np.where(second, (c >= mi) & (c <= i), (c > i) & (c <= mi - 1))
        w[1, 2 + lv] = np.where(second, (c >= mi) & (c <= i - 1), (c >= i) & (c <= mi - 1))
        m[0, lv] = same & (i >= mi) & (c < mi)
        m[1, lv] = same & (i < mi) & (c >= mi)
    m[:, _N_MASKS - 1] = np.eye(cs)
    w_all = w.reshape(2, _N_EXP_BLOCKS * cs, cs)
    lmask = np.tile(m, (1, 1, 1, GLA_HEADS))
    return jnp.asarray(w_all, _BF), jnp.asarray(lmask, _F32)


def _gla_kernel(g_ref, wz_ref, bz_ref, wall_ref, lmask_ref, onorm_ref, st0_ref,
                o_ref, stfin_ref, la_scr, o_scr, st_scr, *, n):
    cs = GLA_CHUNK
    nc = n // cs
    gate_rows = 128
    key_head = lax.broadcasted_iota(jnp.int32, (1, GLA_KEY), 1) // GLA_DK
    head_lanes = [(key_head == h).astype(_F32) for h in range(GLA_HEADS)]
    head_lanes_bf = [_bf(m) for m in head_lanes]
    zero_v = jnp.zeros((cs, GLA_DV), _BF)

    def log_decays(t, carry):
        rows = pl.ds(pl.multiple_of(t * gate_rows, gate_rows), gate_rows)
        z = _dot3(g_ref[rows, GLA_LR_OFF:GLA_IN_W], wz_ref[...]) + bz_ref[...]
        la_scr[rows, :] = (jnp.minimum(z, 0.0) - jnp.log1p(jnp.exp(-jnp.abs(z)))) * (1.0 / GLA_GATE_NORM)
        return carry

    lax.fori_loop(0, n // gate_rows, log_decays, 0)

    def chunk(c, d):
        rows = pl.ds(pl.multiple_of(c * cs, cs), cs)
        q = g_ref[rows, 0:GLA_KEY] * (GLA_DK ** -0.5)
        k = g_ref[rows, GLA_KEY:2 * GLA_KEY]
        v = g_ref[rows, 2 * GLA_KEY:2 * GLA_KEY + GLA_VAL]
        la_hi, la_mid, la_lo = _split3(la_scr[rows, d * GLA_KEY:(d + 1) * GLA_KEY])
        w = wall_ref[d]
        f = jnp.exp((_dot(w, la_lo) + _dot(w, la_mid)) + _dot(w, la_hi))
        st = st_scr[d]
        o = _dot_nt(_bf(q * f[0:cs]), _bf(st))
        p = jnp.zeros((cs, GLA_HEADS * cs), _F32)
        for lv in range(_N_MASKS):
            if lv < len(_GLA_LEVELS):
                fl = f[(2 + lv) * cs:(3 + lv) * cs]
                ql, kl = q * fl, k * fl
            else:
                ql, kl = q, k
            kl = _bf(kl)
            k_bd = jnp.concatenate([kl * m for m in head_lanes_bf], axis=0)
            p = p + lmask_ref[d, lv] * _dot_nt(_bf(ql), k_bd)
        vb = _bf(v)
        v_bd = jnp.concatenate(
            [jnp.concatenate([vb[:, g * GLA_DV:(g + 1) * GLA_DV] if g == h else zero_v for g in range(GLA_HEADS)], axis=1)
             for h in range(GLA_HEADS)], axis=0)
        o_scr[d, rows, :] = o + _dot(_bf(p), v_bd)
        decay = f[cs - 1:cs] if d == 0 else f[0:1]
        u_t = _dot_tn(vb, _bf(k * f[cs:2 * cs]))
        u_t = jnp.concatenate([u_t[h * GLA_DV:(h + 1) * GLA_DV] * head_lanes[h] for h in range(GLA_HEADS)], axis=0)
        st_scr[d] = st * decay + u_t

    st_scr[...] = st0_ref[0]

    def scan(i, carry):
        chunk(i, 0)
        chunk(nc - 1 - i, 1)
        return carry

    lax.fori_loop(0, nc, scan, 0)
    stfin_ref[0] = st_scr[...]

    def finish(c, carry):
        rows = pl.ds(pl.multiple_of(c * cs, cs), cs)
        o = o_scr[1, rows, :] + o_scr[0, rows, :]
        for h in range(GLA_HEADS):
            cols = slice(h * GLA_DV, (h + 1) * GLA_DV)
            gate = g_ref[rows, 2 * GLA_KEY + GLA_VAL + h * GLA_DV:2 * GLA_KEY + GLA_VAL + (h + 1) * GLA_DV]
            o_ref[rows, cols] = _rms(o[:, cols], onorm_ref[...]) * _silu(gate)
        return carry

    lax.fori_loop(0, nc, finish, 0)


def _gla(g_all, wz, bz, consts, onorm, st0, *, n, n_seq, first_block):
    w_all, lmask = consts
    full2 = lambda i: (0, 0)
    return pl.pallas_call(
        functools.partial(_gla_kernel, n=n),
        out_shape=(
            jax.ShapeDtypeStruct((n_seq * n, GLA_VAL), _F32),
            jax.ShapeDtypeStruct((n_seq, 2, GLA_VAL, GLA_KEY), _F32),
        ),
        grid=(n_seq,),
        in_specs=[
            pl.BlockSpec((n, GLA_IN_W), lambda i: (first_block + i, 0)),
            pl.BlockSpec(wz.shape, full2),
            pl.BlockSpec(bz.shape, full2),
            pl.BlockSpec(w_all.shape, lambda i: (0, 0, 0)),
            pl.BlockSpec(lmask.shape, lambda i: (0, 0, 0, 0)),
            pl.BlockSpec((1, GLA_DV), full2),
            pl.BlockSpec((1, 2, GLA_VAL, GLA_KEY), lambda i: (i, 0, 0, 0)),
        ],
        out_specs=(
            pl.BlockSpec((n, GLA_VAL), lambda i: (i, 0)),
            pl.BlockSpec((1, 2, GLA_VAL, GLA_KEY), lambda i: (i, 0, 0, 0)),
        ),
        scratch_shapes=[pltpu.VMEM((n, 2 * GLA_KEY), _F32), pltpu.VMEM((2, n, GLA_VAL), _F32),
                        pltpu.VMEM((2, GLA_VAL, GLA_KEY), _F32)],
        compiler_params=_params(("arbitrary",)),
        name=f"gla_{n}",
    )(g_all, wz, bz, w_all, lmask, onorm, st0)


def _state_to_blockdiag(s):
    b = s.shape[0]
    st = jnp.swapaxes(s, -1, -2)
    eye = jnp.eye(GLA_HEADS, dtype=s.dtype)
    out = st[:, :, :, :, None, :] * eye[None, None, :, None, :, None]
    return out.reshape(b, 2, GLA_VAL, GLA_KEY)


def _blockdiag_to_state(st):
    b = st.shape[0]
    s6 = st.reshape(b, 2, GLA_HEADS, GLA_DV, GLA_HEADS, GLA_DK)
    diag = jnp.stack([s6[:, :, h, :, h, :] for h in range(GLA_HEADS)], axis=2)
    return jnp.swapaxes(diag, -1, -2)


_CONV_PAD = 16
_CONV_ROWS = 128


def _conv_kernel(c_ref, dw_ref, dwb_ref, lng_ref, lnb_ref, pw_ref, o_ref, pad_scr, *, n):
    zeros = jnp.zeros((_CONV_PAD, CONV_CH), _F32)
    pad_scr[0:_CONV_PAD, :] = zeros
    pad_scr[_CONV_PAD + n:2 * _CONV_PAD + n, :] = zeros
    for r0 in range(0, n, _CONV_ROWS):
        a = c_ref[r0:r0 + _CONV_ROWS, :]
        pad_scr[_CONV_PAD + r0:_CONV_PAD + r0 + _CONV_ROWS, :] = a[:, :CONV_CH] * _sigmoid(a[:, CONV_CH:])
    half = CONV_WIDTH // 2
    for r0 in range(0, n, _CONV_ROWS):
        acc = jnp.zeros((_CONV_ROWS // 8, 8, CONV_CH), _F32)
        for w in range(CONV_WIDTH):
            s = _CONV_PAD + r0 + w - half
            acc = acc + pad_scr[s:s + _CONV_ROWS, :].reshape(_CONV_ROWS // 8, 8, CONV_CH) * dw_ref[w][None]
        acc = acc.reshape(_CONV_ROWS, CONV_CH) + dwb_ref[...]
        xc = acc - jnp.mean(acc, axis=-1, keepdims=True)
        y = xc * lax.rsqrt(jnp.mean(xc * xc, axis=-1, keepdims=True) + EPS) * lng_ref[...] + lnb_ref[...]
        o_ref[r0:r0 + _CONV_ROWS, :] = _dot(_bf(_silu(y)), pw_ref[...])


def _conv(c_all, dw, dwb, lng, lnb, pw, *, n, n_seq, first_block):
    full2 = lambda i: (0, 0)
    return pl.pallas_call(
        functools.partial(_conv_kernel, n=n),
        out_shape=jax.ShapeDtypeStruct((n_seq * n, CONV_CH), _F32),
        grid=(n_seq,),
        in_specs=[
            pl.BlockSpec((n, 2 * CONV_CH), lambda i: (first_block + i, 0)),
            pl.BlockSpec((CONV_WIDTH, 8, CONV_CH), lambda i: (0, 0, 0)),
            pl.BlockSpec((1, CONV_CH), full2),
            pl.BlockSpec((1, CONV_CH), full2),
            pl.BlockSpec((1, CONV_CH), full2),
            pl.BlockSpec((CONV_CH, CONV_CH), full2),
        ],
        out_specs=pl.BlockSpec((n, CONV_CH), lambda i: (i, 0)),
        scratch_shapes=[pltpu.VMEM((n + 2 * _CONV_PAD, CONV_CH), _F32)],
        compiler_params=_params(("arbitrary",)),
        name=f"conv_{n}",
    )(c_all, dw, dwb, lng, lnb, pw)


def _softmax_rows(s):
    e = jnp.exp(s - jnp.max(s, axis=-1, keepdims=True))
    return e / jnp.sum(e, axis=-1, keepdims=True)


def _na_ctx_kernel(a_ref, qn_ref, kn_ref, o_ref, kc_ref, vc_ref):
    for h in range(NA_HEADS):
        cols = slice(h * NA_HD, (h + 1) * NA_HD)
        q = _rms(a_ref[:, h * NA_HD:(h + 1) * NA_HD], qn_ref[...])
        k = _rms(a_ref[:, NA_DIM + h * NA_HD:NA_DIM + (h + 1) * NA_HD], kn_ref[...])
        v = a_ref[:, 2 * NA_DIM + h * NA_HD:2 * NA_DIM + (h + 1) * NA_HD]
        kc_ref[0, h] = k
        vc_ref[0, h] = v
        p = _softmax_rows(_dot_nt(_bf(q), _bf(k)) * (NA_HD ** -0.5))
        o_ref[:, cols] = _dot(_bf(p), _bf(v))


def _na_ctx(a_all, qn, kn):
    full2 = lambda i: (0, 0)
    cache = jax.ShapeDtypeStruct((BATCH, NA_HEADS, SEQ, NA_HD), _F32)
    cache_spec = pl.BlockSpec((1, NA_HEADS, SEQ, NA_HD), lambda i: (i, 0, 0, 0))
    return pl.pallas_call(
        _na_ctx_kernel,
        out_shape=(jax.ShapeDtypeStruct((N_CTX_TOK, NA_DIM), _F32), cache, cache),
        grid=(BATCH,),
        in_specs=[
            pl.BlockSpec((SEQ, 3 * NA_DIM), lambda i: (i, 0)),
            pl.BlockSpec((1, NA_HD), full2),
            pl.BlockSpec((1, NA_HD), full2),
        ],
        out_specs=(pl.BlockSpec((SEQ, NA_DIM), lambda i: (i, 0)), cache_spec, cache_spec),
        compiler_params=_params(("arbitrary",)),
        name="na_ctx",
    )(a_all, qn, kn)


_NA_ROWS = DEC_SEQ // GRID_W
_NA_KEYS = NA_WIN_ROWS * GRID_W
_NA_VARIANTS = NA_WIN_ROWS


_RPB_ROWS = 2 * NA_WIN_ROWS - 1
_RPB_COLS = 2 * NA_WIN_COLS - 1


def _na_bias_constants():
    qc = np.arange(GRID_W)[:, None]
    kc = np.arange(GRID_W)[None, :]
    shift = np.stack([(kc - qc + NA_WIN_COLS - 1) == co for co in range(_RPB_COLS)]).astype(np.float32)
    win_start = np.clip(qc - NA_WIN_COLS // 2, 0, GRID_W - NA_WIN_COLS)
    in_win = ((kc >= win_start) & (kc < win_start + NA_WIN_COLS)).astype(np.float32)
    return jnp.asarray(shift), jnp.asarray(in_win)


def _na_bias_kernel(rpb_ref, shift_ref, win_ref, o_ref):
    h = pl.program_id(0)
    in_win = win_ref[...] > 0.0
    tiles = []
    for ro in range(_RPB_ROWS):
        base = (h * _RPB_ROWS + ro) * _RPB_COLS
        acc = rpb_ref[base] * shift_ref[0]
        for co in range(1, _RPB_COLS):
            acc = acc + rpb_ref[base + co] * shift_ref[co]
        tiles.append(jnp.where(in_win, acc, -jnp.inf))
    for t in range(_NA_VARIANTS):
        o_ref[0, t] = jnp.concatenate([tiles[kr - t + NA_WIN_ROWS - 1] for kr in range(NA_WIN_ROWS)], axis=1)


def _na_bias_table(rpb):
    shift, in_win = _na_bias_constants()
    return pl.pallas_call(
        _na_bias_kernel,
        out_shape=jax.ShapeDtypeStruct((NA_HEADS, _NA_VARIANTS, GRID_W, _NA_KEYS), _F32),
        grid=(NA_HEADS,),
        in_specs=[
            pl.BlockSpec(memory_space=pltpu.SMEM),
            pl.BlockSpec(shift.shape, lambda h: (0, 0, 0)),
            pl.BlockSpec(in_win.shape, lambda h: (0, 0)),
        ],
        out_specs=pl.BlockSpec((1, _NA_VARIANTS, GRID_W, _NA_KEYS), lambda h: (h, 0, 0, 0)),
        compiler_params=_params(("arbitrary",)),
        name="na_bias",
    )(rpb.reshape(-1), shift, in_win)


def _na_row_groups():
    start = lambda r: min(max(r - NA_WIN_ROWS // 2, 0), _NA_ROWS - NA_WIN_ROWS)
    groups, r = [], 0
    while r < _NA_ROWS:
        n = 1
        while r + n < _NA_ROWS and start(r + n) == start(r):
            n += 1
        groups.append((r, n, start(r)))
        r += n
    return groups


def _na_lat_kernel(a_ref, kctx_ref, vctx_ref, bias_ref, qn_ref, kn_ref, o_ref,
                   q_scr, k_scr, v_scr, sctx_scr, pctx_scr):
    scale = NA_HD ** -0.5
    for h in range(NA_HEADS):
        cols = slice(h * NA_HD, (h + 1) * NA_HD)
        q_scr[...] = _bf(_rms(a_ref[:, h * NA_HD:(h + 1) * NA_HD], qn_ref[...]))
        k_scr[...] = _bf(_rms(a_ref[:, NA_DIM + h * NA_HD:NA_DIM + (h + 1) * NA_HD], kn_ref[...]))
        v_scr[...] = _bf(a_ref[:, 2 * NA_DIM + h * NA_HD:2 * NA_DIM + (h + 1) * NA_HD])
        sctx_scr[...] = _dot_nt(q_scr[...], _bf(kctx_ref[0, h])) * scale
        for r0, nr, ks in _na_row_groups():
            qrows = slice(r0 * GRID_W, (r0 + nr) * GRID_W)
            krows = slice(ks * GRID_W, ks * GRID_W + _NA_KEYS)
            bias = jnp.concatenate([bias_ref[h, r - ks] for r in range(r0, r0 + nr)], axis=0)
            s_loc = _dot_nt(q_scr[qrows, :], k_scr[krows, :]) * scale + bias
            s_ctx = sctx_scr[qrows, :]
            m = jnp.maximum(jnp.max(s_loc, axis=-1, keepdims=True), jnp.max(s_ctx, axis=-1, keepdims=True))
            e_loc = jnp.exp(s_loc - m)
            e_ctx = jnp.exp(s_ctx - m)
            inv = 1.0 / (jnp.sum(e_loc, axis=-1, keepdims=True) + jnp.sum(e_ctx, axis=-1, keepdims=True))
            pctx_scr[qrows, :] = _bf(e_ctx * inv)
            o_ref[qrows, cols] = _dot(_bf(e_loc * inv), v_scr[krows, :])
        o_ref[:, cols] = o_ref[:, cols] + _dot(pctx_scr[...], _bf(vctx_ref[0, h]))


def _na_lat(a_all, k_ctx, v_ctx, bias, qn, kn):
    full2 = lambda i: (0, 0)
    ctx_spec = pl.BlockSpec((1, NA_HEADS, PAST_LEN, NA_HD), lambda i: (i, 0, 0, 0))
    return pl.pallas_call(
        _na_lat_kernel,
        out_shape=jax.ShapeDtypeStruct((N_LAT_TOK, NA_DIM), _F32),
        grid=(DEC_BATCH,),
        in_specs=[
            pl.BlockSpec((DEC_SEQ, 3 * NA_DIM), lambda i: (N_CTX_TOK // DEC_SEQ + i, 0)),
            ctx_spec,
            ctx_spec,
            pl.BlockSpec(bias.shape, lambda i: (0, 0, 0, 0)),
            pl.BlockSpec((1, NA_HD), full2),
            pl.BlockSpec((1, NA_HD), full2),
        ],
        out_specs=pl.BlockSpec((DEC_SEQ, NA_DIM), lambda i: (i, 0)),
        scratch_shapes=[pltpu.VMEM((DEC_SEQ, NA_HD), _BF)] * 3
        + [pltpu.VMEM((DEC_SEQ, PAST_LEN), _F32), pltpu.VMEM((DEC_SEQ, PAST_LEN), _BF)],
        compiler_params=_params(("arbitrary",)),
        name="na_lat",
    )(a_all, k_ctx, v_ctx, bias, qn, kn)


def _outproj_kernel(gc_ref, gl_ref, cc_ref, cl_ref, nc_ref, nl_ref, xc_ref, xl_ref, mod_ref, nw_ref, wo_ref,
                    rwt_ref, x1_ref, h2_ref, hr_ref, lg_ref):
    i = pl.program_id(0)
    is_ctx = i < N_CTX_TILES
    row = _mod_row(i)
    gla = jnp.where(is_ctx, gc_ref[...], gl_ref[...])
    conv = jnp.where(is_ctx, cc_ref[...], cl_ref[...])
    na = jnp.where(is_ctx, nc_ref[...], nl_ref[...])
    mix = (_dot(_bf(gla), wo_ref[0:GLA_VAL, :])
           + _dot(_bf(conv), wo_ref[GLA_VAL:GLA_VAL + CONV_CH, :])
           + _dot(_bf(na), wo_ref[GLA_VAL + CONV_CH:, :]))
    g1 = mod_ref[pl.ds(row, 1), 2 * D_MODEL:3 * D_MODEL]
    sh2 = mod_ref[pl.ds(row, 1), 3 * D_MODEL:4 * D_MODEL]
    sc2 = mod_ref[pl.ds(row, 1), 4 * D_MODEL:5 * D_MODEL]
    x1 = jnp.where(is_ctx, xc_ref[...], xl_ref[...]) + g1 * mix
    h2 = _rms(x1, nw_ref[...]) * (1.0 + sc2) + sh2
    x1_ref[...] = x1
    h2_ref[...] = _bf(h2)
    for j in range(_ROW_CHUNKS):
        hr_ref[pl.ds(j, ROW_TILE, stride=_ROW_CHUNKS), :] = h2[:, j * _LANE:(j + 1) * _LANE]
    lg_ref[...] = _dot3(rwt_ref[...], h2, dot=_dot_nt)


def _out_proj(gla_c, gla_l, conv_c, conv_l, na_c, na_l, x_ctx, x_lat, lat_first_tile, mod, norm_w, w_out,
              router_wt):
    full2 = lambda i: (0, 0)
    tile = lambda i: (i, 0)
    ctx_tile, lat_tile = _stream_tiles(0)
    _, x_lat_tile = _stream_tiles(lat_first_tile)
    return pl.pallas_call(
        _outproj_kernel,
        out_shape=(
            jax.ShapeDtypeStruct((N_TOK, D_MODEL), _F32),
            jax.ShapeDtypeStruct((N_TOK, D_MODEL), _BF),
            jax.ShapeDtypeStruct((N_TOK * _ROW_CHUNKS, _LANE), _F32),
            jax.ShapeDtypeStruct((N_EXPERTS, N_TOK), _F32),
        ),
        grid=(N_TOK // ROW_TILE,),
        in_specs=[
            pl.BlockSpec((ROW_TILE, GLA_VAL), ctx_tile),
            pl.BlockSpec((ROW_TILE, GLA_VAL), lat_tile),
            pl.BlockSpec((ROW_TILE, CONV_CH), ctx_tile),
            pl.BlockSpec((ROW_TILE, CONV_CH), lat_tile),
            pl.BlockSpec((ROW_TILE, NA_DIM), ctx_tile),
            pl.BlockSpec((ROW_TILE, NA_DIM), lat_tile),
            pl.BlockSpec((ROW_TILE, D_MODEL), ctx_tile),
            pl.BlockSpec((ROW_TILE, D_MODEL), x_lat_tile),
            pl.BlockSpec((MOD_ROWS, 6 * D_MODEL), full2),
            pl.BlockSpec((1, D_MODEL), full2),
            pl.BlockSpec((D_MODEL, D_MODEL), full2),
            pl.BlockSpec((N_EXPERTS, D_MODEL), full2),
        ],
        out_specs=(
            pl.BlockSpec((ROW_TILE, D_MODEL), tile),
            pl.BlockSpec((ROW_TILE, D_MODEL), tile),
            pl.BlockSpec((ROW_TILE * _ROW_CHUNKS, _LANE), tile),
            pl.BlockSpec((N_EXPERTS, ROW_TILE), lambda i: (0, i)),
        ),
        compiler_params=_params(("arbitrary",)),
        name="out_proj",
    )(gla_c, gla_l, conv_c, conv_l, na_c, na_l, x_ctx, x_lat, mod, norm_w, w_out, router_wt)


_PER_GROUP = N_EXPERTS // N_GROUPS
_ROUTE_TILE = 1024


def _first_max(x, idx, axes, sentinel):
    m = x
    for ax in axes:
        m = jnp.max(m, axis=ax, keepdims=True)
    first = jnp.where(x == m, idx, sentinel)
    for ax in axes:
        first = jnp.min(first, axis=ax, keepdims=True)
    return m, first


def _router_kernel(lg_ref, rb_ref, eid_ref, wts_ref, cnt_ref):
    t = lg_ref.shape[1]
    shape3 = (N_GROUPS, _PER_GROUP, t)
    scores = _sigmoid(lg_ref[...])
    biased = (scores + rb_ref[...]).reshape(shape3)
    scores = scores.reshape(shape3)
    neg = -jnp.inf
    in_grp = lax.broadcasted_iota(jnp.int32, shape3, 1)
    grp = lax.broadcasted_iota(jnp.int32, (N_GROUPS, 1, t), 0)
    expert = lax.broadcasted_iota(jnp.int32, shape3, 0) * _PER_GROUP + in_grp
    m1, i1 = _first_max(biased, in_grp, (1,), _PER_GROUP)
    m2 = jnp.max(jnp.where(in_grp == i1, neg, biased), axis=1, keepdims=True)
    gscore = m1 + m2
    keep = jnp.zeros((N_GROUPS, 1, t), _F32)
    for _ in range(TOPK_GROUPS):
        _, gi = _first_max(gscore, grp, (0,), N_GROUPS)
        hit = grp == gi
        keep = jnp.where(hit, 1.0, keep)
        gscore = jnp.where(hit, neg, gscore)
    masked = jnp.where(keep > 0.0, biased, neg)
    ids, picked = [], []
    hits = jnp.zeros(shape3, _F32)
    for _ in range(TOP_K):
        _, ei = _first_max(masked, expert, (1, 0), N_EXPERTS)
        hit = expert == ei
        hits = jnp.where(hit, 1.0, hits)
        sc = jnp.sum(jnp.sum(jnp.where(hit, scores, 0.0), axis=1, keepdims=True), axis=0, keepdims=True)
        ids.append(ei.reshape(1, t))
        picked.append(sc.reshape(1, t))
        masked = jnp.where(hit, neg, masked)
    picked = jnp.concatenate(picked, axis=0)
    den = jnp.sum(picked, axis=0, keepdims=True)
    eid_ref[...] = jnp.concatenate(ids, axis=0)
    wts_ref[...] = picked / den * ROUTED_SCALE
    cnt_ref[0] = jnp.sum(hits, axis=2, keepdims=True).reshape(N_EXPERTS, 1)


def _router(logits_t, router_bias):
    tile = lambda i: (0, i)
    return pl.pallas_call(
        _router_kernel,
        out_shape=(jax.ShapeDtypeStruct((TOP_K, N_TOK), jnp.int32), jax.ShapeDtypeStruct((TOP_K, N_TOK), _F32),
                   jax.ShapeDtypeStruct((N_TOK // _ROUTE_TILE, N_EXPERTS, 1), _F32)),
        grid=(N_TOK // _ROUTE_TILE,),
        in_specs=[
            pl.BlockSpec((N_EXPERTS, _ROUTE_TILE), tile),
            pl.BlockSpec((N_EXPERTS, 1), lambda i: (0, 0)),
        ],
        out_specs=(pl.BlockSpec((TOP_K, _ROUTE_TILE), tile), pl.BlockSpec((TOP_K, _ROUTE_TILE), tile),
                   pl.BlockSpec((1, N_EXPERTS, 1), lambda i: (i, 0, 0))),
        compiler_params=_params(("arbitrary",)),
        name="router",
    )(logits_t, router_bias)


_HALF_TOK = N_CTX_TOK
_N_HALVES = N_TOK // _HALF_TOK
_HALF_ASSIGN = _HALF_TOK * TOP_K
_GB = 576
_GB_MAX = _HALF_ASSIGN // _GB + N_EXPERTS
_LANE = 128
_ROW_CHUNKS = D_MODEL // _LANE
_GS = _GB + 1
_STAGE_ROWS = (_ROW_CHUNKS * _GS + 7) // 8 * 8
_RMW_BATCH = 8
_TOK_BITS = 12
assert _HALF_TOK == 1 << _TOK_BITS
_FFN_SPLIT = 1


def _moe_plan(eid, wts, tile_counts):
    tok = jnp.arange(N_TOK, dtype=jnp.int32)
    key = (((tok >> _TOK_BITS) * N_EXPERTS)[None, :] + eid) * _HALF_TOK + (tok & (_HALF_TOK - 1))[None, :]
    key_s, gate_s = lax.sort((key.reshape(-1), wts.reshape(-1)), num_keys=1)
    row_off = (key_s & (_HALF_TOK - 1)) * _ROW_CHUNKS
    count = jnp.sum(tile_counts.reshape(_N_HALVES, -1, N_EXPERTS), axis=1).astype(jnp.int32)
    row_end = jnp.cumsum(count.reshape(-1)).reshape(_N_HALVES, N_EXPERTS)
    row_start = row_end - count
    nblk = (count + _GB - 1) // _GB
    blk_end = jnp.cumsum(nblk, axis=1)
    blk_start = blk_end - nblk
    n_used = blk_end[:, -1]
    b = jnp.minimum(jnp.arange(_GB_MAX, dtype=jnp.int32)[None, :], n_used[:, None] - 1)
    grp = jnp.sum((b[:, :, None] >= blk_end[:, None, :]).astype(jnp.int32), axis=-1)
    onehot = (grp[:, :, None] == jnp.arange(N_EXPERTS, dtype=jnp.int32)).astype(jnp.int32)
    pick = lambda v: jnp.sum(onehot * v[:, None, :], axis=-1)
    within = b - pick(blk_start)
    start = pick(row_start) + within * _GB
    length = jnp.clip(pick(count) - within * _GB, 0, _GB)
    pad = jnp.zeros((_GB,), jnp.int32)
    return (jnp.concatenate([row_off, pad]), jnp.concatenate([gate_s, pad.astype(_F32)]),
            grp.astype(jnp.int32), start.astype(jnp.int32), length.astype(jnp.int32), n_used.astype(jnp.int32))


def _gmm_kernel(be_ref, bstart_ref, blen_ref, nused_ref, off_ref, gate_ref,
                h_ref, wg_ref, wu_ref, wd_ref, acc_ref, xt_scr, yt_scr, wgb_scr, wub_scr, wdb_scr):
    s = pl.program_id(0)
    n_used = nused_ref[0]
    last = _GB_MAX - 1

    @pl.when(s == 0)
    def _():
        acc_ref[...] = jnp.zeros_like(acc_ref)
        xt_scr[...] = jnp.zeros_like(xt_scr)
        yt_scr[...] = jnp.zeros_like(yt_scr)

    fb = jnp.clip(s - 1, 0, last)

    @pl.when((s == 0) | (be_ref[fb] != be_ref[jnp.clip(s - 2, 0, last)]))
    def _():
        wgb_scr[...] = _bf(wg_ref[0, 0])
        wub_scr[...] = _bf(wu_ref[0, 0])
        wdb_scr[...] = _bf(wd_ref[0, 0])

    @pl.when(s < n_used + 2)
    def _():
        slot = s % 2
        other = 1 - slot

        cb = jnp.clip(s - 2, 0, last)
        c_start = bstart_ref[cb]
        c_len = blen_ref[cb]
        for m0 in range(0, _GB, _RMW_BATCH):
            pending = []
            for mi in range(m0, m0 + _RMW_BATCH):
                valid = mi < c_len
                row = jnp.where(valid, off_ref[c_start + mi], _HALF_TOK * _ROW_CHUNKS)
                gate = jnp.where(valid, gate_ref[c_start + mi], 0.0)
                rows = pl.ds(pl.multiple_of(row, _ROW_CHUNKS), _ROW_CHUNKS)
                contrib = gate * yt_scr[slot, pl.ds(mi, _ROW_CHUNKS, stride=_GS), :]
                pending.append((rows, acc_ref[rows, :] + contrib))
            for rows, val in pending:
                acc_ref[rows, :] = val

        part = _GB // _FFN_SPLIT
        for p in range(_FFN_SPLIT):
            x = _bf(jnp.concatenate([xt_scr[other, j * _GS + p * part:j * _GS + (p + 1) * part, :]
                                     for j in range(_ROW_CHUNKS)], axis=-1))
            act = _silu(_dot(x, wgb_scr[...])) * _dot(x, wub_scr[...])
            y = _dot(_bf(act), wdb_scr[...])
            for j in range(_ROW_CHUNKS):
                yt_scr[other, j * _GS + p * part:j * _GS + (p + 1) * part, :] = y[:, j * _LANE:(j + 1) * _LANE]

        g_start = bstart_ref[jnp.minimum(s, last)]
        for mi in range(_GB):
            src = pl.ds(pl.multiple_of(off_ref[g_start + mi], _ROW_CHUNKS), _ROW_CHUNKS)
            xt_scr[slot, pl.ds(mi, _ROW_CHUNKS, stride=_GS), :] = h_ref[src, :]


def _gmm(layer, half, plan, h_rows, wg, wu, wd):
    row_off, gate_s, grp, start, length, n_used = plan
    expert = lambda s, be, *_: (layer, be[jnp.clip(s - 1, 0, _GB_MAX - 1)], 0, 0)
    acc_rows = (_HALF_TOK + 1) * _ROW_CHUNKS
    grid_spec = pltpu.PrefetchScalarGridSpec(
        num_scalar_prefetch=6,
        grid=(_GB_MAX + 2,),
        in_specs=[
            pl.BlockSpec((_HALF_TOK * _ROW_CHUNKS, _LANE), lambda b, *_: (half, 0), pipeline_mode=pl.Buffered(1)),
            pl.BlockSpec((1, 1, D_MODEL, EXPERT_DIM), expert),
            pl.BlockSpec((1, 1, D_MODEL, EXPERT_DIM), expert),
            pl.BlockSpec((1, 1, EXPERT_DIM, D_MODEL), expert),
        ],
        out_specs=pl.BlockSpec((acc_rows, _LANE), lambda b, *_: (0, 0), pipeline_mode=pl.Buffered(1)),
        scratch_shapes=[pltpu.VMEM((2, _STAGE_ROWS, _LANE), _F32)] * 2
        + [pltpu.VMEM((D_MODEL, EXPERT_DIM), _BF)] * 2 + [pltpu.VMEM((EXPERT_DIM, D_MODEL), _BF)],
    )
    acc = pl.pallas_call(
        _gmm_kernel,
        out_shape=jax.ShapeDtypeStruct((acc_rows, _LANE), _F32),
        grid_spec=grid_spec,
        compiler_params=_params(("arbitrary",)),
        name="moe_experts",
    )(grp[half], start[half], length[half], n_used[half:half + 1], row_off, gate_s, h_rows, wg, wu, wd)
    return acc


_FIN_TILE = 512


def _moe_finish_kernel(h_ref, rc_ref, rl_ref, x1_ref, mod_ref, sg_ref, su_ref, sd_ref, o_ref):
    t = pl.program_id(0)
    tiles_per_half = _HALF_TOK // _FIN_TILE
    is_ctx = t < tiles_per_half
    row = jnp.where(is_ctx, 0, 1 + (t - tiles_per_half) // (DEC_SEQ // _FIN_TILE))
    h = h_ref[...]
    shared = _dot(_bf(_silu(_dot(h, _bf(sg_ref[0]))) * _dot(h, _bf(su_ref[0]))), _bf(sd_ref[0]))
    routed = jnp.concatenate(
        [jnp.where(is_ctx, rc_ref[pl.ds(j, _FIN_TILE, stride=_ROW_CHUNKS), :],
                   rl_ref[pl.ds(j, _FIN_TILE, stride=_ROW_CHUNKS), :]) for j in range(_ROW_CHUNKS)], axis=-1)
    g2 = mod_ref[pl.ds(row, 1), 5 * D_MODEL:6 * D_MODEL]
    o_ref[...] = x1_ref[...] + g2 * (routed + shared)


def _moe_finish(layer, h2, routed_c, routed_l, x1, mod, sg, su, sd):
    tile = lambda t: (t, 0)
    tiles_per_half = _HALF_TOK // _FIN_TILE
    ctx_tile = lambda t: (jnp.minimum(t, tiles_per_half - 1), 0)
    lat_tile = lambda t: (jnp.maximum(t - tiles_per_half, 0), 0)
    shared = lambda t: (layer, 0, 0)
    return pl.pallas_call(
        _moe_finish_kernel,
        out_shape=jax.ShapeDtypeStruct((N_TOK, D_MODEL), _F32),
        grid=(N_TOK // _FIN_TILE,),
        in_specs=[
            pl.BlockSpec((_FIN_TILE, D_MODEL), tile),
            pl.BlockSpec((_FIN_TILE * _ROW_CHUNKS, _LANE), ctx_tile),
            pl.BlockSpec((_FIN_TILE * _ROW_CHUNKS, _LANE), lat_tile),
            pl.BlockSpec((_FIN_TILE, D_MODEL), tile),
            pl.BlockSpec((MOD_ROWS, 6 * D_MODEL), lambda t: (0, 0)),
            pl.BlockSpec((1, D_MODEL, SHARED_DIM), shared),
            pl.BlockSpec((1, D_MODEL, SHARED_DIM), shared),
            pl.BlockSpec((1, SHARED_DIM, D_MODEL), shared),
        ],
        out_specs=pl.BlockSpec((_FIN_TILE, D_MODEL), tile),
        compiler_params=_params(("arbitrary",)),
        name="moe_finish",
    )(h2, routed_c, routed_l, x1, mod, sg, su, sd)


def kernel(x_prompt, x_sample, state_gla, cache_na_k, cache_na_v, c, c_ctx, w_ada, b_ada, norm_mix, norm_ffn, w_in, gla_w_gate, gla_b_gate, gla_out_norm, conv_dw, conv_dw_b, conv_ln_g, conv_ln_b, conv_pw, na_q_norm, na_k_norm, na_rpb, w_out, router_w, router_bias, exp_w_gate, exp_w_up, exp_w_down, sh_w_gate, sh_w_up, sh_w_down):
    x_ctx, x_lat, x_lat_tile = x_prompt.reshape(N_CTX_TOK, D_MODEL), x_sample.reshape(N_LAT_TOK, D_MODEL), 0
    cvec = jnp.concatenate([c_ctx[None], c, jnp.zeros((MOD_ROWS - 1 - DEC_BATCH, D_MODEL), _F32)], axis=0)
    mod_all = _ada_mod(cvec, w_ada, b_ada)
    gla_consts = _gla_constants()
    zero_state = jnp.zeros((BATCH, 2, GLA_VAL, GLA_KEY), _F32)
    lat_first = N_CTX_TOK // DEC_SEQ

    states, keys, vals = [], [], []
    for l in range(DEPTH):
        mod = mod_all[l]
        wi = w_in[l]
        w_gla = _bf(jnp.pad(wi[:, :GLA_LR_OFF + 2 * GLA_GATE_RANK], ((0, 0), (0, GLA_IN_W - GLA_LR_OFF - 2 * GLA_GATE_RANK))))
        conv_off = GLA_LR_OFF + 2 * GLA_GATE_RANK
        w_conv = _bf(wi[:, conv_off:conv_off + 2 * CONV_CH])
        w_na = _bf(wi[:, conv_off + 2 * CONV_CH:])
        g_all, c_all, a_all = _in_proj(x_ctx, x_lat, x_lat_tile, norm_mix[l][None], mod, w_gla, w_conv, w_na)

        wz = jnp.zeros((GLA_IN_W - GLA_LR_OFF, 2 * GLA_KEY), _F32)
        wz = wz.at[:GLA_GATE_RANK, :GLA_KEY].set(gla_w_gate[l, 0])
        wz = wz.at[GLA_GATE_RANK:2 * GLA_GATE_RANK, GLA_KEY:].set(gla_w_gate[l, 1])
        bz = gla_b_gate[l].reshape(1, 2 * GLA_KEY)
        onorm = gla_out_norm[l][None]
        gla_c, st_c = _gla(g_all, wz, bz, gla_consts, onorm, zero_state, n=SEQ, n_seq=BATCH, first_block=0)
        gla_l, _ = _gla(g_all, wz, bz, gla_consts, onorm, _state_to_blockdiag(state_gla[:, l]),
                        n=DEC_SEQ, n_seq=DEC_BATCH, first_block=lat_first)

        conv_args = (jnp.broadcast_to(conv_dw[l][:, None, :], (CONV_WIDTH, 8, CONV_CH)), conv_dw_b[l][None],
                     conv_ln_g[l][None], conv_ln_b[l][None], _bf(conv_pw[l]))
        conv_c = _conv(c_all, *conv_args, n=SEQ, n_seq=BATCH, first_block=0)
        conv_l = _conv(c_all, *conv_args, n=DEC_SEQ, n_seq=DEC_BATCH, first_block=lat_first)

        qn, kn = na_q_norm[l][None], na_k_norm[l][None]
        na_c, k_l, v_l = _na_ctx(a_all, qn, kn)
        na_l = _na_lat(a_all, cache_na_k[:, l], cache_na_v[:, l], _na_bias_table(na_rpb[l]), qn, kn)

        x1, h2, h_rows, logits_t = _out_proj(gla_c, gla_l, conv_c, conv_l, na_c, na_l, x_ctx, x_lat, x_lat_tile, mod,
                                     norm_ffn[l][None], _bf(w_out[l]), router_w[l].T)
        plan = _moe_plan(*_router(logits_t, router_bias[l][:, None]))
        routed = [_gmm(l, half, plan, h_rows, exp_w_gate, exp_w_up, exp_w_down) for half in range(_N_HALVES)]
        x = _moe_finish(l, h2, routed[0], routed[1], x1, mod, sh_w_gate, sh_w_up, sh_w_down)
        x_ctx, x_lat, x_lat_tile = x, x, N_CTX_TILES

        states.append(_blockdiag_to_state(st_c))
        keys.append(k_l)
        vals.append(v_l)

    y_prompt = x[:N_CTX_TOK].reshape(BATCH, SEQ, D_MODEL)
    y_sample = x[N_CTX_TOK:].reshape(DEC_BATCH, DEC_SEQ, D_MODEL)
    return (y_prompt, y_sample, jnp.stack(states, axis=1), jnp.stack(keys, axis=1), jnp.stack(vals, axis=1))
```

```python
import functools

import numpy as np
import jax
import jax.numpy as jnp
from jax import lax
from jax.experimental import pallas as pl
from jax.experimental.pallas import tpu as pltpu

D_MODEL = 1024
BATCH = 16
SEQ = 256
DEPTH = 2
DEC_BATCH = 4
DEC_SEQ = 1024
PAST_LEN = 256
GRID_W = 64
GLA_HEADS = 4
GLA_DK = 64
GLA_DV = 128
GLA_KEY = GLA_HEADS * GLA_DK
GLA_VAL = GLA_HEADS * GLA_DV
GLA_GATE_RANK = 16
GLA_GATE_NORM = 16.0
GLA_CHUNK = 64
CONV_CH = 256
CONV_WIDTH = 31
NA_HEADS = 4
NA_HD = 64
NA_DIM = NA_HEADS * NA_HD
NA_WIN_ROWS = 8
NA_WIN_COLS = 16
N_EXPERTS = 64
TOP_K = 8
N_GROUPS = 8
TOPK_GROUPS = 4
EXPERT_DIM = 256
SHARED_DIM = 256
ROUTED_SCALE = 2.5
EPS = 1e-6

N_CTX_TOK = BATCH * SEQ
N_LAT_TOK = DEC_BATCH * DEC_SEQ
N_TOK = N_CTX_TOK + N_LAT_TOK
ROW_TILE = 512
N_CTX_TILES = N_CTX_TOK // ROW_TILE
TILES_PER_LAT_SEQ = DEC_SEQ // ROW_TILE
MOD_ROWS = 8
GLA_IN_W = 1664
GLA_LR_OFF = 2 * GLA_KEY + 2 * GLA_VAL
VMEM_LIMIT = 56 * 1024 * 1024

_BF = jnp.bfloat16
_F32 = jnp.float32


def _bf(x):
    return x.astype(_BF)


def _dot(a, b):
    return jnp.dot(a, b, preferred_element_type=_F32)


def _dot_nt(a, b):
    return lax.dot_general(a, b, (((1,), (1,)), ((), ())), preferred_element_type=_F32)


def _dot_tn(a, b):
    return lax.dot_general(a, b, (((0,), (0,)), ((), ())), preferred_element_type=_F32)


def _split2(x):
    hi = _bf(x)
    lo = _bf(x - hi.astype(_F32))
    return hi, lo


def _dot3(a, b, dot=_dot):
    a_hi, a_lo = _split2(a)
    b_hi, b_lo = _split2(b)
    return (dot(a_lo, b_hi) + dot(a_hi, b_lo)) + dot(a_hi, b_hi)


def _sigmoid(x):
    return 1.0 / (1.0 + jnp.exp(-x))


def _silu(x):
    return x * _sigmoid(x)


def _rms(x, w):
    return x * lax.rsqrt(jnp.mean(x * x, axis=-1, keepdims=True) + EPS) * w


def _params(sem):
    return pltpu.CompilerParams(dimension_semantics=sem, vmem_limit_bytes=VMEM_LIMIT)


def _mod_row(i):
    return jnp.where(i < N_CTX_TILES, 0, 1 + (i - N_CTX_TILES) // TILES_PER_LAT_SEQ)


def _ada_kernel(cv_ref, w_ref, b_ref, o_ref):
    o_ref[0] = _dot3(_silu(cv_ref[...]), w_ref[0]) + b_ref[0]


def _ada_mod(cvec, w_ada, b_ada):
    tn = 1024
    n_out = 6 * D_MODEL
    return pl.pallas_call(
        _ada_kernel,
        out_shape=jax.ShapeDtypeStruct((DEPTH, MOD_ROWS, n_out), _F32),
        grid=(DEPTH, n_out // tn),
        in_specs=[
            pl.BlockSpec((MOD_ROWS, D_MODEL), lambda l, j: (0, 0)),
            pl.BlockSpec((1, D_MODEL, tn), lambda l, j: (l, 0, j)),
            pl.BlockSpec((1, 1, tn), lambda l, j: (l, 0, j)),
        ],
        out_specs=pl.BlockSpec((1, MOD_ROWS, tn), lambda l, j: (l, 0, j)),
        compiler_params=_params(("arbitrary", "arbitrary")),
        name="ada_mod",
    )(cvec, w_ada, b_ada.reshape(DEPTH, 1, n_out))


def _inproj_kernel(xc_ref, xl_ref, nw_ref, mod_ref, wg_ref, wc_ref, wa_ref, g_ref, c_ref, a_ref):
    i = pl.program_id(0)
    row = _mod_row(i)
    sh = mod_ref[pl.ds(row, 1), 0:D_MODEL]
    sc = mod_ref[pl.ds(row, 1), D_MODEL:2 * D_MODEL]
    x = jnp.where(i < N_CTX_TILES, xc_ref[...], xl_ref[...])
    h = _bf(_rms(x, nw_ref[...]) * (1.0 + sc) + sh)
    g_ref[...] = _dot(h, wg_ref[...])
    c_ref[...] = _dot(h, wc_ref[...])
    a_ref[...] = _dot(h, wa_ref[...])


def _stream_tiles(lat_first_tile):
    ctx_tile = lambda i: (jnp.minimum(i, N_CTX_TILES - 1), 0)
    lat_tile = lambda i: (jnp.maximum(i - N_CTX_TILES, 0) + lat_first_tile, 0)
    return ctx_tile, lat_tile


def _in_proj(x_ctx, x_lat, lat_first_tile, norm_w, mod, w_gla, w_conv, w_na):
    full = lambda i: (0, 0)
    tile = lambda i: (i, 0)
    ctx_tile, lat_tile = _stream_tiles(lat_first_tile)
    return pl.pallas_call(
        _inproj_kernel,
        out_shape=(
            jax.ShapeDtypeStruct((N_TOK, GLA_IN_W), _F32),
            jax.ShapeDtypeStruct((N_TOK, 2 * CONV_CH), _F32),
            jax.ShapeDtypeStruct((N_TOK, 3 * NA_DIM), _F32),
        ),
        grid=(N_TOK // ROW_TILE,),
        in_specs=[
            pl.BlockSpec((ROW_TILE, D_MODEL), ctx_tile),
            pl.BlockSpec((ROW_TILE, D_MODEL), lat_tile),
            pl.BlockSpec((1, D_MODEL), full),
            pl.BlockSpec((MOD_ROWS, 6 * D_MODEL), full),
            pl.BlockSpec((D_MODEL, GLA_IN_W), full),
            pl.BlockSpec((D_MODEL, 2 * CONV_CH), full),
            pl.BlockSpec((D_MODEL, 3 * NA_DIM), full),
        ],
        out_specs=(
            pl.BlockSpec((ROW_TILE, GLA_IN_W), tile),
            pl.BlockSpec((ROW_TILE, 2 * CONV_CH), tile),
            pl.BlockSpec((ROW_TILE, 3 * NA_DIM), tile),
        ),
        compiler_params=_params(("arbitrary",)),
        name="in_proj",
    )(x_ctx, x_lat, norm_w, mod, w_gla, w_conv, w_na)


_GLA_LEVELS = (32, 16, 8, 4, 2, 1)
_N_EXP_BLOCKS = 2 + len(_GLA_LEVELS)
_N_MASKS = len(_GLA_LEVELS) + 1


def _gla_constants():
    cs = GLA_CHUNK
    r = np.arange(cs)
    i = r[:, None]
    c = r[None, :]
    w = np.zeros((2, _N_EXP_BLOCKS, cs, cs), np.float32)
    m = np.zeros((2, _N_MASKS, cs, cs), np.float32)
    w[0, 0] = c <= i
    w[0, 1] = c > i
    w[1, 0] = c >= i
    w[1, 1] = c < i
    for lv, half in enumerate(_GLA_LEVELS):
        mid = (r // (2 * half)) * (2 * half) + half
        mi = mid[:, None]
        second = (r >= mid)[:, None]
        same = (r[:, None] // (2 * half)) == (r[None, :] // (2 * half))
        w[0, 2 + lv] = np.where(second, (c >= mi) & (c <= i), (c > i) & (c <= mi - 1))
        w[1, 2 + lv] = np.where(second, (c >= mi) & (c <= i - 1), (c >= i) & (c <= mi - 1))
        m[0, lv] = same & (i >= mi) & (c < mi)
        m[1, lv] = same & (i < mi) & (c >= mi)
    m[:, _N_MASKS - 1] = np.eye(cs)
    w_all = w.reshape(2, _N_EXP_BLOCKS * cs, cs)
    lmask = np.tile(m, (1, 1, 1, GLA_HEADS))
    return jnp.asarray(w_all, _BF), jnp.asarray(lmask, _F32)


def _gla_kernel(g_ref, wz_ref, bz_ref, wall_ref, lmask_ref, onorm_ref, st0_ref,
                o_ref, stfin_ref, la_scr, o_scr, st_scr, *, n):
    cs = GLA_CHUNK
    nc = n // cs
    gate_rows = 128
    key_head = lax.broadcasted_iota(jnp.int32, (1, GLA_KEY), 1) // GLA_DK
    head_lanes = [(key_head == h).astype(_F32) for h in range(GLA_HEADS)]
    head_lanes_bf = [_bf(m) for m in head_lanes]
    zero_v = jnp.zeros((cs, GLA_DV), _BF)

    def log_decays(t, carry):
        rows = pl.ds(pl.multiple_of(t * gate_rows, gate_rows), gate_rows)
        z = _dot3(g_ref[rows, GLA_LR_OFF:GLA_IN_W], wz_ref[...]) + bz_ref[...]
        la_scr[rows, :] = (jnp.minimum(z, 0.0) - jnp.log1p(jnp.exp(-jnp.abs(z)))) * (1.0 / GLA_GATE_NORM)
        return carry

    lax.fori_loop(0, n // gate_rows, log_decays, 0)

    def chunk(c, d):
        rows = pl.ds(pl.multiple_of(c * cs, cs), cs)
        q = g_ref[rows, 0:GLA_KEY] * (GLA_DK ** -0.5)
        k = g_ref[rows, GLA_KEY:2 * GLA_KEY]
        v = g_ref[rows, 2 * GLA_KEY:2 * GLA_KEY + GLA_VAL]
        la_hi, la_lo = _split2(la_scr[rows, d * GLA_KEY:(d + 1) * GLA_KEY])
        w = wall_ref[d]
        f = jnp.exp(_dot(w, la_lo) + _dot(w, la_hi))
        st = st_scr[d]
        o = _dot_nt(_bf(q * f[0:cs]), _bf(st))
        p = jnp.zeros((cs, GLA_HEADS * cs), _F32)
        for lv in range(_N_MASKS):
            if lv < len(_GLA_LEVELS):
                fl = f[(2 + lv) * cs:(3 + lv) * cs]
                ql, kl = q * fl, k * fl
            else:
                ql, kl = q, k
            kl = _bf(kl)
            k_bd = jnp.concatenate([kl * m for m in head_lanes_bf], axis=0)
            p = p + lmask_ref[d, lv] * _dot_nt(_bf(ql), k_bd)
        vb = _bf(v)
        v_bd = jnp.concatenate(
            [jnp.concatenate([vb[:, g * GLA_DV:(g + 1) * GLA_DV] if g == h else zero_v for g in range(GLA_HEADS)], axis=1)
             for h in range(GLA_HEADS)], axis=0)
        o_scr[d, rows, :] = o + _dot(_bf(p), v_bd)
        decay = f[cs - 1:cs] if d == 0 else f[0:1]
        u_t = _dot_tn(vb, _bf(k * f[cs:2 * cs]))
        u_t = jnp.concatenate([u_t[h * GLA_DV:(h + 1) * GLA_DV] * head_lanes[h] for h in range(GLA_HEADS)], axis=0)
        st_scr[d] = st * decay + u_t

    st_scr[...] = st0_ref[0]

    def scan(i, carry):
        chunk(i, 0)
        chunk(nc - 1 - i, 1)
        return carry

    lax.fori_loop(0, nc, scan, 0)
    for d in range(2):
        for h in range(GLA_HEADS):
            stfin_ref[0, d, h] = st_scr[d, h * GLA_DV:(h + 1) * GLA_DV, h * GLA_DK:(h + 1) * GLA_DK]

    def finish(c, carry):
        rows = pl.ds(pl.multiple_of(c * cs, cs), cs)
        o = o_scr[1, rows, :] + o_scr[0, rows, :]
        for h in range(GLA_HEADS):
            cols = slice(h * GLA_DV, (h + 1) * GLA_DV)
            gate = g_ref[rows, 2 * GLA_KEY + GLA_VAL + h * GLA_DV:2 * GLA_KEY + GLA_VAL + (h + 1) * GLA_DV]
            o_ref[rows, cols] = _rms(o[:, cols], onorm_ref[...]) * _silu(gate)
        return carry

    lax.fori_loop(0, nc, finish, 0)


def _gla(g_all, wz, bz, consts, onorm, st0, *, n, n_seq, first_block):
    w_all, lmask = consts
    full2 = lambda i: (0, 0)
    return pl.pallas_call(
        functools.partial(_gla_kernel, n=n),
        out_shape=(
            jax.ShapeDtypeStruct((n_seq * n, GLA_VAL), _F32),
            jax.ShapeDtypeStruct((n_seq, 2, GLA_HEADS, GLA_DV, GLA_DK), _F32),
        ),
        grid=(n_seq,),
        in_specs=[
            pl.BlockSpec((n, GLA_IN_W), lambda i: (first_block + i, 0)),
            pl.BlockSpec(wz.shape, full2),
            pl.BlockSpec(bz.shape, full2),
            pl.BlockSpec(w_all.shape, lambda i: (0, 0, 0)),
            pl.BlockSpec(lmask.shape, lambda i: (0, 0, 0, 0)),
            pl.BlockSpec((1, GLA_DV), full2),
            pl.BlockSpec((1, 2, GLA_VAL, GLA_KEY), lambda i: (i, 0, 0, 0)),
        ],
        out_specs=(
            pl.BlockSpec((n, GLA_VAL), lambda i: (i, 0)),
            pl.BlockSpec((1, 2, GLA_HEADS, GLA_DV, GLA_DK), lambda i: (i, 0, 0, 0, 0)),
        ),
        scratch_shapes=[pltpu.VMEM((n, 2 * GLA_KEY), _F32), pltpu.VMEM((2, n, GLA_VAL), _F32),
                        pltpu.VMEM((2, GLA_VAL, GLA_KEY), _F32)],
        compiler_params=_params(("arbitrary",)),
        name=f"gla_{n}",
    )(g_all, wz, bz, w_all, lmask, onorm, st0)


def _state_to_blockdiag(s):
    b = s.shape[0]
    st = jnp.swapaxes(s, -1, -2)
    eye = jnp.eye(GLA_HEADS, dtype=s.dtype)
    out = st[:, :, :, :, None, :] * eye[None, None, :, None, :, None]
    return out.reshape(b, 2, GLA_VAL, GLA_KEY)


_CONV_PAD = 16
_CONV_ROWS = 128


def _conv_kernel(c_ref, dw_ref, dwb_ref, lng_ref, lnb_ref, pw_ref, o_ref, pad_scr, *, n):
    zeros = jnp.zeros((_CONV_PAD, CONV_CH), _F32)
    pad_scr[0:_CONV_PAD, :] = zeros
    pad_scr[_CONV_PAD + n:2 * _CONV_PAD + n, :] = zeros
    for r0 in range(0, n, _CONV_ROWS):
        a = c_ref[r0:r0 + _CONV_ROWS, :]
        pad_scr[_CONV_PAD + r0:_CONV_PAD + r0 + _CONV_ROWS, :] = a[:, :CONV_CH] * _sigmoid(a[:, CONV_CH:])
    half = CONV_WIDTH // 2
    for r0 in range(0, n, _CONV_ROWS):
        acc = jnp.zeros((_CONV_ROWS // 8, 8, CONV_CH), _F32)
        for w in range(CONV_WIDTH):
            s = _CONV_PAD + r0 + w - half
            acc = acc + pad_scr[s:s + _CONV_ROWS, :].reshape(_CONV_ROWS // 8, 8, CONV_CH) * dw_ref[w][None]
        acc = acc.reshape(_CONV_ROWS, CONV_CH) + dwb_ref[...]
        xc = acc - jnp.mean(acc, axis=-1, keepdims=True)
        y = xc * lax.rsqrt(jnp.mean(xc * xc, axis=-1, keepdims=True) + EPS) * lng_ref[...] + lnb_ref[...]
        o_ref[r0:r0 + _CONV_ROWS, :] = _dot(_bf(_silu(y)), pw_ref[...])


def _conv(c_all, dw, dwb, lng, lnb, pw, *, n, n_seq, first_block):
    full2 = lambda i: (0, 0)
    return pl.pallas_call(
        functools.partial(_conv_kernel, n=n),
        out_shape=jax.ShapeDtypeStruct((n_seq * n, CONV_CH), _F32),
        grid=(n_seq,),
        in_specs=[
            pl.BlockSpec((n, 2 * CONV_CH), lambda i: (first_block + i, 0)),
            pl.BlockSpec((CONV_WIDTH, 8, CONV_CH), lambda i: (0, 0, 0)),
            pl.BlockSpec((1, CONV_CH), full2),
            pl.BlockSpec((1, CONV_CH), full2),
            pl.BlockSpec((1, CONV_CH), full2),
            pl.BlockSpec((CONV_CH, CONV_CH), full2),
        ],
        out_specs=pl.BlockSpec((n, CONV_CH), lambda i: (i, 0)),
        scratch_shapes=[pltpu.VMEM((n + 2 * _CONV_PAD, CONV_CH), _F32)],
        compiler_params=_params(("arbitrary",)),
        name=f"conv_{n}",
    )(c_all, dw, dwb, lng, lnb, pw)


def _softmax_rows(s):
    e = jnp.exp(s - jnp.max(s, axis=-1, keepdims=True))
    return e / jnp.sum(e, axis=-1, keepdims=True)


def _na_ctx_kernel(a_ref, qn_ref, kn_ref, o_ref, kc_ref, vc_ref):
    for h in range(NA_HEADS):
        cols = slice(h * NA_HD, (h + 1) * NA_HD)
        q = _rms(a_ref[:, h * NA_HD:(h + 1) * NA_HD], qn_ref[...])
        k = _rms(a_ref[:, NA_DIM + h * NA_HD:NA_DIM + (h + 1) * NA_HD], kn_ref[...])
        v = a_ref[:, 2 * NA_DIM + h * NA_HD:2 * NA_DIM + (h + 1) * NA_HD]
        kc_ref[0, h] = k
        vc_ref[0, h] = v
        p = _softmax_rows(_dot_nt(_bf(q), _bf(k)) * (NA_HD ** -0.5))
        o_ref[:, cols] = _dot(_bf(p), _bf(v))


def _na_ctx(a_all, qn, kn):
    full2 = lambda i: (0, 0)
    cache = jax.ShapeDtypeStruct((BATCH, NA_HEADS, SEQ, NA_HD), _F32)
    cache_spec = pl.BlockSpec((1, NA_HEADS, SEQ, NA_HD), lambda i: (i, 0, 0, 0))
    return pl.pallas_call(
        _na_ctx_kernel,
        out_shape=(jax.ShapeDtypeStruct((N_CTX_TOK, NA_DIM), _F32), cache, cache),
        grid=(BATCH,),
        in_specs=[
            pl.BlockSpec((SEQ, 3 * NA_DIM), lambda i: (i, 0)),
            pl.BlockSpec((1, NA_HD), full2),
            pl.BlockSpec((1, NA_HD), full2),
        ],
        out_specs=(pl.BlockSpec((SEQ, NA_DIM), lambda i: (i, 0)), cache_spec, cache_spec),
        compiler_params=_params(("arbitrary",)),
        name="na_ctx",
    )(a_all, qn, kn)


_NA_ROWS = DEC_SEQ // GRID_W
_NA_KEYS = NA_WIN_ROWS * GRID_W
_NA_VARIANTS = NA_WIN_ROWS


_RPB_ROWS = 2 * NA_WIN_ROWS - 1
_RPB_COLS = 2 * NA_WIN_COLS - 1


def _na_bias_constants():
    qc = np.arange(GRID_W)[:, None]
    kc = np.arange(GRID_W)[None, :]
    shift = np.stack([(kc - qc + NA_WIN_COLS - 1) == co for co in range(_RPB_COLS)]).astype(np.float32)
    win_start = np.clip(qc - NA_WIN_COLS // 2, 0, GRID_W - NA_WIN_COLS)
    in_win = ((kc >= win_start) & (kc < win_start + NA_WIN_COLS)).astype(np.float32)
    return jnp.asarray(shift), jnp.asarray(in_win)


def _na_bias_kernel(rpb_ref, shift_ref, win_ref, o_ref):
    h = pl.program_id(0)
    in_win = win_ref[...] > 0.0
    tiles = []
    for ro in range(_RPB_ROWS):
        base = (h * _RPB_ROWS + ro) * _RPB_COLS
        acc = rpb_ref[base] * shift_ref[0]
        for co in range(1, _RPB_COLS):
            acc = acc + rpb_ref[base + co] * shift_ref[co]
        tiles.append(jnp.where(in_win, acc, -jnp.inf))
    for t in range(_NA_VARIANTS):
        o_ref[0, t] = jnp.concatenate([tiles[kr - t + NA_WIN_ROWS - 1] for kr in range(NA_WIN_ROWS)], axis=1)


def _na_bias_table(rpb):
    shift, in_win = _na_bias_constants()
    return pl.pallas_call(
        _na_bias_kernel,
        out_shape=jax.ShapeDtypeStruct((NA_HEADS, _NA_VARIANTS, GRID_W, _NA_KEYS), _F32),
        grid=(NA_HEADS,),
        in_specs=[
            pl.BlockSpec(memory_space=pltpu.SMEM),
            pl.BlockSpec(shift.shape, lambda h: (0, 0, 0)),
            pl.BlockSpec(in_win.shape, lambda h: (0, 0)),
        ],
        out_specs=pl.BlockSpec((1, _NA_VARIANTS, GRID_W, _NA_KEYS), lambda h: (h, 0, 0, 0)),
        compiler_params=_params(("arbitrary",)),
        name="na_bias",
    )(rpb.reshape(-1), shift, in_win)


def _na_row_groups():
    start = lambda r: min(max(r - NA_WIN_ROWS // 2, 0), _NA_ROWS - NA_WIN_ROWS)
    groups, r = [], 0
    while r < _NA_ROWS:
        n = 1
        while r + n < _NA_ROWS and start(r + n) == start(r):
            n += 1
        groups.append((r, n, start(r)))
        r += n
    return groups


def _na_lat_kernel(a_ref, kctx_ref, vctx_ref, bias_ref, qn_ref, kn_ref, o_ref,
                   q_scr, k_scr, v_scr, sctx_scr, pctx_scr):
    scale = NA_HD ** -0.5
    for h in range(NA_HEADS):
        cols = slice(h * NA_HD, (h + 1) * NA_HD)
        q_scr[...] = _bf(_rms(a_ref[:, h * NA_HD:(h + 1) * NA_HD], qn_ref[...]))
        k_scr[...] = _bf(_rms(a_ref[:, NA_DIM + h * NA_HD:NA_DIM + (h + 1) * NA_HD], kn_ref[...]))
        v_scr[...] = _bf(a_ref[:, 2 * NA_DIM + h * NA_HD:2 * NA_DIM + (h + 1) * NA_HD])
        sctx_scr[...] = _dot_nt(q_scr[...], _bf(kctx_ref[0, h])) * scale
        for r0, nr, ks in _na_row_groups():
            qrows = slice(r0 * GRID_W, (r0 + nr) * GRID_W)
            krows = slice(ks * GRID_W, ks * GRID_W + _NA_KEYS)
            bias = jnp.concatenate([bias_ref[h, r - ks] for r in range(r0, r0 + nr)], axis=0)
            s_loc = _dot_nt(q_scr[qrows, :], k_scr[krows, :]) * scale + bias
            s_ctx = sctx_scr[qrows, :]
            m = jnp.maximum(jnp.max(s_loc, axis=-1, keepdims=True), jnp.max(s_ctx, axis=-1, keepdims=True))
            e_loc = jnp.exp(s_loc - m)
            e_ctx = jnp.exp(s_ctx - m)
            inv = 1.0 / (jnp.sum(e_loc, axis=-1, keepdims=True) + jnp.sum(e_ctx, axis=-1, keepdims=True))
            pctx_scr[qrows, :] = _bf(e_ctx * inv)
            o_ref[qrows, cols] = _dot(_bf(e_loc * inv), v_scr[krows, :])
        o_ref[:, cols] = o_ref[:, cols] + _dot(pctx_scr[...], _bf(vctx_ref[0, h]))


def _na_lat(a_all, k_ctx, v_ctx, bias, qn, kn):
    full2 = lambda i: (0, 0)
    ctx_spec = pl.BlockSpec((1, NA_HEADS, PAST_LEN, NA_HD), lambda i: (i, 0, 0, 0))
    return pl.pallas_call(
        _na_lat_kernel,
        out_shape=jax.ShapeDtypeStruct((N_LAT_TOK, NA_DIM), _F32),
        grid=(DEC_BATCH,),
        in_specs=[
            pl.BlockSpec((DEC_SEQ, 3 * NA_DIM), lambda i: (N_CTX_TOK // DEC_SEQ + i, 0)),
            ctx_spec,
            ctx_spec,
            pl.BlockSpec(bias.shape, lambda i: (0, 0, 0, 0)),
            pl.BlockSpec((1, NA_HD), full2),
            pl.BlockSpec((1, NA_HD), full2),
        ],
        out_specs=pl.BlockSpec((DEC_SEQ, NA_DIM), lambda i: (i, 0)),
        scratch_shapes=[pltpu.VMEM((DEC_SEQ, NA_HD), _BF)] * 3
        + [pltpu.VMEM((DEC_SEQ, PAST_LEN), _F32), pltpu.VMEM((DEC_SEQ, PAST_LEN), _BF)],
        compiler_params=_params(("arbitrary",)),
        name="na_lat",
    )(a_all, k_ctx, v_ctx, bias, qn, kn)


def _outproj_kernel(gc_ref, gl_ref, cc_ref, cl_ref, nc_ref, nl_ref, xc_ref, xl_ref, mod_ref, nw_ref, wo_ref,
                    rwt_ref, x1_ref, h2_ref, hr_ref, lg_ref):
    i = pl.program_id(0)
    is_ctx = i < N_CTX_TILES
    row = _mod_row(i)
    gla = jnp.where(is_ctx, gc_ref[...], gl_ref[...])
    conv = jnp.where(is_ctx, cc_ref[...], cl_ref[...])
    na = jnp.where(is_ctx, nc_ref[...], nl_ref[...])
    mix = (_dot(_bf(gla), wo_ref[0:GLA_VAL, :])
           + _dot(_bf(conv), wo_ref[GLA_VAL:GLA_VAL + CONV_CH, :])
           + _dot(_bf(na), wo_ref[GLA_VAL + CONV_CH:, :]))
    g1 = mod_ref[pl.ds(row, 1), 2 * D_MODEL:3 * D_MODEL]
    sh2 = mod_ref[pl.ds(row, 1), 3 * D_MODEL:4 * D_MODEL]
    sc2 = mod_ref[pl.ds(row, 1), 4 * D_MODEL:5 * D_MODEL]
    x1 = jnp.where(is_ctx, xc_ref[...], xl_ref[...]) + g1 * mix
    h2 = _rms(x1, nw_ref[...]) * (1.0 + sc2) + sh2
    x1_ref[...] = x1
    h2_ref[...] = _bf(h2)
    for j in range(_ROW_CHUNKS):
        hr_ref[pl.ds(j, ROW_TILE, stride=_ROW_CHUNKS), :] = h2[:, j * _LANE:(j + 1) * _LANE]
    lg_ref[...] = _dot3(rwt_ref[...], h2, dot=_dot_nt)


def _out_proj(gla_c, gla_l, conv_c, conv_l, na_c, na_l, x_ctx, x_lat, lat_first_tile, mod, norm_w, w_out,
              router_wt):
    full2 = lambda i: (0, 0)
    tile = lambda i: (i, 0)
    ctx_tile, lat_tile = _stream_tiles(0)
    _, x_lat_tile = _stream_tiles(lat_first_tile)
    return pl.pallas_call(
        _outproj_kernel,
        out_shape=(
            jax.ShapeDtypeStruct((N_TOK, D_MODEL), _F32),
            jax.ShapeDtypeStruct((N_TOK, D_MODEL), _BF),
            jax.ShapeDtypeStruct((N_TOK * _ROW_CHUNKS, _LANE), _F32),
            jax.ShapeDtypeStruct((N_EXPERTS, N_TOK), _F32),
        ),
        grid=(N_TOK // ROW_TILE,),
        in_specs=[
            pl.BlockSpec((ROW_TILE, GLA_VAL), ctx_tile),
            pl.BlockSpec((ROW_TILE, GLA_VAL), lat_tile),
            pl.BlockSpec((ROW_TILE, CONV_CH), ctx_tile),
            pl.BlockSpec((ROW_TILE, CONV_CH), lat_tile),
            pl.BlockSpec((ROW_TILE, NA_DIM), ctx_tile),
            pl.BlockSpec((ROW_TILE, NA_DIM), lat_tile),
            pl.BlockSpec((ROW_TILE, D_MODEL), ctx_tile),
            pl.BlockSpec((ROW_TILE, D_MODEL), x_lat_tile),
            pl.BlockSpec((MOD_ROWS, 6 * D_MODEL), full2),
            pl.BlockSpec((1, D_MODEL), full2),
            pl.BlockSpec((D_MODEL, D_MODEL), full2),
            pl.BlockSpec((N_EXPERTS, D_MODEL), full2),
        ],
        out_specs=(
            pl.BlockSpec((ROW_TILE, D_MODEL), tile),
            pl.BlockSpec((ROW_TILE, D_MODEL), tile),
            pl.BlockSpec((ROW_TILE * _ROW_CHUNKS, _LANE), tile),
            pl.BlockSpec((N_EXPERTS, ROW_TILE), lambda i: (0, i)),
        ),
        compiler_params=_params(("arbitrary",)),
        name="out_proj",
    )(gla_c, gla_l, conv_c, conv_l, na_c, na_l, x_ctx, x_lat, mod, norm_w, w_out, router_wt)


_PER_GROUP = N_EXPERTS // N_GROUPS
_ROUTE_TILE = 1024


def _first_max(x, idx, axes, sentinel):
    m = x
    for ax in axes:
        m = jnp.max(m, axis=ax, keepdims=True)
    first = jnp.where(x == m, idx, sentinel)
    for ax in axes:
        first = jnp.min(first, axis=ax, keepdims=True)
    return m, first


def _router_kernel(lg_ref, rb_ref, eid_ref, wts_ref, cnt_ref):
    t = lg_ref.shape[1]
    shape3 = (N_GROUPS, _PER_GROUP, t)
    scores = _sigmoid(lg_ref[...])
    biased = (scores + rb_ref[...]).reshape(shape3)
    scores = scores.reshape(shape3)
    neg = -jnp.inf
    in_grp = lax.broadcasted_iota(jnp.int32, shape3, 1)
    grp = lax.broadcasted_iota(jnp.int32, (N_GROUPS, 1, t), 0)
    expert = lax.broadcasted_iota(jnp.int32, shape3, 0) * _PER_GROUP + in_grp
    m1, i1 = _first_max(biased, in_grp, (1,), _PER_GROUP)
    m2 = jnp.max(jnp.where(in_grp == i1, neg, biased), axis=1, keepdims=True)
    gscore = m1 + m2
    keep = jnp.zeros((N_GROUPS, 1, t), _F32)
    for _ in range(TOPK_GROUPS):
        _, gi = _first_max(gscore, grp, (0,), N_GROUPS)
        hit = grp == gi
        keep = jnp.where(hit, 1.0, keep)
        gscore = jnp.where(hit, neg, gscore)
    masked = jnp.where(keep > 0.0, biased, neg)
    ids, picked = [], []
    hits = jnp.zeros(shape3, _F32)
    for _ in range(TOP_K):
        _, ei = _first_max(masked, expert, (1, 0), N_EXPERTS)
        hit = expert == ei
        hits = jnp.where(hit, 1.0, hits)
        sc = jnp.sum(jnp.sum(jnp.where(hit, scores, 0.0), axis=1, keepdims=True), axis=0, keepdims=True)
        ids.append(ei.reshape(1, t))
        picked.append(sc.reshape(1, t))
        masked = jnp.where(hit, neg, masked)
    picked = jnp.concatenate(picked, axis=0)
    den = jnp.sum(picked, axis=0, keepdims=True)
    eid_ref[...] = jnp.concatenate(ids, axis=0)
    wts_ref[...] = picked / den * ROUTED_SCALE
    cnt_ref[0] = jnp.sum(hits, axis=2, keepdims=True).reshape(N_EXPERTS, 1)


def _router(logits_t, router_bias):
    tile = lambda i: (0, i)
    return pl.pallas_call(
        _router_kernel,
        out_shape=(jax.ShapeDtypeStruct((TOP_K, N_TOK), jnp.int32), jax.ShapeDtypeStruct((TOP_K, N_TOK), _F32),
                   jax.ShapeDtypeStruct((N_TOK // _ROUTE_TILE, N_EXPERTS, 1), _F32)),
        grid=(N_TOK // _ROUTE_TILE,),
        in_specs=[
            pl.BlockSpec((N_EXPERTS, _ROUTE_TILE), tile),
            pl.BlockSpec((N_EXPERTS, 1), lambda i: (0, 0)),
        ],
        out_specs=(pl.BlockSpec((TOP_K, _ROUTE_TILE), tile), pl.BlockSpec((TOP_K, _ROUTE_TILE), tile),
                   pl.BlockSpec((1, N_EXPERTS, 1), lambda i: (i, 0, 0))),
        compiler_params=_params(("arbitrary",)),
        name="router",
    )(logits_t, router_bias)


_HALF_TOK = N_CTX_TOK
_N_HALVES = N_TOK // _HALF_TOK
_HALF_ASSIGN = _HALF_TOK * TOP_K
_GB = 576
_GB_MAX = _HALF_ASSIGN // _GB + N_EXPERTS
_LANE = 128
_ROW_CHUNKS = D_MODEL // _LANE
_GS = _GB + 1
_STAGE_ROWS = (_ROW_CHUNKS * _GS + 7) // 8 * 8
_RMW_BATCH = 8
_TOK_BITS = 12
assert _HALF_TOK == 1 << _TOK_BITS
_FFN_SPLIT = 1


def _moe_plan(eid, wts, tile_counts):
    tok = jnp.arange(N_TOK, dtype=jnp.int32)
    key = (((tok >> _TOK_BITS) * N_EXPERTS)[None, :] + eid) * _HALF_TOK + (tok & (_HALF_TOK - 1))[None, :]
    key_s, gate_s = lax.sort((key.reshape(-1), wts.reshape(-1)), num_keys=1)
    row_off = (key_s & (_HALF_TOK - 1)) * _ROW_CHUNKS
    count = jnp.sum(tile_counts.reshape(_N_HALVES, -1, N_EXPERTS), axis=1).astype(jnp.int32)
    row_end = jnp.cumsum(count.reshape(-1)).reshape(_N_HALVES, N_EXPERTS)
    row_start = row_end - count
    nblk = (count + _GB - 1) // _GB
    blk_end = jnp.cumsum(nblk, axis=1)
    blk_start = blk_end - nblk
    n_used = blk_end[:, -1]
    b = jnp.minimum(jnp.arange(_GB_MAX, dtype=jnp.int32)[None, :], n_used[:, None] - 1)
    grp = jnp.sum((b[:, :, None] >= blk_end[:, None, :]).astype(jnp.int32), axis=-1)
    onehot = (grp[:, :, None] == jnp.arange(N_EXPERTS, dtype=jnp.int32)).astype(jnp.int32)
    pick = lambda v: jnp.sum(onehot * v[:, None, :], axis=-1)
    within = b - pick(blk_start)
    start = pick(row_start) + within * _GB
    length = jnp.clip(pick(count) - within * _GB, 0, _GB)
    pad = jnp.zeros((_GB,), jnp.int32)
    return (jnp.concatenate([row_off, pad]), jnp.concatenate([gate_s, pad.astype(_F32)]),
            grp.astype(jnp.int32), start.astype(jnp.int32), length.astype(jnp.int32), n_used.astype(jnp.int32))


def _gmm_kernel(be_ref, bstart_ref, blen_ref, nused_ref, off_ref, gate_ref,
                h_ref, wg_ref, wu_ref, wd_ref, acc_ref, xt_scr, yt_scr, wgb_scr, wub_scr, wdb_scr):
    s = pl.program_id(0)
    n_used = nused_ref[0]
    last = _GB_MAX - 1

    @pl.when(s == 0)
    def _():
        acc_ref[...] = jnp.zeros_like(acc_ref)
        xt_scr[...] = jnp.zeros_like(xt_scr)
        yt_scr[...] = jnp.zeros_like(yt_scr)

    fb = jnp.clip(s - 1, 0, last)

    @pl.when((s == 0) | (be_ref[fb] != be_ref[jnp.clip(s - 2, 0, last)]))
    def _():
        wgb_scr[...] = _bf(wg_ref[0, 0])
        wub_scr[...] = _bf(wu_ref[0, 0])
        wdb_scr[...] = _bf(wd_ref[0, 0])

    @pl.when(s < n_used + 2)
    def _():
        slot = s % 2
        other = 1 - slot

        cb = jnp.clip(s - 2, 0, last)
        c_start = bstart_ref[cb]
        c_len = blen_ref[cb]
        for m0 in range(0, _GB, _RMW_BATCH):
            pending = []
            for mi in range(m0, m0 + _RMW_BATCH):
                valid = mi < c_len
                row = jnp.where(valid, off_ref[c_start + mi], _HALF_TOK * _ROW_CHUNKS)
                gate = jnp.where(valid, gate_ref[c_start + mi], 0.0)
                rows = pl.ds(pl.multiple_of(row, _ROW_CHUNKS), _ROW_CHUNKS)
                contrib = gate * yt_scr[slot, pl.ds(mi, _ROW_CHUNKS, stride=_GS), :]
                pending.append((rows, acc_ref[rows, :] + contrib))
            for rows, val in pending:
                acc_ref[rows, :] = val

        part = _GB // _FFN_SPLIT
        for p in range(_FFN_SPLIT):
            x = _bf(jnp.concatenate([xt_scr[other, j * _GS + p * part:j * _GS + (p + 1) * part, :]
                                     for j in range(_ROW_CHUNKS)], axis=-1))
            act = _silu(_dot(x, wgb_scr[...])) * _dot(x, wub_scr[...])
            y = _dot(_bf(act), wdb_scr[...])
            for j in range(_ROW_CHUNKS):
                yt_scr[other, j * _GS + p * part:j * _GS + (p + 1) * part, :] = y[:, j * _LANE:(j + 1) * _LANE]

        g_start = bstart_ref[jnp.minimum(s, last)]
        for mi in range(_GB):
            src = pl.ds(pl.multiple_of(off_ref[g_start + mi], _ROW_CHUNKS), _ROW_CHUNKS)
            xt_scr[slot, pl.ds(mi, _ROW_CHUNKS, stride=_GS), :] = h_ref[src, :]


def _gmm(layer, half, plan, h_rows, wg, wu, wd):
    row_off, gate_s, grp, start, length, n_used = plan
    expert = lambda s, be, *_: (layer, be[jnp.clip(s - 1, 0, _GB_MAX - 1)], 0, 0)
    acc_rows = (_HALF_TOK + 1) * _ROW_CHUNKS
    grid_spec = pltpu.PrefetchScalarGridSpec(
        num_scalar_prefetch=6,
        grid=(_GB_MAX + 2,),
        in_specs=[
            pl.BlockSpec((_HALF_TOK * _ROW_CHUNKS, _LANE), lambda b, *_: (half, 0), pipeline_mode=pl.Buffered(1)),
            pl.BlockSpec((1, 1, D_MODEL, EXPERT_DIM), expert),
            pl.BlockSpec((1, 1, D_MODEL, EXPERT_DIM), expert),
            pl.BlockSpec((1, 1, EXPERT_DIM, D_MODEL), expert),
        ],
        out_specs=pl.BlockSpec((acc_rows, _LANE), lambda b, *_: (0, 0), pipeline_mode=pl.Buffered(1)),
        scratch_shapes=[pltpu.VMEM((2, _STAGE_ROWS, _LANE), _F32)] * 2
        + [pltpu.VMEM((D_MODEL, EXPERT_DIM), _BF)] * 2 + [pltpu.VMEM((EXPERT_DIM, D_MODEL), _BF)],
    )
    acc = pl.pallas_call(
        _gmm_kernel,
        out_shape=jax.ShapeDtypeStruct((acc_rows, _LANE), _F32),
        grid_spec=grid_spec,
        compiler_params=_params(("arbitrary",)),
        name="moe_experts",
    )(grp[half], start[half], length[half], n_used[half:half + 1], row_off, gate_s, h_rows, wg, wu, wd)
    return acc


_FIN_TILE = 512


def _moe_finish_kernel(h_ref, rc_ref, rl_ref, x1_ref, mod_ref, sg_ref, su_ref, sd_ref, o_ref):
    t = pl.program_id(0)
    tiles_per_half = _HALF_TOK // _FIN_TILE
    is_ctx = t < tiles_per_half
    row = jnp.where(is_ctx, 0, 1 + (t - tiles_per_half) // (DEC_SEQ // _FIN_TILE))
    h = h_ref[...]
    shared = _dot(_bf(_silu(_dot(h, _bf(sg_ref[0]))) * _dot(h, _bf(su_ref[0]))), _bf(sd_ref[0]))
    routed = jnp.concatenate(
        [jnp.where(is_ctx, rc_ref[pl.ds(j, _FIN_TILE, stride=_ROW_CHUNKS), :],
                   rl_ref[pl.ds(j, _FIN_TILE, stride=_ROW_CHUNKS), :]) for j in range(_ROW_CHUNKS)], axis=-1)
    g2 = mod_ref[pl.ds(row, 1), 5 * D_MODEL:6 * D_MODEL]
    o_ref[...] = x1_ref[...] + g2 * (routed + shared)


def _moe_finish(layer, h2, routed_c, routed_l, x1, mod, sg, su, sd):
    tile = lambda t: (t, 0)
    tiles_per_half = _HALF_TOK // _FIN_TILE
    ctx_tile = lambda t: (jnp.minimum(t, tiles_per_half - 1), 0)
    lat_tile = lambda t: (jnp.maximum(t - tiles_per_half, 0), 0)
    shared = lambda t: (layer, 0, 0)
    return pl.pallas_call(
        _moe_finish_kernel,
        out_shape=jax.ShapeDtypeStruct((N_TOK, D_MODEL), _F32),
        grid=(N_TOK // _FIN_TILE,),
        in_specs=[
            pl.BlockSpec((_FIN_TILE, D_MODEL), tile),
            pl.BlockSpec((_FIN_TILE * _ROW_CHUNKS, _LANE), ctx_tile),
            pl.BlockSpec((_FIN_TILE * _ROW_CHUNKS, _LANE), lat_tile),
            pl.BlockSpec((_FIN_TILE, D_MODEL), tile),
            pl.BlockSpec((MOD_ROWS, 6 * D_MODEL), lambda t: (0, 0)),
            pl.BlockSpec((1, D_MODEL, SHARED_DIM), shared),
            pl.BlockSpec((1, D_MODEL, SHARED_DIM), shared),
            pl.BlockSpec((1, SHARED_DIM, D_MODEL), shared),
        ],
        out_specs=pl.BlockSpec((_FIN_TILE, D_MODEL), tile),
        compiler_params=_params(("arbitrary",)),
        name="moe_finish",
    )(h2, routed_c, routed_l, x1, mod, sg, su, sd)


def kernel(x_prompt, x_sample, state_gla, cache_na_k, cache_na_v, c, c_ctx, w_ada, b_ada, norm_mix, norm_ffn, w_in, gla_w_gate, gla_b_gate, gla_out_norm, conv_dw, conv_dw_b, conv_ln_g, conv_ln_b, conv_pw, na_q_norm, na_k_norm, na_rpb, w_out, router_w, router_bias, exp_w_gate, exp_w_up, exp_w_down, sh_w_gate, sh_w_up, sh_w_down):
    x_ctx, x_lat, x_lat_tile = x_prompt.reshape(N_CTX_TOK, D_MODEL), x_sample.reshape(N_LAT_TOK, D_MODEL), 0
    cvec = jnp.concatenate([c_ctx[None], c, jnp.zeros((MOD_ROWS - 1 - DEC_BATCH, D_MODEL), _F32)], axis=0)
    mod_all = _ada_mod(cvec, w_ada, b_ada)
    gla_consts = _gla_constants()
    zero_state = jnp.zeros((BATCH, 2, GLA_VAL, GLA_KEY), _F32)
    lat_first = N_CTX_TOK // DEC_SEQ

    states, keys, vals = [], [], []
    for l in range(DEPTH):
        mod = mod_all[l]
        wi = w_in[l]
        w_gla = _bf(jnp.pad(wi[:, :GLA_LR_OFF + 2 * GLA_GATE_RANK], ((0, 0), (0, GLA_IN_W - GLA_LR_OFF - 2 * GLA_GATE_RANK))))
        conv_off = GLA_LR_OFF + 2 * GLA_GATE_RANK
        w_conv = _bf(wi[:, conv_off:conv_off + 2 * CONV_CH])
        w_na = _bf(wi[:, conv_off + 2 * CONV_CH:])
        g_all, c_all, a_all = _in_proj(x_ctx, x_lat, x_lat_tile, norm_mix[l][None], mod, w_gla, w_conv, w_na)

        wz = jnp.zeros((GLA_IN_W - GLA_LR_OFF, 2 * GLA_KEY), _F32)
        wz = wz.at[:GLA_GATE_RANK, :GLA_KEY].set(gla_w_gate[l, 0])
        wz = wz.at[GLA_GATE_RANK:2 * GLA_GATE_RANK, GLA_KEY:].set(gla_w_gate[l, 1])
        bz = gla_b_gate[l].reshape(1, 2 * GLA_KEY)
        onorm = gla_out_norm[l][None]
        gla_c, st_c = _gla(g_all, wz, bz, gla_consts, onorm, zero_state, n=SEQ, n_seq=BATCH, first_block=0)
        gla_l, _ = _gla(g_all, wz, bz, gla_consts, onorm, _state_to_blockdiag(state_gla[:, l]),
                        n=DEC_SEQ, n_seq=DEC_BATCH, first_block=lat_first)

        conv_args = (jnp.broadcast_to(conv_dw[l][:, None, :], (CONV_WIDTH, 8, CONV_CH)), conv_dw_b[l][None],
                     conv_ln_g[l][None], conv_ln_b[l][None], _bf(conv_pw[l]))
        conv_c = _conv(c_all, *conv_args, n=SEQ, n_seq=BATCH, first_block=0)
        conv_l = _conv(c_all, *conv_args, n=DEC_SEQ, n_seq=DEC_BATCH, first_block=lat_first)

        qn, kn = na_q_norm[l][None], na_k_norm[l][None]
        na_c, k_l, v_l = _na_ctx(a_all, qn, kn)
        na_l = _na_lat(a_all, cache_na_k[:, l], cache_na_v[:, l], _na_bias_table(na_rpb[l]), qn, kn)

        x1, h2, h_rows, logits_t = _out_proj(gla_c, gla_l, conv_c, conv_l, na_c, na_l, x_ctx, x_lat, x_lat_tile, mod,
                                     norm_ffn[l][None], _bf(w_out[l]), router_w[l].T)
        plan = _moe_plan(*_router(logits_t, router_bias[l][:, None]))
        routed = [_gmm(l, half, plan, h_rows, exp_w_gate, exp_w_up, exp_w_down) for half in range(_N_HALVES)]
        x = _moe_finish(l, h2, routed[0], routed[1], x1, mod, sh_w_gate, sh_w_up, sh_w_down)
        x_ctx, x_lat, x_lat_tile = x, x, N_CTX_TILES

        states.append(jnp.swapaxes(st_c, -1, -2))
        keys.append(k_l)
        vals.append(v_l)

    y_prompt = x[:N_CTX_TOK].reshape(BATCH, SEQ, D_MODEL)
    y_sample = x[N_CTX_TOK:].reshape(DEC_BATCH, DEC_SEQ, D_MODEL)
    return (y_prompt, y_sample, jnp.stack(states, axis=1), jnp.stack(keys, axis=1), jnp.stack(vals, axis=1))
```

```python
import functools

import numpy as np
import jax
import jax.numpy as jnp
from jax import lax
from jax.experimental import pallas as pl
from jax.experimental.pallas import tpu as pltpu

D_MODEL = 1024
BATCH = 16
SEQ = 256
DEPTH = 2
DEC_BATCH = 4
DEC_SEQ = 1024
PAST_LEN = 256
GRID_W = 64
GLA_HEADS = 4
GLA_DK = 64
GLA_DV = 128
GLA_KEY = GLA_HEADS * GLA_DK
GLA_VAL = GLA_HEADS * GLA_DV
GLA_GATE_RANK = 16
GLA_GATE_NORM = 16.0
GLA_CHUNK = 64
CONV_CH = 256
CONV_WIDTH = 31
NA_HEADS = 4
NA_HD = 64
NA_DIM = NA_HEADS * NA_HD
NA_WIN_ROWS = 8
NA_WIN_COLS = 16
N_EXPERTS = 64
TOP_K = 8
N_GROUPS = 8
TOPK_GROUPS = 4
EXPERT_DIM = 256
SHARED_DIM = 256
ROUTED_SCALE = 2.5
EPS = 1e-6

N_CTX_TOK = BATCH * SEQ
N_LAT_TOK = DEC_BATCH * DEC_SEQ
N_TOK = N_CTX_TOK + N_LAT_TOK
ROW_TILE = 512
N_CTX_TILES = N_CTX_TOK // ROW_TILE
TILES_PER_LAT_SEQ = DEC_SEQ // ROW_TILE
MOD_ROWS = 8
GLA_IN_W = 1664
GLA_LR_OFF = 2 * GLA_KEY + 2 * GLA_VAL
VMEM_LIMIT = 56 * 1024 * 1024

_BF = jnp.bfloat16
_F32 = jnp.float32


def _bf(x):
    return x.astype(_BF)


def _dot(a, b):
    return jnp.dot(a, b, preferred_element_type=_F32)


def _dot_nt(a, b):
    return lax.dot_general(a, b, (((1,), (1,)), ((), ())), preferred_element_type=_F32)


def _dot_tn(a, b):
    return lax.dot_general(a, b, (((0,), (0,)), ((), ())), preferred_element_type=_F32)


def _split2(x):
    hi = _bf(x)
    lo = _bf(x - hi.astype(_F32))
    return hi, lo


def _dot3(a, b, dot=_dot):
    a_hi, a_lo = _split2(a)
    b_hi, b_lo = _split2(b)
    return (dot(a_lo, b_hi) + dot(a_hi, b_lo)) + dot(a_hi, b_hi)


def _sigmoid(x):
    return 1.0 / (1.0 + jnp.exp(-x))


def _silu(x):
    return x * _sigmoid(x)


def _rms(x, w):
    return x * lax.rsqrt(jnp.mean(x * x, axis=-1, keepdims=True) + EPS) * w


def _params(sem):
    return pltpu.CompilerParams(dimension_semantics=sem, vmem_limit_bytes=VMEM_LIMIT)


def _mod_row(i):
    return jnp.where(i < N_CTX_TILES, 0, 1 + (i - N_CTX_TILES) // TILES_PER_LAT_SEQ)


def _ada_kernel(cv_ref, w_ref, b_ref, o_ref):
    o_ref[0] = _dot3(_silu(cv_ref[...]), w_ref[0]) + b_ref[0]


def _ada_mod(cvec, w_ada, b_ada):
    tn = 1024
    n_out = 6 * D_MODEL
    return pl.pallas_call(
        _ada_kernel,
        out_shape=jax.ShapeDtypeStruct((DEPTH, MOD_ROWS, n_out), _F32),
        grid=(DEPTH, n_out // tn),
        in_specs=[
            pl.BlockSpec((MOD_ROWS, D_MODEL), lambda l, j: (0, 0)),
            pl.BlockSpec((1, D_MODEL, tn), lambda l, j: (l, 0, j)),
            pl.BlockSpec((1, 1, tn), lambda l, j: (l, 0, j)),
        ],
        out_specs=pl.BlockSpec((1, MOD_ROWS, tn), lambda l, j: (l, 0, j)),
        compiler_params=_params(("arbitrary", "arbitrary")),
        name="ada_mod",
    )(cvec, w_ada, b_ada.reshape(DEPTH, 1, n_out))


def _inproj_kernel(xc_ref, xl_ref, nw_ref, mod_ref, wg_ref, wc_ref, wa_ref, g_ref, c_ref, a_ref):
    i = pl.program_id(0)
    row = _mod_row(i)
    sh = mod_ref[pl.ds(row, 1), 0:D_MODEL]
    sc = mod_ref[pl.ds(row, 1), D_MODEL:2 * D_MODEL]
    x = jnp.where(i < N_CTX_TILES, xc_ref[...], xl_ref[...])
    h = _bf(_rms(x, nw_ref[...]) * (1.0 + sc) + sh)
    g_ref[...] = _dot(h, wg_ref[...])
    c_ref[...] = _dot(h, wc_ref[...])
    a_ref[...] = _dot(h, wa_ref[...])


def _stream_tiles(lat_first_tile):
    ctx_tile = lambda i: (jnp.minimum(i, N_CTX_TILES - 1), 0)
    lat_tile = lambda i: (jnp.maximum(i - N_CTX_TILES, 0) + lat_first_tile, 0)
    return ctx_tile, lat_tile


def _in_proj(x_ctx, x_lat, lat_first_tile, norm_w, mod, w_gla, w_conv, w_na):
    full = lambda i: (0, 0)
    tile = lambda i: (i, 0)
    ctx_tile, lat_tile = _stream_tiles(lat_first_tile)
    return pl.pallas_call(
        _inproj_kernel,
        out_shape=(
            jax.ShapeDtypeStruct((N_TOK, GLA_IN_W), _F32),
            jax.ShapeDtypeStruct((N_TOK, 2 * CONV_CH), _F32),
            jax.ShapeDtypeStruct((N_TOK, 3 * NA_DIM), _F32),
        ),
        grid=(N_TOK // ROW_TILE,),
        in_specs=[
            pl.BlockSpec((ROW_TILE, D_MODEL), ctx_tile),
            pl.BlockSpec((ROW_TILE, D_MODEL), lat_tile),
            pl.BlockSpec((1, D_MODEL), full),
            pl.BlockSpec((MOD_ROWS, 6 * D_MODEL), full),
            pl.BlockSpec((D_MODEL, GLA_IN_W), full),
            pl.BlockSpec((D_MODEL, 2 * CONV_CH), full),
            pl.BlockSpec((D_MODEL, 3 * NA_DIM), full),
        ],
        out_specs=(
            pl.BlockSpec((ROW_TILE, GLA_IN_W), tile),
            pl.BlockSpec((ROW_TILE, 2 * CONV_CH), tile),
            pl.BlockSpec((ROW_TILE, 3 * NA_DIM), tile),
        ),
        compiler_params=_params(("arbitrary",)),
        name="in_proj",
    )(x_ctx, x_lat, norm_w, mod, w_gla, w_conv, w_na)


_GLA_LEVELS = (32, 16, 8, 4, 2, 1)
_N_EXP_BLOCKS = 2 + len(_GLA_LEVELS)
_N_MASKS = len(_GLA_LEVELS) + 1


def _gla_constants():
    cs = GLA_CHUNK
    r = np.arange(cs)
    i = r[:, None]
    c = r[None, :]
    w = np.zeros((2, _N_EXP_BLOCKS, cs, cs), np.float32)
    m = np.zeros((2, _N_MASKS, cs, cs), np.float32)
    w[0, 0] = c <= i
    w[0, 1] = c > i
    w[1, 0] = c >= i
    w[1, 1] = c < i
    for lv, half in enumerate(_GLA_LEVELS):
        mid = (r // (2 * half)) * (2 * half) + half
        mi = mid[:, None]
        second = (r >= mid)[:, None]
        same = (r[:, None] // (2 * half)) == (r[None, :] // (2 * half))
        w[0, 2 + lv] = np.where(second, (c >= mi) & (c <= i), (c > i) & (c <= mi - 1))
        w[1, 2 + lv] = np.where(second, (c >= mi) & (c <= i - 1), (c >= i) & (c <= mi - 1))
        m[0, lv] = same & (i >= mi) & (c < mi)
        m[1, lv] = same & (i < mi) & (c >= mi)
    m[:, _N_MASKS - 1] = np.eye(cs)
    w_all = w.reshape(2, _N_EXP_BLOCKS * cs, cs)
    lmask = np.tile(m, (1, 1, 1, GLA_HEADS))
    return jnp.asarray(w_all, _BF), jnp.asarray(lmask, _F32)


def _gla_kernel(g_ref, wz_ref, bz_ref, wall_ref, lmask_ref, onorm_ref, st0_ref,
                o_ref, stfin_ref, la_scr, o_scr, st_scr, *, n):
    cs = GLA_CHUNK
    nc = n // cs
    gate_rows = 128
    key_head = lax.broadcasted_iota(jnp.int32, (1, GLA_KEY), 1) // GLA_DK
    head_lanes = [(key_head == h).astype(_F32) for h in range(GLA_HEADS)]
    head_lanes_bf = [_bf(m) for m in head_lanes]
    zero_v = jnp.zeros((cs, GLA_DV), _BF)

    def log_decays(t, carry):
        rows = pl.ds(pl.multiple_of(t * gate_rows, gate_rows), gate_rows)
        z = _dot3(g_ref[rows, GLA_LR_OFF:GLA_IN_W], wz_ref[...]) + bz_ref[...]
        la_scr[rows, :] = (jnp.minimum(z, 0.0) - jnp.log1p(jnp.exp(-jnp.abs(z)))) * (1.0 / GLA_GATE_NORM)
        return carry

    lax.fori_loop(0, n // gate_rows, log_decays, 0)

    def chunk(c, d):
        rows = pl.ds(pl.multiple_of(c * cs, cs), cs)
        q = g_ref[rows, 0:GLA_KEY] * (GLA_DK ** -0.5)
        k = g_ref[rows, GLA_KEY:2 * GLA_KEY]
        v = g_ref[rows, 2 * GLA_KEY:2 * GLA_KEY + GLA_VAL]
        la_hi, la_lo = _split2(la_scr[rows, d * GLA_KEY:(d + 1) * GLA_KEY])
        w = wall_ref[d]
        f = jnp.exp(_dot(w, la_lo) + _dot(w, la_hi))
        st = st_scr[d]
        o = _dot_nt(_bf(q * f[0:cs]), _bf(st))
        p = jnp.zeros((cs, GLA_HEADS * cs), _F32)
        for lv in range(_N_MASKS):
            if lv < len(_GLA_LEVELS):
                fl = f[(2 + lv) * cs:(3 + lv) * cs]
                ql, kl = q * fl, k * fl
            else:
                ql, kl = q, k
            kl = _bf(kl)
            k_bd = jnp.concatenate([kl * m for m in head_lanes_bf], axis=0)
            p = p + lmask_ref[d, lv] * _dot_nt(_bf(ql), k_bd)
        vb = _bf(v)
        v_bd = jnp.concatenate(
            [jnp.concatenate([vb[:, g * GLA_DV:(g + 1) * GLA_DV] if g == h else zero_v for g in range(GLA_HEADS)], axis=1)
             for h in range(GLA_HEADS)], axis=0)
        o_scr[d, rows, :] = o + _dot(_bf(p), v_bd)
        decay = f[cs - 1:cs] if d == 0 else f[0:1]
        u_t = _dot_tn(vb, _bf(k * f[cs:2 * cs]))
        u_t = jnp.concatenate([u_t[h * GLA_DV:(h + 1) * GLA_DV] * head_lanes[h] for h in range(GLA_HEADS)], axis=0)
        st_scr[d] = st * decay + u_t

    st_scr[...] = st0_ref[0]

    def scan(i, carry):
        chunk(i, 0)
        chunk(nc - 1 - i, 1)
        return carry

    lax.fori_loop(0, nc, scan, 0)
    for d in range(2):
        for h in range(GLA_HEADS):
            stfin_ref[0, d, h] = st_scr[d, h * GLA_DV:(h + 1) * GLA_DV, h * GLA_DK:(h + 1) * GLA_DK]

    def finish(c, carry):
        rows = pl.ds(pl.multiple_of(c * cs, cs), cs)
        o = o_scr[1, rows, :] + o_scr[0, rows, :]
        for h in range(GLA_HEADS):
            cols = slice(h * GLA_DV, (h + 1) * GLA_DV)
            gate = g_ref[rows, 2 * GLA_KEY + GLA_VAL + h * GLA_DV:2 * GLA_KEY + GLA_VAL + (h + 1) * GLA_DV]
            o_ref[rows, cols] = _rms(o[:, cols], onorm_ref[...]) * _silu(gate)
        return carry

    lax.fori_loop(0, nc, finish, 0)


def _gla(g_all, wz, bz, consts, onorm, st0, *, n, n_seq, first_block):
    w_all, lmask = consts
    full2 = lambda i: (0, 0)
    return pl.pallas_call(
        functools.partial(_gla_kernel, n=n),
        out_shape=(
            jax.ShapeDtypeStruct((n_seq * n, GLA_VAL), _F32),
            jax.ShapeDtypeStruct((n_seq, 2, GLA_HEADS, GLA_DV, GLA_DK), _F32),
        ),
        grid=(n_seq,),
        in_specs=[
            pl.BlockSpec((n, GLA_IN_W), lambda i: (first_block + i, 0)),
            pl.BlockSpec(wz.shape, full2),
            pl.BlockSpec(bz.shape, full2),
            pl.BlockSpec(w_all.shape, lambda i: (0, 0, 0)),
            pl.BlockSpec(lmask.shape, lambda i: (0, 0, 0, 0)),
            pl.BlockSpec((1, GLA_DV), full2),
            pl.BlockSpec((1, 2, GLA_VAL, GLA_KEY), lambda i: (i, 0, 0, 0)),
        ],
        out_specs=(
            pl.BlockSpec((n, GLA_VAL), lambda i: (i, 0)),
            pl.BlockSpec((1, 2, GLA_HEADS, GLA_DV, GLA_DK), lambda i: (i, 0, 0, 0, 0)),
        ),
        scratch_shapes=[pltpu.VMEM((n, 2 * GLA_KEY), _F32), pltpu.VMEM((2, n, GLA_VAL), _F32),
                        pltpu.VMEM((2, GLA_VAL, GLA_KEY), _F32)],
        compiler_params=_params(("arbitrary",)),
        name=f"gla_{n}",
    )(g_all, wz, bz, w_all, lmask, onorm, st0)


def _state_to_blockdiag(s):
    b = s.shape[0]
    st = jnp.swapaxes(s, -1, -2)
    eye = jnp.eye(GLA_HEADS, dtype=s.dtype)
    out = st[:, :, :, :, None, :] * eye[None, None, :, None, :, None]
    return out.reshape(b, 2, GLA_VAL, GLA_KEY)


_CONV_PAD = 16
_CONV_ROWS = 128


def _conv_kernel(c_ref, dw_ref, dwb_ref, lng_ref, lnb_ref, pw_ref, o_ref, pad_scr, *, n):
    zeros = jnp.zeros((_CONV_PAD, CONV_CH), _F32)
    pad_scr[0:_CONV_PAD, :] = zeros
    pad_scr[_CONV_PAD + n:2 * _CONV_PAD + n, :] = zeros
    for r0 in range(0, n, _CONV_ROWS):
        a = c_ref[r0:r0 + _CONV_ROWS, :]
        pad_scr[_CONV_PAD + r0:_CONV_PAD + r0 + _CONV_ROWS, :] = a[:, :CONV_CH] * _sigmoid(a[:, CONV_CH:])
    half = CONV_WIDTH // 2
    for r0 in range(0, n, _CONV_ROWS):
        acc = jnp.zeros((_CONV_ROWS // 8, 8, CONV_CH), _F32)
        for w in range(CONV_WIDTH):
            s = _CONV_PAD + r0 + w - half
            acc = acc + pad_scr[s:s + _CONV_ROWS, :].reshape(_CONV_ROWS // 8, 8, CONV_CH) * dw_ref[w][None]
        acc = acc.reshape(_CONV_ROWS, CONV_CH) + dwb_ref[...]
        xc = acc - jnp.mean(acc, axis=-1, keepdims=True)
        y = xc * lax.rsqrt(jnp.mean(xc * xc, axis=-1, keepdims=True) + EPS) * lng_ref[...] + lnb_ref[...]
        o_ref[r0:r0 + _CONV_ROWS, :] = _dot(_bf(_silu(y)), pw_ref[...])


def _conv(c_all, dw, dwb, lng, lnb, pw, *, n, n_seq, first_block):
    full2 = lambda i: (0, 0)
    return pl.pallas_call(
        functools.partial(_conv_kernel, n=n),
        out_shape=jax.ShapeDtypeStruct((n_seq * n, CONV_CH), _F32),
        grid=(n_seq,),
        in_specs=[
            pl.BlockSpec((n, 2 * CONV_CH), lambda i: (first_block + i, 0)),
            pl.BlockSpec((CONV_WIDTH, 8, CONV_CH), lambda i: (0, 0, 0)),
            pl.BlockSpec((1, CONV_CH), full2),
            pl.BlockSpec((1, CONV_CH), full2),
            pl.BlockSpec((1, CONV_CH), full2),
            pl.BlockSpec((CONV_CH, CONV_CH), full2),
        ],
        out_specs=pl.BlockSpec((n, CONV_CH), lambda i: (i, 0)),
        scratch_shapes=[pltpu.VMEM((n + 2 * _CONV_PAD, CONV_CH), _F32)],
        compiler_params=_params(("arbitrary",)),
        name=f"conv_{n}",
    )(c_all, dw, dwb, lng, lnb, pw)


def _softmax_rows(s):
    e = jnp.exp(s - jnp.max(s, axis=-1, keepdims=True))
    return e / jnp.sum(e, axis=-1, keepdims=True)


def _na_ctx_kernel(a_ref, qn_ref, kn_ref, o_ref, kc_ref, vc_ref):
    for h in range(NA_HEADS):
        cols = slice(h * NA_HD, (h + 1) * NA_HD)
        q = _rms(a_ref[:, h * NA_HD:(h + 1) * NA_HD], qn_ref[...])
        k = _rms(a_ref[:, NA_DIM + h * NA_HD:NA_DIM + (h + 1) * NA_HD], kn_ref[...])
        v = a_ref[:, 2 * NA_DIM + h * NA_HD:2 * NA_DIM + (h + 1) * NA_HD]
        kc_ref[0, h] = k
        vc_ref[0, h] = v
        p = _softmax_rows(_dot_nt(_bf(q), _bf(k)) * (NA_HD ** -0.5))
        o_ref[:, cols] = _dot(_bf(p), _bf(v))


def _na_ctx(a_all, qn, kn):
    full2 = lambda i: (0, 0)
    cache = jax.ShapeDtypeStruct((BATCH, NA_HEADS, SEQ, NA_HD), _F32)
    cache_spec = pl.BlockSpec((1, NA_HEADS, SEQ, NA_HD), lambda i: (i, 0, 0, 0))
    return pl.pallas_call(
        _na_ctx_kernel,
        out_shape=(jax.ShapeDtypeStruct((N_CTX_TOK, NA_DIM), _F32), cache, cache),
        grid=(BATCH,),
        in_specs=[
            pl.BlockSpec((SEQ, 3 * NA_DIM), lambda i: (i, 0)),
            pl.BlockSpec((1, NA_HD), full2),
            pl.BlockSpec((1, NA_HD), full2),
        ],
        out_specs=(pl.BlockSpec((SEQ, NA_DIM), lambda i: (i, 0)), cache_spec, cache_spec),
        compiler_params=_params(("arbitrary",)),
        name="na_ctx",
    )(a_all, qn, kn)


_NA_ROWS = DEC_SEQ // GRID_W
_NA_KEYS = NA_WIN_ROWS * GRID_W
_NA_VARIANTS = NA_WIN_ROWS


_RPB_ROWS = 2 * NA_WIN_ROWS - 1
_RPB_COLS = 2 * NA_WIN_COLS - 1


def _na_bias_constants():
    qc = np.arange(GRID_W)[:, None]
    kc = np.arange(GRID_W)[None, :]
    shift = np.stack([(kc - qc + NA_WIN_COLS - 1) == co for co in range(_RPB_COLS)]).astype(np.float32)
    win_start = np.clip(qc - NA_WIN_COLS // 2, 0, GRID_W - NA_WIN_COLS)
    in_win = ((kc >= win_start) & (kc < win_start + NA_WIN_COLS)).astype(np.float32)
    return jnp.asarray(shift), jnp.asarray(in_win)


def _na_bias_kernel(rpb_ref, shift_ref, win_ref, o_ref):
    h = pl.program_id(0)
    in_win = win_ref[...] > 0.0
    tiles = []
    for ro in range(_RPB_ROWS):
        base = (h * _RPB_ROWS + ro) * _RPB_COLS
        acc = rpb_ref[base] * shift_ref[0]
        for co in range(1, _RPB_COLS):
            acc = acc + rpb_ref[base + co] * shift_ref[co]
        tiles.append(jnp.where(in_win, acc, -jnp.inf))
    for t in range(_NA_VARIANTS):
        o_ref[0, t] = jnp.concatenate([tiles[kr - t + NA_WIN_ROWS - 1] for kr in range(NA_WIN_ROWS)], axis=1)


def _na_bias_table(rpb):
    shift, in_win = _na_bias_constants()
    return pl.pallas_call(
        _na_bias_kernel,
        out_shape=jax.ShapeDtypeStruct((NA_HEADS, _NA_VARIANTS, GRID_W, _NA_KEYS), _F32),
        grid=(NA_HEADS,),
        in_specs=[
            pl.BlockSpec(memory_space=pltpu.SMEM),
            pl.BlockSpec(shift.shape, lambda h: (0, 0, 0)),
            pl.BlockSpec(in_win.shape, lambda h: (0, 0)),
        ],
        out_specs=pl.BlockSpec((1, _NA_VARIANTS, GRID_W, _NA_KEYS), lambda h: (h, 0, 0, 0)),
        compiler_params=_params(("arbitrary",)),
        name="na_bias",
    )(rpb.reshape(-1), shift, in_win)


def _na_row_groups():
    start = lambda r: min(max(r - NA_WIN_ROWS // 2, 0), _NA_ROWS - NA_WIN_ROWS)
    groups, r = [], 0
    while r < _NA_ROWS:
        n = 1
        while r + n < _NA_ROWS and start(r + n) == start(r):
            n += 1
        groups.append((r, n, start(r)))
        r += n
    return groups


def _na_lat_kernel(a_ref, kctx_ref, vctx_ref, bias_ref, qn_ref, kn_ref, o_ref,
                   q_scr, k_scr, v_scr, sctx_scr, pctx_scr):
    scale = NA_HD ** -0.5
    for h in range(NA_HEADS):
        cols = slice(h * NA_HD, (h + 1) * NA_HD)
        q_scr[...] = _bf(_rms(a_ref[:, h * NA_HD:(h + 1) * NA_HD], qn_ref[...]))
        k_scr[...] = _bf(_rms(a_ref[:, NA_DIM + h * NA_HD:NA_DIM + (h + 1) * NA_HD], kn_ref[...]))
        v_scr[...] = _bf(a_ref[:, 2 * NA_DIM + h * NA_HD:2 * NA_DIM + (h + 1) * NA_HD])
        sctx_scr[...] = _dot_nt(q_scr[...], _bf(kctx_ref[0, h])) * scale
        for r0, nr, ks in _na_row_groups():
            qrows = slice(r0 * GRID_W, (r0 + nr) * GRID_W)
            krows = slice(ks * GRID_W, ks * GRID_W + _NA_KEYS)
            bias = jnp.concatenate([bias_ref[h, r - ks] for r in range(r0, r0 + nr)], axis=0)
            s_loc = _dot_nt(q_scr[qrows, :], k_scr[krows, :]) * scale + bias
            s_ctx = sctx_scr[qrows, :]
            m = jnp.maximum(jnp.max(s_loc, axis=-1, keepdims=True), jnp.max(s_ctx, axis=-1, keepdims=True))
            e_loc = jnp.exp(s_loc - m)
            e_ctx = jnp.exp(s_ctx - m)
            inv = 1.0 / (jnp.sum(e_loc, axis=-1, keepdims=True) + jnp.sum(e_ctx, axis=-1, keepdims=True))
            pctx_scr[qrows, :] = _bf(e_ctx * inv)
            o_ref[qrows, cols] = _dot(_bf(e_loc * inv), v_scr[krows, :])
        o_ref[:, cols] = o_ref[:, cols] + _dot(pctx_scr[...], _bf(vctx_ref[0, h]))


def _na_lat(a_all, k_ctx, v_ctx, bias, qn, kn):
    full2 = lambda i: (0, 0)
    ctx_spec = pl.BlockSpec((1, NA_HEADS, PAST_LEN, NA_HD), lambda i: (i, 0, 0, 0))
    return pl.pallas_call(
        _na_lat_kernel,
        out_shape=jax.ShapeDtypeStruct((N_LAT_TOK, NA_DIM), _F32),
        grid=(DEC_BATCH,),
        in_specs=[
            pl.BlockSpec((DEC_SEQ, 3 * NA_DIM), lambda i: (N_CTX_TOK // DEC_SEQ + i, 0)),
            ctx_spec,
            ctx_spec,
            pl.BlockSpec(bias.shape, lambda i: (0, 0, 0, 0)),
            pl.BlockSpec((1, NA_HD), full2),
            pl.BlockSpec((1, NA_HD), full2),
        ],
        out_specs=pl.BlockSpec((DEC_SEQ, NA_DIM), lambda i: (i, 0)),
        scratch_shapes=[pltpu.VMEM((DEC_SEQ, NA_HD), _BF)] * 3
        + [pltpu.VMEM((DEC_SEQ, PAST_LEN), _F32), pltpu.VMEM((DEC_SEQ, PAST_LEN), _BF)],
        compiler_params=_params(("arbitrary",)),
        name="na_lat",
    )(a_all, k_ctx, v_ctx, bias, qn, kn)


def _outproj_kernel(gc_ref, gl_ref, cc_ref, cl_ref, nc_ref, nl_ref, xc_ref, xl_ref, mod_ref, nw_ref, wo_ref,
                    rw_ref, x1_ref, h2_ref, hr_ref, lg_ref):
    i = pl.program_id(0)
    is_ctx = i < N_CTX_TILES
    row = _mod_row(i)
    gla = jnp.where(is_ctx, gc_ref[...], gl_ref[...])
    conv = jnp.where(is_ctx, cc_ref[...], cl_ref[...])
    na = jnp.where(is_ctx, nc_ref[...], nl_ref[...])
    mix = (_dot(_bf(gla), wo_ref[0:GLA_VAL, :])
           + _dot(_bf(conv), wo_ref[GLA_VAL:GLA_VAL + CONV_CH, :])
           + _dot(_bf(na), wo_ref[GLA_VAL + CONV_CH:, :]))
    g1 = mod_ref[pl.ds(row, 1), 2 * D_MODEL:3 * D_MODEL]
    sh2 = mod_ref[pl.ds(row, 1), 3 * D_MODEL:4 * D_MODEL]
    sc2 = mod_ref[pl.ds(row, 1), 4 * D_MODEL:5 * D_MODEL]
    x1 = jnp.where(is_ctx, xc_ref[...], xl_ref[...]) + g1 * mix
    h2 = _rms(x1, nw_ref[...]) * (1.0 + sc2) + sh2
    x1_ref[...] = x1
    h2_ref[...] = _bf(h2)
    for j in range(_ROW_CHUNKS):
        hr_ref[pl.ds(j, ROW_TILE, stride=_ROW_CHUNKS), :] = h2[:, j * _LANE:(j + 1) * _LANE]
    lg_ref[...] = _dot3(h2, rw_ref[...])


def _out_proj(gla_c, gla_l, conv_c, conv_l, na_c, na_l, x_ctx, x_lat, lat_first_tile, mod, norm_w, w_out,
              router_w):
    full2 = lambda i: (0, 0)
    tile = lambda i: (i, 0)
    ctx_tile, lat_tile = _stream_tiles(0)
    _, x_lat_tile = _stream_tiles(lat_first_tile)
    return pl.pallas_call(
        _outproj_kernel,
        out_shape=(
            jax.ShapeDtypeStruct((N_TOK, D_MODEL), _F32),
            jax.ShapeDtypeStruct((N_TOK, D_MODEL), _BF),
            jax.ShapeDtypeStruct((N_TOK * _ROW_CHUNKS, _LANE), _F32),
            jax.ShapeDtypeStruct((N_TOK, N_EXPERTS), _F32),
        ),
        grid=(N_TOK // ROW_TILE,),
        in_specs=[
            pl.BlockSpec((ROW_TILE, GLA_VAL), ctx_tile),
            pl.BlockSpec((ROW_TILE, GLA_VAL), lat_tile),
            pl.BlockSpec((ROW_TILE, CONV_CH), ctx_tile),
            pl.BlockSpec((ROW_TILE, CONV_CH), lat_tile),
            pl.BlockSpec((ROW_TILE, NA_DIM), ctx_tile),
            pl.BlockSpec((ROW_TILE, NA_DIM), lat_tile),
            pl.BlockSpec((ROW_TILE, D_MODEL), ctx_tile),
            pl.BlockSpec((ROW_TILE, D_MODEL), x_lat_tile),
            pl.BlockSpec((MOD_ROWS, 6 * D_MODEL), full2),
            pl.BlockSpec((1, D_MODEL), full2),
            pl.BlockSpec((D_MODEL, D_MODEL), full2),
            pl.BlockSpec((D_MODEL, N_EXPERTS), full2),
        ],
        out_specs=(
            pl.BlockSpec((ROW_TILE, D_MODEL), tile),
            pl.BlockSpec((ROW_TILE, D_MODEL), tile),
            pl.BlockSpec((ROW_TILE * _ROW_CHUNKS, _LANE), tile),
            pl.BlockSpec((ROW_TILE, N_EXPERTS), tile),
        ),
        compiler_params=_params(("arbitrary",)),
        name="out_proj",
    )(gla_c, gla_l, conv_c, conv_l, na_c, na_l, x_ctx, x_lat, mod, norm_w, w_out, router_w)


_PER_GROUP = N_EXPERTS // N_GROUPS
_ROUTE_TILE = 1024


def _first_max(x, idx, axes, sentinel):
    m = x
    for ax in axes:
        m = jnp.max(m, axis=ax, keepdims=True)
    first = jnp.where(x == m, idx, sentinel)
    for ax in axes:
        first = jnp.min(first, axis=ax, keepdims=True)
    return m, first


def _router_kernel(lg_ref, rb_ref, eid_ref, wts_ref, cnt_ref):
    t = lg_ref.shape[1]
    shape3 = (N_GROUPS, _PER_GROUP, t)
    scores = _sigmoid(lg_ref[...])
    biased = (scores + rb_ref[...]).reshape(shape3)
    scores = scores.reshape(shape3)
    neg = -jnp.inf
    in_grp = lax.broadcasted_iota(jnp.int32, shape3, 1)
    grp = lax.broadcasted_iota(jnp.int32, (N_GROUPS, 1, t), 0)
    expert = lax.broadcasted_iota(jnp.int32, shape3, 0) * _PER_GROUP + in_grp
    m1, i1 = _first_max(biased, in_grp, (1,), _PER_GROUP)
    m2 = jnp.max(jnp.where(in_grp == i1, neg, biased), axis=1, keepdims=True)
    gscore = m1 + m2
    keep = jnp.zeros((N_GROUPS, 1, t), _F32)
    for _ in range(TOPK_GROUPS):
        _, gi = _first_max(gscore, grp, (0,), N_GROUPS)
        hit = grp == gi
        keep = jnp.where(hit, 1.0, keep)
        gscore = jnp.where(hit, neg, gscore)
    masked = jnp.where(keep > 0.0, biased, neg)
    ids, picked = [], []
    hits = jnp.zeros(shape3, _F32)
    for _ in range(TOP_K):
        _, ei = _first_max(masked, expert, (1, 0), N_EXPERTS)
        hit = expert == ei
        hits = jnp.where(hit, 1.0, hits)
        sc = jnp.sum(jnp.sum(jnp.where(hit, scores, 0.0), axis=1, keepdims=True), axis=0, keepdims=True)
        ids.append(ei.reshape(1, t))
        picked.append(sc.reshape(1, t))
        masked = jnp.where(hit, neg, masked)
    picked = jnp.concatenate(picked, axis=0)
    den = jnp.sum(picked, axis=0, keepdims=True)
    eid_ref[...] = jnp.concatenate(ids, axis=0)
    wts_ref[...] = picked / den * ROUTED_SCALE
    cnt_ref[0] = jnp.sum(hits, axis=2, keepdims=True).reshape(N_EXPERTS, 1)


def _router(logits_t, router_bias):
    tile = lambda i: (0, i)
    return pl.pallas_call(
        _router_kernel,
        out_shape=(jax.ShapeDtypeStruct((TOP_K, N_TOK), jnp.int32), jax.ShapeDtypeStruct((TOP_K, N_TOK), _F32),
                   jax.ShapeDtypeStruct((N_TOK // _ROUTE_TILE, N_EXPERTS, 1), _F32)),
        grid=(N_TOK // _ROUTE_TILE,),
        in_specs=[
            pl.BlockSpec((N_EXPERTS, _ROUTE_TILE), tile),
            pl.BlockSpec((N_EXPERTS, 1), lambda i: (0, 0)),
        ],
        out_specs=(pl.BlockSpec((TOP_K, _ROUTE_TILE), tile), pl.BlockSpec((TOP_K, _ROUTE_TILE), tile),
                   pl.BlockSpec((1, N_EXPERTS, 1), lambda i: (i, 0, 0))),
        compiler_params=_params(("arbitrary",)),
        name="router",
    )(logits_t, router_bias)


_HALF_TOK = N_CTX_TOK
_N_HALVES = N_TOK // _HALF_TOK
_HALF_ASSIGN = _HALF_TOK * TOP_K
_GB = 576
_GB_MAX = _HALF_ASSIGN // _GB + N_EXPERTS
_LANE = 128
_ROW_CHUNKS = D_MODEL // _LANE
_GS = _GB + 1
_STAGE_ROWS = (_ROW_CHUNKS * _GS + 7) // 8 * 8
_RMW_BATCH = 8
_TOK_BITS = 12
assert _HALF_TOK == 1 << _TOK_BITS
_FFN_SPLIT = 1


def _moe_plan(eid, wts, tile_counts):
    tok = jnp.arange(N_TOK, dtype=jnp.int32)
    key = (((tok >> _TOK_BITS) * N_EXPERTS)[None, :] + eid) * _HALF_TOK + (tok & (_HALF_TOK - 1))[None, :]
    key_s, gate_s = lax.sort((key.reshape(-1), wts.reshape(-1)), num_keys=1)
    row_off = (key_s & (_HALF_TOK - 1)) * _ROW_CHUNKS
    count = jnp.sum(tile_counts.reshape(_N_HALVES, -1, N_EXPERTS), axis=1).astype(jnp.int32)
    row_end = jnp.cumsum(count.reshape(-1)).reshape(_N_HALVES, N_EXPERTS)
    row_start = row_end - count
    nblk = (count + _GB - 1) // _GB
    blk_end = jnp.cumsum(nblk, axis=1)
    blk_start = blk_end - nblk
    n_used = blk_end[:, -1]
    b = jnp.minimum(jnp.arange(_GB_MAX, dtype=jnp.int32)[None, :], n_used[:, None] - 1)
    grp = jnp.sum((b[:, :, None] >= blk_end[:, None, :]).astype(jnp.int32), axis=-1)
    onehot = (grp[:, :, None] == jnp.arange(N_EXPERTS, dtype=jnp.int32)).astype(jnp.int32)
    pick = lambda v: jnp.sum(onehot * v[:, None, :], axis=-1)
    within = b - pick(blk_start)
    start = pick(row_start) + within * _GB
    length = jnp.clip(pick(count) - within * _GB, 0, _GB)
    pad = jnp.zeros((_GB,), jnp.int32)
    return (jnp.concatenate([row_off, pad]), jnp.concatenate([gate_s, pad.astype(_F32)]),
            grp.astype(jnp.int32), start.astype(jnp.int32), length.astype(jnp.int32), n_used.astype(jnp.int32))


def _gmm_kernel(be_ref, bstart_ref, blen_ref, nused_ref, off_ref, gate_ref,
                h_ref, wg_ref, wu_ref, wd_ref, acc_ref, xt_scr, yt_scr, wgb_scr, wub_scr, wdb_scr):
    s = pl.program_id(0)
    n_used = nused_ref[0]
    last = _GB_MAX - 1

    @pl.when(s == 0)
    def _():
        acc_ref[...] = jnp.zeros_like(acc_ref)
        xt_scr[...] = jnp.zeros_like(xt_scr)
        yt_scr[...] = jnp.zeros_like(yt_scr)

    fb = jnp.clip(s - 1, 0, last)

    @pl.when((s == 0) | (be_ref[fb] != be_ref[jnp.clip(s - 2, 0, last)]))
    def _():
        wgb_scr[...] = _bf(wg_ref[0, 0])
        wub_scr[...] = _bf(wu_ref[0, 0])
        wdb_scr[...] = _bf(wd_ref[0, 0])

    @pl.when(s < n_used + 2)
    def _():
        slot = s % 2
        other = 1 - slot

        cb = jnp.clip(s - 2, 0, last)
        c_start = bstart_ref[cb]
        c_len = blen_ref[cb]
        for m0 in range(0, _GB, _RMW_BATCH):
            pending = []
            for mi in range(m0, m0 + _RMW_BATCH):
                valid = mi < c_len
                row = jnp.where(valid, off_ref[c_start + mi], _HALF_TOK * _ROW_CHUNKS)
                gate = jnp.where(valid, gate_ref[c_start + mi], 0.0)
                rows = pl.ds(pl.multiple_of(row, _ROW_CHUNKS), _ROW_CHUNKS)
                contrib = gate * yt_scr[slot, pl.ds(mi, _ROW_CHUNKS, stride=_GS), :]
                pending.append((rows, acc_ref[rows, :] + contrib))
            for rows, val in pending:
                acc_ref[rows, :] = val

        part = _GB // _FFN_SPLIT
        for p in range(_FFN_SPLIT):
            x = _bf(jnp.concatenate([xt_scr[other, j * _GS + p * part:j * _GS + (p + 1) * part, :]
                                     for j in range(_ROW_CHUNKS)], axis=-1))
            act = _silu(_dot(x, wgb_scr[...])) * _dot(x, wub_scr[...])
            y = _dot(_bf(act), wdb_scr[...])
            for j in range(_ROW_CHUNKS):
                yt_scr[other, j * _GS + p * part:j * _GS + (p + 1) * part, :] = y[:, j * _LANE:(j + 1) * _LANE]

        g_start = bstart_ref[jnp.minimum(s, last)]
        for mi in range(_GB):
            src = pl.ds(pl.multiple_of(off_ref[g_start + mi], _ROW_CHUNKS), _ROW_CHUNKS)
            xt_scr[slot, pl.ds(mi, _ROW_CHUNKS, stride=_GS), :] = h_ref[src, :]


def _gmm(layer, half, plan, h_rows, wg, wu, wd):
    row_off, gate_s, grp, start, length, n_used = plan
    expert = lambda s, be, *_: (layer, be[jnp.clip(s - 1, 0, _GB_MAX - 1)], 0, 0)
    acc_rows = (_HALF_TOK + 1) * _ROW_CHUNKS
    grid_spec = pltpu.PrefetchScalarGridSpec(
        num_scalar_prefetch=6,
        grid=(_GB_MAX + 2,),
        in_specs=[
            pl.BlockSpec((_HALF_TOK * _ROW_CHUNKS, _LANE), lambda b, *_: (half, 0), pipeline_mode=pl.Buffered(1)),
            pl.BlockSpec((1, 1, D_MODEL, EXPERT_DIM), expert),
            pl.BlockSpec((1, 1, D_MODEL, EXPERT_DIM), expert),
            pl.BlockSpec((1, 1, EXPERT_DIM, D_MODEL), expert),
        ],
        out_specs=pl.BlockSpec((acc_rows, _LANE), lambda b, *_: (0, 0), pipeline_mode=pl.Buffered(1)),
        scratch_shapes=[pltpu.VMEM((2, _STAGE_ROWS, _LANE), _F32)] * 2
        + [pltpu.VMEM((D_MODEL, EXPERT_DIM), _BF)] * 2 + [pltpu.VMEM((EXPERT_DIM, D_MODEL), _BF)],
    )
    acc = pl.pallas_call(
        _gmm_kernel,
        out_shape=jax.ShapeDtypeStruct((acc_rows, _LANE), _F32),
        grid_spec=grid_spec,
        compiler_params=_params(("arbitrary",)),
        name="moe_experts",
    )(grp[half], start[half], length[half], n_used[half:half + 1], row_off, gate_s, h_rows, wg, wu, wd)
    return acc


_FIN_TILE = 512


def _moe_finish_kernel(h_ref, rc_ref, rl_ref, x1_ref, mod_ref, sg_ref, su_ref, sd_ref, o_ref):
    t = pl.program_id(0)
    tiles_per_half = _HALF_TOK // _FIN_TILE
    is_ctx = t < tiles_per_half
    row = jnp.where(is_ctx, 0, 1 + (t - tiles_per_half) // (DEC_SEQ // _FIN_TILE))
    h = h_ref[...]
    shared = _dot(_bf(_silu(_dot(h, _bf(sg_ref[0]))) * _dot(h, _bf(su_ref[0]))), _bf(sd_ref[0]))
    routed = jnp.concatenate(
        [jnp.where(is_ctx, rc_ref[pl.ds(j, _FIN_TILE, stride=_ROW_CHUNKS), :],
                   rl_ref[pl.ds(j, _FIN_TILE, stride=_ROW_CHUNKS), :]) for j in range(_ROW_CHUNKS)], axis=-1)
    g2 = mod_ref[pl.ds(row, 1), 5 * D_MODEL:6 * D_MODEL]
    o_ref[...] = x1_ref[...] + g2 * (routed + shared)


def _moe_finish(layer, h2, routed_c, routed_l, x1, mod, sg, su, sd):
    tile = lambda t: (t, 0)
    tiles_per_half = _HALF_TOK // _FIN_TILE
    ctx_tile = lambda t: (jnp.minimum(t, tiles_per_half - 1), 0)
    lat_tile = lambda t: (jnp.maximum(t - tiles_per_half, 0), 0)
    shared = lambda t: (layer, 0, 0)
    return pl.pallas_call(
        _moe_finish_kernel,
        out_shape=jax.ShapeDtypeStruct((N_TOK, D_MODEL), _F32),
        grid=(N_TOK // _FIN_TILE,),
        in_specs=[
            pl.BlockSpec((_FIN_TILE, D_MODEL), tile),
            pl.BlockSpec((_FIN_TILE * _ROW_CHUNKS, _LANE), ctx_tile),
            pl.BlockSpec((_FIN_TILE * _ROW_CHUNKS, _LANE), lat_tile),
            pl.BlockSpec((_FIN_TILE, D_MODEL), tile),
            pl.BlockSpec((MOD_ROWS, 6 * D_MODEL), lambda t: (0, 0)),
            pl.BlockSpec((1, D_MODEL, SHARED_DIM), shared),
            pl.BlockSpec((1, D_MODEL, SHARED_DIM), shared),
            pl.BlockSpec((1, SHARED_DIM, D_MODEL), shared),
        ],
        out_specs=pl.BlockSpec((_FIN_TILE, D_MODEL), tile),
        compiler_params=_params(("arbitrary",)),
        name="moe_finish",
    )(h2, routed_c, routed_l, x1, mod, sg, su, sd)


def kernel(x_prompt, x_sample, state_gla, cache_na_k, cache_na_v, c, c_ctx, w_ada, b_ada, norm_mix, norm_ffn, w_in, gla_w_gate, gla_b_gate, gla_out_norm, conv_dw, conv_dw_b, conv_ln_g, conv_ln_b, conv_pw, na_q_norm, na_k_norm, na_rpb, w_out, router_w, router_bias, exp_w_gate, exp_w_up, exp_w_down, sh_w_gate, sh_w_up, sh_w_down):
    x_ctx, x_lat, x_lat_tile = x_prompt.reshape(N_CTX_TOK, D_MODEL), x_sample.reshape(N_LAT_TOK, D_MODEL), 0
    cvec = jnp.concatenate([c_ctx[None], c, jnp.zeros((MOD_ROWS - 1 - DEC_BATCH, D_MODEL), _F32)], axis=0)
    mod_all = _ada_mod(cvec, w_ada, b_ada)
    gla_consts = _gla_constants()
    zero_state = jnp.zeros((BATCH, 2, GLA_VAL, GLA_KEY), _F32)
    lat_first = N_CTX_TOK // DEC_SEQ

    states, keys, vals = [], [], []
    for l in range(DEPTH):
        mod = mod_all[l]
        wi = w_in[l]
        w_gla = _bf(jnp.pad(wi[:, :GLA_LR_OFF + 2 * GLA_GATE_RANK], ((0, 0), (0, GLA_IN_W - GLA_LR_OFF - 2 * GLA_GATE_RANK))))
        conv_off = GLA_LR_OFF + 2 * GLA_GATE_RANK
        w_conv = _bf(wi[:, conv_off:conv_off + 2 * CONV_CH])
        w_na = _bf(wi[:, conv_off + 2 * CONV_CH:])
        g_all, c_all, a_all = _in_proj(x_ctx, x_lat, x_lat_tile, norm_mix[l][None], mod, w_gla, w_conv, w_na)

        wz = jnp.zeros((GLA_IN_W - GLA_LR_OFF, 2 * GLA_KEY), _F32)
        wz = wz.at[:GLA_GATE_RANK, :GLA_KEY].set(gla_w_gate[l, 0])
        wz = wz.at[GLA_GATE_RANK:2 * GLA_GATE_RANK, GLA_KEY:].set(gla_w_gate[l, 1])
        bz = gla_b_gate[l].reshape(1, 2 * GLA_KEY)
        onorm = gla_out_norm[l][None]
        gla_c, st_c = _gla(g_all, wz, bz, gla_consts, onorm, zero_state, n=SEQ, n_seq=BATCH, first_block=0)
        gla_l, _ = _gla(g_all, wz, bz, gla_consts, onorm, _state_to_blockdiag(state_gla[:, l]),
                        n=DEC_SEQ, n_seq=DEC_BATCH, first_block=lat_first)

        conv_args = (jnp.broadcast_to(conv_dw[l][:, None, :], (CONV_WIDTH, 8, CONV_CH)), conv_dw_b[l][None],
                     conv_ln_g[l][None], conv_ln_b[l][None], _bf(conv_pw[l]))
        conv_c = _conv(c_all, *conv_args, n=SEQ, n_seq=BATCH, first_block=0)
        conv_l = _conv(c_all, *conv_args, n=DEC_SEQ, n_seq=DEC_BATCH, first_block=lat_first)

        qn, kn = na_q_norm[l][None], na_k_norm[l][None]
        na_c, k_l, v_l = _na_ctx(a_all, qn, kn)
        na_l = _na_lat(a_all, cache_na_k[:, l], cache_na_v[:, l], _na_bias_table(na_rpb[l]), qn, kn)

        x1, h2, h_rows, logits = _out_proj(gla_c, gla_l, conv_c, conv_l, na_c, na_l, x_ctx, x_lat, x_lat_tile, mod,
                                     norm_ffn[l][None], _bf(w_out[l]), router_w[l])
        plan = _moe_plan(*_router(logits.T, router_bias[l][:, None]))
        routed = [_gmm(l, half, plan, h_rows, exp_w_gate, exp_w_up, exp_w_down) for half in range(_N_HALVES)]
        x = _moe_finish(l, h2, routed[0], routed[1], x1, mod, sh_w_gate, sh_w_up, sh_w_down)
        x_ctx, x_lat, x_lat_tile = x, x, N_CTX_TILES

        states.append(jnp.swapaxes(st_c, -1, -2))
        keys.append(k_l)
        vals.append(v_l)

    y_prompt = x[:N_CTX_TOK].reshape(BATCH, SEQ, D_MODEL)
    y_sample = x[N_CTX_TOK:].reshape(DEC_BATCH, DEC_SEQ, D_MODEL)
    return (y_prompt, y_sample, jnp.stack(states, axis=1), jnp.stack(keys, axis=1), jnp.stack(vals, axis=1))
```

```python
import functools

import numpy as np
import jax
import jax.numpy as jnp
from jax import lax
from jax.experimental import pallas as pl
from jax.experimental.pallas import tpu as pltpu

D_MODEL = 1024
BATCH = 16
SEQ = 256
DEPTH = 2
DEC_BATCH = 4
DEC_SEQ = 1024
PAST_LEN = 256
GRID_W = 64
GLA_HEADS = 4
GLA_DK = 64
GLA_DV = 128
GLA_KEY = GLA_HEADS * GLA_DK
GLA_VAL = GLA_HEADS * GLA_DV
GLA_GATE_RANK = 16
GLA_GATE_NORM = 16.0
GLA_CHUNK = 64
CONV_CH = 256
CONV_WIDTH = 31
NA_HEADS = 4
NA_HD = 64
NA_DIM = NA_HEADS * NA_HD
NA_WIN_ROWS = 8
NA_WIN_COLS = 16
N_EXPERTS = 64
TOP_K = 8
N_GROUPS = 8
TOPK_GROUPS = 4
EXPERT_DIM = 256
SHARED_DIM = 256
ROUTED_SCALE = 2.5
EPS = 1e-6

N_CTX_TOK = BATCH * SEQ
N_LAT_TOK = DEC_BATCH * DEC_SEQ
N_TOK = N_CTX_TOK + N_LAT_TOK
ROW_TILE = 512
N_CTX_TILES = N_CTX_TOK // ROW_TILE
TILES_PER_LAT_SEQ = DEC_SEQ // ROW_TILE
MOD_ROWS = 8
GLA_IN_W = 1664
GLA_LR_OFF = 2 * GLA_KEY + 2 * GLA_VAL
VMEM_LIMIT = 56 * 1024 * 1024

_BF = jnp.bfloat16
_F32 = jnp.float32


def _bf(x):
    return x.astype(_BF)


def _dot(a, b):
    return jnp.dot(a, b, preferred_element_type=_F32)


def _dot_nt(a, b):
    return lax.dot_general(a, b, (((1,), (1,)), ((), ())), preferred_element_type=_F32)


def _dot_tn(a, b):
    return lax.dot_general(a, b, (((0,), (0,)), ((), ())), preferred_element_type=_F32)


def _split2(x):
    hi = _bf(x)
    lo = _bf(x - hi.astype(_F32))
    return hi, lo


def _dot3(a, b, dot=_dot):
    a_hi, a_lo = _split2(a)
    b_hi, b_lo = _split2(b)
    return (dot(a_lo, b_hi) + dot(a_hi, b_lo)) + dot(a_hi, b_hi)


def _sigmoid(x):
    return 1.0 / (1.0 + jnp.exp(-x))


def _silu(x):
    return x * _sigmoid(x)


def _rms(x, w):
    return x * lax.rsqrt(jnp.mean(x * x, axis=-1, keepdims=True) + EPS) * w


def _params(sem):
    return pltpu.CompilerParams(dimension_semantics=sem, vmem_limit_bytes=VMEM_LIMIT)


def _mod_row(i):
    return jnp.where(i < N_CTX_TILES, 0, 1 + (i - N_CTX_TILES) // TILES_PER_LAT_SEQ)


def _ada_kernel(cv_ref, w_ref, b_ref, o_ref):
    o_ref[0] = _dot3(_silu(cv_ref[...]), w_ref[0]) + b_ref[0]


def _ada_mod(cvec, w_ada, b_ada):
    tn = 1024
    n_out = 6 * D_MODEL
    return pl.pallas_call(
        _ada_kernel,
        out_shape=jax.ShapeDtypeStruct((DEPTH, MOD_ROWS, n_out), _F32),
        grid=(DEPTH, n_out // tn),
        in_specs=[
            pl.BlockSpec((MOD_ROWS, D_MODEL), lambda l, j: (0, 0)),
            pl.BlockSpec((1, D_MODEL, tn), lambda l, j: (l, 0, j)),
            pl.BlockSpec((1, 1, tn), lambda l, j: (l, 0, j)),
        ],
        out_specs=pl.BlockSpec((1, MOD_ROWS, tn), lambda l, j: (l, 0, j)),
        compiler_params=_params(("arbitrary", "arbitrary")),
        name="ada_mod",
    )(cvec, w_ada, b_ada.reshape(DEPTH, 1, n_out))


def _inproj_kernel(xc_ref, xl_ref, nw_ref, mod_ref, wg_ref, wc_ref, wa_ref, g_ref, c_ref, a_ref):
    i = pl.program_id(0)
    row = _mod_row(i)
    sh = mod_ref[pl.ds(row, 1), 0:D_MODEL]
    sc = mod_ref[pl.ds(row, 1), D_MODEL:2 * D_MODEL]
    x = jnp.where(i < N_CTX_TILES, xc_ref[...], xl_ref[...])
    h = _bf(_rms(x, nw_ref[...]) * (1.0 + sc) + sh)
    g_ref[...] = _dot(h, wg_ref[...])
    c_ref[...] = _dot(h, wc_ref[...])
    a_ref[...] = _dot(h, wa_ref[...])


def _stream_tiles(lat_first_tile):
    ctx_tile = lambda i: (jnp.minimum(i, N_CTX_TILES - 1), 0)
    lat_tile = lambda i: (jnp.maximum(i - N_CTX_TILES, 0) + lat_first_tile, 0)
    return ctx_tile, lat_tile


def _in_proj(x_ctx, x_lat, lat_first_tile, norm_w, mod, w_gla, w_conv, w_na):
    full = lambda i: (0, 0)
    tile = lambda i: (i, 0)
    ctx_tile, lat_tile = _stream_tiles(lat_first_tile)
    return pl.pallas_call(
        _inproj_kernel,
        out_shape=(
            jax.ShapeDtypeStruct((N_TOK, GLA_IN_W), _F32),
            jax.ShapeDtypeStruct((N_TOK, 2 * CONV_CH), _F32),
            jax.ShapeDtypeStruct((N_TOK, 3 * NA_DIM), _F32),
        ),
        grid=(N_TOK // ROW_TILE,),
        in_specs=[
            pl.BlockSpec((ROW_TILE, D_MODEL), ctx_tile),
            pl.BlockSpec((ROW_TILE, D_MODEL), lat_tile),
            pl.BlockSpec((1, D_MODEL), full),
            pl.BlockSpec((MOD_ROWS, 6 * D_MODEL), full),
            pl.BlockSpec((D_MODEL, GLA_IN_W), full),
            pl.BlockSpec((D_MODEL, 2 * CONV_CH), full),
            pl.BlockSpec((D_MODEL, 3 * NA_DIM), full),
        ],
        out_specs=(
            pl.BlockSpec((ROW_TILE, GLA_IN_W), tile),
            pl.BlockSpec((ROW_TILE, 2 * CONV_CH), tile),
            pl.BlockSpec((ROW_TILE, 3 * NA_DIM), tile),
        ),
        compiler_params=_params(("arbitrary",)),
        name="in_proj",
    )(x_ctx, x_lat, norm_w, mod, w_gla, w_conv, w_na)


_GLA_LEVELS = (32, 16, 8, 4, 2, 1)
_GLA_CTX_SEQS = 2
_N_EXP_BLOCKS = 2 + len(_GLA_LEVELS)
_N_MASKS = len(_GLA_LEVELS) + 1


def _gla_constants():
    cs = GLA_CHUNK
    r = np.arange(cs)
    i = r[:, None]
    c = r[None, :]
    w = np.zeros((2, _N_EXP_BLOCKS, cs, cs), np.float32)
    m = np.zeros((2, _N_MASKS, cs, cs), np.float32)
    w[0, 0] = c <= i
    w[0, 1] = c > i
    w[1, 0] = c >= i
    w[1, 1] = c < i
    for lv, half in enumerate(_GLA_LEVELS):
        mid = (r // (2 * half)) * (2 * half) + half
        mi = mid[:, None]
        second = (r >= mid)[:, None]
        same = (r[:, None] // (2 * half)) == (r[None, :] // (2 * half))
        w[0, 2 + lv] = np.where(second, (c >= mi) & (c <= i), (c > i) & (c <= mi - 1))
        w[1, 2 + lv] = np.where(second, (c >= mi) & (c <= i - 1), (c >= i) & (c <= mi - 1))
        m[0, lv] = same & (i >= mi) & (c < mi)
        m[1, lv] = same & (i < mi) & (c >= mi)
    m[:, _N_MASKS - 1] = np.eye(cs)
    w_all = w.reshape(2, _N_EXP_BLOCKS * cs, cs)
    lmask = np.tile(m, (1, 1, 1, GLA_HEADS))
    return jnp.asarray(w_all, _BF), jnp.asarray(lmask, _F32)


def _gla_kernel(g_ref, wz_ref, bz_ref, wall_ref, lmask_ref, onorm_ref, st0_ref,
                o_ref, stfin_ref, la_scr, o_scr, st_scr, *, n, seqs):
    cs = GLA_CHUNK
    nc = n // cs
    gate_rows = 128
    key_head = lax.broadcasted_iota(jnp.int32, (1, GLA_KEY), 1) // GLA_DK
    head_lanes = [(key_head == h).astype(_F32) for h in range(GLA_HEADS)]
    head_lanes_bf = [_bf(m) for m in head_lanes]
    zero_v = jnp.zeros((cs, GLA_DV), _BF)

    def log_decays(t, carry):
        rows = pl.ds(pl.multiple_of(t * gate_rows, gate_rows), gate_rows)
        z = _dot3(g_ref[rows, GLA_LR_OFF:GLA_IN_W], wz_ref[...]) + bz_ref[...]
        la_scr[rows, :] = (jnp.minimum(z, 0.0) - jnp.log1p(jnp.exp(-jnp.abs(z)))) * (1.0 / GLA_GATE_NORM)
        return carry

    lax.fori_loop(0, seqs * n // gate_rows, log_decays, 0)

    def chunk(c, d, sq):
        rows = pl.ds(pl.multiple_of(sq * n + c * cs, cs), cs)
        q = g_ref[rows, 0:GLA_KEY] * (GLA_DK ** -0.5)
        k = g_ref[rows, GLA_KEY:2 * GLA_KEY]
        v = g_ref[rows, 2 * GLA_KEY:2 * GLA_KEY + GLA_VAL]
        la_hi, la_lo = _split2(la_scr[rows, d * GLA_KEY:(d + 1) * GLA_KEY])
        w = wall_ref[d]
        f = jnp.exp(_dot(w, la_lo) + _dot(w, la_hi))
        st = st_scr[sq, d]
        o = _dot_nt(_bf(q * f[0:cs]), _bf(st))
        p = jnp.zeros((cs, GLA_HEADS * cs), _F32)
        for lv in range(_N_MASKS):
            if lv < len(_GLA_LEVELS):
                fl = f[(2 + lv) * cs:(3 + lv) * cs]
                ql, kl = q * fl, k * fl
            else:
                ql, kl = q, k
            kl = _bf(kl)
            k_bd = jnp.concatenate([kl * m for m in head_lanes_bf], axis=0)
            p = p + lmask_ref[d, lv] * _dot_nt(_bf(ql), k_bd)
        vb = _bf(v)
        v_bd = jnp.concatenate(
            [jnp.concatenate([vb[:, g * GLA_DV:(g + 1) * GLA_DV] if g == h else zero_v for g in range(GLA_HEADS)], axis=1)
             for h in range(GLA_HEADS)], axis=0)
        o_scr[d, rows, :] = o + _dot(_bf(p), v_bd)
        decay = f[cs - 1:cs] if d == 0 else f[0:1]
        u_t = _dot_tn(vb, _bf(k * f[cs:2 * cs]))
        u_t = jnp.concatenate([u_t[h * GLA_DV:(h + 1) * GLA_DV] * head_lanes[h] for h in range(GLA_HEADS)], axis=0)
        st_scr[sq, d] = st * decay + u_t

    st_scr[...] = st0_ref[...]

    def scan(i, carry):
        for sq in range(seqs):
            chunk(i, 0, sq)
            chunk(nc - 1 - i, 1, sq)
        return carry

    lax.fori_loop(0, nc, scan, 0)
    for sq in range(seqs):
        for d in range(2):
            for h in range(GLA_HEADS):
                stfin_ref[sq, d, h] = st_scr[sq, d, h * GLA_DV:(h + 1) * GLA_DV, h * GLA_DK:(h + 1) * GLA_DK]

    def finish(c, carry):
        rows = pl.ds(pl.multiple_of(c * cs, cs), cs)
        o = o_scr[1, rows, :] + o_scr[0, rows, :]
        for h in range(GLA_HEADS):
            cols = slice(h * GLA_DV, (h + 1) * GLA_DV)
            gate = g_ref[rows, 2 * GLA_KEY + GLA_VAL + h * GLA_DV:2 * GLA_KEY + GLA_VAL + (h + 1) * GLA_DV]
            o_ref[rows, cols] = _rms(o[:, cols], onorm_ref[...]) * _silu(gate)
        return carry

    lax.fori_loop(0, seqs * nc, finish, 0)


def _gla(g_all, wz, bz, consts, onorm, st0, *, n, n_seq, first_block, seqs):
    w_all, lmask = consts
    full2 = lambda i: (0, 0)
    return pl.pallas_call(
        functools.partial(_gla_kernel, n=n, seqs=seqs),
        out_shape=(
            jax.ShapeDtypeStruct((n_seq * n, GLA_VAL), _F32),
            jax.ShapeDtypeStruct((n_seq, 2, GLA_HEADS, GLA_DV, GLA_DK), _F32),
        ),
        grid=(n_seq // seqs,),
        in_specs=[
            pl.BlockSpec((seqs * n, GLA_IN_W), lambda i: (first_block // seqs + i, 0)),
            pl.BlockSpec(wz.shape, full2),
            pl.BlockSpec(bz.shape, full2),
            pl.BlockSpec(w_all.shape, lambda i: (0, 0, 0)),
            pl.BlockSpec(lmask.shape, lambda i: (0, 0, 0, 0)),
            pl.BlockSpec((1, GLA_DV), full2),
            pl.BlockSpec((seqs, 2, GLA_VAL, GLA_KEY), lambda i: (i, 0, 0, 0)),
        ],
        out_specs=(
            pl.BlockSpec((seqs * n, GLA_VAL), lambda i: (i, 0)),
            pl.BlockSpec((seqs, 2, GLA_HEADS, GLA_DV, GLA_DK), lambda i: (i, 0, 0, 0, 0)),
        ),
        scratch_shapes=[pltpu.VMEM((seqs * n, 2 * GLA_KEY), _F32), pltpu.VMEM((2, seqs * n, GLA_VAL), _F32),
                        pltpu.VMEM((seqs, 2, GLA_VAL, GLA_KEY), _F32)],
        compiler_params=_params(("arbitrary",)),
        name=f"gla_{n}",
    )(g_all, wz, bz, w_all, lmask, onorm, st0)


def _state_to_blockdiag(s):
    b = s.shape[0]
    st = jnp.swapaxes(s, -1, -2)
    eye = jnp.eye(GLA_HEADS, dtype=s.dtype)
    out = st[:, :, :, :, None, :] * eye[None, None, :, None, :, None]
    return out.reshape(b, 2, GLA_VAL, GLA_KEY)


_CONV_PAD = 16
_CONV_ROWS = 128


def _conv_kernel(c_ref, dw_ref, dwb_ref, lng_ref, lnb_ref, pw_ref, o_ref, pad_scr, *, n):
    zeros = jnp.zeros((_CONV_PAD, CONV_CH), _F32)
    pad_scr[0:_CONV_PAD, :] = zeros
    pad_scr[_CONV_PAD + n:2 * _CONV_PAD + n, :] = zeros
    for r0 in range(0, n, _CONV_ROWS):
        a = c_ref[r0:r0 + _CONV_ROWS, :]
        pad_scr[_CONV_PAD + r0:_CONV_PAD + r0 + _CONV_ROWS, :] = a[:, :CONV_CH] * _sigmoid(a[:, CONV_CH:])
    half = CONV_WIDTH // 2
    for r0 in range(0, n, _CONV_ROWS):
        acc = jnp.zeros((_CONV_ROWS // 8, 8, CONV_CH), _F32)
        for w in range(CONV_WIDTH):
            s = _CONV_PAD + r0 + w - half
            acc = acc + pad_scr[s:s + _CONV_ROWS, :].reshape(_CONV_ROWS // 8, 8, CONV_CH) * dw_ref[w][None]
        acc = acc.reshape(_CONV_ROWS, CONV_CH) + dwb_ref[...]
        xc = acc - jnp.mean(acc, axis=-1, keepdims=True)
        y = xc * lax.rsqrt(jnp.mean(xc * xc, axis=-1, keepdims=True) + EPS) * lng_ref[...] + lnb_ref[...]
        o_ref[r0:r0 + _CONV_ROWS, :] = _dot(_bf(_silu(y)), pw_ref[...])


def _conv(c_all, dw, dwb, lng, lnb, pw, *, n, n_seq, first_block):
    full2 = lambda i: (0, 0)
    return pl.pallas_call(
        functools.partial(_conv_kernel, n=n),
        out_shape=jax.ShapeDtypeStruct((n_seq * n, CONV_CH), _F32),
        grid=(n_seq,),
        in_specs=[
            pl.BlockSpec((n, 2 * CONV_CH), lambda i: (first_block + i, 0)),
            pl.BlockSpec((CONV_WIDTH, 8, CONV_CH), lambda i: (0, 0, 0)),
            pl.BlockSpec((1, CONV_CH), full2),
            pl.BlockSpec((1, CONV_CH), full2),
            pl.BlockSpec((1, CONV_CH), full2),
            pl.BlockSpec((CONV_CH, CONV_CH), full2),
        ],
        out_specs=pl.BlockSpec((n, CONV_CH), lambda i: (i, 0)),
        scratch_shapes=[pltpu.VMEM((n + 2 * _CONV_PAD, CONV_CH), _F32)],
        compiler_params=_params(("arbitrary",)),
        name=f"conv_{n}",
    )(c_all, dw, dwb, lng, lnb, pw)


def _softmax_rows(s):
    e = jnp.exp(s - jnp.max(s, axis=-1, keepdims=True))
    return e / jnp.sum(e, axis=-1, keepdims=True)


def _na_ctx_kernel(a_ref, qn_ref, kn_ref, o_ref, kc_ref, vc_ref):
    for h in range(NA_HEADS):
        cols = slice(h * NA_HD, (h + 1) * NA_HD)
        q = _rms(a_ref[:, h * NA_HD:(h + 1) * NA_HD], qn_ref[...])
        k = _rms(a_ref[:, NA_DIM + h * NA_HD:NA_DIM + (h + 1) * NA_HD], kn_ref[...])
        v = a_ref[:, 2 * NA_DIM + h * NA_HD:2 * NA_DIM + (h + 1) * NA_HD]
        kc_ref[0, h] = k
        vc_ref[0, h] = v
        p = _softmax_rows(_dot_nt(_bf(q), _bf(k)) * (NA_HD ** -0.5))
        o_ref[:, cols] = _dot(_bf(p), _bf(v))


def _na_ctx(a_all, qn, kn):
    full2 = lambda i: (0, 0)
    cache = jax.ShapeDtypeStruct((BATCH, NA_HEADS, SEQ, NA_HD), _F32)
    cache_spec = pl.BlockSpec((1, NA_HEADS, SEQ, NA_HD), lambda i: (i, 0, 0, 0))
    return pl.pallas_call(
        _na_ctx_kernel,
        out_shape=(jax.ShapeDtypeStruct((N_CTX_TOK, NA_DIM), _F32), cache, cache),
        grid=(BATCH,),
        in_specs=[
            pl.BlockSpec((SEQ, 3 * NA_DIM), lambda i: (i, 0)),
            pl.BlockSpec((1, NA_HD), full2),
            pl.BlockSpec((1, NA_HD), full2),
        ],
        out_specs=(pl.BlockSpec((SEQ, NA_DIM), lambda i: (i, 0)), cache_spec, cache_spec),
        compiler_params=_params(("arbitrary",)),
        name="na_ctx",
    )(a_all, qn, kn)


_NA_ROWS = DEC_SEQ // GRID_W
_NA_KEYS = NA_WIN_ROWS * GRID_W
_NA_VARIANTS = NA_WIN_ROWS


_RPB_ROWS = 2 * NA_WIN_ROWS - 1
_RPB_COLS = 2 * NA_WIN_COLS - 1


def _na_bias_constants():
    qc = np.arange(GRID_W)[:, None]
    kc = np.arange(GRID_W)[None, :]
    shift = np.stack([(kc - qc + NA_WIN_COLS - 1) == co for co in range(_RPB_COLS)]).astype(np.float32)
    win_start = np.clip(qc - NA_WIN_COLS // 2, 0, GRID_W - NA_WIN_COLS)
    in_win = ((kc >= win_start) & (kc < win_start + NA_WIN_COLS)).astype(np.float32)
    return jnp.asarray(shift), jnp.asarray(in_win)


def _na_bias_kernel(rpb_ref, shift_ref, win_ref, o_ref):
    h = pl.program_id(0)
    in_win = win_ref[...] > 0.0
    tiles = []
    for ro in range(_RPB_ROWS):
        base = (h * _RPB_ROWS + ro) * _RPB_COLS
        acc = rpb_ref[base] * shift_ref[0]
        for co in range(1, _RPB_COLS):
            acc = acc + rpb_ref[base + co] * shift_ref[co]
        tiles.append(jnp.where(in_win, acc, -jnp.inf))
    for t in range(_NA_VARIANTS):
        o_ref[0, t] = jnp.concatenate([tiles[kr - t + NA_WIN_ROWS - 1] for kr in range(NA_WIN_ROWS)], axis=1)


def _na_bias_table(rpb):
    shift, in_win = _na_bias_constants()
    return pl.pallas_call(
        _na_bias_kernel,
        out_shape=jax.ShapeDtypeStruct((NA_HEADS, _NA_VARIANTS, GRID_W, _NA_KEYS), _F32),
        grid=(NA_HEADS,),
        in_specs=[
            pl.BlockSpec(memory_space=pltpu.SMEM),
            pl.BlockSpec(shift.shape, lambda h: (0, 0, 0)),
            pl.BlockSpec(in_win.shape, lambda h: (0, 0)),
        ],
        out_specs=pl.BlockSpec((1, _NA_VARIANTS, GRID_W, _NA_KEYS), lambda h: (h, 0, 0, 0)),
        compiler_params=_params(("arbitrary",)),
        name="na_bias",
    )(rpb.reshape(-1), shift, in_win)


def _na_row_groups():
    start = lambda r: min(max(r - NA_WIN_ROWS // 2, 0), _NA_ROWS - NA_WIN_ROWS)
    groups, r = [], 0
    while r < _NA_ROWS:
        n = 1
        while r + n < _NA_ROWS and start(r + n) == start(r):
            n += 1
        groups.append((r, n, start(r)))
        r += n
    return groups


def _na_lat_kernel(a_ref, kctx_ref, vctx_ref, bias_ref, qn_ref, kn_ref, o_ref,
                   q_scr, k_scr, v_scr, sctx_scr, pctx_scr):
    scale = NA_HD ** -0.5
    for h in range(NA_HEADS):
        cols = slice(h * NA_HD, (h + 1) * NA_HD)
        q_scr[...] = _bf(_rms(a_ref[:, h * NA_HD:(h + 1) * NA_HD], qn_ref[...]))
        k_scr[...] = _bf(_rms(a_ref[:, NA_DIM + h * NA_HD:NA_DIM + (h + 1) * NA_HD], kn_ref[...]))
        v_scr[...] = _bf(a_ref[:, 2 * NA_DIM + h * NA_HD:2 * NA_DIM + (h + 1) * NA_HD])
        sctx_scr[...] = _dot_nt(q_scr[...], _bf(kctx_ref[0, h])) * scale
        for r0, nr, ks in _na_row_groups():
            qrows = slice(r0 * GRID_W, (r0 + nr) * GRID_W)
            krows = slice(ks * GRID_W, ks * GRID_W + _NA_KEYS)
            bias = jnp.concatenate([bias_ref[h, r - ks] for r in range(r0, r0 + nr)], axis=0)
            s_loc = _dot_nt(q_scr[qrows, :], k_scr[krows, :]) * scale + bias
            s_ctx = sctx_scr[qrows, :]
            m = jnp.maximum(jnp.max(s_loc, axis=-1, keepdims=True), jnp.max(s_ctx, axis=-1, keepdims=True))
            e_loc = jnp.exp(s_loc - m)
            e_ctx = jnp.exp(s_ctx - m)
            inv = 1.0 / (jnp.sum(e_loc, axis=-1, keepdims=True) + jnp.sum(e_ctx, axis=-1, keepdims=True))
            pctx_scr[qrows, :] = _bf(e_ctx * inv)
            o_ref[qrows, cols] = _dot(_bf(e_loc * inv), v_scr[krows, :])
        o_ref[:, cols] = o_ref[:, cols] + _dot(pctx_scr[...], _bf(vctx_ref[0, h]))


def _na_lat(a_all, k_ctx, v_ctx, bias, qn, kn):
    full2 = lambda i: (0, 0)
    ctx_spec = pl.BlockSpec((1, NA_HEADS, PAST_LEN, NA_HD), lambda i: (i, 0, 0, 0))
    return pl.pallas_call(
        _na_lat_kernel,
        out_shape=jax.ShapeDtypeStruct((N_LAT_TOK, NA_DIM), _F32),
        grid=(DEC_BATCH,),
        in_specs=[
            pl.BlockSpec((DEC_SEQ, 3 * NA_DIM), lambda i: (N_CTX_TOK // DEC_SEQ + i, 0)),
            ctx_spec,
            ctx_spec,
            pl.BlockSpec(bias.shape, lambda i: (0, 0, 0, 0)),
            pl.BlockSpec((1, NA_HD), full2),
            pl.BlockSpec((1, NA_HD), full2),
        ],
        out_specs=pl.BlockSpec((DEC_SEQ, NA_DIM), lambda i: (i, 0)),
        scratch_shapes=[pltpu.VMEM((DEC_SEQ, NA_HD), _BF)] * 3
        + [pltpu.VMEM((DEC_SEQ, PAST_LEN), _F32), pltpu.VMEM((DEC_SEQ, PAST_LEN), _BF)],
        compiler_params=_params(("arbitrary",)),
        name="na_lat",
    )(a_all, k_ctx, v_ctx, bias, qn, kn)


def _outproj_kernel(gc_ref, gl_ref, cc_ref, cl_ref, nc_ref, nl_ref, xc_ref, xl_ref, mod_ref, nw_ref, wo_ref,
                    rwt_ref, x1_ref, h2_ref, hr_ref, lg_ref):
    i = pl.program_id(0)
    is_ctx = i < N_CTX_TILES
    row = _mod_row(i)
    gla = jnp.where(is_ctx, gc_ref[...], gl_ref[...])
    conv = jnp.where(is_ctx, cc_ref[...], cl_ref[...])
    na = jnp.where(is_ctx, nc_ref[...], nl_ref[...])
    mix = (_dot(_bf(gla), wo_ref[0:GLA_VAL, :])
           + _dot(_bf(conv), wo_ref[GLA_VAL:GLA_VAL + CONV_CH, :])
           + _dot(_bf(na), wo_ref[GLA_VAL + CONV_CH:, :]))
    g1 = mod_ref[pl.ds(row, 1), 2 * D_MODEL:3 * D_MODEL]
    sh2 = mod_ref[pl.ds(row, 1), 3 * D_MODEL:4 * D_MODEL]
    sc2 = mod_ref[pl.ds(row, 1), 4 * D_MODEL:5 * D_MODEL]
    x1 = jnp.where(is_ctx, xc_ref[...], xl_ref[...]) + g1 * mix
    h2 = _rms(x1, nw_ref[...]) * (1.0 + sc2) + sh2
    x1_ref[...] = x1
    h2_ref[...] = _bf(h2)
    for j in range(_ROW_CHUNKS):
        hr_ref[pl.ds(j, ROW_TILE, stride=_ROW_CHUNKS), :] = h2[:, j * _LANE:(j + 1) * _LANE]
    lg_ref[...] = _dot3(rwt_ref[...], h2, dot=_dot_nt)


def _out_proj(gla_c, gla_l, conv_c, conv_l, na_c, na_l, x_ctx, x_lat, lat_first_tile, mod, norm_w, w_out,
              router_wt):
    full2 = lambda i: (0, 0)
    tile = lambda i: (i, 0)
    ctx_tile, lat_tile = _stream_tiles(0)
    _, x_lat_tile = _stream_tiles(lat_first_tile)
    return pl.pallas_call(
        _outproj_kernel,
        out_shape=(
            jax.ShapeDtypeStruct((N_TOK, D_MODEL), _F32),
            jax.ShapeDtypeStruct((N_TOK, D_MODEL), _BF),
            jax.ShapeDtypeStruct((N_TOK * _ROW_CHUNKS, _LANE), _F32),
            jax.ShapeDtypeStruct((N_EXPERTS, N_TOK), _F32),
        ),
        grid=(N_TOK // ROW_TILE,),
        in_specs=[
            pl.BlockSpec((ROW_TILE, GLA_VAL), ctx_tile),
            pl.BlockSpec((ROW_TILE, GLA_VAL), lat_tile),
            pl.BlockSpec((ROW_TILE, CONV_CH), ctx_tile),
            pl.BlockSpec((ROW_TILE, CONV_CH), lat_tile),
            pl.BlockSpec((ROW_TILE, NA_DIM), ctx_tile),
            pl.BlockSpec((ROW_TILE, NA_DIM), lat_tile),
            pl.BlockSpec((ROW_TILE, D_MODEL), ctx_tile),
            pl.BlockSpec((ROW_TILE, D_MODEL), x_lat_tile),
            pl.BlockSpec((MOD_ROWS, 6 * D_MODEL), full2),
            pl.BlockSpec((1, D_MODEL), full2),
            pl.BlockSpec((D_MODEL, D_MODEL), full2),
            pl.BlockSpec((N_EXPERTS, D_MODEL), full2),
        ],
        out_specs=(
            pl.BlockSpec((ROW_TILE, D_MODEL), tile),
            pl.BlockSpec((ROW_TILE, D_MODEL), tile),
            pl.BlockSpec((ROW_TILE * _ROW_CHUNKS, _LANE), tile),
            pl.BlockSpec((N_EXPERTS, ROW_TILE), lambda i: (0, i)),
        ),
        compiler_params=_params(("arbitrary",)),
        name="out_proj",
    )(gla_c, gla_l, conv_c, conv_l, na_c, na_l, x_ctx, x_lat, mod, norm_w, w_out, router_wt)


_PER_GROUP = N_EXPERTS // N_GROUPS
_ROUTE_TILE = 1024


def _first_max(x, idx, axes, sentinel):
    m = x
    for ax in axes:
        m = jnp.max(m, axis=ax, keepdims=True)
    first = jnp.where(x == m, idx, sentinel)
    for ax in axes:
        first = jnp.min(first, axis=ax, keepdims=True)
    return m, first


def _router_kernel(lg_ref, rb_ref, eid_ref, wts_ref, cnt_ref):
    t = lg_ref.shape[1]
    shape3 = (N_GROUPS, _PER_GROUP, t)
    scores = _sigmoid(lg_ref[...])
    biased = (scores + rb_ref[...]).reshape(shape3)
    scores = scores.reshape(shape3)
    neg = -jnp.inf
    in_grp = lax.broadcasted_iota(jnp.int32, shape3, 1)
    grp = lax.broadcasted_iota(jnp.int32, (N_GROUPS, 1, t), 0)
    expert = lax.broadcasted_iota(jnp.int32, shape3, 0) * _PER_GROUP + in_grp
    m1, i1 = _first_max(biased, in_grp, (1,), _PER_GROUP)
    m2 = jnp.max(jnp.where(in_grp == i1, neg, biased), axis=1, keepdims=True)
    gscore = m1 + m2
    keep = jnp.zeros((N_GROUPS, 1, t), _F32)
    for _ in range(TOPK_GROUPS):
        _, gi = _first_max(gscore, grp, (0,), N_GROUPS)
        hit = grp == gi
        keep = jnp.where(hit, 1.0, keep)
        gscore = jnp.where(hit, neg, gscore)
    masked = jnp.where(keep > 0.0, biased, neg)
    ids, picked = [], []
    hits = jnp.zeros(shape3, _F32)
    for _ in range(TOP_K):
        _, ei = _first_max(masked, expert, (1, 0), N_EXPERTS)
        hit = expert == ei
        hits = jnp.where(hit, 1.0, hits)
        sc = jnp.sum(jnp.sum(jnp.where(hit, scores, 0.0), axis=1, keepdims=True), axis=0, keepdims=True)
        ids.append(ei.reshape(1, t))
        picked.append(sc.reshape(1, t))
        masked = jnp.where(hit, neg, masked)
    picked = jnp.concatenate(picked, axis=0)
    den = jnp.sum(picked, axis=0, keepdims=True)
    eid_ref[...] = jnp.concatenate(ids, axis=0)
    wts_ref[...] = picked / den * ROUTED_SCALE
    cnt_ref[0] = jnp.sum(hits, axis=2, keepdims=True).reshape(N_EXPERTS, 1)


def _router(logits_t, router_bias):
    tile = lambda i: (0, i)
    return pl.pallas_call(
        _router_kernel,
        out_shape=(jax.ShapeDtypeStruct((TOP_K, N_TOK), jnp.int32), jax.ShapeDtypeStruct((TOP_K, N_TOK), _F32),
                   jax.ShapeDtypeStruct((N_TOK // _ROUTE_TILE, N_EXPERTS, 1), _F32)),
        grid=(N_TOK // _ROUTE_TILE,),
        in_specs=[
            pl.BlockSpec((N_EXPERTS, _ROUTE_TILE), tile),
            pl.BlockSpec((N_EXPERTS, 1), lambda i: (0, 0)),
        ],
        out_specs=(pl.BlockSpec((TOP_K, _ROUTE_TILE), tile), pl.BlockSpec((TOP_K, _ROUTE_TILE), tile),
                   pl.BlockSpec((1, N_EXPERTS, 1), lambda i: (i, 0, 0))),
        compiler_params=_params(("arbitrary",)),
        name="router",
    )(logits_t, router_bias)


_HALF_TOK = N_CTX_TOK
_N_HALVES = N_TOK // _HALF_TOK
_HALF_ASSIGN = _HALF_TOK * TOP_K
_GB = 576
_GB_MAX = _HALF_ASSIGN // _GB + N_EXPERTS
_LANE = 128
_ROW_CHUNKS = D_MODEL // _LANE
_GS = _GB + 1
_STAGE_ROWS = (_ROW_CHUNKS * _GS + 7) // 8 * 8
_RMW_BATCH = 8
_TOK_BITS = 12
assert _HALF_TOK == 1 << _TOK_BITS
_FFN_SPLIT = 1


def _moe_plan(eid, wts, tile_counts):
    tok = jnp.arange(N_TOK, dtype=jnp.int32)
    key = (((tok >> _TOK_BITS) * N_EXPERTS)[None, :] + eid) * _HALF_TOK + (tok & (_HALF_TOK - 1))[None, :]
    key_s, gate_s = lax.sort((key.reshape(-1), wts.reshape(-1)), num_keys=1)
    row_off = (key_s & (_HALF_TOK - 1)) * _ROW_CHUNKS
    count = jnp.sum(tile_counts.reshape(_N_HALVES, -1, N_EXPERTS), axis=1).astype(jnp.int32)
    row_end = jnp.cumsum(count.reshape(-1)).reshape(_N_HALVES, N_EXPERTS)
    row_start = row_end - count
    nblk = (count + _GB - 1) // _GB
    blk_end = jnp.cumsum(nblk, axis=1)
    blk_start = blk_end - nblk
    n_used = blk_end[:, -1]
    b = jnp.minimum(jnp.arange(_GB_MAX, dtype=jnp.int32)[None, :], n_used[:, None] - 1)
    grp = jnp.sum((b[:, :, None] >= blk_end[:, None, :]).astype(jnp.int32), axis=-1)
    onehot = (grp[:, :, None] == jnp.arange(N_EXPERTS, dtype=jnp.int32)).astype(jnp.int32)
    pick = lambda v: jnp.sum(onehot * v[:, None, :], axis=-1)
    within = b - pick(blk_start)
    start = pick(row_start) + within * _GB
    length = jnp.clip(pick(count) - within * _GB, 0, _GB)
    pad = jnp.zeros((_GB,), jnp.int32)
    return (jnp.concatenate([row_off, pad]), jnp.concatenate([gate_s, pad.astype(_F32)]),
            grp.astype(jnp.int32), start.astype(jnp.int32), length.astype(jnp.int32), n_used.astype(jnp.int32))


def _gmm_kernel(be_ref, bstart_ref, blen_ref, nused_ref, off_ref, gate_ref,
                h_ref, wg_ref, wu_ref, wd_ref, acc_ref, xt_scr, yt_scr, wgb_scr, wub_scr, wdb_scr):
    s = pl.program_id(0)
    n_used = nused_ref[0]
    last = _GB_MAX - 1

    @pl.when(s == 0)
    def _():
        acc_ref[...] = jnp.zeros_like(acc_ref)
        xt_scr[...] = jnp.zeros_like(xt_scr)
        yt_scr[...] = jnp.zeros_like(yt_scr)

    fb = jnp.clip(s - 1, 0, last)

    @pl.when((s == 0) | (be_ref[fb] != be_ref[jnp.clip(s - 2, 0, last)]))
    def _():
        wgb_scr[...] = _bf(wg_ref[0, 0])
        wub_scr[...] = _bf(wu_ref[0, 0])
        wdb_scr[...] = _bf(wd_ref[0, 0])

    @pl.when(s < n_used + 2)
    def _():
        slot = s % 2
        other = 1 - slot

        cb = jnp.clip(s - 2, 0, last)
        c_start = bstart_ref[cb]
        c_len = blen_ref[cb]
        for m0 in range(0, _GB, _RMW_BATCH):
            pending = []
            for mi in range(m0, m0 + _RMW_BATCH):
                valid = mi < c_len
                row = jnp.where(valid, off_ref[c_start + mi], _HALF_TOK * _ROW_CHUNKS)
                gate = jnp.where(valid, gate_ref[c_start + mi], 0.0)
                rows = pl.ds(pl.multiple_of(row, _ROW_CHUNKS), _ROW_CHUNKS)
                contrib = gate * yt_scr[slot, pl.ds(mi, _ROW_CHUNKS, stride=_GS), :]
                pending.append((rows, acc_ref[rows, :] + contrib))
            for rows, val in pending:
                acc_ref[rows, :] = val

        part = _GB // _FFN_SPLIT
        for p in range(_FFN_SPLIT):
            x = _bf(jnp.concatenate([xt_scr[other, j * _GS + p * part:j * _GS + (p + 1) * part, :]
                                     for j in range(_ROW_CHUNKS)], axis=-1))
            act = _silu(_dot(x, wgb_scr[...])) * _dot(x, wub_scr[...])
            y = _dot(_bf(act), wdb_scr[...])
            for j in range(_ROW_CHUNKS):
                yt_scr[other, j * _GS + p * part:j * _GS + (p + 1) * part, :] = y[:, j * _LANE:(j + 1) * _LANE]

        g_start = bstart_ref[jnp.minimum(s, last)]
        for mi in range(_GB):
            src = pl.ds(pl.multiple_of(off_ref[g_start + mi], _ROW_CHUNKS), _ROW_CHUNKS)
            xt_scr[slot, pl.ds(mi, _ROW_CHUNKS, stride=_GS), :] = h_ref[src, :]


def _gmm(layer, half, plan, h_rows, wg, wu, wd):
    row_off, gate_s, grp, start, length, n_used = plan
    expert = lambda s, be, *_: (layer, be[jnp.clip(s - 1, 0, _GB_MAX - 1)], 0, 0)
    acc_rows = (_HALF_TOK + 1) * _ROW_CHUNKS
    grid_spec = pltpu.PrefetchScalarGridSpec(
        num_scalar_prefetch=6,
        grid=(_GB_MAX + 2,),
        in_specs=[
            pl.BlockSpec((_HALF_TOK * _ROW_CHUNKS, _LANE), lambda b, *_: (half, 0), pipeline_mode=pl.Buffered(1)),
            pl.BlockSpec((1, 1, D_MODEL, EXPERT_DIM), expert),
            pl.BlockSpec((1, 1, D_MODEL, EXPERT_DIM), expert),
            pl.BlockSpec((1, 1, EXPERT_DIM, D_MODEL), expert),
        ],
        out_specs=pl.BlockSpec((acc_rows, _LANE), lambda b, *_: (0, 0), pipeline_mode=pl.Buffered(1)),
        scratch_shapes=[pltpu.VMEM((2, _STAGE_ROWS, _LANE), _F32)] * 2
        + [pltpu.VMEM((D_MODEL, EXPERT_DIM), _BF)] * 2 + [pltpu.VMEM((EXPERT_DIM, D_MODEL), _BF)],
    )
    acc = pl.pallas_call(
        _gmm_kernel,
        out_shape=jax.ShapeDtypeStruct((acc_rows, _LANE), _F32),
        grid_spec=grid_spec,
        compiler_params=_params(("arbitrary",)),
        name="moe_experts",
    )(grp[half], start[half], length[half], n_used[half:half + 1], row_off, gate_s, h_rows, wg, wu, wd)
    return acc


_FIN_TILE = 512


def _moe_finish_kernel(h_ref, rc_ref, rl_ref, x1_ref, mod_ref, sg_ref, su_ref, sd_ref, o_ref):
    t = pl.program_id(0)
    tiles_per_half = _HALF_TOK // _FIN_TILE
    is_ctx = t < tiles_per_half
    row = jnp.where(is_ctx, 0, 1 + (t - tiles_per_half) // (DEC_SEQ // _FIN_TILE))
    h = h_ref[...]
    shared = _dot(_bf(_silu(_dot(h, _bf(sg_ref[0]))) * _dot(h, _bf(su_ref[0]))), _bf(sd_ref[0]))
    routed = jnp.concatenate(
        [jnp.where(is_ctx, rc_ref[pl.ds(j, _FIN_TILE, stride=_ROW_CHUNKS), :],
                   rl_ref[pl.ds(j, _FIN_TILE, stride=_ROW_CHUNKS), :]) for j in range(_ROW_CHUNKS)], axis=-1)
    g2 = mod_ref[pl.ds(row, 1), 5 * D_MODEL:6 * D_MODEL]
    o_ref[...] = x1_ref[...] + g2 * (routed + shared)


def _moe_finish(layer, h2, routed_c, routed_l, x1, mod, sg, su, sd):
    tile = lambda t: (t, 0)
    tiles_per_half = _HALF_TOK // _FIN_TILE
    ctx_tile = lambda t: (jnp.minimum(t, tiles_per_half - 1), 0)
    lat_tile = lambda t: (jnp.maximum(t - tiles_per_half, 0), 0)
    shared = lambda t: (layer, 0, 0)
    return pl.pallas_call(
        _moe_finish_kernel,
        out_shape=jax.ShapeDtypeStruct((N_TOK, D_MODEL), _F32),
        grid=(N_TOK // _FIN_TILE,),
        in_specs=[
            pl.BlockSpec((_FIN_TILE, D_MODEL), tile),
            pl.BlockSpec((_FIN_TILE * _ROW_CHUNKS, _LANE), ctx_tile),
            pl.BlockSpec((_FIN_TILE * _ROW_CHUNKS, _LANE), lat_tile),
            pl.BlockSpec((_FIN_TILE, D_MODEL), tile),
            pl.BlockSpec((MOD_ROWS, 6 * D_MODEL), lambda t: (0, 0)),
            pl.BlockSpec((1, D_MODEL, SHARED_DIM), shared),
            pl.BlockSpec((1, D_MODEL, SHARED_DIM), shared),
            pl.BlockSpec((1, SHARED_DIM, D_MODEL), shared),
        ],
        out_specs=pl.BlockSpec((_FIN_TILE, D_MODEL), tile),
        compiler_params=_params(("arbitrary",)),
        name="moe_finish",
    )(h2, routed_c, routed_l, x1, mod, sg, su, sd)


def kernel(x_prompt, x_sample, state_gla, cache_na_k, cache_na_v, c, c_ctx, w_ada, b_ada, norm_mix, norm_ffn, w_in, gla_w_gate, gla_b_gate, gla_out_norm, conv_dw, conv_dw_b, conv_ln_g, conv_ln_b, conv_pw, na_q_norm, na_k_norm, na_rpb, w_out, router_w, router_bias, exp_w_gate, exp_w_up, exp_w_down, sh_w_gate, sh_w_up, sh_w_down):
    x_ctx, x_lat, x_lat_tile = x_prompt.reshape(N_CTX_TOK, D_MODEL), x_sample.reshape(N_LAT_TOK, D_MODEL), 0
    cvec = jnp.concatenate([c_ctx[None], c, jnp.zeros((MOD_ROWS - 1 - DEC_BATCH, D_MODEL), _F32)], axis=0)
    mod_all = _ada_mod(cvec, w_ada, b_ada)
    gla_consts = _gla_constants()
    zero_state = jnp.zeros((BATCH, 2, GLA_VAL, GLA_KEY), _F32)
    lat_first = N_CTX_TOK // DEC_SEQ

    states, keys, vals = [], [], []
    for l in range(DEPTH):
        mod = mod_all[l]
        wi = w_in[l]
        w_gla = _bf(jnp.pad(wi[:, :GLA_LR_OFF + 2 * GLA_GATE_RANK], ((0, 0), (0, GLA_IN_W - GLA_LR_OFF - 2 * GLA_GATE_RANK))))
        conv_off = GLA_LR_OFF + 2 * GLA_GATE_RANK
        w_conv = _bf(wi[:, conv_off:conv_off + 2 * CONV_CH])
        w_na = _bf(wi[:, conv_off + 2 * CONV_CH:])
        g_all, c_all, a_all = _in_proj(x_ctx, x_lat, x_lat_tile, norm_mix[l][None], mod, w_gla, w_conv, w_na)

        wz = jnp.zeros((GLA_IN_W - GLA_LR_OFF, 2 * GLA_KEY), _F32)
        wz = wz.at[:GLA_GATE_RANK, :GLA_KEY].set(gla_w_gate[l, 0])
        wz = wz.at[GLA_GATE_RANK:2 * GLA_GATE_RANK, GLA_KEY:].set(gla_w_gate[l, 1])
        bz = gla_b_gate[l].reshape(1, 2 * GLA_KEY)
        onorm = gla_out_norm[l][None]
        gla_c, st_c = _gla(g_all, wz, bz, gla_consts, onorm, zero_state, n=SEQ, n_seq=BATCH, first_block=0,
                          seqs=_GLA_CTX_SEQS)
        gla_l, _ = _gla(g_all, wz, bz, gla_consts, onorm, _state_to_blockdiag(state_gla[:, l]),
                        n=DEC_SEQ, n_seq=DEC_BATCH, first_block=lat_first, seqs=1)

        conv_args = (jnp.broadcast_to(conv_dw[l][:, None, :], (CONV_WIDTH, 8, CONV_CH)), conv_dw_b[l][None],
                     conv_ln_g[l][None], conv_ln_b[l][None], _bf(conv_pw[l]))
        conv_c = _conv(c_all, *conv_args, n=SEQ, n_seq=BATCH, first_block=0)
        conv_l = _conv(c_all, *conv_args, n=DEC_SEQ, n_seq=DEC_BATCH, first_block=lat_first)

        qn, kn = na_q_norm[l][None], na_k_norm[l][None]
        na_c, k_l, v_l = _na_ctx(a_all, qn, kn)
        na_l = _na_lat(a_all, cache_na_k[:, l], cache_na_v[:, l], _na_bias_table(na_rpb[l]), qn, kn)

        x1, h2, h_rows, logits_t = _out_proj(gla_c, gla_l, conv_c, conv_l, na_c, na_l, x_ctx, x_lat, x_lat_tile, mod,
                                     norm_ffn[l][None], _bf(w_out[l]), router_w[l].T)
        plan = _moe_plan(*_router(logits_t, router_bias[l][:, None]))
        routed = [_gmm(l, half, plan, h_rows, exp_w_gate, exp_w_up, exp_w_down) for half in range(_N_HALVES)]
        x = _moe_finish(l, h2, routed[0], routed[1], x1, mod, sh_w_gate, sh_w_up, sh_w_down)
        x_ctx, x_lat, x_lat_tile = x, x, N_CTX_TILES

        states.append(jnp.swapaxes(st_c, -1, -2))
        keys.append(k_l)
        vals.append(v_l)

    y_prompt = x[:N_CTX_TOK].reshape(BATCH, SEQ, D_MODEL)
    y_sample = x[N_CTX_TOK:].reshape(DEC_BATCH, DEC_SEQ, D_MODEL)
    return (y_prompt, y_sample, jnp.stack(states, axis=1), jnp.stack(keys, axis=1), jnp.stack(vals, axis=1))
```
